```python
import math
import jax, jax.numpy as jnp
from jax import lax
import numpy as np

D_MODEL = 1024
BATCH = 8
SEQ = 2048
DEPTH = 1

NSA_HEADS = 8
NSA_KV_GROUPS = 2
NSA_REP = NSA_HEADS // NSA_KV_GROUPS
NSA_HEAD_DIM = 64
CMP_LEN = 32
CMP_STRIDE = 16
SLC_LEN = 64
SLC_TOPK = 8
WIN_LEN = 512
Q_BLK = 128
SEL_Q_BLK = 64
FORCE_SCORE = 1.0e4
NEG = -1.0e30
GDN_HEADS = 4
GDN_HEAD_DIM = 128
GDN_CONV = 4
GDN_CHUNK = 64
FFN_DIM = 2816
FFN_CONV = 3
DEEPNORM_ALPHA = (2.0 * DEPTH) ** 0.25
DEEPNORM_BETA = (8.0 * DEPTH) ** -0.25
LN_EPS = 1e-5
RMS_EPS = 1e-6

NSA_Q_W = NSA_HEADS * NSA_HEAD_DIM
NSA_KV_W = NSA_KV_GROUPS * NSA_HEAD_DIM
GDN_W = GDN_HEADS * GDN_HEAD_DIM
IN_WIDTHS = (NSA_Q_W, NSA_KV_W, NSA_KV_W, NSA_KV_W, NSA_KV_W, NSA_KV_W, NSA_KV_W,
             3 * NSA_HEADS, 3 * GDN_W, GDN_HEADS, GDN_HEADS, GDN_W, 2 * D_MODEL)
IN_WIDTH = sum(IN_WIDTHS)

kernel_name = "nsa_gdn_gated_hybrid_deepnorm"


def _split_in(h):
    offs = np.cumsum((0,) + IN_WIDTHS)
    return [h[..., int(offs[i]):int(offs[i + 1])] for i in range(len(IN_WIDTHS))]


def layer_norm(x, g, b):
    xf = x.astype(jnp.float32)
    mu = jnp.mean(xf, -1, keepdims=True)
    var = jnp.mean(jnp.square(xf - mu), -1, keepdims=True)
    return ((xf - mu) * lax.rsqrt(var + LN_EPS) * g + b).astype(x.dtype)


def rms_norm(x, w):
    xf = x.astype(jnp.float32)
    return xf * lax.rsqrt(jnp.mean(jnp.square(xf), -1, keepdims=True) + RMS_EPS) * w


def l2_norm(x):
    return x * lax.rsqrt(jnp.sum(jnp.square(x), -1, keepdims=True) + RMS_EPS)


def causal_dwconv(x, w):
    K, S = w.shape[0], x.shape[1]
    xp = jnp.pad(x, ((0, 0), (K - 1, 0), (0, 0)))
    out = xp[:, 0:S] * w[0]
    for j in range(1, K):
        out = out + xp[:, j:j + S] * w[j]
    return out


def alibi_slopes(n):
    return jnp.asarray([2.0 ** (-8.0 * (h + 1) / n) for h in range(n)], jnp.float32)


def nsa_attention(q, kc, vc, ks, vs, kw, vw, gates, cmp_pos, cmp_w1, cmp_w2):
    B, S, _ = q.shape
    G, R, DH = NSA_KV_GROUPS, NSA_REP, NSA_HEAD_DIM
    dt = q.dtype
    f32 = jnp.float32
    scale = DH ** -0.5
    q = q.reshape(B, S, G, R, DH).transpose(0, 2, 3, 1, 4)
    heads = lambda t: t.reshape(B, S, G, DH).transpose(0, 2, 1, 3)
    kc, vc, ks, vs, kw, vw = (heads(t) for t in (kc, vc, ks, vs, kw, vw))
    slopes = alibi_slopes(NSA_HEADS).reshape(G, R)
    t_pos = jnp.arange(S)

    n_cmp = (S - CMP_LEN) // CMP_STRIDE + 1
    cmp_idx = np.arange(n_cmp)[:, None] * CMP_STRIDE + np.arange(CMP_LEN)[None, :]

    def compress(t, i):
        blk = t[:, :, cmp_idx] + cmp_pos[i]
        blk = blk.reshape(B, G, n_cmp, CMP_LEN * DH)
        return jax.nn.gelu(blk @ cmp_w1[i]) @ cmp_w2[i]

    k_cmp, v_cmp = compress(kc, 0), compress(vc, 1)
    cmp_end = jnp.asarray(cmp_idx[:, -1])
    dist_c = (t_pos[:, None] - cmp_end[None, :]).astype(f32)
    valid_c = dist_c >= 0
    s_c = jnp.einsum('bgrtd,bgnd->bgrtn', q, k_cmp).astype(f32) * scale
    s_c = jnp.where(valid_c, s_c - slopes[None, :, :, None, None] * dist_c, NEG)
    p_cmp = jax.nn.softmax(s_c, axis=-1) * valid_c
    o_cmp = jnp.einsum('bgrtn,bgnd->bgrtd', p_cmp.astype(dt), v_cmp)

    n_slc = S // SLC_LEN
    c_start = np.arange(n_cmp)[:, None] * CMP_STRIDE
    s_start = np.arange(n_slc)[None, :] * SLC_LEN
    overlap = ((c_start < s_start + SLC_LEN) & (c_start + CMP_LEN > s_start)).astype(np.float32)
    score = jnp.einsum('bgrtn,nj->bgtj', p_cmp, jnp.asarray(overlap))
    blk = jnp.arange(n_slc)[None, :]
    cur = (t_pos // SLC_LEN)[:, None]
    forced = (blk == 0) | (blk == cur) | (blk == cur - 1)
    score = jnp.where(forced, FORCE_SCORE, jnp.where(blk <= cur, score, -1.0))
    k_sel = min(SLC_TOPK, n_slc)
    _, idx = lax.top_k(score, k_sel)

    ks_blk = ks.reshape(B, G, n_slc, SLC_LEN, DH)
    vs_blk = vs.reshape(B, G, n_slc, SLC_LEN, DH)
    nq = S // SEL_Q_BLK
    q_ch = jnp.moveaxis(q.reshape(B, G, R, nq, SEL_Q_BLK, DH), 3, 0)
    i_ch = jnp.moveaxis(idx.reshape(B, G, nq, SEL_Q_BLK, k_sel), 2, 0)
    t_ch = t_pos.reshape(nq, SEL_Q_BLK)
    bi = jnp.arange(B)[:, None, None, None]
    gi = jnp.arange(G)[None, :, None, None]
    offs = jnp.arange(SLC_LEN)
    n_key = k_sel * SLC_LEN

    def sel_chunk(args):
        qc, ic, tc = args
        kg = ks_blk[bi, gi, ic].reshape(B, G, SEL_Q_BLK, n_key, DH)
        vg = vs_blk[bi, gi, ic].reshape(B, G, SEL_Q_BLK, n_key, DH)
        kpos = (ic[..., None] * SLC_LEN + offs).reshape(B, G, SEL_Q_BLK, n_key)
        dist = (tc[:, None] - kpos).astype(f32)
        s = jnp.einsum('bgrtd,bgtsd->bgrts', qc, kg).astype(f32) * scale
        s = jnp.where((dist >= 0)[:, :, None],
                      s - slopes[None, :, :, None, None] * dist[:, :, None], NEG)
        p = jax.nn.softmax(s, axis=-1)
        return jnp.einsum('bgrts,bgtsd->bgrtd', p.astype(dt), vg)

    o_slc = lax.map(sel_chunk, (q_ch, i_ch, t_ch))
    o_slc = jnp.moveaxis(o_slc, 0, 3).reshape(B, G, R, S, DH)

    nb = S // Q_BLK
    nw = WIN_LEN // Q_BLK

    def band(t):
        tb = jnp.pad(t.reshape(B, G, nb, Q_BLK, DH), ((0, 0), (0, 0), (nw, 0), (0, 0), (0, 0)))
        return jnp.concatenate([tb[:, :, j:j + nb] for j in range(nw + 1)], axis=3)

    kwb, vwb = band(kw), band(vw)
    kpos_w = (jnp.arange(nb)[:, None] - nw) * Q_BLK + jnp.arange((nw + 1) * Q_BLK)[None, :]
    tq = jnp.arange(nb)[:, None] * Q_BLK + jnp.arange(Q_BLK)[None, :]
    dist_w = tq[:, :, None] - kpos_w[:, None, :]
    valid_w = (dist_w >= 0) & (dist_w < WIN_LEN) & (kpos_w[:, None, :] >= 0)
    qb = q.reshape(B, G, R, nb, Q_BLK, DH)
    s_w = jnp.einsum('bgrcqd,bgckd->bgrcqk', qb, kwb).astype(f32) * scale
    s_w = jnp.where(valid_w, s_w - slopes[None, :, :, None, None, None] * dist_w.astype(f32), NEG)
    p_w = jax.nn.softmax(s_w, axis=-1)
    o_win = jnp.einsum('bgrcqk,bgckd->bgrcqd', p_w.astype(dt), vwb).reshape(B, G, R, S, DH)

    g = jax.nn.sigmoid(gates).reshape(B, S, 3, G, R).transpose(2, 0, 3, 4, 1)[..., None]
    o = g[0] * o_cmp + g[1] * o_slc + g[2] * o_win
    return o.transpose(0, 3, 1, 2, 4).reshape(B, S, NSA_Q_W)


def gated_deltanet(qkv, beta_raw, decay_raw, gate_raw, conv_w, a_log, dt_bias, norm_w):
    B, S, _ = qkv.shape
    H, Dh, C = GDN_HEADS, GDN_HEAD_DIM, GDN_CHUNK
    dt = qkv.dtype
    f32 = jnp.float32
    qkv = jax.nn.silu(causal_dwconv(qkv, conv_w)).astype(f32)
    q, k, v = jnp.split(qkv, 3, axis=-1)
    heads = lambda t: t.reshape(B, S, H, Dh).transpose(0, 2, 1, 3)
    q = l2_norm(heads(q)) * (Dh ** -0.5)
    k = l2_norm(heads(k))
    v = heads(v)
    beta = jax.nn.sigmoid(beta_raw.astype(f32)).transpose(0, 2, 1)
    g = (-jnp.exp(a_log.astype(f32))[None, :, None]
         * jax.nn.softplus(decay_raw.astype(f32) + dt_bias.astype(f32)).transpose(0, 2, 1))
    N = S // C
    q, k, v = (t.reshape(B, H, N, C, Dh) for t in (q, k, v))
    beta, g = beta.reshape(B, H, N, C), g.reshape(B, H, N, C)
    gcum = jnp.cumsum(g, axis=-1)
    causal = jnp.tril(jnp.ones((C, C), bool))
    strict = jnp.tril(jnp.ones((C, C), bool), -1)
    decay = jnp.exp(jnp.where(causal, gcum[..., :, None] - gcum[..., None, :], -jnp.inf))
    kb = k * beta[..., None]
    m = jnp.where(strict, jnp.einsum('bhncd,bhnsd->bhncs', kb, k) * decay, 0.0)
    eye = jnp.eye(C, dtype=f32)
    T = lax.linalg.triangular_solve(eye + m, jnp.broadcast_to(eye, m.shape),
                                    left_side=True, lower=True, unit_diagonal=True)
    u = T @ (v * beta[..., None])
    w = T @ (kb * jnp.exp(gcum)[..., None])
    a_qk = jnp.einsum('bhncd,bhnsd->bhncs', q, k) * decay

    def step(state, xs):
        qn, kn, un, wn, an, gn = xs
        v_new = un - wn @ state
        o = (qn * jnp.exp(gn)[..., None]) @ state + an @ v_new
        g_last = gn[..., -1:]
        state = (state * jnp.exp(g_last)[..., None]
                 + jnp.einsum('bhcd,bhce->bhde', kn * jnp.exp(g_last - gn)[..., None], v_new))
        return state, o

    xs = tuple(jnp.moveaxis(t, 2, 0) for t in (q, k, u, w, a_qk, gcum))
    _, o = lax.scan(step, jnp.zeros((B, H, Dh, Dh), f32), xs)
    o = jnp.moveaxis(o, 0, 2).reshape(B, H, S, Dh).transpose(0, 2, 1, 3)
    o = rms_norm(o, norm_w) * jax.nn.silu(gate_raw.astype(f32).reshape(B, S, H, Dh))
    return o.reshape(B, S, GDN_W).astype(dt)


def conv_ffn(x, w_up, conv_w, w_down):
    h = causal_dwconv(x @ w_up, conv_w)
    gate, val = jnp.split(h, 2, axis=-1)
    return (jax.nn.silu(gate) * val) @ w_down


def setup_inputs(seed: int = 0) -> dict:
    key = jax.random.key(seed)
    ks = jax.random.split(key, 24)
    f32 = jnp.float32
    nrm = lambda k, shape, fan_in, gain=1.0: jax.random.normal(k, shape, f32) * (gain * fan_in ** -0.5)
    offs = np.cumsum((0,) + IN_WIDTHS)
    col_scale = np.ones(IN_WIDTH, np.float32)
    for i in (2, 4, 6):
        col_scale[offs[i]:offs[i + 1]] = DEEPNORM_BETA
    col_scale[offs[8] + 2 * GDN_W:offs[9]] = DEEPNORM_BETA
    dtv = jnp.exp(jax.random.uniform(ks[8], (DEPTH, GDN_HEADS), f32,
                                     minval=math.log(1e-3), maxval=math.log(1e-1)))
    return {
        "x": jax.random.normal(ks[0], (BATCH, SEQ, D_MODEL), f32),
        "w_in": nrm(ks[1], (DEPTH, D_MODEL, IN_WIDTH), D_MODEL) * jnp.asarray(col_scale),
        "nsa_cmp_pos": 0.02 * jax.random.normal(ks[2], (DEPTH, 2, CMP_LEN, NSA_HEAD_DIM), f32),
        "nsa_cmp_w1": nrm(ks[3], (DEPTH, 2, CMP_LEN * NSA_HEAD_DIM, NSA_HEAD_DIM), CMP_LEN * NSA_HEAD_DIM),
        "nsa_cmp_w2": nrm(ks[4], (DEPTH, 2, NSA_HEAD_DIM, NSA_HEAD_DIM), NSA_HEAD_DIM),
        "w_nsa_out": nrm(ks[5], (DEPTH, NSA_Q_W, D_MODEL), NSA_Q_W, DEEPNORM_BETA),
        "gdn_conv_w": nrm(ks[6], (DEPTH, GDN_CONV, 3 * GDN_W), GDN_CONV),
        "gdn_a_log": jnp.log(jax.random.uniform(ks[7], (DEPTH, GDN_HEADS), f32, minval=1.0, maxval=16.0)),
        "gdn_dt_bias": dtv + jnp.log(-jnp.expm1(-dtv)),
        "gdn_norm_w": 1.0 + 0.02 * jax.random.normal(ks[9], (DEPTH, GDN_HEAD_DIM), f32),
        "w_gdn_out": nrm(ks[10], (DEPTH, GDN_W, D_MODEL), GDN_W, DEEPNORM_BETA),
        "w_o": nrm(ks[11], (DEPTH, D_MODEL, D_MODEL), D_MODEL, DEEPNORM_BETA),
        "ln1_g": 1.0 + 0.02 * jax.random.normal(ks[12], (DEPTH, D_MODEL), f32),
        "ln1_b": 0.02 * jax.random.normal(ks[13], (DEPTH, D_MODEL), f32),
        "ffn_w_up": nrm(ks[14], (DEPTH, D_MODEL, 2 * FFN_DIM), D_MODEL, DEEPNORM_BETA),
        "ffn_conv_w": nrm(ks[15], (DEPTH, FFN_CONV, 2 * FFN_DIM), FFN_CONV),
        "ffn_w_down": nrm(ks[16], (DEPTH, FFN_DIM, D_MODEL), FFN_DIM, DEEPNORM_BETA),
        "ln2_g": 1.0 + 0.02 * jax.random.normal(ks[17], (DEPTH, D_MODEL), f32),
        "ln2_b": 0.02 * jax.random.normal(ks[18], (DEPTH, D_MODEL), f32),
    }


def reference(x, w_in, nsa_cmp_pos, nsa_cmp_w1, nsa_cmp_w2, w_nsa_out, gdn_conv_w, gdn_a_log,
              gdn_dt_bias, gdn_norm_w, w_gdn_out, w_o, ln1_g, ln1_b, ffn_w_up, ffn_conv_w,
              ffn_w_down, ln2_g, ln2_b):
    for l in range(DEPTH):
        h = x @ w_in[l]
        (nsa_q, cmp_k, cmp_v, slc_k, slc_v, win_k, win_v, nsa_gate,
         gdn_qkv, gdn_beta, gdn_decay, gdn_gate, merge_gate) = _split_in(h)
        y_a = nsa_attention(nsa_q, cmp_k, cmp_v, slc_k, slc_v, win_k, win_v, nsa_gate,
                            nsa_cmp_pos[l], nsa_cmp_w1[l], nsa_cmp_w2[l]) @ w_nsa_out[l]
        y_b = gated_deltanet(gdn_qkv, gdn_beta, gdn_decay, gdn_gate, gdn_conv_w[l],
                             gdn_a_log[l], gdn_dt_bias[l], gdn_norm_w[l]) @ w_gdn_out[l]
        gate_a, gate_b = jnp.split(merge_gate, 2, axis=-1)
        mix = (jax.nn.sigmoid(gate_a) * y_a + jax.nn.sigmoid(gate_b) * y_b) @ w_o[l]
        x = layer_norm(DEEPNORM_ALPHA * x + mix, ln1_g[l], ln1_b[l])
        f = conv_ffn(x, ffn_w_up[l], ffn_conv_w[l], ffn_w_down[l])
        x = layer_norm(DEEPNORM_ALPHA * x + f, ln2_g[l], ln2_b[l])
    return x
```

```python
import functools

import numpy as np
import jax
import jax.numpy as jnp
from jax import lax
from jax.experimental import pallas as pl
from jax.experimental.pallas import tpu as pltpu

F32 = jnp.float32
BF16 = jnp.bfloat16

D_MODEL = 1024
NSA_HEADS = 8
NSA_KV_GROUPS = 2
NSA_REP = NSA_HEADS // NSA_KV_GROUPS
NSA_HEAD_DIM = 64
CMP_LEN = 32
CMP_STRIDE = 16
SLC_LEN = 64
SLC_TOPK = 8
WIN_LEN = 512
FORCE_SCORE = 1.0e4
NEG = -1.0e30
GDN_HEADS = 4
GDN_HEAD_DIM = 128
GDN_CONV = 4
GDN_CHUNK = 64
FFN_DIM = 2816
FFN_CONV = 3
DEPTH = 1
DEEPNORM_ALPHA = (2.0 * DEPTH) ** 0.25
LN_EPS = 1e-5
RMS_EPS = 1e-6

NSA_Q_W = NSA_HEADS * NSA_HEAD_DIM
NSA_KV_W = NSA_KV_GROUPS * NSA_HEAD_DIM
GDN_W = GDN_HEADS * GDN_HEAD_DIM

LANES = 128
VMEM_LIMIT_BYTES = 56 * 1024 * 1024

AUG_SEL0 = 64
AUG_POS_HI = 96
AUG_POS_LO = 97
AUG_PAD = 98
BIG = 2.0 ** 100
Q_TILE = 128
N_SLC = 32

SM_BETA = 12
SM_DECAY = 16

NT_DIMS = (((1,), (1,)), ((), ()))


def _dot(a, b, **kw):
    return jnp.dot(a, b, preferred_element_type=F32, **kw)


def _dot_nt(a, b, **kw):
    return lax.dot_general(a, b, NT_DIMS, preferred_element_type=F32, **kw)


def _cparams(*sem):
    return pltpu.CompilerParams(dimension_semantics=sem, vmem_limit_bytes=VMEM_LIMIT_BYTES)


def _const_spec(shape):
    nd = len(shape)
    return pl.BlockSpec(shape, lambda *_: (0,) * nd, pipeline_mode=pl.Buffered(1))


_INPROJ_GROUPS = (("a", 1024), ("cmp", 256), ("small", 256), ("gqkv", 3 * GDN_W), ("ggate", GDN_W),
                  ("merge", 2 * D_MODEL))
_INPROJ_WIDTH = sum(w for _, w in _INPROJ_GROUPS)
INPROJ_TM = 512


def _regroup_w_in(w):
    o = np.cumsum((0, NSA_Q_W) + (NSA_KV_W,) * 6 + (3 * NSA_HEADS, 3 * GDN_W, GDN_HEADS, GDN_HEADS, GDN_W,
                                                   2 * D_MODEL))
    seg = lambda i: w[:, int(o[i]):int(o[i + 1])]
    q, ck, cv, sk, sv, wk, wv, gate, gqkv, beta, decay, ggate, merge = (seg(i) for i in range(13))
    hd = NSA_HEAD_DIM
    grp = lambda t, g: t[:, g * hd:(g + 1) * hd]
    slabs = [jnp.concatenate([grp(sk, g), grp(sv, g), grp(wk, g), grp(wv, g)], axis=1) for g in range(2)]
    cmpkv = jnp.concatenate([ck, cv], axis=1)
    gate = gate.reshape(-1, 3, NSA_KV_GROUPS, NSA_REP)
    smalls = []
    for g in range(NSA_KV_GROUPS):
        gg = gate[:, :, g, :].reshape(-1, 3 * NSA_REP)
        pad = jnp.zeros((w.shape[0], LANES - 3 * NSA_REP - 2 * GDN_HEADS), w.dtype)
        smalls.append(jnp.concatenate([gg, beta, decay, pad], axis=1))
    return jnp.concatenate([q] + slabs + [cmpkv] + smalls + [gqkv, ggate, merge], axis=1).astype(BF16)


def _inproj_body(x_ref, w_ref, a_ref, cmp_ref, small_ref, gqkv_ref, ggate_ref, merge_ref):
    x = x_ref[...]
    outs = (a_ref, cmp_ref, small_ref, gqkv_ref, ggate_ref, merge_ref)
    c0 = 0
    for ref, (_, width) in zip(outs, _INPROJ_GROUPS):
        for s in range(0, width, 512):
            e = min(s + 512, width)
            ref[:, s:e] = _dot(x, w_ref[:, c0 + s:c0 + e]).astype(ref.dtype)
        c0 += width


def _inproj(xb, w):
    m = xb.shape[0]
    dts = dict(a=BF16, cmp=BF16, small=F32, gqkv=F32, ggate=BF16, merge=BF16)
    return pl.pallas_call(
        _inproj_body,
        grid=(m // INPROJ_TM,),
        in_specs=[pl.BlockSpec((INPROJ_TM, D_MODEL), lambda i: (i, 0)), _const_spec((D_MODEL, _INPROJ_WIDTH))],
        out_specs=[pl.BlockSpec((INPROJ_TM, wd), lambda i: (i, 0)) for _, wd in _INPROJ_GROUPS],
        out_shape=[jax.ShapeDtypeStruct((m, wd), dts[n]) for n, wd in _INPROJ_GROUPS],
        compiler_params=_cparams("parallel"),
        name="inproj",
    )(xb, w)


def _compress_weights(cmp_pos, cmp_w1, cmp_w2):
    hd, half = NSA_HEAD_DIM, CMP_LEN // 2
    w1r = cmp_w1.reshape(2, 2, half, hd, hd)
    eye = jnp.eye(2, dtype=cmp_w1.dtype)
    w1e = jnp.einsum("whlde,wv,gk->lwgdhvke", w1r, eye, eye).reshape(half * 4 * hd, 2 * 4 * hd)
    posr = cmp_pos.reshape(2, 2, half, hd)
    post = jnp.broadcast_to(posr.transpose(1, 2, 0, 3)[:, :, :, None, :], (2, half, 2, 2, hd))
    post = jnp.concatenate([post.reshape(2, half * 4 * hd), jnp.zeros((6, half * 4 * hd), cmp_pos.dtype)], axis=0)
    w2sel = jnp.zeros((2, 2, 2 * hd, LANES), cmp_w2.dtype)
    for g in range(2):
        w2sel = w2sel.at[:, g, g * hd:(g + 1) * hd, :hd].set(cmp_w2)
    return w1e.astype(BF16), post.astype(BF16), w2sel.reshape(4, 2 * hd, LANES).astype(BF16)


def _compress_body(t_ref, w1_ref, pos_ref, w2_ref, aug_ref, out_ref):
    p = _dot(t_ref[0], w1_ref[...])
    pp = _dot(pos_ref[...], w1_ref[...])
    nxt = pltpu.roll(p[:, 256:], p.shape[0] - 1, axis=0)
    pre = p[:, :256] + nxt + pp[0:1, :256] + pp[1:2, 256:]
    h = jax.nn.gelu(pre).astype(BF16)
    n_idx = lax.broadcasted_iota(jnp.int32, (p.shape[0], LANES), 0)
    real = n_idx < p.shape[0] - 1
    for which in range(2):
        hw = h[:, which * LANES:(which + 1) * LANES]
        for g in range(2):
            o = _dot(hw, w2_ref[which * 2 + g]) + aug_ref[which]
            out_ref[0, which * 2 + g] = jnp.where(real, o, 0.0).astype(out_ref.dtype)


def _compress(cmpkv, w1e, post, w2sel, aug):
    b, s, _ = cmpkv.shape
    nblk = s // CMP_STRIDE
    t2 = cmpkv.reshape(b, nblk, CMP_STRIDE * 256)
    return pl.pallas_call(
        _compress_body,
        grid=(b,),
        in_specs=[pl.BlockSpec((1, nblk, CMP_STRIDE * 256), lambda i: (i, 0, 0)),
                  _const_spec(w1e.shape), _const_spec(post.shape), _const_spec(w2sel.shape), _const_spec(aug.shape)],
        out_specs=pl.BlockSpec((1, 4, nblk, LANES), lambda i: (i, 0, 0, 0)),
        out_shape=jax.ShapeDtypeStruct((b, 4, nblk, LANES), BF16),
        compiler_params=_cparams("parallel"),
        name="compress",
    )(t2, w1e, post, w2sel, aug)


def _nsa_consts(s):
    hd = NSA_HEAD_DIM
    t = np.arange(s)
    kx_sel = np.zeros((s, hd), np.float32)
    kx_sel[t, t // SLC_LEN] = 1.0
    kx_sel[:, AUG_POS_HI - hd] = t // 256
    kx_sel[:, AUG_POS_LO - hd] = t % 256
    kx_win = np.zeros((s + WIN_LEN, hd), np.float32)
    kx_win[WIN_LEN + t, AUG_POS_HI - hd] = t // 256
    kx_win[WIN_LEN + t, AUG_POS_LO - hd] = t % 256
    kx_win[:WIN_LEN, AUG_PAD - hd] = 1.0
    vx = np.zeros((s + WIN_LEN, hd), np.float32)
    vx[:, 0] = 1.0
    vx_win = vx.copy()
    vx_win[:WIN_LEN] = 0.0
    n_cmp = s // CMP_STRIDE
    cmp_aug = np.zeros((2, n_cmp, LANES), np.float32)
    end = np.arange(n_cmp) * CMP_STRIDE + CMP_LEN - 1
    cmp_aug[0, :, AUG_POS_HI] = end // 256
    cmp_aug[0, :, AUG_POS_LO] = end % 256
    qx = np.zeros((NSA_KV_GROUPS, 8, LANES), np.float32)
    for h in range(NSA_HEADS):
        slope = 2.0 ** (-8.0 * (h + 1) / NSA_HEADS)
        qx[h // NSA_REP, h % NSA_REP, AUG_POS_HI] = slope * 256.0
        qx[h // NSA_REP, h % NSA_REP, AUG_POS_LO] = slope
        qx[h // NSA_REP, h % NSA_REP, AUG_PAD] = -BIG
    c0 = np.arange(n_cmp)[None, :] * CMP_STRIDE
    s0 = np.arange(s // SLC_LEN)[:, None] * SLC_LEN
    ov_t = ((c0 < s0 + SLC_LEN) & (c0 + CMP_LEN > s0)).astype(np.float32)
    ov_t[:, (s - CMP_LEN) // CMP_STRIDE + 1:] = 0.0
    tq = np.arange(Q_TILE)[:, None]
    causal = np.where(tq >= np.arange(Q_TILE)[None, :], 0.0, NEG).astype(np.float32)
    c = np.arange(WIN_LEN + Q_TILE)[None, :]
    band = np.where((c > tq) & (c <= tq + WIN_LEN), 0.0, NEG).astype(np.float32)
    j = jnp.asarray
    return dict(kx_sel=j(kx_sel, BF16), kx_win=j(kx_win, BF16), vx_sel=j(vx[:s], BF16), vx_win=j(vx_win, BF16),
                cmp_aug=j(cmp_aug), qx=j(qx), ov_t=j(ov_t), causal=j(causal), band=j(band))


def _softmax_pv(s, v):
    m = jnp.max(s, axis=1, keepdims=True)
    p = jnp.exp(s - m)
    acc = _dot(p.astype(BF16), v)
    return acc[:, :NSA_HEAD_DIM] / acc[:, NSA_HEAD_DIM:NSA_HEAD_DIM + 1]


def _nsa_body(q_ref, slab_ref, kc_ref, vc_ref, gate_ref, kxs_ref, kxw_ref, vxs_ref, vxw_ref, qx_ref, ovt_ref,
              causal_ref, band_ref, out_ref, ks_s, vs_s, kw_s, vw_s):
    hd, rep, tq = NSA_HEAD_DIM, NSA_REP, Q_TILE
    i = pl.program_id(2)
    s_len = slab_ref.shape[1]

    @pl.when(i == 0)
    def _():
        slab = slab_ref[0]
        ks_s[:, :hd] = slab[:, 0:hd]
        ks_s[:, hd:] = kxs_ref[...]
        vs_s[:, :hd] = slab[:, hd:2 * hd]
        vs_s[:, hd:] = vxs_ref[...]
        kw_s[:WIN_LEN, :hd] = jnp.zeros((WIN_LEN, hd), BF16)
        kw_s[WIN_LEN:, :hd] = slab[:, 2 * hd:3 * hd]
        kw_s[:, hd:] = kxw_ref[...]
        vw_s[:WIN_LEN, :hd] = jnp.zeros((WIN_LEN, hd), BF16)
        vw_s[WIN_LEN:, :hd] = slab[:, 3 * hd:4 * hd]
        vw_s[:, hd:] = vxw_ref[...]

    lane = lax.broadcasted_iota(jnp.int32, (tq, LANES), 1)
    qf = q_ref[0].astype(F32) * (hd ** -0.5)
    heads = []
    for r in range(rep):
        blk = qf[:, (r // 2) * LANES:(r // 2 + 1) * LANES]
        if r % 2:
            blk = pltpu.roll(blk, hd, axis=1)
        heads.append(jnp.where(lane < hd, blk, qx_ref[0, r:r + 1, :]))
    qa0 = jnp.concatenate(heads, axis=0)

    row = lax.broadcasted_iota(jnp.int32, (rep * tq, LANES), 0)
    t_row = i * tq + (row & (tq - 1))
    n_col = lax.broadcasted_iota(jnp.int32, (rep * tq, LANES), 1)
    valid = t_row >= n_col * CMP_STRIDE + (CMP_LEN - 1)
    sc = jnp.where(valid, _dot_nt(qa0.astype(BF16), kc_ref[0, 0]), NEG)
    mc = jnp.max(sc, axis=1, keepdims=True)
    ec = jnp.where(valid, jnp.exp(sc - mc), 0.0)
    lc = jnp.sum(ec, axis=1, keepdims=True)
    pc = ec * jnp.where(lc > 0.0, 1.0 / lc, 0.0)
    o_cmp = _dot(pc.astype(BF16), vc_ref[0, 0])[:, :hd]
    psum = pc[0:tq] + pc[tq:2 * tq] + pc[2 * tq:3 * tq] + pc[3 * tq:4 * tq]
    score_t = _dot_nt(ovt_ref[...], psum, precision=lax.Precision.HIGHEST)

    jb = lax.broadcasted_iota(jnp.int32, (N_SLC, tq), 0)
    cur = (i * tq + lax.broadcasted_iota(jnp.int32, (N_SLC, tq), 1)) // SLC_LEN
    forced = (jb == 0) | (jb == cur) | (jb == cur - 1)
    score_t = jnp.where(forced, FORCE_SCORE, jnp.where(jb <= cur, score_t, -1.0))
    rank = jnp.zeros((N_SLC, tq), F32)
    for jp in range(N_SLC):
        other = score_t[jp:jp + 1, :]
        ge = jnp.where(other >= score_t, 1.0, 0.0)
        gt = jnp.where(other > score_t, 1.0, 0.0)
        rank = rank + jnp.where(jb > jp, ge, gt)
    bias_t = jnp.where(rank < float(SLC_TOPK), 0.0, -BIG)
    pad_t = jnp.concatenate([jnp.zeros((AUG_SEL0, tq), F32), bias_t,
                             jnp.zeros((LANES - AUG_SEL0 - N_SLC, tq), F32)], axis=0)
    sel_bias = pad_t.T
    qa = (qa0 + jnp.concatenate([sel_bias] * rep, axis=0)).astype(BF16)

    def tile_scores(c):
        start = pl.multiple_of(c * tq, tq)
        return _dot_nt(qa, ks_s[pl.ds(start, tq), :]), vs_s[pl.ds(start, tq), :]

    def online(s, v, m, acc):
        m_new = jnp.maximum(m, jnp.max(s, axis=1, keepdims=True))
        p = jnp.exp(s - m_new)
        return m_new, acc * jnp.exp(m - m_new) + _dot(p.astype(BF16), v)

    def sel_step(c, carry):
        s, v = tile_scores(c)
        return online(s, v, *carry)

    m0 = jnp.full((rep * tq, 1), -3.0e38, F32)
    acc0 = jnp.zeros((rep * tq, LANES), F32)
    m1, acc1 = lax.fori_loop(0, i, sel_step, (m0, acc0))
    s_d, v_d = tile_scores(i)
    s_d = s_d + jnp.concatenate([causal_ref[...]] * rep, axis=0)
    _, acc_s = online(s_d, v_d, m1, acc1)
    o_slc = acc_s[:, :hd] / acc_s[:, hd:hd + 1]

    w0 = pl.multiple_of(i * tq, tq)
    s_w = _dot_nt(qa, kw_s[pl.ds(w0, WIN_LEN + tq), :]) + jnp.concatenate([band_ref[...]] * rep, axis=0)
    o_win = _softmax_pv(s_w, vw_s[pl.ds(w0, WIN_LEN + tq), :])

    sg = jax.nn.sigmoid(gate_ref[0])
    outs = []
    for r in range(rep):
        rows = slice(r * tq, (r + 1) * tq)
        outs.append(sg[:, r:r + 1] * o_cmp[rows] + sg[:, rep + r:rep + r + 1] * o_slc[rows]
                    + sg[:, 2 * rep + r:2 * rep + r + 1] * o_win[rows])
    out_ref[0] = jnp.concatenate(outs, axis=1).astype(out_ref.dtype)


def _nsa(a, cmp_kv, small, consts):
    b, s, _ = a.shape
    nt = s // Q_TILE
    c = consts
    in_specs = [
        pl.BlockSpec((1, Q_TILE, 256), lambda bi, g, i: (bi, i, g)),
        pl.BlockSpec((1, s, 256), lambda bi, g, i: (bi, 0, 2 + g)),
        pl.BlockSpec((1, 1, s // CMP_STRIDE, LANES), lambda bi, g, i: (bi, g, 0, 0)),
        pl.BlockSpec((1, 1, s // CMP_STRIDE, LANES), lambda bi, g, i: (bi, 2 + g, 0, 0)),
        pl.BlockSpec((1, Q_TILE, LANES), lambda bi, g, i: (bi, i, g)),
        _const_spec(c["kx_sel"].shape), _const_spec(c["kx_win"].shape), _const_spec(c["vx_sel"].shape),
        _const_spec(c["vx_win"].shape),
        pl.BlockSpec((1, 8, LANES), lambda bi, g, i: (g, 0, 0)),
        _const_spec(c["ov_t"].shape), _const_spec(c["causal"].shape), _const_spec(c["band"].shape),
    ]
    return pl.pallas_call(
        _nsa_body,
        grid=(b, NSA_KV_GROUPS, nt),
        in_specs=in_specs,
        out_specs=pl.BlockSpec((1, Q_TILE, 256), lambda bi, g, i: (bi, i, g)),
        out_shape=jax.ShapeDtypeStruct((b, s, NSA_Q_W), BF16),
        scratch_shapes=[pltpu.VMEM((s, LANES), BF16), pltpu.VMEM((s, LANES), BF16),
                        pltpu.VMEM((s + WIN_LEN, LANES), BF16), pltpu.VMEM((s + WIN_LEN, LANES), BF16)],
        compiler_params=_cparams("parallel", "parallel", "arbitrary"),
        name="nsa",
    )(a, a, cmp_kv, cmp_kv, small, c["kx_sel"], c["kx_win"], c["vx_sel"], c["vx_win"], c["qx"], c["ov_t"],
      c["causal"], c["band"])


GDN_TS = 256
GDN_BLK = 128
GDN_HALO = 8


def _gdn_intra_body(x_ref, prev_ref, small_ref, cw_ref, alog_ref, dtb_ref, lt_ref,
                    u_ref, w_ref, qg_ref, kdt_ref, aqk_ref, eg_ref, xp_s):
    ts, dh, nh, blk = GDN_TS, GDN_HEAD_DIM, GDN_HEADS, GDN_BLK
    i = pl.program_id(1)
    xp_s[0:GDN_HALO, :] = jnp.where(i == 0, 0.0, prev_ref[0])
    xp_s[GDN_HALO:, :] = x_ref[0]
    conv = cw_ref[0:1, :] * xp_s[pl.ds(GDN_HALO - GDN_CONV + 1, ts), :]
    for j in range(1, GDN_CONV):
        conv = conv + cw_ref[j:j + 1, :] * xp_s[pl.ds(GDN_HALO - GDN_CONV + 1 + j, ts), :]
    act = jax.nn.silu(conv)

    sm = small_ref[0]
    beta = jax.nn.sigmoid(sm)
    g = -jnp.exp(alog_ref[...]) * jax.nn.softplus(sm + dtb_ref[...])
    gcum = _dot(lt_ref[...], g, precision=lax.Precision.HIGHEST)
    eg = jnp.exp(gcum)
    eg_ref[0] = eg

    ri = lax.broadcasted_iota(jnp.int32, (blk, blk), 0)
    ci = lax.broadcasted_iota(jnp.int32, (blk, blk), 1)
    same = (ri // GDN_CHUNK) == (ci // GDN_CHUNK)
    causal = same & (ri >= ci)
    strict = same & (ri > ci)
    eye = (ri == ci).astype(F32)

    for pb in range(ts // blk):
        rows = slice(pb * blk, (pb + 1) * blk)
        gc = gcum[rows]
        gc_t = gc.T
        first = lax.broadcasted_iota(jnp.int32, (blk, LANES), 0) < GDN_CHUNK
        g_last = jnp.where(first, gc[GDN_CHUNK - 1:GDN_CHUNK, :], gc[blk - 1:blk, :])
        e_dec = jnp.exp(g_last - gc)
        for h in range(nh):
            q = act[rows, h * dh:(h + 1) * dh]
            k = act[rows, GDN_W + h * dh:GDN_W + (h + 1) * dh]
            v = act[rows, 2 * GDN_W + h * dh:2 * GDN_W + (h + 1) * dh]
            q = q * lax.rsqrt(jnp.sum(q * q, axis=1, keepdims=True) + RMS_EPS) * (dh ** -0.5)
            k = k * lax.rsqrt(jnp.sum(k * k, axis=1, keepdims=True) + RMS_EPS)
            b_col = beta[rows, SM_BETA + h:SM_BETA + h + 1]
            eg_col = eg[rows, SM_DECAY + h:SM_DECAY + h + 1]
            gdiff = gc[:, SM_DECAY + h:SM_DECAY + h + 1] - gc_t[SM_DECAY + h:SM_DECAY + h + 1, :]
            decay = jnp.exp(jnp.where(causal, gdiff, NEG))
            kb = k * b_col
            kbf, kf, qf = kb.astype(BF16), k.astype(BF16), q.astype(BF16)
            a = jnp.where(strict, -_dot_nt(kbf, kf) * decay, 0.0)
            p_acc = eye + a
            x = a
            for _ in range(5):
                xb = x.astype(BF16)
                x = _dot(xb, xb)
                p_acc = p_acc + _dot(p_acc.astype(BF16), x.astype(BF16))
            tb = p_acc.astype(BF16)
            u_ref[0, h, rows, :] = _dot(tb, (v * b_col).astype(BF16))
            w_ref[0, h, rows, :] = _dot(tb, (kb * eg_col).astype(BF16)).astype(BF16)
            qg_ref[0, h, rows, :] = (q * eg_col).astype(BF16)
            aqk_ref[0, h, rows, :] = jnp.where(causal, _dot_nt(qf, kf) * decay, 0.0).astype(BF16)
            kdec = k * e_dec[:, SM_DECAY + h:SM_DECAY + h + 1]
            kdt_ref[0, h, :, rows] = kdec.T.astype(BF16)


def _gdn_intra(gqkv, small, conv_w, alog_l, dtb_l, lt):
    b, s, _ = gqkv.shape
    ts, nh, dh = GDN_TS, GDN_HEADS, GDN_HEAD_DIM
    hspec = lambda: pl.BlockSpec((1, nh, ts, dh), lambda bi, i: (bi, 0, i, 0))
    return pl.pallas_call(
        _gdn_intra_body,
        grid=(b, s // ts),
        in_specs=[
            pl.BlockSpec((1, ts, 3 * GDN_W), lambda bi, i: (bi, i, 0)),
            pl.BlockSpec((1, GDN_HALO, 3 * GDN_W), lambda bi, i: (bi, jnp.maximum(i * (ts // GDN_HALO) - 1, 0), 0)),
            pl.BlockSpec((1, ts, LANES), lambda bi, i: (bi, i, 0)),
            _const_spec(conv_w.shape), _const_spec(alog_l.shape), _const_spec(dtb_l.shape), _const_spec(lt.shape),
        ],
        out_specs=[hspec(), hspec(), hspec(),
                   pl.BlockSpec((1, nh, dh, ts), lambda bi, i: (bi, 0, 0, i)),
                   hspec(),
                   pl.BlockSpec((1, ts, LANES), lambda bi, i: (bi, i, 0))],
        out_shape=[jax.ShapeDtypeStruct((b, nh, s, dh), F32), jax.ShapeDtypeStruct((b, nh, s, dh), BF16),
                   jax.ShapeDtypeStruct((b, nh, s, dh), BF16), jax.ShapeDtypeStruct((b, nh, dh, s), BF16),
                   jax.ShapeDtypeStruct((b, nh, s, GDN_BLK), BF16), jax.ShapeDtypeStruct((b, s, LANES), F32)],
        scratch_shapes=[pltpu.VMEM((ts + GDN_HALO, 3 * GDN_W), F32)],
        compiler_params=_cparams("parallel", "parallel"),
        name="gdn_intra",
    )(gqkv, gqkv, small, conv_w, alog_l, dtb_l, lt)


def _gdn_scan_body(u_ref, w_ref, qg_ref, kdt_ref, aqk_ref, eg_ref, gate_ref, nw_ref, out_ref, st_s):
    nh, dh, ck, blk = GDN_HEADS, GDN_HEAD_DIM, GDN_CHUNK, GDN_BLK
    st_s[...] = jnp.zeros(st_s.shape, F32)
    zeros = jnp.zeros((ck, dh), BF16)

    def pair(pi, carry):
        r0 = pl.multiple_of(pi * blk, blk)
        egb = eg_ref[0, pl.ds(r0, blk), :]
        kdt = [kdt_ref[0, h, :, pl.ds(r0, blk)] for h in range(nh)]
        for half in range(2):
            rows = pl.ds(r0 + half * ck, ck)
            for h in range(nh):
                st = st_s[h]
                stb = st.astype(BF16)
                v_new = u_ref[0, h, rows, :] - _dot(w_ref[0, h, rows, :], stb)
                vb = v_new.astype(BF16)
                v_pad = jnp.concatenate([vb, zeros] if half == 0 else [zeros, vb], axis=0)
                o = _dot(qg_ref[0, h, rows, :], stb) + _dot(aqk_ref[0, h, rows, :], v_pad)
                d_last = egb[(half + 1) * ck - 1:(half + 1) * ck, SM_DECAY + h:SM_DECAY + h + 1]
                st_s[h] = st * d_last + _dot(kdt[h], v_pad)
                ms = jnp.mean(o * o, axis=1, keepdims=True)
                gt = gate_ref[0, rows, h * dh:(h + 1) * dh].astype(F32)
                out_ref[0, rows, h * dh:(h + 1) * dh] = (
                    o * lax.rsqrt(ms + RMS_EPS) * nw_ref[...] * jax.nn.silu(gt)).astype(out_ref.dtype)
        return carry

    lax.fori_loop(0, u_ref.shape[2] // blk, pair, 0)


def _gdn_scan(u, w, qg, kdt, aqk, eg, ggate, norm_w):
    b, nh, s, dh = u.shape
    hspec = lambda last: pl.BlockSpec((1, nh, s, last), lambda bi: (bi, 0, 0, 0))
    return pl.pallas_call(
        _gdn_scan_body,
        grid=(b,),
        in_specs=[hspec(dh), hspec(dh), hspec(dh),
                  pl.BlockSpec((1, nh, dh, s), lambda bi: (bi, 0, 0, 0)),
                  hspec(GDN_BLK),
                  pl.BlockSpec((1, s, LANES), lambda bi: (bi, 0, 0)),
                  pl.BlockSpec((1, s, GDN_W), lambda bi: (bi, 0, 0)),
                  _const_spec(norm_w.shape)],
        out_specs=pl.BlockSpec((1, s, GDN_W), lambda bi: (bi, 0, 0)),
        out_shape=jax.ShapeDtypeStruct((b, s, GDN_W), BF16),
        scratch_shapes=[pltpu.VMEM((nh, dh, dh), F32)],
        compiler_params=_cparams("parallel"),
        name="gdn_scan",
    )(u, w, qg, kdt, aqk, eg, ggate, norm_w)


MERGE_TM = 512


def _layer_norm(y, g, b):
    mu = jnp.mean(y, axis=1, keepdims=True)
    d = y - mu
    var = jnp.mean(d * d, axis=1, keepdims=True)
    return d * lax.rsqrt(var + LN_EPS) * g + b


def _merge_body(x_ref, oa_ref, ob_ref, ga_ref, gb_ref, wa_ref, wb_ref, wo_ref, g_ref, b_ref, y_ref, yb_ref):
    ya = _dot(oa_ref[...], wa_ref[...])
    yb = _dot(ob_ref[...], wb_ref[...])
    mixin = jax.nn.sigmoid(ga_ref[...].astype(F32)) * ya + jax.nn.sigmoid(gb_ref[...].astype(F32)) * yb
    mix = _dot(mixin.astype(BF16), wo_ref[...])
    y = _layer_norm(DEEPNORM_ALPHA * x_ref[...] + mix, g_ref[...], b_ref[...])
    y_ref[...] = y
    yb_ref[...] = y.astype(BF16)


def _merge(x2, oa, ob, mgate, wa, wb, wo, g, b):
    m = x2.shape[0]
    tm, d = MERGE_TM, D_MODEL
    row = lambda wd, col=0: pl.BlockSpec((tm, wd), lambda i, col=col: (i, col))
    return pl.pallas_call(
        _merge_body,
        grid=(m // tm,),
        in_specs=[row(d), row(NSA_Q_W), row(GDN_W), row(d, 0), row(d, 1),
                  _const_spec(wa.shape), _const_spec(wb.shape), _const_spec(wo.shape),
                  _const_spec(g.shape), _const_spec(b.shape)],
        out_specs=[row(d), row(d)],
        out_shape=[jax.ShapeDtypeStruct((m, d), F32), jax.ShapeDtypeStruct((m, d), BF16)],
        compiler_params=_cparams("parallel"),
        name="merge",
    )(x2, oa, ob, mgate, mgate, wa, wb, wo, g, b)


FFN_TM = 512
FFN_HALO = 16
FFN_CK = 256


def _ffn_body(x_ref, xb_ref, prev_ref, wg_ref, wv_ref, cg_ref, cv_ref, wd_ref, g_ref, b_ref, out_ref, acc_s,
              *, tiles_per_seq):
    tm = FFN_TM
    i = pl.program_id(0)
    prev = prev_ref[...]
    prev = jnp.where(i % tiles_per_seq == 0, jnp.zeros_like(prev), prev)
    xc = jnp.concatenate([prev, xb_ref[...]], axis=0)

    def conv(h, cw_ref, c0):
        out = cw_ref[FFN_CONV - 1:FFN_CONV, c0:c0 + FFN_CK] * h[FFN_HALO:]
        for j in range(FFN_CONV - 1):
            shifted = pltpu.roll(h, FFN_CONV - 1 - j, axis=0)[FFN_HALO:]
            out = out + cw_ref[j:j + 1, c0:c0 + FFN_CK] * shifted
        return out

    for c in range(FFN_DIM // FFN_CK):
        c0 = c * FFN_CK
        hg = conv(_dot(xc, wg_ref[:, c0:c0 + FFN_CK]), cg_ref, c0)
        hv = conv(_dot(xc, wv_ref[:, c0:c0 + FFN_CK]), cv_ref, c0)
        part = _dot((jax.nn.silu(hg) * hv).astype(BF16), wd_ref[c0:c0 + FFN_CK, :])
        if c == 0:
            acc_s[...] = part
        else:
            acc_s[...] += part
    out_ref[...] = _layer_norm(DEEPNORM_ALPHA * x_ref[...] + acc_s[...], g_ref[...], b_ref[...])


def _ffn(x1, x1b, wg, wv, cg, cv, wd, g, b, seq):
    m = x1.shape[0]
    tm, d = FFN_TM, D_MODEL
    return pl.pallas_call(
        functools.partial(_ffn_body, tiles_per_seq=seq // tm),
        grid=(m // tm,),
        in_specs=[pl.BlockSpec((tm, d), lambda i: (i, 0)),
                  pl.BlockSpec((tm, d), lambda i: (i, 0)),
                  pl.BlockSpec((FFN_HALO, d), lambda i: (jnp.maximum(i * (tm // FFN_HALO) - 1, 0), 0)),
                  _const_spec(wg.shape), _const_spec(wv.shape), _const_spec(cg.shape), _const_spec(cv.shape),
                  _const_spec(wd.shape), _const_spec(g.shape), _const_spec(b.shape)],
        out_specs=pl.BlockSpec((tm, d), lambda i: (i, 0)),
        out_shape=jax.ShapeDtypeStruct((m, d), F32),
        scratch_shapes=[pltpu.VMEM((tm, d), F32)],
        compiler_params=_cparams("parallel"),
        name="ffn",
    )(x1, x1b, x1b, wg, wv, cg, cv, wd, g, b)


def _lane_vec(vals, lane0):
    return jnp.zeros((1, LANES), F32).at[0, lane0:lane0 + vals.shape[0]].set(vals.astype(F32))


def _layer(x, w_in, cmp_pos, cmp_w1, cmp_w2, w_nsa_out, gdn_conv_w, gdn_a_log, gdn_dt_bias, gdn_norm_w,
           w_gdn_out, w_o, ln1_g, ln1_b, ffn_w_up, ffn_conv_w, ffn_w_down, ln2_g, ln2_b):
    b, s, d = x.shape
    m = b * s
    x2 = x.reshape(m, d)
    a, cmpkv, small, gqkv, ggate, mgate = _inproj(x2.astype(BF16), _regroup_w_in(w_in))

    consts = _nsa_consts(s)
    w1e, post, w2sel = _compress_weights(cmp_pos, cmp_w1, cmp_w2)
    cmp_kv = _compress(cmpkv.reshape(b, s, 256), w1e, post, w2sel, consts["cmp_aug"])
    small3 = small.reshape(b, s, 256)
    o_nsa = _nsa(a.reshape(b, s, 1024), cmp_kv, small3, consts)

    ck = GDN_CHUNK
    tri = np.tril(np.ones((ck, ck), np.float32))
    lt = jnp.asarray(np.kron(np.eye(GDN_TS // ck, dtype=np.float32), tri))
    u, w, qg, kdt, aqk, eg = _gdn_intra(gqkv.reshape(b, s, 3 * GDN_W), small3, gdn_conv_w,
                                        _lane_vec(gdn_a_log, SM_DECAY), _lane_vec(gdn_dt_bias, SM_DECAY), lt)
    o_gdn = _gdn_scan(u, w, qg, kdt, aqk, eg, ggate.reshape(b, s, GDN_W), gdn_norm_w.reshape(1, GDN_HEAD_DIM))

    x1, x1b = _merge(x2, o_nsa.reshape(m, NSA_Q_W), o_gdn.reshape(m, GDN_W), mgate,
                     w_nsa_out.astype(BF16), w_gdn_out.astype(BF16), w_o.astype(BF16),
                     ln1_g.reshape(1, d), ln1_b.reshape(1, d))
    out = _ffn(x1, x1b, ffn_w_up[:, :FFN_DIM].astype(BF16), ffn_w_up[:, FFN_DIM:].astype(BF16),
               ffn_conv_w[:, :FFN_DIM], ffn_conv_w[:, FFN_DIM:], ffn_w_down.astype(BF16),
               ln2_g.reshape(1, d), ln2_b.reshape(1, d), s)
    return out.reshape(b, s, d)


def kernel(x, w_in, nsa_cmp_pos, nsa_cmp_w1, nsa_cmp_w2, w_nsa_out, gdn_conv_w, gdn_a_log, gdn_dt_bias, gdn_norm_w, w_gdn_out, w_o, ln1_g, ln1_b, ffn_w_up, ffn_conv_w, ffn_w_down, ln2_g, ln2_b):
    for l in range(DEPTH):
        x = _layer(x, w_in[l], nsa_cmp_pos[l], nsa_cmp_w1[l], nsa_cmp_w2[l], w_nsa_out[l], gdn_conv_w[l],
                   gdn_a_log[l], gdn_dt_bias[l], gdn_norm_w[l], w_gdn_out[l], w_o[l], ln1_g[l], ln1_b[l],
                   ffn_w_up[l], ffn_conv_w[l], ffn_w_down[l], ln2_g[l], ln2_b[l])
    return x
```

```python
import functools

import numpy as np
import jax
import jax.numpy as jnp
from jax import lax
from jax.experimental import pallas as pl
from jax.experimental.pallas import tpu as pltpu

F32 = jnp.float32
BF16 = jnp.bfloat16

D_MODEL = 1024
NSA_HEADS = 8
NSA_KV_GROUPS = 2
NSA_REP = NSA_HEADS // NSA_KV_GROUPS
NSA_HEAD_DIM = 64
CMP_LEN = 32
CMP_STRIDE = 16
SLC_LEN = 64
SLC_TOPK = 8
WIN_LEN = 512
FORCE_SCORE = 1.0e4
NEG = -1.0e30
GDN_HEADS = 4
GDN_HEAD_DIM = 128
GDN_CONV = 4
GDN_CHUNK = 64
FFN_DIM = 2816
FFN_CONV = 3
DEPTH = 1
DEEPNORM_ALPHA = (2.0 * DEPTH) ** 0.25
LN_EPS = 1e-5
RMS_EPS = 1e-6

NSA_Q_W = NSA_HEADS * NSA_HEAD_DIM
NSA_KV_W = NSA_KV_GROUPS * NSA_HEAD_DIM
GDN_W = GDN_HEADS * GDN_HEAD_DIM

LANES = 128
VMEM_LIMIT_BYTES = 56 * 1024 * 1024

AUG_SEL0 = 64
AUG_POS_HI = 96
AUG_POS_LO = 97
AUG_PAD = 98
BIG = 2.0 ** 100
Q_TILE = 128
N_SLC = 32
V_ROWS = 80
SEL_KC = 512

SM_BETA = 12
SM_DECAY = 16

NT_DIMS = (((1,), (1,)), ((), ()))


def _dot(a, b, **kw):
    return jnp.dot(a, b, preferred_element_type=F32, **kw)


def _dot_nt(a, b, **kw):
    return lax.dot_general(a, b, NT_DIMS, preferred_element_type=F32, **kw)


def _cparams(*sem):
    return pltpu.CompilerParams(dimension_semantics=sem, vmem_limit_bytes=VMEM_LIMIT_BYTES)


def _const_spec(shape):
    nd = len(shape)
    return pl.BlockSpec(shape, lambda *_: (0,) * nd, pipeline_mode=pl.Buffered(1))


_INPROJ_GROUPS = (("keys", 256, BF16), ("cmp", 256, BF16), ("small", LANES, F32), ("gqkv", 3 * GDN_W, F32),
                  ("ggate", GDN_W, BF16), ("merge", 2 * D_MODEL, BF16))
_INPROJ_WIDTH = sum(w for _, w, _ in _INPROJ_GROUPS)
_INPROJ_T_ROWS = NSA_Q_W + 4 * NSA_HEAD_DIM
_GATE_T_ROWS = 32
INPROJ_TM = 512


def _regroup_w_in(w):
    o = np.cumsum((0, NSA_Q_W) + (NSA_KV_W,) * 6 + (3 * NSA_HEADS, 3 * GDN_W, GDN_HEADS, GDN_HEADS, GDN_W,
                                                   2 * D_MODEL))
    seg = lambda i: w[:, int(o[i]):int(o[i + 1])]
    q, ck, cv, sk, sv, wk, wv, gate, gqkv, beta, decay, ggate, merge = (seg(i) for i in range(13))
    hd = NSA_HEAD_DIM
    grp = lambda t, g: t[:, g * hd:(g + 1) * hd]
    keys = jnp.concatenate([grp(sk, 0), grp(wk, 0), grp(sk, 1), grp(wk, 1)], axis=1)
    cmpkv = jnp.concatenate([ck, cv], axis=1)
    zeros = lambda n: jnp.zeros((w.shape[0], n), w.dtype)
    small = jnp.concatenate([zeros(SM_BETA), beta, decay, zeros(LANES - SM_DECAY - GDN_HEADS)], axis=1)
    w_rows = jnp.concatenate([keys, cmpkv, small, gqkv, ggate, merge], axis=1).astype(BF16)
    vals = jnp.concatenate([grp(sv, 0), grp(wv, 0), grp(sv, 1), grp(wv, 1)], axis=1)
    w_t = jnp.concatenate([q, vals], axis=1).T.astype(BF16)
    gate = gate.reshape(-1, 3, NSA_KV_GROUPS, NSA_REP)
    gpad = zeros(_GATE_T_ROWS // 2 - 3 * NSA_REP)
    w_g = jnp.concatenate([t for g in range(NSA_KV_GROUPS)
                           for t in (gate[:, :, g, :].reshape(-1, 3 * NSA_REP), gpad)], axis=1).T.astype(BF16)
    return w_rows, w_t, w_g


def _inproj_body(x_ref, w_ref, wt_ref, wg_ref, keys_ref, cmp_ref, small_ref, gqkv_ref, ggate_ref, merge_ref,
                 qvt_ref, gt_ref):
    x = x_ref[...]
    outs = (keys_ref, cmp_ref, small_ref, gqkv_ref, ggate_ref, merge_ref)
    c0 = 0
    for ref, (_, width, _) in zip(outs, _INPROJ_GROUPS):
        for s in range(0, width, 512):
            e = min(s + 512, width)
            ref[:, s:e] = _dot(x, w_ref[:, c0 + s:c0 + e]).astype(ref.dtype)
        c0 += width
    for s in range(0, _INPROJ_T_ROWS, 256):
        qvt_ref[s:s + 256, :] = _dot_nt(wt_ref[s:s + 256, :], x).astype(qvt_ref.dtype)
    gt_ref[...] = _dot_nt(wg_ref[...], x)


def _inproj(xb, w_rows, w_t, w_g):
    m = xb.shape[0]
    tm = INPROJ_TM
    return pl.pallas_call(
        _inproj_body,
        grid=(m // tm,),
        in_specs=[pl.BlockSpec((tm, D_MODEL), lambda i: (i, 0)), _const_spec(w_rows.shape),
                  _const_spec(w_t.shape), _const_spec(w_g.shape)],
        out_specs=[pl.BlockSpec((tm, wd), lambda i: (i, 0)) for _, wd, _ in _INPROJ_GROUPS]
        + [pl.BlockSpec((_INPROJ_T_ROWS, tm), lambda i: (0, i)), pl.BlockSpec((_GATE_T_ROWS, tm), lambda i: (0, i))],
        out_shape=[jax.ShapeDtypeStruct((m, wd), dt) for _, wd, dt in _INPROJ_GROUPS]
        + [jax.ShapeDtypeStruct((_INPROJ_T_ROWS, m), BF16), jax.ShapeDtypeStruct((_GATE_T_ROWS, m), F32)],
        compiler_params=_cparams("parallel"),
        name="inproj",
    )(xb, w_rows, w_t, w_g)


def _compress_weights(cmp_pos, cmp_w1, cmp_w2):
    hd, half = NSA_HEAD_DIM, CMP_LEN // 2
    w1r = cmp_w1.reshape(2, 2, half, hd, hd)
    eye = jnp.eye(2, dtype=cmp_w1.dtype)
    w1e = jnp.einsum("whlde,wv,gk->lwgdhvke", w1r, eye, eye).reshape(half * 4 * hd, 2 * 4 * hd)
    posr = cmp_pos.reshape(2, 2, half, hd)
    post = jnp.broadcast_to(posr.transpose(1, 2, 0, 3)[:, :, :, None, :], (2, half, 2, 2, hd))
    post = jnp.concatenate([post.reshape(2, half * 4 * hd), jnp.zeros((6, half * 4 * hd), cmp_pos.dtype)], axis=0)
    w2sel = jnp.zeros((2, 2, 2 * hd, LANES), cmp_w2.dtype)
    for g in range(2):
        w2sel = w2sel.at[:, g, g * hd:(g + 1) * hd, :hd].set(cmp_w2)
    return w1e.astype(BF16), post.astype(BF16), w2sel.reshape(4, 2 * hd, LANES).astype(BF16)


def _compress_body(t_ref, w1_ref, pos_ref, w2_ref, aug_ref, out_ref):
    p = _dot(t_ref[0], w1_ref[...])
    pp = _dot(pos_ref[...], w1_ref[...])
    nxt = pltpu.roll(p[:, 256:], p.shape[0] - 1, axis=0)
    pre = p[:, :256] + nxt + pp[0:1, :256] + pp[1:2, 256:]
    h = jax.nn.gelu(pre).astype(BF16)
    n_idx = lax.broadcasted_iota(jnp.int32, (p.shape[0], LANES), 0)
    real = n_idx < p.shape[0] - 1
    for which in range(2):
        hw = h[:, which * LANES:(which + 1) * LANES]
        for g in range(2):
            o = jnp.where(real, _dot(hw, w2_ref[which * 2 + g]) + aug_ref[which], 0.0)
            out_ref[0, which * 2 + g] = (o if which == 0 else o.T).astype(out_ref.dtype)


def _compress(cmpkv, w1e, post, w2sel, aug):
    b, s, _ = cmpkv.shape
    nblk = s // CMP_STRIDE
    t2 = cmpkv.reshape(b, nblk, CMP_STRIDE * 256)
    return pl.pallas_call(
        _compress_body,
        grid=(b,),
        in_specs=[pl.BlockSpec((1, nblk, CMP_STRIDE * 256), lambda i: (i, 0, 0)),
                  _const_spec(w1e.shape), _const_spec(post.shape), _const_spec(w2sel.shape), _const_spec(aug.shape)],
        out_specs=pl.BlockSpec((1, 4, nblk, LANES), lambda i: (i, 0, 0, 0)),
        out_shape=jax.ShapeDtypeStruct((b, 4, nblk, LANES), BF16),
        compiler_params=_cparams("parallel"),
        name="compress",
    )(t2, w1e, post, w2sel, aug)


def _nsa_consts(s):
    hd, rep = NSA_HEAD_DIM, NSA_REP
    t = np.arange(s)
    kx_sel = np.zeros((s, hd), np.float32)
    kx_sel[t, t // SLC_LEN] = 1.0
    kx_sel[:, AUG_POS_HI - hd] = t // 256
    kx_sel[:, AUG_POS_LO - hd] = t % 256
    kx_win = np.zeros((s + WIN_LEN, hd), np.float32)
    kx_win[WIN_LEN + t, AUG_POS_HI - hd] = t // 256
    kx_win[WIN_LEN + t, AUG_POS_LO - hd] = t % 256
    kx_win[:WIN_LEN, AUG_PAD - hd] = 1.0
    vx_sel = np.zeros((V_ROWS - hd, s), np.float32)
    vx_sel[0] = 1.0
    vx_win = np.zeros((V_ROWS - hd, s + WIN_LEN), np.float32)
    vx_win[0, WIN_LEN:] = 1.0
    n_cmp = s // CMP_STRIDE
    cmp_aug = np.zeros((2, n_cmp, LANES), np.float32)
    end = np.arange(n_cmp) * CMP_STRIDE + CMP_LEN - 1
    cmp_aug[0, :, AUG_POS_HI] = end // 256
    cmp_aug[0, :, AUG_POS_LO] = end % 256
    qx = np.zeros((NSA_KV_GROUPS, LANES - AUG_POS_HI, rep * Q_TILE), np.float32)
    for h in range(NSA_HEADS):
        slope = 2.0 ** (-8.0 * (h + 1) / NSA_HEADS)
        lanes = slice((h % rep) * Q_TILE, (h % rep + 1) * Q_TILE)
        qx[h // rep, 0, lanes] = slope * 256.0
        qx[h // rep, 1, lanes] = slope
        qx[h // rep, AUG_PAD - AUG_POS_HI, lanes] = -BIG
    c0 = np.arange(n_cmp)[None, :] * CMP_STRIDE
    s0 = np.arange(s // SLC_LEN)[:, None] * SLC_LEN
    ov_t = ((c0 < s0 + SLC_LEN) & (c0 + CMP_LEN > s0)).astype(np.float32)
    ov_t[:, (s - CMP_LEN) // CMP_STRIDE + 1:] = 0.0
    kk = np.arange(Q_TILE)[:, None]
    qq = np.arange(Q_TILE)[None, :]
    causal = np.tile(np.where(kk <= qq, 0.0, NEG).astype(np.float32), (1, rep))
    after = np.tile(np.where(kk > qq, 0.0, NEG).astype(np.float32), (1, rep))
    j = jnp.asarray
    return dict(kx_sel=j(kx_sel, BF16), kx_win=j(kx_win, BF16), vx_sel=j(vx_sel, BF16), vx_win=j(vx_win, BF16),
                cmp_aug=j(cmp_aug), qx=j(qx), ov_t=j(ov_t), causal=j(causal), after=j(after))


def _nsa_body(qt_ref, k_ref, vt_ref, kc_ref, vct_ref, gt_ref, kxs_ref, kxw_ref, vxs_ref, vxw_ref, qx_ref, ovt_ref,
              causal_ref, after_ref, out_ref, ks_s, kw_s, vs_s, vw_s):
    hd, rep, tq = NSA_HEAD_DIM, NSA_REP, Q_TILE
    nq = rep * tq
    i = pl.program_id(2)

    @pl.when(i == 0)
    def _():
        keys = k_ref[0]
        ks_s[:, :hd] = keys[:, :hd]
        ks_s[:, hd:] = kxs_ref[...]
        kw_s[:WIN_LEN, :hd] = jnp.zeros((WIN_LEN, hd), BF16)
        kw_s[WIN_LEN:, :hd] = keys[:, hd:]
        kw_s[:, hd:] = kxw_ref[...]
        vals = vt_ref[...]
        vs_s[:hd, :] = vals[:hd]
        vs_s[hd:, :] = vxs_ref[...]
        vw_s[:hd, :WIN_LEN] = jnp.zeros((hd, WIN_LEN), BF16)
        vw_s[:hd, WIN_LEN:] = vals[hd:]
        vw_s[hd:, :] = vxw_ref[...]

    qt = qt_ref[...]
    q64 = jnp.concatenate([qt[r * hd:(r + 1) * hd, :] for r in range(rep)], axis=1).astype(F32) * (hd ** -0.5)
    qx = qx_ref[0]

    def q_aug(sel_rows):
        return jnp.concatenate([q64, sel_rows, qx], axis=0).astype(BF16)

    n_row = lax.broadcasted_iota(jnp.int32, (LANES, nq), 0)
    t_lane = i * tq + (lax.broadcasted_iota(jnp.int32, (LANES, nq), 1) & (tq - 1))
    valid = t_lane >= n_row * CMP_STRIDE + (CMP_LEN - 1)
    sc = jnp.where(valid, _dot(kc_ref[0, 0], q_aug(jnp.zeros((N_SLC, nq), F32))), NEG)
    mc = jnp.max(sc, axis=0, keepdims=True)
    ec = jnp.where(valid, jnp.exp(sc - mc), 0.0)
    lc = jnp.sum(ec, axis=0, keepdims=True)
    pc = ec * jnp.where(lc > 0.0, 1.0 / lc, 0.0)
    o_cmp = _dot(vct_ref[0, 0], pc.astype(BF16))[:hd]
    psum = pc[:, 0:tq] + pc[:, tq:2 * tq] + pc[:, 2 * tq:3 * tq] + pc[:, 3 * tq:4 * tq]
    score_t = _dot(ovt_ref[...], psum, precision=lax.Precision.HIGHEST)

    jb = lax.broadcasted_iota(jnp.int32, (N_SLC, tq), 0)
    cur = (i * tq + lax.broadcasted_iota(jnp.int32, (N_SLC, tq), 1)) // SLC_LEN
    forced = (jb == 0) | (jb == cur) | (jb == cur - 1)
    score_t = jnp.where(forced, FORCE_SCORE, jnp.where(jb <= cur, score_t, -1.0))
    rank = jnp.zeros((N_SLC, tq), F32)
    for jp in range(N_SLC):
        other = score_t[jp:jp + 1, :]
        ge = jnp.where(other >= score_t, 1.0, 0.0)
        gt = jnp.where(other > score_t, 1.0, 0.0)
        rank = rank + jnp.where(jb > jp, ge, gt)
    bias_t = jnp.where(rank < float(SLC_TOPK), 0.0, -BIG)
    qa = q_aug(jnp.concatenate([bias_t] * rep, axis=1))

    def chunk_scores(j):
        start = pl.multiple_of(j * SEL_KC, SEL_KC)
        return _dot(ks_s[pl.ds(start, SEL_KC), :], qa), vs_s[:, pl.ds(start, SEL_KC)]

    def online(s, v, m, acc):
        m_new = jnp.maximum(m, jnp.max(s, axis=0, keepdims=True))
        p = jnp.exp(s - m_new)
        return m_new, acc * jnp.exp(m - m_new) + _dot(v, p.astype(BF16))

    def sel_step(j, carry):
        s, v = chunk_scores(j)
        return online(s, v, *carry)

    m0 = jnp.full((1, nq), -3.0e38, F32)
    acc0 = jnp.zeros((V_ROWS, nq), F32)
    n_full = (i * tq) // SEL_KC
    m1, acc1 = lax.fori_loop(0, n_full, sel_step, (m0, acc0))
    s_d, v_d = chunk_scores(n_full)
    k_abs = n_full * SEL_KC + lax.broadcasted_iota(jnp.int32, (SEL_KC, nq), 0)
    t_abs = i * tq + (lax.broadcasted_iota(jnp.int32, (SEL_KC, nq), 1) & (tq - 1))
    _, acc_s = online(jnp.where(k_abs <= t_abs, s_d, NEG), v_d, m1, acc1)
    o_slc = acc_s[:hd] * (1.0 / acc_s[hd:hd + 1])

    w0 = pl.multiple_of(i * tq, tq)
    s_w = _dot(kw_s[pl.ds(w0, WIN_LEN + tq), :], qa)
    s_w = jnp.concatenate([s_w[:tq] + after_ref[...], s_w[tq:WIN_LEN], s_w[WIN_LEN:] + causal_ref[...]], axis=0)
    p_w = jnp.exp(s_w - jnp.max(s_w, axis=0, keepdims=True))
    acc_w = _dot(vw_s[:, pl.ds(w0, WIN_LEN + tq)], p_w.astype(BF16))
    o_win = acc_w[:hd] * (1.0 / acc_w[hd:hd + 1])

    sg = jax.nn.sigmoid(gt_ref[...])
    for pair in range(rep // 2):
        halves = []
        for r in (2 * pair, 2 * pair + 1):
            lanes = slice(r * tq, (r + 1) * tq)
            halves.append(sg[r:r + 1] * o_cmp[:, lanes] + sg[rep + r:rep + r + 1] * o_slc[:, lanes]
                          + sg[2 * rep + r:2 * rep + r + 1] * o_win[:, lanes])
        out_ref[0, :, pair * LANES:(pair + 1) * LANES] = jnp.concatenate(halves, axis=0).T.astype(out_ref.dtype)


def _nsa(qvt, keys, cmp_kv, gt, consts, b, s):
    nt = s // Q_TILE
    c = consts
    in_specs = [
        pl.BlockSpec((2 * LANES, Q_TILE), lambda bi, g, i: (g, bi * nt + i)),
        pl.BlockSpec((1, s, LANES), lambda bi, g, i: (bi, 0, g)),
        pl.BlockSpec((LANES, s), lambda bi, g, i: (NSA_Q_W // LANES + g, bi)),
        pl.BlockSpec((1, 1, s // CMP_STRIDE, LANES), lambda bi, g, i: (bi, g, 0, 0)),
        pl.BlockSpec((1, 1, s // CMP_STRIDE, LANES), lambda bi, g, i: (bi, 2 + g, 0, 0)),
        pl.BlockSpec((_GATE_T_ROWS // 2, Q_TILE), lambda bi, g, i: (g, bi * nt + i)),
        _const_spec(c["kx_sel"].shape), _const_spec(c["kx_win"].shape), _const_spec(c["vx_sel"].shape),
        _const_spec(c["vx_win"].shape),
        pl.BlockSpec((1,) + c["qx"].shape[1:], lambda bi, g, i: (g, 0, 0)),
        _const_spec(c["ov_t"].shape), _const_spec(c["causal"].shape), _const_spec(c["after"].shape),
    ]
    return pl.pallas_call(
        _nsa_body,
        grid=(b, NSA_KV_GROUPS, nt),
        in_specs=in_specs,
        out_specs=pl.BlockSpec((1, Q_TILE, 2 * LANES), lambda bi, g, i: (bi, i, g)),
        out_shape=jax.ShapeDtypeStruct((b, s, NSA_Q_W), BF16),
        scratch_shapes=[pltpu.VMEM((s, LANES), BF16), pltpu.VMEM((s + WIN_LEN, LANES), BF16),
                        pltpu.VMEM((V_ROWS, s), BF16), pltpu.VMEM((V_ROWS, s + WIN_LEN), BF16)],
        compiler_params=_cparams("parallel", "parallel", "arbitrary"),
        name="nsa",
    )(qvt, keys, qvt, cmp_kv, cmp_kv, gt, c["kx_sel"], c["kx_win"], c["vx_sel"], c["vx_win"], c["qx"], c["ov_t"],
      c["causal"], c["after"])


GDN_TS = 256
GDN_BLK = 128
GDN_HALO = 8


def _gdn_intra_body(x_ref, prev_ref, small_ref, cw_ref, alog_ref, dtb_ref, lt_ref,
                    u_ref, w_ref, qg_ref, kdt_ref, aqk_ref, eg_ref, xp_s):
    ts, dh, nh, blk = GDN_TS, GDN_HEAD_DIM, GDN_HEADS, GDN_BLK
    i = pl.program_id(1)
    xp_s[0:GDN_HALO, :] = jnp.where(i == 0, 0.0, prev_ref[0])
    xp_s[GDN_HALO:, :] = x_ref[0]
    conv = cw_ref[0:1, :] * xp_s[pl.ds(GDN_HALO - GDN_CONV + 1, ts), :]
    for j in range(1, GDN_CONV):
        conv = conv + cw_ref[j:j + 1, :] * xp_s[pl.ds(GDN_HALO - GDN_CONV + 1 + j, ts), :]
    act = jax.nn.silu(conv)

    sm = small_ref[0]
    beta = jax.nn.sigmoid(sm)
    g = -jnp.exp(alog_ref[...]) * jax.nn.softplus(sm + dtb_ref[...])
    gcum = _dot(lt_ref[...], g, precision=lax.Precision.HIGHEST)
    eg = jnp.exp(gcum)
    eg_ref[0] = eg

    ri = lax.broadcasted_iota(jnp.int32, (blk, blk), 0)
    ci = lax.broadcasted_iota(jnp.int32, (blk, blk), 1)
    same = (ri // GDN_CHUNK) == (ci // GDN_CHUNK)
    causal = same & (ri >= ci)
    strict = same & (ri > ci)
    eye = (ri == ci).astype(F32)

    for pb in range(ts // blk):
        rows = slice(pb * blk, (pb + 1) * blk)
        gc = gcum[rows]
        gc_t = gc.T
        first = lax.broadcasted_iota(jnp.int32, (blk, LANES), 0) < GDN_CHUNK
        g_last = jnp.where(first, gc[GDN_CHUNK - 1:GDN_CHUNK, :], gc[blk - 1:blk, :])
        e_dec = jnp.exp(g_last - gc)
        for h in range(nh):
            q = act[rows, h * dh:(h + 1) * dh]
            k = act[rows, GDN_W + h * dh:GDN_W + (h + 1) * dh]
            v = act[rows, 2 * GDN_W + h * dh:2 * GDN_W + (h + 1) * dh]
            q = q * lax.rsqrt(jnp.sum(q * q, axis=1, keepdims=True) + RMS_EPS) * (dh ** -0.5)
            k = k * lax.rsqrt(jnp.sum(k * k, axis=1, keepdims=True) + RMS_EPS)
            b_col = beta[rows, SM_BETA + h:SM_BETA + h + 1]
            eg_col = eg[rows, SM_DECAY + h:SM_DECAY + h + 1]
            gdiff = gc[:, SM_DECAY + h:SM_DECAY + h + 1] - gc_t[SM_DECAY + h:SM_DECAY + h + 1, :]
            decay = jnp.exp(jnp.where(causal, gdiff, NEG))
            kb = k * b_col
            kbf, kf, qf = kb.astype(BF16), k.astype(BF16), q.astype(BF16)
            a = jnp.where(strict, -_dot_nt(kbf, kf) * decay, 0.0)
            p_acc = eye + a
            x = a
            for _ in range(5):
                xb = x.astype(BF16)
                x = _dot(xb, xb)
                p_acc = p_acc + _dot(p_acc.astype(BF16), x.astype(BF16))
            tb = p_acc.astype(BF16)
            u_ref[0, h, rows, :] = _dot(tb, (v * b_col).astype(BF16))
            w_ref[0, h, rows, :] = _dot(tb, (kb * eg_col).astype(BF16)).astype(BF16)
            qg_ref[0, h, rows, :] = (q * eg_col).astype(BF16)
            aqk_ref[0, h, rows, :] = jnp.where(causal, _dot_nt(qf, kf) * decay, 0.0).astype(BF16)
            kdec = k * e_dec[:, SM_DECAY + h:SM_DECAY + h + 1]
            kdt_ref[0, h, :, rows] = kdec.T.astype(BF16)


def _gdn_intra(gqkv, small, conv_w, alog_l, dtb_l, lt):
    b, s, _ = gqkv.shape
    ts, nh, dh = GDN_TS, GDN_HEADS, GDN_HEAD_DIM
    hspec = lambda: pl.BlockSpec((1, nh, ts, dh), lambda bi, i: (bi, 0, i, 0))
    return pl.pallas_call(
        _gdn_intra_body,
        grid=(b, s // ts),
        in_specs=[
            pl.BlockSpec((1, ts, 3 * GDN_W), lambda bi, i: (bi, i, 0)),
            pl.BlockSpec((1, GDN_HALO, 3 * GDN_W), lambda bi, i: (bi, jnp.maximum(i * (ts // GDN_HALO) - 1, 0), 0)),
            pl.BlockSpec((1, ts, LANES), lambda bi, i: (bi, i, 0)),
            _const_spec(conv_w.shape), _const_spec(alog_l.shape), _const_spec(dtb_l.shape), _const_spec(lt.shape),
        ],
        out_specs=[hspec(), hspec(), hspec(),
                   pl.BlockSpec((1, nh, dh, ts), lambda bi, i: (bi, 0, 0, i)),
                   hspec(),
                   pl.BlockSpec((1, ts, LANES), lambda bi, i: (bi, i, 0))],
        out_shape=[jax.ShapeDtypeStruct((b, nh, s, dh), F32), jax.ShapeDtypeStruct((b, nh, s, dh), BF16),
                   jax.ShapeDtypeStruct((b, nh, s, dh), BF16), jax.ShapeDtypeStruct((b, nh, dh, s), BF16),
                   jax.ShapeDtypeStruct((b, nh, s, GDN_BLK), BF16), jax.ShapeDtypeStruct((b, s, LANES), F32)],
        scratch_shapes=[pltpu.VMEM((ts + GDN_HALO, 3 * GDN_W), F32)],
        compiler_params=_cparams("parallel", "parallel"),
        name="gdn_intra",
    )(gqkv, gqkv, small, conv_w, alog_l, dtb_l, lt)


def _gdn_scan_body(u_ref, w_ref, qg_ref, kdt_ref, aqk_ref, eg_ref, gate_ref, nw_ref, out_ref, st_s):
    nh, dh, ck, blk = GDN_HEADS, GDN_HEAD_DIM, GDN_CHUNK, GDN_BLK
    st_s[...] = jnp.zeros(st_s.shape, F32)
    zeros = jnp.zeros((ck, dh), BF16)

    def pair(pi, carry):
        r0 = pl.multiple_of(pi * blk, blk)
        egb = eg_ref[0, pl.ds(r0, blk), :]
        kdt = [kdt_ref[0, h, :, pl.ds(r0, blk)] for h in range(nh)]
        for half in range(2):
            rows = pl.ds(r0 + half * ck, ck)
            for h in range(nh):
                st = st_s[h]
                stb = st.astype(BF16)
                v_new = u_ref[0, h, rows, :] - _dot(w_ref[0, h, rows, :], stb)
                vb = v_new.astype(BF16)
                v_pad = jnp.concatenate([vb, zeros] if half == 0 else [zeros, vb], axis=0)
                o = _dot(qg_ref[0, h, rows, :], stb) + _dot(aqk_ref[0, h, rows, :], v_pad)
                d_last = egb[(half + 1) * ck - 1:(half + 1) * ck, SM_DECAY + h:SM_DECAY + h + 1]
                st_s[h] = st * d_last + _dot(kdt[h], v_pad)
                ms = jnp.mean(o * o, axis=1, keepdims=True)
                gt = gate_ref[0, rows, h * dh:(h + 1) * dh].astype(F32)
                out_ref[0, rows, h * dh:(h + 1) * dh] = (
                    o * lax.rsqrt(ms + RMS_EPS) * nw_ref[...] * jax.nn.silu(gt)).astype(out_ref.dtype)
        return carry

    lax.fori_loop(0, u_ref.shape[2] // blk, pair, 0)


def _gdn_scan(u, w, qg, kdt, aqk, eg, ggate, norm_w):
    b, nh, s, dh = u.shape
    hspec = lambda last: pl.BlockSpec((1, nh, s, last), lambda bi: (bi, 0, 0, 0))
    return pl.pallas_call(
        _gdn_scan_body,
        grid=(b,),
        in_specs=[hspec(dh), hspec(dh), hspec(dh),
                  pl.BlockSpec((1, nh, dh, s), lambda bi: (bi, 0, 0, 0)),
                  hspec(GDN_BLK),
                  pl.BlockSpec((1, s, LANES), lambda bi: (bi, 0, 0)),
                  pl.BlockSpec((1, s, GDN_W), lambda bi: (bi, 0, 0)),
                  _const_spec(norm_w.shape)],
        out_specs=pl.BlockSpec((1, s, GDN_W), lambda bi: (bi, 0, 0)),
        out_shape=jax.ShapeDtypeStruct((b, s, GDN_W), BF16),
        scratch_shapes=[pltpu.VMEM((nh, dh, dh), F32)],
        compiler_params=_cparams("parallel"),
        name="gdn_scan",
    )(u, w, qg, kdt, aqk, eg, ggate, norm_w)


MERGE_TM = 512


def _layer_norm(y, g, b):
    mu = jnp.mean(y, axis=1, keepdims=True)
    d = y - mu
    var = jnp.mean(d * d, axis=1, keepdims=True)
    return d * lax.rsqrt(var + LN_EPS) * g + b


def _merge_body(x_ref, oa_ref, ob_ref, ga_ref, gb_ref, wa_ref, wb_ref, wo_ref, g_ref, b_ref, y_ref, yb_ref):
    ya = _dot(oa_ref[...], wa_ref[...])
    yb = _dot(ob_ref[...], wb_ref[...])
    mixin = jax.nn.sigmoid(ga_ref[...].astype(F32)) * ya + jax.nn.sigmoid(gb_ref[...].astype(F32)) * yb
    mix = _dot(mixin.astype(BF16), wo_ref[...])
    y = _layer_norm(DEEPNORM_ALPHA * x_ref[...] + mix, g_ref[...], b_ref[...])
    y_ref[...] = y
    yb_ref[...] = y.astype(BF16)


def _merge(x2, oa, ob, mgate, wa, wb, wo, g, b):
    m = x2.shape[0]
    tm, d = MERGE_TM, D_MODEL
    row = lambda wd, col=0: pl.BlockSpec((tm, wd), lambda i, col=col: (i, col))
    return pl.pallas_call(
        _merge_body,
        grid=(m // tm,),
        in_specs=[row(d), row(NSA_Q_W), row(GDN_W), row(d, 0), row(d, 1),
                  _const_spec(wa.shape), _const_spec(wb.shape), _const_spec(wo.shape),
                  _const_spec(g.shape), _const_spec(b.shape)],
        out_specs=[row(d), row(d)],
        out_shape=[jax.ShapeDtypeStruct((m, d), F32), jax.ShapeDtypeStruct((m, d), BF16)],
        compiler_params=_cparams("parallel"),
        name="merge",
    )(x2, oa, ob, mgate, mgate, wa, wb, wo, g, b)


FFN_TM = 512
FFN_HALO = 16
FFN_CK = 256


def _ffn_body(x_ref, xb_ref, prev_ref, wg_ref, wv_ref, cg_ref, cv_ref, wd_ref, g_ref, b_ref, out_ref, acc_s,
              *, tiles_per_seq):
    i = pl.program_id(0)
    prev = prev_ref[...]
    prev = jnp.where(i % tiles_per_seq == 0, jnp.zeros_like(prev), prev)
    xc = jnp.concatenate([prev, xb_ref[...]], axis=0)

    def conv(h, cw_ref, c0):
        out = cw_ref[FFN_CONV - 1:FFN_CONV, c0:c0 + FFN_CK] * h[FFN_HALO:]
        for j in range(FFN_CONV - 1):
            shifted = pltpu.roll(h, FFN_CONV - 1 - j, axis=0)[FFN_HALO:]
            out = out + cw_ref[j:j + 1, c0:c0 + FFN_CK] * shifted
        return out

    for c in range(FFN_DIM // FFN_CK):
        c0 = c * FFN_CK
        hg = conv(_dot(xc, wg_ref[:, c0:c0 + FFN_CK]), cg_ref, c0)
        hv = conv(_dot(xc, wv_ref[:, c0:c0 + FFN_CK]), cv_ref, c0)
        part = _dot((jax.nn.silu(hg) * hv).astype(BF16), wd_ref[c0:c0 + FFN_CK, :])
        if c == 0:
            acc_s[...] = part
        else:
            acc_s[...] += part
    out_ref[...] = _layer_norm(DEEPNORM_ALPHA * x_ref[...] + acc_s[...], g_ref[...], b_ref[...])


def _ffn(x1, x1b, wg, wv, cg, cv, wd, g, b, seq):
    m = x1.shape[0]
    tm, d = FFN_TM, D_MODEL
    return pl.pallas_call(
        functools.partial(_ffn_body, tiles_per_seq=seq // tm),
        grid=(m // tm,),
        in_specs=[pl.BlockSpec((tm, d), lambda i: (i, 0)),
                  pl.BlockSpec((tm, d), lambda i: (i, 0)),
                  pl.BlockSpec((FFN_HALO, d), lambda i: (jnp.maximum(i * (tm // FFN_HALO) - 1, 0), 0)),
                  _const_spec(wg.shape), _const_spec(wv.shape), _const_spec(cg.shape), _const_spec(cv.shape),
                  _const_spec(wd.shape), _const_spec(g.shape), _const_spec(b.shape)],
        out_specs=pl.BlockSpec((tm, d), lambda i: (i, 0)),
        out_shape=jax.ShapeDtypeStruct((m, d), F32),
        scratch_shapes=[pltpu.VMEM((tm, d), F32)],
        compiler_params=_cparams("parallel"),
        name="ffn",
    )(x1, x1b, x1b, wg, wv, cg, cv, wd, g, b)


def _lane_vec(vals, lane0):
    return jnp.zeros((1, LANES), F32).at[0, lane0:lane0 + vals.shape[0]].set(vals.astype(F32))


def _layer(x, w_in, cmp_pos, cmp_w1, cmp_w2, w_nsa_out, gdn_conv_w, gdn_a_log, gdn_dt_bias, gdn_norm_w,
           w_gdn_out, w_o, ln1_g, ln1_b, ffn_w_up, ffn_conv_w, ffn_w_down, ln2_g, ln2_b):
    b, s, d = x.shape
    m = b * s
    x2 = x.reshape(m, d)
    keys, cmpkv, small, gqkv, ggate, mgate, qvt, gt = _inproj(x2.astype(BF16), *_regroup_w_in(w_in))

    consts = _nsa_consts(s)
    w1e, post, w2sel = _compress_weights(cmp_pos, cmp_w1, cmp_w2)
    cmp_kv = _compress(cmpkv.reshape(b, s, 256), w1e, post, w2sel, consts["cmp_aug"])
    o_nsa = _nsa(qvt, keys.reshape(b, s, 256), cmp_kv, gt, consts, b, s)

    ck = GDN_CHUNK
    tri = np.tril(np.ones((ck, ck), np.float32))
    lt = jnp.asarray(np.kron(np.eye(GDN_TS // ck, dtype=np.float32), tri))
    u, w, qg, kdt, aqk, eg = _gdn_intra(gqkv.reshape(b, s, 3 * GDN_W), small.reshape(b, s, LANES), gdn_conv_w,
                                        _lane_vec(gdn_a_log, SM_DECAY), _lane_vec(gdn_dt_bias, SM_DECAY), lt)
    o_gdn = _gdn_scan(u, w, qg, kdt, aqk, eg, ggate.reshape(b, s, GDN_W), gdn_norm_w.reshape(1, GDN_HEAD_DIM))

    x1, x1b = _merge(x2, o_nsa.reshape(m, NSA_Q_W), o_gdn.reshape(m, GDN_W), mgate,
                     w_nsa_out.astype(BF16), w_gdn_out.astype(BF16), w_o.astype(BF16),
                     ln1_g.reshape(1, d), ln1_b.reshape(1, d))
    out = _ffn(x1, x1b, ffn_w_up[:, :FFN_DIM].astype(BF16), ffn_w_up[:, FFN_DIM:].astype(BF16),
               ffn_conv_w[:, :FFN_DIM], ffn_conv_w[:, FFN_DIM:], ffn_w_down.astype(BF16),
               ln2_g.reshape(1, d), ln2_b.reshape(1, d), s)
    return out.reshape(b, s, d)


def kernel(x, w_in, nsa_cmp_pos, nsa_cmp_w1, nsa_cmp_w2, w_nsa_out, gdn_conv_w, gdn_a_log, gdn_dt_bias, gdn_norm_w, w_gdn_out, w_o, ln1_g, ln1_b, ffn_w_up, ffn_conv_w, ffn_w_down, ln2_g, ln2_b):
    for l in range(DEPTH):
        x = _layer(x, w_in[l], nsa_cmp_pos[l], nsa_cmp_w1[l], nsa_cmp_w2[l], w_nsa_out[l], gdn_conv_w[l],
                   gdn_a_log[l], gdn_dt_bias[l], gdn_norm_w[l], w_gdn_out[l], w_o[l], ln1_g[l], ln1_b[l],
                   ffn_w_up[l], ffn_conv_w[l], ffn_w_down[l], ln2_g[l], ln2_b[l])
    return x
```

```python
import functools

import numpy as np
import jax
import jax.numpy as jnp
from jax import lax
from jax.experimental import pallas as pl
from jax.experimental.pallas import tpu as pltpu

F32 = jnp.float32
BF16 = jnp.bfloat16

D_MODEL = 1024
NSA_HEADS = 8
NSA_KV_GROUPS = 2
NSA_REP = NSA_HEADS // NSA_KV_GROUPS
NSA_HEAD_DIM = 64
CMP_LEN = 32
CMP_STRIDE = 16
SLC_LEN = 64
SLC_TOPK = 8
WIN_LEN = 512
FORCE_SCORE = 1.0e4
NEG = -1.0e30
GDN_HEADS = 4
GDN_HEAD_DIM = 128
GDN_CONV = 4
GDN_CHUNK = 64
FFN_DIM = 2816
FFN_CONV = 3
DEPTH = 1
DEEPNORM_ALPHA = (2.0 * DEPTH) ** 0.25
LN_EPS = 1e-5
RMS_EPS = 1e-6

NSA_Q_W = NSA_HEADS * NSA_HEAD_DIM
NSA_KV_W = NSA_KV_GROUPS * NSA_HEAD_DIM
GDN_W = GDN_HEADS * GDN_HEAD_DIM

LANES = 128
VMEM_LIMIT_BYTES = 56 * 1024 * 1024

AUG_SEL0 = 64
AUG_POS_HI = 96
AUG_POS_LO = 97
AUG_PAD = 98
BIG = 2.0 ** 100
Q_TILE = 128
N_SLC = 32
V_ROWS = 80
SEL_KC = 512

SM_BETA = 12
SM_DECAY = 16

NT_DIMS = (((1,), (1,)), ((), ()))


def _dot(a, b, **kw):
    return jnp.dot(a, b, preferred_element_type=F32, **kw)


def _dot_nt(a, b, **kw):
    return lax.dot_general(a, b, NT_DIMS, preferred_element_type=F32, **kw)


def _cparams(*sem):
    return pltpu.CompilerParams(dimension_semantics=sem, vmem_limit_bytes=VMEM_LIMIT_BYTES)


def _const_spec(shape):
    nd = len(shape)
    return pl.BlockSpec(shape, lambda *_: (0,) * nd, pipeline_mode=pl.Buffered(1))


_INPROJ_GROUPS = (("keys", 256, BF16), ("cmp", 256, BF16), ("small", LANES, F32), ("gqkv", 3 * GDN_W, F32),
                  ("ggate", GDN_W, BF16), ("merge", 2 * D_MODEL, BF16))
_INPROJ_WIDTH = sum(w for _, w, _ in _INPROJ_GROUPS)
_INPROJ_T_ROWS = NSA_Q_W + 4 * NSA_HEAD_DIM
_GATE_T_ROWS = 32
INPROJ_TM = 512


def _regroup_w_in(w):
    o = np.cumsum((0, NSA_Q_W) + (NSA_KV_W,) * 6 + (3 * NSA_HEADS, 3 * GDN_W, GDN_HEADS, GDN_HEADS, GDN_W,
                                                   2 * D_MODEL))
    seg = lambda i: w[:, int(o[i]):int(o[i + 1])]
    q, ck, cv, sk, sv, wk, wv, gate, gqkv, beta, decay, ggate, merge = (seg(i) for i in range(13))
    hd = NSA_HEAD_DIM
    grp = lambda t, g: t[:, g * hd:(g + 1) * hd]
    keys = jnp.concatenate([grp(sk, 0), grp(wk, 0), grp(sk, 1), grp(wk, 1)], axis=1)
    cmpkv = jnp.concatenate([ck, cv], axis=1)
    zeros = lambda n: jnp.zeros((w.shape[0], n), w.dtype)
    small = jnp.concatenate([zeros(SM_BETA), beta, decay, zeros(LANES - SM_DECAY - GDN_HEADS)], axis=1)
    w_rows = jnp.concatenate([keys, cmpkv, small, gqkv, ggate, merge], axis=1).astype(BF16)
    vals = jnp.concatenate([grp(sv, 0), grp(wv, 0), grp(sv, 1), grp(wv, 1)], axis=1)
    w_t = jnp.concatenate([q, vals], axis=1).T.astype(BF16)
    gate = gate.reshape(-1, 3, NSA_KV_GROUPS, NSA_REP)
    gpad = zeros(_GATE_T_ROWS // 2 - 3 * NSA_REP)
    w_g = jnp.concatenate([t for g in range(NSA_KV_GROUPS)
                           for t in (gate[:, :, g, :].reshape(-1, 3 * NSA_REP), gpad)], axis=1).T.astype(BF16)
    return w_rows, w_t, w_g


def _inproj_body(x_ref, w_ref, wt_ref, wg_ref, keys_ref, cmp_ref, small_ref, gqkv_ref, ggate_ref, merge_ref,
                 qvt_ref, gt_ref):
    x = x_ref[...]
    outs = (keys_ref, cmp_ref, small_ref, gqkv_ref, ggate_ref, merge_ref)
    c0 = 0
    for ref, (_, width, _) in zip(outs, _INPROJ_GROUPS):
        for s in range(0, width, 512):
            e = min(s + 512, width)
            ref[:, s:e] = _dot(x, w_ref[:, c0 + s:c0 + e]).astype(ref.dtype)
        c0 += width
    for s in range(0, _INPROJ_T_ROWS, 256):
        qvt_ref[s:s + 256, :] = _dot_nt(wt_ref[s:s + 256, :], x).astype(qvt_ref.dtype)
    gt_ref[...] = _dot_nt(wg_ref[...], x)


def _inproj(xb, w_rows, w_t, w_g):
    m = xb.shape[0]
    tm = INPROJ_TM
    return pl.pallas_call(
        _inproj_body,
        grid=(m // tm,),
        in_specs=[pl.BlockSpec((tm, D_MODEL), lambda i: (i, 0)), _const_spec(w_rows.shape),
                  _const_spec(w_t.shape), _const_spec(w_g.shape)],
        out_specs=[pl.BlockSpec((tm, wd), lambda i: (i, 0)) for _, wd, _ in _INPROJ_GROUPS]
        + [pl.BlockSpec((_INPROJ_T_ROWS, tm), lambda i: (0, i)), pl.BlockSpec((_GATE_T_ROWS, tm), lambda i: (0, i))],
        out_shape=[jax.ShapeDtypeStruct((m, wd), dt) for _, wd, dt in _INPROJ_GROUPS]
        + [jax.ShapeDtypeStruct((_INPROJ_T_ROWS, m), BF16), jax.ShapeDtypeStruct((_GATE_T_ROWS, m), F32)],
        compiler_params=_cparams("parallel"),
        name="inproj",
    )(xb, w_rows, w_t, w_g)


def _compress_weights(cmp_pos, cmp_w1, cmp_w2):
    hd, half = NSA_HEAD_DIM, CMP_LEN // 2
    w1r = cmp_w1.reshape(2, 2, half, hd, hd)
    eye = jnp.eye(2, dtype=cmp_w1.dtype)
    w1e = jnp.einsum("whlde,wv,gk->lwgdhvke", w1r, eye, eye).reshape(half * 4 * hd, 2 * 4 * hd)
    posr = cmp_pos.reshape(2, 2, half, hd)
    post = jnp.broadcast_to(posr.transpose(1, 2, 0, 3)[:, :, :, None, :], (2, half, 2, 2, hd))
    post = jnp.concatenate([post.reshape(2, half * 4 * hd), jnp.zeros((6, half * 4 * hd), cmp_pos.dtype)], axis=0)
    w2sel = jnp.zeros((2, 2, 2 * hd, LANES), cmp_w2.dtype)
    for g in range(2):
        w2sel = w2sel.at[:, g, g * hd:(g + 1) * hd, :hd].set(cmp_w2)
    return w1e.astype(BF16), post.astype(BF16), w2sel.reshape(4, 2 * hd, LANES).astype(BF16)


def _compress_body(t_ref, w1_ref, pos_ref, w2_ref, aug_ref, out_ref):
    p = _dot(t_ref[0], w1_ref[...])
    pp = _dot(pos_ref[...], w1_ref[...])
    nxt = pltpu.roll(p[:, 256:], p.shape[0] - 1, axis=0)
    pre = p[:, :256] + nxt + pp[0:1, :256] + pp[1:2, 256:]
    h = jax.nn.gelu(pre).astype(BF16)
    n_idx = lax.broadcasted_iota(jnp.int32, (p.shape[0], LANES), 0)
    real = n_idx < p.shape[0] - 1
    for which in range(2):
        hw = h[:, which * LANES:(which + 1) * LANES]
        for g in range(2):
            o = jnp.where(real, _dot(hw, w2_ref[which * 2 + g]) + aug_ref[which], 0.0)
            out_ref[0, which * 2 + g] = (o if which == 0 else o.T).astype(out_ref.dtype)


def _compress(cmpkv, w1e, post, w2sel, aug):
    b, s, _ = cmpkv.shape
    nblk = s // CMP_STRIDE
    t2 = cmpkv.reshape(b, nblk, CMP_STRIDE * 256)
    return pl.pallas_call(
        _compress_body,
        grid=(b,),
        in_specs=[pl.BlockSpec((1, nblk, CMP_STRIDE * 256), lambda i: (i, 0, 0)),
                  _const_spec(w1e.shape), _const_spec(post.shape), _const_spec(w2sel.shape), _const_spec(aug.shape)],
        out_specs=pl.BlockSpec((1, 4, nblk, LANES), lambda i: (i, 0, 0, 0)),
        out_shape=jax.ShapeDtypeStruct((b, 4, nblk, LANES), BF16),
        compiler_params=_cparams("parallel"),
        name="compress",
    )(t2, w1e, post, w2sel, aug)


def _nsa_consts(s):
    hd, rep = NSA_HEAD_DIM, NSA_REP
    t = np.arange(s)
    kx_win = np.zeros((s + WIN_LEN, hd), np.float32)
    kx_win[WIN_LEN + t, AUG_POS_HI - hd] = t // 256
    kx_win[WIN_LEN + t, AUG_POS_LO - hd] = t % 256
    kx_win[:WIN_LEN, AUG_PAD - hd] = 1.0
    kx_sel = kx_win.copy()
    kx_sel[WIN_LEN + t, t // SLC_LEN] = 1.0
    vx_win = np.zeros((V_ROWS - hd, s + WIN_LEN), np.float32)
    vx_win[0, WIN_LEN:] = 1.0
    vx_sel = vx_win
    n_cmp = s // CMP_STRIDE
    cmp_aug = np.zeros((2, n_cmp, LANES), np.float32)
    end = np.arange(n_cmp) * CMP_STRIDE + CMP_LEN - 1
    cmp_aug[0, :, AUG_POS_HI] = end // 256
    cmp_aug[0, :, AUG_POS_LO] = end % 256
    qx = np.zeros((NSA_KV_GROUPS, LANES - AUG_POS_HI, rep * Q_TILE), np.float32)
    for h in range(NSA_HEADS):
        slope = 2.0 ** (-8.0 * (h + 1) / NSA_HEADS)
        lanes = slice((h % rep) * Q_TILE, (h % rep + 1) * Q_TILE)
        qx[h // rep, 0, lanes] = slope * 256.0
        qx[h // rep, 1, lanes] = slope
        qx[h // rep, AUG_PAD - AUG_POS_HI, lanes] = -BIG
    c0 = np.arange(n_cmp)[None, :] * CMP_STRIDE
    s0 = np.arange(s // SLC_LEN)[:, None] * SLC_LEN
    ov_t = ((c0 < s0 + SLC_LEN) & (c0 + CMP_LEN > s0)).astype(np.float32)
    ov_t[:, (s - CMP_LEN) // CMP_STRIDE + 1:] = 0.0
    kk = np.arange(Q_TILE)[:, None]
    qq = np.arange(Q_TILE)[None, :]
    causal = np.tile(np.where(kk <= qq, 0.0, NEG).astype(np.float32), (1, rep))
    after = np.tile(np.where(kk > qq, 0.0, NEG).astype(np.float32), (1, rep))
    j = jnp.asarray
    return dict(kx_sel=j(kx_sel, BF16), kx_win=j(kx_win, BF16), vx_sel=j(vx_sel, BF16), vx_win=j(vx_win, BF16),
                cmp_aug=j(cmp_aug), qx=j(qx), ov_t=j(ov_t), causal=j(causal), after=j(after))


def _nsa_body(qt_ref, k_ref, vt_ref, kc_ref, vct_ref, gt_ref, kxs_ref, kxw_ref, vxs_ref, vxw_ref, qx_ref, ovt_ref,
              causal_ref, after_ref, out_ref, ks_s, kw_s, vs_s, vw_s):
    hd, rep, tq = NSA_HEAD_DIM, NSA_REP, Q_TILE
    nq = rep * tq
    i = pl.program_id(2)

    @pl.when(i == 0)
    def _():
        keys = k_ref[0]
        ks_s[:WIN_LEN, :hd] = jnp.zeros((WIN_LEN, hd), BF16)
        ks_s[WIN_LEN:, :hd] = keys[:, :hd]
        ks_s[:, hd:] = kxs_ref[...]
        kw_s[:WIN_LEN, :hd] = jnp.zeros((WIN_LEN, hd), BF16)
        kw_s[WIN_LEN:, :hd] = keys[:, hd:]
        kw_s[:, hd:] = kxw_ref[...]
        vals = vt_ref[...]
        vs_s[:hd, :WIN_LEN] = jnp.zeros((hd, WIN_LEN), BF16)
        vs_s[:hd, WIN_LEN:] = vals[:hd]
        vs_s[hd:, :] = vxs_ref[...]
        vw_s[:hd, :WIN_LEN] = jnp.zeros((hd, WIN_LEN), BF16)
        vw_s[:hd, WIN_LEN:] = vals[hd:]
        vw_s[hd:, :] = vxw_ref[...]

    qt = qt_ref[...]
    q64 = jnp.concatenate([qt[r * hd:(r + 1) * hd, :] for r in range(rep)], axis=1).astype(F32) * (hd ** -0.5)
    qx = qx_ref[0]

    def q_aug(sel_rows):
        return jnp.concatenate([q64, sel_rows, qx], axis=0).astype(BF16)

    n_row = lax.broadcasted_iota(jnp.int32, (LANES, nq), 0)
    t_lane = i * tq + (lax.broadcasted_iota(jnp.int32, (LANES, nq), 1) & (tq - 1))
    valid = t_lane >= n_row * CMP_STRIDE + (CMP_LEN - 1)
    qa0 = q_aug(jnp.zeros((N_SLC, nq), F32))
    sc = jnp.where(valid, _dot(kc_ref[0, 0], qa0), NEG)
    mc = jnp.max(sc, axis=0, keepdims=True)
    ec = jnp.where(valid, jnp.exp(sc - mc), 0.0)
    lc = jnp.sum(ec, axis=0, keepdims=True)
    pc = ec * jnp.where(lc > 0.0, 1.0 / lc, 0.0)
    o_cmp = _dot(vct_ref[0, 0], pc.astype(BF16))[:hd]
    psum = pc[:, 0:tq] + pc[:, tq:2 * tq] + pc[:, 2 * tq:3 * tq] + pc[:, 3 * tq:4 * tq]
    score_t = _dot(ovt_ref[...], psum, precision=lax.Precision.HIGHEST)

    w0 = pl.multiple_of(i * tq, tq)
    s_w = _dot(kw_s[pl.ds(w0, WIN_LEN + tq), :], qa0)
    s_w = jnp.concatenate([s_w[:tq] + after_ref[...], s_w[tq:WIN_LEN], s_w[WIN_LEN:] + causal_ref[...]], axis=0)
    p_w = jnp.exp(s_w - jnp.max(s_w, axis=0, keepdims=True))
    acc_w = _dot(vw_s[:, pl.ds(w0, WIN_LEN + tq)], p_w.astype(BF16))
    o_win = acc_w[:hd] * (1.0 / acc_w[hd:hd + 1])

    jb = lax.broadcasted_iota(jnp.int32, (N_SLC, tq), 0)
    cur = (i * tq + lax.broadcasted_iota(jnp.int32, (N_SLC, tq), 1)) // SLC_LEN
    forced = (jb == 0) | (jb == cur) | (jb == cur - 1)
    score_t = jnp.where(forced, FORCE_SCORE, jnp.where(jb <= cur, score_t, -1.0))
    rank = jnp.zeros((N_SLC, tq), F32)
    for jp in range(N_SLC):
        other = score_t[jp:jp + 1, :]
        ge = jnp.where(other >= score_t, 1.0, 0.0)
        gt = jnp.where(other > score_t, 1.0, 0.0)
        rank = rank + jnp.where(jb > jp, ge, gt)
    sel = rank < float(SLC_TOPK)
    qa = q_aug(jnp.concatenate([jnp.where(sel, 0.0, -BIG)] * rep, axis=1))
    lo_blk = jnp.min(jnp.where(sel & (jb >= 2) & (jb <= cur), jb.astype(F32), float(N_SLC)))
    lo_key = (lo_blk.astype(jnp.int32) // 2) * tq

    e_key = i * tq - WIN_LEN
    t0 = pl.multiple_of(jnp.where(e_key > 0, WIN_LEN, 0), tq)
    s_main = _dot(ks_s[pl.ds(w0, WIN_LEN + tq), :], qa)
    s_s = jnp.concatenate([_dot(ks_s[pl.ds(t0, tq), :], qa), s_main[:WIN_LEN],
                           s_main[WIN_LEN:] + causal_ref[...]], axis=0)
    m_s = jnp.max(s_s, axis=0, keepdims=True)
    p_s = jnp.exp(s_s - m_s).astype(BF16)
    acc_s = _dot(vs_s[:, pl.ds(t0, tq)], p_s[:tq]) + _dot(vs_s[:, pl.ds(w0, WIN_LEN + tq)], p_s[tq:])

    def early_step(c, carry):
        m, acc = carry
        k0 = tq + c * SEL_KC
        start = pl.multiple_of(WIN_LEN + k0, tq)
        k_abs = k0 + lax.broadcasted_iota(jnp.int32, (SEL_KC, nq), 0)
        s = jnp.where(k_abs < e_key, _dot(ks_s[pl.ds(start, SEL_KC), :], qa), NEG)
        m_new = jnp.maximum(m, jnp.max(s, axis=0, keepdims=True))
        p = jnp.exp(s - m_new).astype(BF16)
        return m_new, acc * jnp.exp(m - m_new) + _dot(vs_s[:, pl.ds(start, SEL_KC)], p)

    c_hi = (e_key - tq + SEL_KC - 1) // SEL_KC
    c_lo = jnp.where(lo_key < e_key, (lo_key - tq) // SEL_KC, c_hi)
    _, acc_s = lax.fori_loop(c_lo, c_hi, early_step, (m_s, acc_s))
    o_slc = acc_s[:hd] * (1.0 / acc_s[hd:hd + 1])

    sg = jax.nn.sigmoid(gt_ref[...])
    for pair in range(rep // 2):
        halves = []
        for r in (2 * pair, 2 * pair + 1):
            lanes = slice(r * tq, (r + 1) * tq)
            halves.append(sg[r:r + 1] * o_cmp[:, lanes] + sg[rep + r:rep + r + 1] * o_slc[:, lanes]
                          + sg[2 * rep + r:2 * rep + r + 1] * o_win[:, lanes])
        out_ref[0, :, pair * LANES:(pair + 1) * LANES] = jnp.concatenate(halves, axis=0).T.astype(out_ref.dtype)


def _nsa(qvt, keys, cmp_kv, gt, consts, b, s):
    nt = s // Q_TILE
    c = consts
    in_specs = [
        pl.BlockSpec((2 * LANES, Q_TILE), lambda bi, g, i: (g, bi * nt + i)),
        pl.BlockSpec((1, s, LANES), lambda bi, g, i: (bi, 0, g)),
        pl.BlockSpec((LANES, s), lambda bi, g, i: (NSA_Q_W // LANES + g, bi)),
        pl.BlockSpec((1, 1, s // CMP_STRIDE, LANES), lambda bi, g, i: (bi, g, 0, 0)),
        pl.BlockSpec((1, 1, s // CMP_STRIDE, LANES), lambda bi, g, i: (bi, 2 + g, 0, 0)),
        pl.BlockSpec((_GATE_T_ROWS // 2, Q_TILE), lambda bi, g, i: (g, bi * nt + i)),
        _const_spec(c["kx_sel"].shape), _const_spec(c["kx_win"].shape), _const_spec(c["vx_sel"].shape),
        _const_spec(c["vx_win"].shape),
        pl.BlockSpec((1,) + c["qx"].shape[1:], lambda bi, g, i: (g, 0, 0)),
        _const_spec(c["ov_t"].shape), _const_spec(c["causal"].shape), _const_spec(c["after"].shape),
    ]
    return pl.pallas_call(
        _nsa_body,
        grid=(b, NSA_KV_GROUPS, nt),
        in_specs=in_specs,
        out_specs=pl.BlockSpec((1, Q_TILE, 2 * LANES), lambda bi, g, i: (bi, i, g)),
        out_shape=jax.ShapeDtypeStruct((b, s, NSA_Q_W), BF16),
        scratch_shapes=[pltpu.VMEM((s + WIN_LEN, LANES), BF16), pltpu.VMEM((s + WIN_LEN, LANES), BF16),
                        pltpu.VMEM((V_ROWS, s + WIN_LEN), BF16), pltpu.VMEM((V_ROWS, s + WIN_LEN), BF16)],
        compiler_params=_cparams("parallel", "parallel", "arbitrary"),
        name="nsa",
    )(qvt, keys, qvt, cmp_kv, cmp_kv, gt, c["kx_sel"], c["kx_win"], c["vx_sel"], c["vx_win"], c["qx"], c["ov_t"],
      c["causal"], c["after"])


GDN_TS = 512
GDN_BLK = 128
GDN_HALO = 8


def _gdn_intra_body(x_ref, prev_ref, small_ref, cw_ref, alog_ref, dtb_ref, lt_ref,
                    u_ref, w_ref, qg_ref, kdt_ref, aqk_ref, eg_ref, xp_s, x_s, p_s, rhs_s):
    ts, dh, nh, blk = GDN_TS, GDN_HEAD_DIM, GDN_HEADS, GDN_BLK
    i = pl.program_id(1)
    xp_s[0:GDN_HALO, :] = jnp.where(i == 0, 0.0, prev_ref[0])
    xp_s[GDN_HALO:, :] = x_ref[0]
    conv = cw_ref[0:1, :] * xp_s[pl.ds(GDN_HALO - GDN_CONV + 1, ts), :]
    for j in range(1, GDN_CONV):
        conv = conv + cw_ref[j:j + 1, :] * xp_s[pl.ds(GDN_HALO - GDN_CONV + 1 + j, ts), :]
    act = jax.nn.silu(conv)

    sm = small_ref[0]
    beta = jax.nn.sigmoid(sm)
    g = -jnp.exp(alog_ref[...]) * jax.nn.softplus(sm + dtb_ref[...])
    gcum = jnp.concatenate([_dot(lt_ref[...], g[r:r + blk], precision=lax.Precision.HIGHEST)
                            for r in range(0, ts, blk)], axis=0)
    eg = jnp.exp(gcum)
    eg_ref[0] = eg

    ri = lax.broadcasted_iota(jnp.int32, (blk, blk), 0)
    ci = lax.broadcasted_iota(jnp.int32, (blk, blk), 1)
    same = (ri // GDN_CHUNK) == (ci // GDN_CHUNK)
    causal = same & (ri >= ci)
    strict = same & (ri > ci)
    eye = (ri == ci).astype(F32)

    for pb in range(ts // blk):
        rows = slice(pb * blk, (pb + 1) * blk)
        gc = gcum[rows]
        gc_t = gc.T
        first = lax.broadcasted_iota(jnp.int32, (blk, LANES), 0) < GDN_CHUNK
        g_last = jnp.where(first, gc[GDN_CHUNK - 1:GDN_CHUNK, :], gc[blk - 1:blk, :])
        e_dec = jnp.exp(g_last - gc)
        for h in range(nh):
            q = act[rows, h * dh:(h + 1) * dh]
            k = act[rows, GDN_W + h * dh:GDN_W + (h + 1) * dh]
            v = act[rows, 2 * GDN_W + h * dh:2 * GDN_W + (h + 1) * dh]
            q = q * lax.rsqrt(jnp.sum(q * q, axis=1, keepdims=True) + RMS_EPS) * (dh ** -0.5)
            k = k * lax.rsqrt(jnp.sum(k * k, axis=1, keepdims=True) + RMS_EPS)
            b_col = beta[rows, SM_BETA + h:SM_BETA + h + 1]
            eg_col = eg[rows, SM_DECAY + h:SM_DECAY + h + 1]
            gdiff = gc[:, SM_DECAY + h:SM_DECAY + h + 1] - gc_t[SM_DECAY + h:SM_DECAY + h + 1, :]
            decay = jnp.exp(jnp.where(causal, gdiff, NEG))
            kb = k * b_col
            kbf, kf, qf = kb.astype(BF16), k.astype(BF16), q.astype(BF16)
            a = jnp.where(strict, -_dot_nt(kbf, kf) * decay, 0.0)
            c = pb * nh + h
            x_s[c] = a.astype(BF16)
            p_s[c] = eye + a
            rhs_s[c] = jnp.concatenate([v * b_col, kb * eg_col], axis=1).astype(BF16)
            qg_ref[0, h, rows, :] = (q * eg_col).astype(BF16)
            aqk_ref[0, h, rows, :] = jnp.where(causal, _dot_nt(qf, kf) * decay, 0.0).astype(BF16)
            kdec = k * e_dec[:, SM_DECAY + h:SM_DECAY + h + 1]
            kdt_ref[0, h, :, rows] = kdec.T.astype(BF16)

    n_chain = (ts // blk) * nh
    for _ in range(5):
        for c in range(n_chain):
            xb = x_s[c]
            xn = _dot(xb, xb).astype(BF16)
            x_s[c] = xn
            p = p_s[c]
            p_s[c] = p + _dot(p.astype(BF16), xn)
    for c in range(n_chain):
        pb, h = divmod(c, nh)
        rows = slice(pb * blk, (pb + 1) * blk)
        uw = _dot(p_s[c].astype(BF16), rhs_s[c])
        u_ref[0, h, rows, :] = uw[:, :dh]
        w_ref[0, h, rows, :] = uw[:, dh:].astype(BF16)


def _gdn_intra(gqkv, small, conv_w, alog_l, dtb_l, lt):
    b, s, _ = gqkv.shape
    ts, nh, dh = GDN_TS, GDN_HEADS, GDN_HEAD_DIM
    n_chain = (ts // GDN_BLK) * nh
    hspec = lambda: pl.BlockSpec((1, nh, ts, dh), lambda bi, i: (bi, 0, i, 0))
    return pl.pallas_call(
        _gdn_intra_body,
        grid=(b, s // ts),
        in_specs=[
            pl.BlockSpec((1, ts, 3 * GDN_W), lambda bi, i: (bi, i, 0)),
            pl.BlockSpec((1, GDN_HALO, 3 * GDN_W), lambda bi, i: (bi, jnp.maximum(i * (ts // GDN_HALO) - 1, 0), 0)),
            pl.BlockSpec((1, ts, LANES), lambda bi, i: (bi, i, 0)),
            _const_spec(conv_w.shape), _const_spec(alog_l.shape), _const_spec(dtb_l.shape), _const_spec(lt.shape),
        ],
        out_specs=[hspec(), hspec(), hspec(),
                   pl.BlockSpec((1, nh, dh, ts), lambda bi, i: (bi, 0, 0, i)),
                   hspec(),
                   pl.BlockSpec((1, ts, LANES), lambda bi, i: (bi, i, 0))],
        out_shape=[jax.ShapeDtypeStruct((b, nh, s, dh), F32), jax.ShapeDtypeStruct((b, nh, s, dh), BF16),
                   jax.ShapeDtypeStruct((b, nh, s, dh), BF16), jax.ShapeDtypeStruct((b, nh, dh, s), BF16),
                   jax.ShapeDtypeStruct((b, nh, s, GDN_BLK), BF16), jax.ShapeDtypeStruct((b, s, LANES), F32)],
        scratch_shapes=[pltpu.VMEM((ts + GDN_HALO, 3 * GDN_W), F32),
                        pltpu.VMEM((n_chain, GDN_BLK, GDN_BLK), BF16), pltpu.VMEM((n_chain, GDN_BLK, GDN_BLK), F32),
                        pltpu.VMEM((n_chain, GDN_BLK, 2 * dh), BF16)],
        compiler_params=_cparams("parallel", "parallel"),
        name="gdn_intra",
    )(gqkv, gqkv, small, conv_w, alog_l, dtb_l, lt)


def _gdn_scan_body(u_ref, w_ref, qg_ref, kdt_ref, aqk_ref, eg_ref, gate_ref, nw_ref, out_ref, st_s):
    nh, dh, ck, blk = GDN_HEADS, GDN_HEAD_DIM, GDN_CHUNK, GDN_BLK
    st_s[...] = jnp.zeros(st_s.shape, F32)
    zeros = jnp.zeros((ck, dh), BF16)

    def pair(pi, carry):
        r0 = pl.multiple_of(pi * blk, blk)
        egb = eg_ref[0, pl.ds(r0, blk), :]
        kdt = [kdt_ref[0, h, :, pl.ds(r0, blk)] for h in range(nh)]
        for half in range(2):
            rows = pl.ds(r0 + half * ck, ck)
            for h in range(nh):
                st = st_s[h]
                stb = st.astype(BF16)
                v_new = u_ref[0, h, rows, :] - _dot(w_ref[0, h, rows, :], stb)
                vb = v_new.astype(BF16)
                v_pad = jnp.concatenate([vb, zeros] if half == 0 else [zeros, vb], axis=0)
                o = _dot(qg_ref[0, h, rows, :], stb) + _dot(aqk_ref[0, h, rows, :], v_pad)
                d_last = egb[(half + 1) * ck - 1:(half + 1) * ck, SM_DECAY + h:SM_DECAY + h + 1]
                st_s[h] = st * d_last + _dot(kdt[h], v_pad)
                ms = jnp.mean(o * o, axis=1, keepdims=True)
                gt = gate_ref[0, rows, h * dh:(h + 1) * dh].astype(F32)
                out_ref[0, rows, h * dh:(h + 1) * dh] = (
                    o * lax.rsqrt(ms + RMS_EPS) * nw_ref[...] * jax.nn.silu(gt)).astype(out_ref.dtype)
        return carry

    lax.fori_loop(0, u_ref.shape[2] // blk, pair, 0)


def _gdn_scan(u, w, qg, kdt, aqk, eg, ggate, norm_w):
    b, nh, s, dh = u.shape
    hspec = lambda last: pl.BlockSpec((1, nh, s, last), lambda bi: (bi, 0, 0, 0))
    return pl.pallas_call(
        _gdn_scan_body,
        grid=(b,),
        in_specs=[hspec(dh), hspec(dh), hspec(dh),
                  pl.BlockSpec((1, nh, dh, s), lambda bi: (bi, 0, 0, 0)),
                  hspec(GDN_BLK),
                  pl.BlockSpec((1, s, LANES), lambda bi: (bi, 0, 0)),
                  pl.BlockSpec((1, s, GDN_W), lambda bi: (bi, 0, 0)),
                  _const_spec(norm_w.shape)],
        out_specs=pl.BlockSpec((1, s, GDN_W), lambda bi: (bi, 0, 0)),
        out_shape=jax.ShapeDtypeStruct((b, s, GDN_W), BF16),
        scratch_shapes=[pltpu.VMEM((nh, dh, dh), F32)],
        compiler_params=_cparams("parallel"),
        name="gdn_scan",
    )(u, w, qg, kdt, aqk, eg, ggate, norm_w)


MERGE_TM = 512


def _layer_norm(y, g, b):
    mu = jnp.mean(y, axis=1, keepdims=True)
    d = y - mu
    var = jnp.mean(d * d, axis=1, keepdims=True)
    return d * lax.rsqrt(var + LN_EPS) * g + b


def _merge_body(x_ref, oa_ref, ob_ref, ga_ref, gb_ref, wa_ref, wb_ref, wo_ref, g_ref, b_ref, y_ref, yb_ref):
    ya = _dot(oa_ref[...], wa_ref[...])
    yb = _dot(ob_ref[...], wb_ref[...])
    mixin = jax.nn.sigmoid(ga_ref[...].astype(F32)) * ya + jax.nn.sigmoid(gb_ref[...].astype(F32)) * yb
    mix = _dot(mixin.astype(BF16), wo_ref[...])
    y = _layer_norm(DEEPNORM_ALPHA * x_ref[...] + mix, g_ref[...], b_ref[...])
    y_ref[...] = y
    yb_ref[...] = y.astype(BF16)


def _merge(x2, oa, ob, mgate, wa, wb, wo, g, b):
    m = x2.shape[0]
    tm, d = MERGE_TM, D_MODEL
    row = lambda wd, col=0: pl.BlockSpec((tm, wd), lambda i, col=col: (i, col))
    return pl.pallas_call(
        _merge_body,
        grid=(m // tm,),
        in_specs=[row(d), row(NSA_Q_W), row(GDN_W), row(d, 0), row(d, 1),
                  _const_spec(wa.shape), _const_spec(wb.shape), _const_spec(wo.shape),
                  _const_spec(g.shape), _const_spec(b.shape)],
        out_specs=[row(d), row(d)],
        out_shape=[jax.ShapeDtypeStruct((m, d), F32), jax.ShapeDtypeStruct((m, d), BF16)],
        compiler_params=_cparams("parallel"),
        name="merge",
    )(x2, oa, ob, mgate, mgate, wa, wb, wo, g, b)


FFN_TM = 512
FFN_HALO = 16
FFN_CK = 256


def _ffn_body(x_ref, xb_ref, prev_ref, wg_ref, wv_ref, cg_ref, cv_ref, wd_ref, g_ref, b_ref, out_ref, acc_s,
              *, tiles_per_seq):
    i = pl.program_id(0)
    prev = prev_ref[...]
    prev = jnp.where(i % tiles_per_seq == 0, jnp.zeros_like(prev), prev)
    xc = jnp.concatenate([prev, xb_ref[...]], axis=0)

    def conv(h, cw_ref, c0):
        out = cw_ref[FFN_CONV - 1:FFN_CONV, c0:c0 + FFN_CK] * h[FFN_HALO:]
        for j in range(FFN_CONV - 1):
            shifted = pltpu.roll(h, FFN_CONV - 1 - j, axis=0)[FFN_HALO:]
            out = out + cw_ref[j:j + 1, c0:c0 + FFN_CK] * shifted
        return out

    for c in range(FFN_DIM // FFN_CK):
        c0 = c * FFN_CK
        hg = conv(_dot(xc, wg_ref[:, c0:c0 + FFN_CK]), cg_ref, c0)
        hv = conv(_dot(xc, wv_ref[:, c0:c0 + FFN_CK]), cv_ref, c0)
        part = _dot((jax.nn.silu(hg) * hv).astype(BF16), wd_ref[c0:c0 + FFN_CK, :])
        if c == 0:
            acc_s[...] = part
        else:
            acc_s[...] += part
    out_ref[...] = _layer_norm(DEEPNORM_ALPHA * x_ref[...] + acc_s[...], g_ref[...], b_ref[...])


def _ffn(x1, x1b, wg, wv, cg, cv, wd, g, b, seq):
    m = x1.shape[0]
    tm, d = FFN_TM, D_MODEL
    return pl.pallas_call(
        functools.partial(_ffn_body, tiles_per_seq=seq // tm),
        grid=(m // tm,),
        in_specs=[pl.BlockSpec((tm, d), lambda i: (i, 0)),
                  pl.BlockSpec((tm, d), lambda i: (i, 0)),
                  pl.BlockSpec((FFN_HALO, d), lambda i: (jnp.maximum(i * (tm // FFN_HALO) - 1, 0), 0)),
                  _const_spec(wg.shape), _const_spec(wv.shape), _const_spec(cg.shape), _const_spec(cv.shape),
                  _const_spec(wd.shape), _const_spec(g.shape), _const_spec(b.shape)],
        out_specs=pl.BlockSpec((tm, d), lambda i: (i, 0)),
        out_shape=jax.ShapeDtypeStruct((m, d), F32),
        scratch_shapes=[pltpu.VMEM((tm, d), F32)],
        compiler_params=_cparams("parallel"),
        name="ffn",
    )(x1, x1b, x1b, wg, wv, cg, cv, wd, g, b)


def _lane_vec(vals, lane0):
    return jnp.zeros((1, LANES), F32).at[0, lane0:lane0 + vals.shape[0]].set(vals.astype(F32))


def _layer(x, w_in, cmp_pos, cmp_w1, cmp_w2, w_nsa_out, gdn_conv_w, gdn_a_log, gdn_dt_bias, gdn_norm_w,
           w_gdn_out, w_o, ln1_g, ln1_b, ffn_w_up, ffn_conv_w, ffn_w_down, ln2_g, ln2_b):
    b, s, d = x.shape
    m = b * s
    x2 = x.reshape(m, d)
    keys, cmpkv, small, gqkv, ggate, mgate, qvt, gt = _inproj(x2.astype(BF16), *_regroup_w_in(w_in))

    consts = _nsa_consts(s)
    w1e, post, w2sel = _compress_weights(cmp_pos, cmp_w1, cmp_w2)
    cmp_kv = _compress(cmpkv.reshape(b, s, 256), w1e, post, w2sel, consts["cmp_aug"])
    o_nsa = _nsa(qvt, keys.reshape(b, s, 256), cmp_kv, gt, consts, b, s)

    ck = GDN_CHUNK
    tri = np.tril(np.ones((ck, ck), np.float32))
    lt = jnp.asarray(np.kron(np.eye(GDN_BLK // ck, dtype=np.float32), tri))
    u, w, qg, kdt, aqk, eg = _gdn_intra(gqkv.reshape(b, s, 3 * GDN_W), small.reshape(b, s, LANES), gdn_conv_w,
                                        _lane_vec(gdn_a_log, SM_DECAY), _lane_vec(gdn_dt_bias, SM_DECAY), lt)
    o_gdn = _gdn_scan(u, w, qg, kdt, aqk, eg, ggate.reshape(b, s, GDN_W), gdn_norm_w.reshape(1, GDN_HEAD_DIM))

    x1, x1b = _merge(x2, o_nsa.reshape(m, NSA_Q_W), o_gdn.reshape(m, GDN_W), mgate,
                     w_nsa_out.astype(BF16), w_gdn_out.astype(BF16), w_o.astype(BF16),
                     ln1_g.reshape(1, d), ln1_b.reshape(1, d))
    out = _ffn(x1, x1b, ffn_w_up[:, :FFN_DIM].astype(BF16), ffn_w_up[:, FFN_DIM:].astype(BF16),
               ffn_conv_w[:, :FFN_DIM], ffn_conv_w[:, FFN_DIM:], ffn_w_down.astype(BF16),
               ln2_g.reshape(1, d), ln2_b.reshape(1, d), s)
    return out.reshape(b, s, d)


def kernel(x, w_in, nsa_cmp_pos, nsa_cmp_w1, nsa_cmp_w2, w_nsa_out, gdn_conv_w, gdn_a_log, gdn_dt_bias, gdn_norm_w, w_gdn_out, w_o, ln1_g, ln1_b, ffn_w_up, ffn_conv_w, ffn_w_down, ln2_g, ln2_b):
    for l in range(DEPTH):
        x = _layer(x, w_in[l], nsa_cmp_pos[l], nsa_cmp_w1[l], nsa_cmp_w2[l], w_nsa_out[l], gdn_conv_w[l],
                   gdn_a_log[l], gdn_dt_bias[l], gdn_norm_w[l], w_gdn_out[l], w_o[l], ln1_g[l], ln1_b[l],
                   ffn_w_up[l], ffn_conv_w[l], ffn_w_down[l], ln2_g[l], ln2_b[l])
    return x
```

```python
import functools

import numpy as np
import jax
import jax.numpy as jnp
from jax import lax
from jax.experimental import pallas as pl
from jax.experimental.pallas import tpu as pltpu

F32 = jnp.float32
BF16 = jnp.bfloat16

D_MODEL = 1024
NSA_HEADS = 8
NSA_KV_GROUPS = 2
NSA_REP = NSA_HEADS // NSA_KV_GROUPS
NSA_HEAD_DIM = 64
CMP_LEN = 32
CMP_STRIDE = 16
SLC_LEN = 64
SLC_TOPK = 8
WIN_LEN = 512
FORCE_SCORE = 1.0e4
NEG = -1.0e30
GDN_HEADS = 4
GDN_HEAD_DIM = 128
GDN_CONV = 4
GDN_CHUNK = 64
FFN_DIM = 2816
FFN_CONV = 3
DEPTH = 1
DEEPNORM_ALPHA = (2.0 * DEPTH) ** 0.25
LN_EPS = 1e-5
RMS_EPS = 1e-6

NSA_Q_W = NSA_HEADS * NSA_HEAD_DIM
NSA_KV_W = NSA_KV_GROUPS * NSA_HEAD_DIM
GDN_W = GDN_HEADS * GDN_HEAD_DIM

LANES = 128
VMEM_LIMIT_BYTES = 56 * 1024 * 1024

AUG_SEL0 = 64
AUG_POS_HI = 96
AUG_POS_LO = 97
AUG_PAD = 98
BIG = 2.0 ** 100
Q_TILE = 128
N_SLC = 32
V_ROWS = 80
SEL_KC = 512

SM_BETA = 12
SM_DECAY = 16

NT_DIMS = (((1,), (1,)), ((), ()))


def _dot(a, b, **kw):
    return jnp.dot(a, b, preferred_element_type=F32, **kw)


def _dot_nt(a, b, **kw):
    return lax.dot_general(a, b, NT_DIMS, preferred_element_type=F32, **kw)


def _cparams(*sem):
    return pltpu.CompilerParams(dimension_semantics=sem, vmem_limit_bytes=VMEM_LIMIT_BYTES)


def _const_spec(shape):
    nd = len(shape)
    return pl.BlockSpec(shape, lambda *_: (0,) * nd, pipeline_mode=pl.Buffered(1))


_INPROJ_GROUPS = (("keys", 256, BF16), ("cmp", 256, BF16), ("small", LANES, F32), ("gqkv", 3 * GDN_W, F32),
                  ("ggate", GDN_W, BF16), ("merge", 2 * D_MODEL, BF16))
_INPROJ_WIDTH = sum(w for _, w, _ in _INPROJ_GROUPS)
_INPROJ_T_ROWS = NSA_Q_W + 4 * NSA_HEAD_DIM
_GATE_T_ROWS = 32
INPROJ_TM = 512


def _regroup_w_in(w):
    o = np.cumsum((0, NSA_Q_W) + (NSA_KV_W,) * 6 + (3 * NSA_HEADS, 3 * GDN_W, GDN_HEADS, GDN_HEADS, GDN_W,
                                                   2 * D_MODEL))
    seg = lambda i: w[:, int(o[i]):int(o[i + 1])]
    q, ck, cv, sk, sv, wk, wv, gate, gqkv, beta, decay, ggate, merge = (seg(i) for i in range(13))
    hd = NSA_HEAD_DIM
    grp = lambda t, g: t[:, g * hd:(g + 1) * hd]
    keys = jnp.concatenate([grp(sk, 0), grp(wk, 0), grp(sk, 1), grp(wk, 1)], axis=1)
    cmpkv = jnp.concatenate([ck, cv], axis=1)
    zeros = lambda n: jnp.zeros((w.shape[0], n), w.dtype)
    small = jnp.concatenate([zeros(SM_BETA), beta, decay, zeros(LANES - SM_DECAY - GDN_HEADS)], axis=1)
    w_rows = jnp.concatenate([keys, cmpkv, small, gqkv, ggate, merge], axis=1).astype(BF16)
    vals = jnp.concatenate([grp(sv, 0), grp(wv, 0), grp(sv, 1), grp(wv, 1)], axis=1)
    w_t = jnp.concatenate([q, vals], axis=1).T.astype(BF16)
    gate = gate.reshape(-1, 3, NSA_KV_GROUPS, NSA_REP)
    gpad = zeros(_GATE_T_ROWS // 2 - 3 * NSA_REP)
    w_g = jnp.concatenate([t for g in range(NSA_KV_GROUPS)
                           for t in (gate[:, :, g, :].reshape(-1, 3 * NSA_REP), gpad)], axis=1).T.astype(BF16)
    return w_rows, w_t, w_g


def _inproj_body(x_ref, w_ref, wt_ref, wg_ref, keys_ref, cmp_ref, small_ref, gqkv_ref, ggate_ref, merge_ref,
                 qvt_ref, gt_ref):
    x = x_ref[...]
    outs = (keys_ref, cmp_ref, small_ref, gqkv_ref, ggate_ref, merge_ref)
    c0 = 0
    for ref, (_, width, _) in zip(outs, _INPROJ_GROUPS):
        for s in range(0, width, 512):
            e = min(s + 512, width)
            ref[:, s:e] = _dot(x, w_ref[:, c0 + s:c0 + e]).astype(ref.dtype)
        c0 += width
    for s in range(0, _INPROJ_T_ROWS, 256):
        qvt_ref[s:s + 256, :] = _dot_nt(wt_ref[s:s + 256, :], x).astype(qvt_ref.dtype)
    gt_ref[...] = _dot_nt(wg_ref[...], x)


def _inproj(xb, w_rows, w_t, w_g):
    m = xb.shape[0]
    tm = INPROJ_TM
    return pl.pallas_call(
        _inproj_body,
        grid=(m // tm,),
        in_specs=[pl.BlockSpec((tm, D_MODEL), lambda i: (i, 0)), _const_spec(w_rows.shape),
                  _const_spec(w_t.shape), _const_spec(w_g.shape)],
        out_specs=[pl.BlockSpec((tm, wd), lambda i: (i, 0)) for _, wd, _ in _INPROJ_GROUPS]
        + [pl.BlockSpec((_INPROJ_T_ROWS, tm), lambda i: (0, i)), pl.BlockSpec((_GATE_T_ROWS, tm), lambda i: (0, i))],
        out_shape=[jax.ShapeDtypeStruct((m, wd), dt) for _, wd, dt in _INPROJ_GROUPS]
        + [jax.ShapeDtypeStruct((_INPROJ_T_ROWS, m), BF16), jax.ShapeDtypeStruct((_GATE_T_ROWS, m), F32)],
        compiler_params=_cparams("parallel"),
        name="inproj",
    )(xb, w_rows, w_t, w_g)


def _compress_weights(cmp_pos, cmp_w1, cmp_w2):
    hd, half = NSA_HEAD_DIM, CMP_LEN // 2
    w1r = cmp_w1.reshape(2, 2, half, hd, hd)
    eye = jnp.eye(2, dtype=cmp_w1.dtype)
    w1e = jnp.einsum("whlde,wv,gk->lwgdhvke", w1r, eye, eye).reshape(half * 4 * hd, 2 * 4 * hd)
    posr = cmp_pos.reshape(2, 2, half, hd)
    post = jnp.broadcast_to(posr.transpose(1, 2, 0, 3)[:, :, :, None, :], (2, half, 2, 2, hd))
    post = jnp.concatenate([post.reshape(2, half * 4 * hd), jnp.zeros((6, half * 4 * hd), cmp_pos.dtype)], axis=0)
    w2sel = jnp.zeros((2, 2, 2 * hd, LANES), cmp_w2.dtype)
    for g in range(2):
        w2sel = w2sel.at[:, g, g * hd:(g + 1) * hd, :hd].set(cmp_w2)
    return w1e.astype(BF16), post.astype(BF16), w2sel.reshape(4, 2 * hd, LANES).astype(BF16)


def _compress_body(t_ref, w1_ref, pos_ref, w2_ref, aug_ref, out_ref):
    p = _dot(t_ref[0], w1_ref[...])
    pp = _dot(pos_ref[...], w1_ref[...])
    nxt = pltpu.roll(p[:, 256:], p.shape[0] - 1, axis=0)
    pre = p[:, :256] + nxt + pp[0:1, :256] + pp[1:2, 256:]
    h = jax.nn.gelu(pre).astype(BF16)
    n_idx = lax.broadcasted_iota(jnp.int32, (p.shape[0], LANES), 0)
    real = n_idx < p.shape[0] - 1
    for which in range(2):
        hw = h[:, which * LANES:(which + 1) * LANES]
        for g in range(2):
            o = jnp.where(real, _dot(hw, w2_ref[which * 2 + g]) + aug_ref[which], 0.0)
            out_ref[0, which * 2 + g] = (o if which == 0 else o.T).astype(out_ref.dtype)


def _compress(cmpkv, w1e, post, w2sel, aug):
    b, s, _ = cmpkv.shape
    nblk = s // CMP_STRIDE
    t2 = cmpkv.reshape(b, nblk, CMP_STRIDE * 256)
    return pl.pallas_call(
        _compress_body,
        grid=(b,),
        in_specs=[pl.BlockSpec((1, nblk, CMP_STRIDE * 256), lambda i: (i, 0, 0)),
                  _const_spec(w1e.shape), _const_spec(post.shape), _const_spec(w2sel.shape), _const_spec(aug.shape)],
        out_specs=pl.BlockSpec((1, 4, nblk, LANES), lambda i: (i, 0, 0, 0)),
        out_shape=jax.ShapeDtypeStruct((b, 4, nblk, LANES), BF16),
        compiler_params=_cparams("parallel"),
        name="compress",
    )(t2, w1e, post, w2sel, aug)


def _nsa_consts(s):
    hd, rep = NSA_HEAD_DIM, NSA_REP
    t = np.arange(s)
    kx_win = np.zeros((s + WIN_LEN, hd), np.float32)
    kx_win[WIN_LEN + t, AUG_POS_HI - hd] = t // 256
    kx_win[WIN_LEN + t, AUG_POS_LO - hd] = t % 256
    kx_win[:WIN_LEN, AUG_PAD - hd] = 1.0
    kx_sel = kx_win.copy()
    kx_sel[WIN_LEN + t, t // SLC_LEN] = 1.0
    vx_win = np.zeros((V_ROWS - hd, s + WIN_LEN), np.float32)
    vx_win[0, WIN_LEN:] = 1.0
    vx_sel = vx_win
    n_cmp = s // CMP_STRIDE
    cmp_aug = np.zeros((2, n_cmp, LANES), np.float32)
    end = np.arange(n_cmp) * CMP_STRIDE + CMP_LEN - 1
    cmp_aug[0, :, AUG_POS_HI] = end // 256
    cmp_aug[0, :, AUG_POS_LO] = end % 256
    qx = np.zeros((NSA_KV_GROUPS, LANES - AUG_POS_HI, rep * Q_TILE), np.float32)
    for h in range(NSA_HEADS):
        slope = 2.0 ** (-8.0 * (h + 1) / NSA_HEADS)
        lanes = slice((h % rep) * Q_TILE, (h % rep + 1) * Q_TILE)
        qx[h // rep, 0, lanes] = slope * 256.0
        qx[h // rep, 1, lanes] = slope
        qx[h // rep, AUG_PAD - AUG_POS_HI, lanes] = -BIG
    c0 = np.arange(n_cmp)[None, :] * CMP_STRIDE
    s0 = np.arange(s // SLC_LEN)[:, None] * SLC_LEN
    ov_t = ((c0 < s0 + SLC_LEN) & (c0 + CMP_LEN > s0)).astype(np.float32)
    ov_t[:, (s - CMP_LEN) // CMP_STRIDE + 1:] = 0.0
    kk = np.arange(Q_TILE)[:, None]
    qq = np.arange(Q_TILE)[None, :]
    causal = np.tile(np.where(kk <= qq, 0.0, NEG).astype(np.float32), (1, rep))
    after = np.tile(np.where(kk > qq, 0.0, NEG).astype(np.float32), (1, rep))
    j = jnp.asarray
    return dict(kx_sel=j(kx_sel, BF16), kx_win=j(kx_win, BF16), vx_sel=j(vx_sel, BF16), vx_win=j(vx_win, BF16),
                cmp_aug=j(cmp_aug), qx=j(qx), ov_t=j(ov_t), causal=j(causal), after=j(after))


def _nsa_body(qt_ref, k_ref, vt_ref, kc_ref, vct_ref, gt_ref, kxs_ref, kxw_ref, vxs_ref, vxw_ref, qx_ref, ovt_ref,
              causal_ref, after_ref, out_ref, ks_s, kw_s, vs_s, vw_s):
    hd, rep, tq = NSA_HEAD_DIM, NSA_REP, Q_TILE
    nq = rep * tq
    i = pl.program_id(2)

    @pl.when(i == 0)
    def _():
        keys = k_ref[0]
        ks_s[:WIN_LEN, :hd] = jnp.zeros((WIN_LEN, hd), BF16)
        ks_s[WIN_LEN:, :hd] = keys[:, :hd]
        ks_s[:, hd:] = kxs_ref[...]
        kw_s[:WIN_LEN, :hd] = jnp.zeros((WIN_LEN, hd), BF16)
        kw_s[WIN_LEN:, :hd] = keys[:, hd:]
        kw_s[:, hd:] = kxw_ref[...]
        vals = vt_ref[...]
        vs_s[:hd, :WIN_LEN] = jnp.zeros((hd, WIN_LEN), BF16)
        vs_s[:hd, WIN_LEN:] = vals[:hd]
        vs_s[hd:, :] = vxs_ref[...]
        vw_s[:hd, :WIN_LEN] = jnp.zeros((hd, WIN_LEN), BF16)
        vw_s[:hd, WIN_LEN:] = vals[hd:]
        vw_s[hd:, :] = vxw_ref[...]

    qt = qt_ref[...]
    q64 = jnp.concatenate([qt[r * hd:(r + 1) * hd, :] for r in range(rep)], axis=1).astype(F32) * (hd ** -0.5)
    qx = qx_ref[0]

    def q_aug(sel_rows):
        return jnp.concatenate([q64, sel_rows, qx], axis=0).astype(BF16)

    n_row = lax.broadcasted_iota(jnp.int32, (LANES, nq), 0)
    t_lane = i * tq + (lax.broadcasted_iota(jnp.int32, (LANES, nq), 1) & (tq - 1))
    valid = t_lane >= n_row * CMP_STRIDE + (CMP_LEN - 1)
    qa0 = q_aug(jnp.zeros((N_SLC, nq), F32))
    sc = jnp.where(valid, _dot(kc_ref[0, 0], qa0), NEG)
    mc = jnp.max(sc, axis=0, keepdims=True)
    ec = jnp.where(valid, jnp.exp(sc - mc), 0.0)
    lc = jnp.sum(ec, axis=0, keepdims=True)
    pc = ec * jnp.where(lc > 0.0, 1.0 / lc, 0.0)
    o_cmp = _dot(vct_ref[0, 0], pc.astype(BF16))[:hd]
    psum = pc[:, 0:tq] + pc[:, tq:2 * tq] + pc[:, 2 * tq:3 * tq] + pc[:, 3 * tq:4 * tq]
    score_t = _dot(ovt_ref[...], psum, precision=lax.Precision.HIGHEST)

    w0 = pl.multiple_of(i * tq, tq)
    s_w = _dot(kw_s[pl.ds(w0, WIN_LEN + tq), :], qa0)
    s_w = jnp.concatenate([s_w[:tq] + after_ref[...], s_w[tq:WIN_LEN], s_w[WIN_LEN:] + causal_ref[...]], axis=0)
    p_w = jnp.exp(s_w - jnp.max(s_w, axis=0, keepdims=True))
    acc_w = _dot(vw_s[:, pl.ds(w0, WIN_LEN + tq)], p_w.astype(BF16))
    o_win = acc_w[:hd] * (1.0 / acc_w[hd:hd + 1])

    jb = lax.broadcasted_iota(jnp.int32, (N_SLC, tq), 0)
    cur = (i * tq + lax.broadcasted_iota(jnp.int32, (N_SLC, tq), 1)) // SLC_LEN
    forced = (jb == 0) | (jb == cur) | (jb == cur - 1)
    score_t = jnp.where(forced, FORCE_SCORE, jnp.where(jb <= cur, score_t, -1.0))
    rank = jnp.zeros((N_SLC, tq), F32)
    for jp in range(N_SLC):
        other = score_t[jp:jp + 1, :]
        ge = jnp.where(other >= score_t, 1.0, 0.0)
        gt = jnp.where(other > score_t, 1.0, 0.0)
        rank = rank + jnp.where(jb > jp, ge, gt)
    sel = rank < float(SLC_TOPK)
    qa = q_aug(jnp.concatenate([jnp.where(sel, 0.0, -BIG)] * rep, axis=1))
    lo_blk = jnp.min(jnp.where(sel & (jb >= 2) & (jb <= cur), jb.astype(F32), float(N_SLC)))
    lo_key = (lo_blk.astype(jnp.int32) // 2) * tq

    e_key = i * tq - WIN_LEN
    t0 = pl.multiple_of(jnp.where(e_key > 0, WIN_LEN, 0), tq)
    s_main = _dot(ks_s[pl.ds(w0, WIN_LEN + tq), :], qa)
    s_s = jnp.concatenate([_dot(ks_s[pl.ds(t0, tq), :], qa), s_main[:WIN_LEN],
                           s_main[WIN_LEN:] + causal_ref[...]], axis=0)
    m_s = jnp.max(s_s, axis=0, keepdims=True)
    p_s = jnp.exp(s_s - m_s).astype(BF16)
    acc_s = _dot(vs_s[:, pl.ds(t0, tq)], p_s[:tq]) + _dot(vs_s[:, pl.ds(w0, WIN_LEN + tq)], p_s[tq:])

    def early_step(c, carry):
        m, acc = carry
        k0 = tq + c * SEL_KC
        start = pl.multiple_of(WIN_LEN + k0, tq)
        k_abs = k0 + lax.broadcasted_iota(jnp.int32, (SEL_KC, nq), 0)
        s = jnp.where(k_abs < e_key, _dot(ks_s[pl.ds(start, SEL_KC), :], qa), NEG)
        m_new = jnp.maximum(m, jnp.max(s, axis=0, keepdims=True))
        p = jnp.exp(s - m_new).astype(BF16)
        return m_new, acc * jnp.exp(m - m_new) + _dot(vs_s[:, pl.ds(start, SEL_KC)], p)

    c_hi = (e_key - tq + SEL_KC - 1) // SEL_KC
    c_lo = jnp.where(lo_key < e_key, (lo_key - tq) // SEL_KC, c_hi)
    _, acc_s = lax.fori_loop(c_lo, c_hi, early_step, (m_s, acc_s))
    o_slc = acc_s[:hd] * (1.0 / acc_s[hd:hd + 1])

    sg = jax.nn.sigmoid(gt_ref[...])
    for pair in range(rep // 2):
        halves = []
        for r in (2 * pair, 2 * pair + 1):
            lanes = slice(r * tq, (r + 1) * tq)
            halves.append(sg[r:r + 1] * o_cmp[:, lanes] + sg[rep + r:rep + r + 1] * o_slc[:, lanes]
                          + sg[2 * rep + r:2 * rep + r + 1] * o_win[:, lanes])
        out_ref[0, :, pair * LANES:(pair + 1) * LANES] = jnp.concatenate(halves, axis=0).T.astype(out_ref.dtype)


def _nsa(qvt, keys, cmp_kv, gt, consts, b, s):
    nt = s // Q_TILE
    c = consts
    in_specs = [
        pl.BlockSpec((2 * LANES, Q_TILE), lambda bi, g, i: (g, bi * nt + i)),
        pl.BlockSpec((1, s, LANES), lambda bi, g, i: (bi, 0, g)),
        pl.BlockSpec((LANES, s), lambda bi, g, i: (NSA_Q_W // LANES + g, bi)),
        pl.BlockSpec((1, 1, s // CMP_STRIDE, LANES), lambda bi, g, i: (bi, g, 0, 0)),
        pl.BlockSpec((1, 1, s // CMP_STRIDE, LANES), lambda bi, g, i: (bi, 2 + g, 0, 0)),
        pl.BlockSpec((_GATE_T_ROWS // 2, Q_TILE), lambda bi, g, i: (g, bi * nt + i)),
        _const_spec(c["kx_sel"].shape), _const_spec(c["kx_win"].shape), _const_spec(c["vx_sel"].shape),
        _const_spec(c["vx_win"].shape),
        pl.BlockSpec((1,) + c["qx"].shape[1:], lambda bi, g, i: (g, 0, 0)),
        _const_spec(c["ov_t"].shape), _const_spec(c["causal"].shape), _const_spec(c["after"].shape),
    ]
    return pl.pallas_call(
        _nsa_body,
        grid=(b, NSA_KV_GROUPS, nt),
        in_specs=in_specs,
        out_specs=pl.BlockSpec((1, Q_TILE, 2 * LANES), lambda bi, g, i: (bi, i, g)),
        out_shape=jax.ShapeDtypeStruct((b, s, NSA_Q_W), BF16),
        scratch_shapes=[pltpu.VMEM((s + WIN_LEN, LANES), BF16), pltpu.VMEM((s + WIN_LEN, LANES), BF16),
                        pltpu.VMEM((V_ROWS, s + WIN_LEN), BF16), pltpu.VMEM((V_ROWS, s + WIN_LEN), BF16)],
        compiler_params=_cparams("parallel", "parallel", "arbitrary"),
        name="nsa",
    )(qvt, keys, qvt, cmp_kv, cmp_kv, gt, c["kx_sel"], c["kx_win"], c["vx_sel"], c["vx_win"], c["qx"], c["ov_t"],
      c["causal"], c["after"])


GDN_TS = 512
GDN_BLK = 128
GDN_HALO = 8
GDN_SCAN_ROWS = GDN_HEAD_DIM + GDN_CHUNK


def _gdn_intra_body(x_ref, prev_ref, small_ref, cw_ref, alog_ref, dtb_ref, lt_ref,
                    nq_ref, co_ref, eg_ref, xp_s, x_s, p_s, rhs_s, qg_s, aqk_s, kdt_s):
    ts, dh, nh, blk = GDN_TS, GDN_HEAD_DIM, GDN_HEADS, GDN_BLK
    i = pl.program_id(1)
    xp_s[0:GDN_HALO, :] = jnp.where(i == 0, 0.0, prev_ref[0])
    xp_s[GDN_HALO:, :] = x_ref[0]
    conv = cw_ref[0:1, :] * xp_s[pl.ds(GDN_HALO - GDN_CONV + 1, ts), :]
    for j in range(1, GDN_CONV):
        conv = conv + cw_ref[j:j + 1, :] * xp_s[pl.ds(GDN_HALO - GDN_CONV + 1 + j, ts), :]
    act = jax.nn.silu(conv)

    sm = small_ref[0]
    beta = jax.nn.sigmoid(sm)
    g = -jnp.exp(alog_ref[...]) * jax.nn.softplus(sm + dtb_ref[...])
    gcum = jnp.concatenate([_dot(lt_ref[...], g[r:r + blk], precision=lax.Precision.HIGHEST)
                            for r in range(0, ts, blk)], axis=0)
    eg = jnp.exp(gcum)
    eg_ref[0] = eg

    ri = lax.broadcasted_iota(jnp.int32, (blk, blk), 0)
    ci = lax.broadcasted_iota(jnp.int32, (blk, blk), 1)
    same = (ri // GDN_CHUNK) == (ci // GDN_CHUNK)
    causal = same & (ri >= ci)
    strict = same & (ri > ci)
    eye = (ri == ci).astype(F32)

    for pb in range(ts // blk):
        rows = slice(pb * blk, (pb + 1) * blk)
        gc = gcum[rows]
        gc_t = gc.T
        first = lax.broadcasted_iota(jnp.int32, (blk, LANES), 0) < GDN_CHUNK
        g_last = jnp.where(first, gc[GDN_CHUNK - 1:GDN_CHUNK, :], gc[blk - 1:blk, :])
        e_dec = jnp.exp(g_last - gc)
        for h in range(nh):
            q = act[rows, h * dh:(h + 1) * dh]
            k = act[rows, GDN_W + h * dh:GDN_W + (h + 1) * dh]
            v = act[rows, 2 * GDN_W + h * dh:2 * GDN_W + (h + 1) * dh]
            q = q * lax.rsqrt(jnp.sum(q * q, axis=1, keepdims=True) + RMS_EPS) * (dh ** -0.5)
            k = k * lax.rsqrt(jnp.sum(k * k, axis=1, keepdims=True) + RMS_EPS)
            b_col = beta[rows, SM_BETA + h:SM_BETA + h + 1]
            eg_col = eg[rows, SM_DECAY + h:SM_DECAY + h + 1]
            gdiff = gc[:, SM_DECAY + h:SM_DECAY + h + 1] - gc_t[SM_DECAY + h:SM_DECAY + h + 1, :]
            decay = jnp.exp(jnp.where(causal, gdiff, NEG))
            kb = k * b_col
            kbf, kf, qf = kb.astype(BF16), k.astype(BF16), q.astype(BF16)
            a = jnp.where(strict, -_dot_nt(kbf, kf) * decay, 0.0)
            c = pb * nh + h
            x_s[c] = a.astype(BF16)
            p_s[c] = eye + a
            rhs_s[c] = jnp.concatenate([v * b_col, kb * eg_col], axis=1).astype(BF16)
            qg_s[c] = q * eg_col
            aqk_s[c] = jnp.where(causal, _dot_nt(qf, kf) * decay, 0.0).astype(BF16)
            kdt_s[c] = (k * e_dec[:, SM_DECAY + h:SM_DECAY + h + 1]).T.astype(BF16)

    n_chain = (ts // blk) * nh
    for _ in range(5):
        for c in range(n_chain):
            xb = x_s[c]
            xn = _dot(xb, xb).astype(BF16)
            x_s[c] = xn
            p = p_s[c]
            p_s[c] = p + _dot(p.astype(BF16), xn)

    tok_half = lax.broadcasted_iota(jnp.int32, (blk, blk), 1) // GDN_CHUNK
    for c in range(n_chain):
        pb, h = divmod(c, nh)
        uw = _dot(p_s[c].astype(BF16), rhs_s[c]).astype(BF16)
        a1 = _dot(aqk_s[c], uw)
        q_loc = qg_s[c] - a1[:, dh:]
        kdt = kdt_s[c]
        for half in range(blk // GDN_CHUNK):
            k1 = _dot(jnp.where(tok_half == half, kdt, jnp.zeros_like(kdt)), uw)
            n = pb * (blk // GDN_CHUNK) + half
            rows = slice(half * GDN_CHUNK, (half + 1) * GDN_CHUNK)
            nq_ref[0, h, n, :dh, :] = (-k1[:, dh:]).astype(BF16)
            nq_ref[0, h, n, dh:, :] = q_loc[rows].astype(BF16)
            co_ref[0, h, n, :dh, :] = k1[:, :dh].astype(BF16)
            co_ref[0, h, n, dh:, :] = a1[rows, :dh].astype(BF16)


def _gdn_intra(gqkv, small, conv_w, alog_l, dtb_l, lt):
    b, s, _ = gqkv.shape
    ts, nh, dh = GDN_TS, GDN_HEADS, GDN_HEAD_DIM
    n_chain = (ts // GDN_BLK) * nh
    sq = lambda dt: pltpu.VMEM((n_chain, GDN_BLK, GDN_BLK), dt)
    cspec = lambda: pl.BlockSpec((1, nh, ts // GDN_CHUNK, GDN_SCAN_ROWS, dh), lambda bi, i: (bi, 0, i, 0, 0))
    cshape = jax.ShapeDtypeStruct((b, nh, s // GDN_CHUNK, GDN_SCAN_ROWS, dh), BF16)
    return pl.pallas_call(
        _gdn_intra_body,
        grid=(b, s // ts),
        in_specs=[
            pl.BlockSpec((1, ts, 3 * GDN_W), lambda bi, i: (bi, i, 0)),
            pl.BlockSpec((1, GDN_HALO, 3 * GDN_W), lambda bi, i: (bi, jnp.maximum(i * (ts // GDN_HALO) - 1, 0), 0)),
            pl.BlockSpec((1, ts, LANES), lambda bi, i: (bi, i, 0)),
            _const_spec(conv_w.shape), _const_spec(alog_l.shape), _const_spec(dtb_l.shape), _const_spec(lt.shape),
        ],
        out_specs=[cspec(), cspec(), pl.BlockSpec((1, ts, LANES), lambda bi, i: (bi, i, 0))],
        out_shape=[cshape, cshape, jax.ShapeDtypeStruct((b, s, LANES), F32)],
        scratch_shapes=[pltpu.VMEM((ts + GDN_HALO, 3 * GDN_W), F32), sq(BF16), sq(F32),
                        pltpu.VMEM((n_chain, GDN_BLK, 2 * dh), BF16), sq(F32), sq(BF16), sq(BF16)],
        compiler_params=_cparams("parallel", "parallel"),
        name="gdn_intra",
    )(gqkv, gqkv, small, conv_w, alog_l, dtb_l, lt)


def _gdn_scan_body(nq_ref, co_ref, eg_ref, gate_ref, nw_ref, out_ref, st_s):
    nh, dh, ck = GDN_HEADS, GDN_HEAD_DIM, GDN_CHUNK
    st_s[...] = jnp.zeros(st_s.shape, F32)

    def chunk(n, carry):
        r0 = pl.multiple_of(n * ck, ck)
        d_row = eg_ref[0, pl.ds(r0 + ck - 1, 1), :]
        for h in range(nh):
            st = st_s[h]
            res = _dot(nq_ref[0, h, n], st.astype(BF16)) + co_ref[0, h, n].astype(F32)
            st_s[h] = st * d_row[:, SM_DECAY + h:SM_DECAY + h + 1] + res[:dh]
            o = res[dh:]
            ms = jnp.mean(o * o, axis=1, keepdims=True)
            gt = gate_ref[0, pl.ds(r0, ck), h * dh:(h + 1) * dh].astype(F32)
            out_ref[0, pl.ds(r0, ck), h * dh:(h + 1) * dh] = (
                o * lax.rsqrt(ms + RMS_EPS) * nw_ref[...] * jax.nn.silu(gt)).astype(out_ref.dtype)
        return carry

    lax.fori_loop(0, nq_ref.shape[2], chunk, 0, unroll=2)


def _gdn_scan(nq, co, eg, ggate, norm_w):
    b, nh, nc, rows, dh = nq.shape
    s = nc * GDN_CHUNK
    cspec = lambda: pl.BlockSpec((1, nh, nc, rows, dh), lambda bi: (bi, 0, 0, 0, 0))
    return pl.pallas_call(
        _gdn_scan_body,
        grid=(b,),
        in_specs=[cspec(), cspec(),
                  pl.BlockSpec((1, s, LANES), lambda bi: (bi, 0, 0)),
                  pl.BlockSpec((1, s, GDN_W), lambda bi: (bi, 0, 0)),
                  _const_spec(norm_w.shape)],
        out_specs=pl.BlockSpec((1, s, GDN_W), lambda bi: (bi, 0, 0)),
        out_shape=jax.ShapeDtypeStruct((b, s, GDN_W), BF16),
        scratch_shapes=[pltpu.VMEM((nh, dh, dh), F32)],
        compiler_params=_cparams("parallel"),
        name="gdn_scan",
    )(nq, co, eg, ggate, norm_w)


MERGE_TM = 512


def _layer_norm(y, g, b):
    mu = jnp.mean(y, axis=1, keepdims=True)
    d = y - mu
    var = jnp.mean(d * d, axis=1, keepdims=True)
    return d * lax.rsqrt(var + LN_EPS) * g + b


def _merge_body(x_ref, oa_ref, ob_ref, ga_ref, gb_ref, wa_ref, wb_ref, wo_ref, g_ref, b_ref, y_ref, yb_ref):
    ya = _dot(oa_ref[...], wa_ref[...])
    yb = _dot(ob_ref[...], wb_ref[...])
    mixin = jax.nn.sigmoid(ga_ref[...].astype(F32)) * ya + jax.nn.sigmoid(gb_ref[...].astype(F32)) * yb
    mix = _dot(mixin.astype(BF16), wo_ref[...])
    y = _layer_norm(DEEPNORM_ALPHA * x_ref[...] + mix, g_ref[...], b_ref[...])
    y_ref[...] = y
    yb_ref[...] = y.astype(BF16)


def _merge(x2, oa, ob, mgate, wa, wb, wo, g, b):
    m = x2.shape[0]
    tm, d = MERGE_TM, D_MODEL
    row = lambda wd, col=0: pl.BlockSpec((tm, wd), lambda i, col=col: (i, col))
    return pl.pallas_call(
        _merge_body,
        grid=(m // tm,),
        in_specs=[row(d), row(NSA_Q_W), row(GDN_W), row(d, 0), row(d, 1),
                  _const_spec(wa.shape), _const_spec(wb.shape), _const_spec(wo.shape),
                  _const_spec(g.shape), _const_spec(b.shape)],
        out_specs=[row(d), row(d)],
        out_shape=[jax.ShapeDtypeStruct((m, d), F32), jax.ShapeDtypeStruct((m, d), BF16)],
        compiler_params=_cparams("parallel"),
        name="merge",
    )(x2, oa, ob, mgate, mgate, wa, wb, wo, g, b)


FFN_TM = 512
FFN_HALO = 16
FFN_CK = 256


def _ffn_body(x_ref, xb_ref, prev_ref, wg_ref, wv_ref, cg_ref, cv_ref, wd_ref, g_ref, b_ref, out_ref, act_s,
              *, tiles_per_seq):
    i = pl.program_id(0)
    prev = prev_ref[...]
    prev = jnp.where(i % tiles_per_seq == 0, jnp.zeros_like(prev), prev)
    xc = jnp.concatenate([prev, xb_ref[...]], axis=0)

    def conv(h, cw_ref, c0):
        out = cw_ref[FFN_CONV - 1:FFN_CONV, c0:c0 + FFN_CK] * h[FFN_HALO:]
        for j in range(FFN_CONV - 1):
            shifted = pltpu.roll(h, FFN_CONV - 1 - j, axis=0)[FFN_HALO:]
            out = out + cw_ref[j:j + 1, c0:c0 + FFN_CK] * shifted
        return out

    for c in range(FFN_DIM // FFN_CK):
        c0 = c * FFN_CK
        hg = conv(_dot(xc, wg_ref[:, c0:c0 + FFN_CK]), cg_ref, c0)
        hv = conv(_dot(xc, wv_ref[:, c0:c0 + FFN_CK]), cv_ref, c0)
        act_s[:, c0:c0 + FFN_CK] = (jax.nn.silu(hg) * hv).astype(BF16)
    f = _dot(act_s[...], wd_ref[...])
    out_ref[...] = _layer_norm(DEEPNORM_ALPHA * x_ref[...] + f, g_ref[...], b_ref[...])


def _ffn(x1, x1b, wg, wv, cg, cv, wd, g, b, seq):
    m = x1.shape[0]
    tm, d = FFN_TM, D_MODEL
    return pl.pallas_call(
        functools.partial(_ffn_body, tiles_per_seq=seq // tm),
        grid=(m // tm,),
        in_specs=[pl.BlockSpec((tm, d), lambda i: (i, 0)),
                  pl.BlockSpec((tm, d), lambda i: (i, 0)),
                  pl.BlockSpec((FFN_HALO, d), lambda i: (jnp.maximum(i * (tm // FFN_HALO) - 1, 0), 0)),
                  _const_spec(wg.shape), _const_spec(wv.shape), _const_spec(cg.shape), _const_spec(cv.shape),
                  _const_spec(wd.shape), _const_spec(g.shape), _const_spec(b.shape)],
        out_specs=pl.BlockSpec((tm, d), lambda i: (i, 0)),
        out_shape=jax.ShapeDtypeStruct((m, d), F32),
        scratch_shapes=[pltpu.VMEM((tm, FFN_DIM), BF16)],
        compiler_params=_cparams("parallel"),
        name="ffn",
    )(x1, x1b, x1b, wg, wv, cg, cv, wd, g, b)


def _lane_vec(vals, lane0):
    return jnp.zeros((1, LANES), F32).at[0, lane0:lane0 + vals.shape[0]].set(vals.astype(F32))


def _layer(x, w_in, cmp_pos, cmp_w1, cmp_w2, w_nsa_out, gdn_conv_w, gdn_a_log, gdn_dt_bias, gdn_norm_w,
           w_gdn_out, w_o, ln1_g, ln1_b, ffn_w_up, ffn_conv_w, ffn_w_down, ln2_g, ln2_b):
    b, s, d = x.shape
    m = b * s
    x2 = x.reshape(m, d)
    keys, cmpkv, small, gqkv, ggate, mgate, qvt, gt = _inproj(x2.astype(BF16), *_regroup_w_in(w_in))

    consts = _nsa_consts(s)
    w1e, post, w2sel = _compress_weights(cmp_pos, cmp_w1, cmp_w2)
    cmp_kv = _compress(cmpkv.reshape(b, s, 256), w1e, post, w2sel, consts["cmp_aug"])
    o_nsa = _nsa(qvt, keys.reshape(b, s, 256), cmp_kv, gt, consts, b, s)

    ck = GDN_CHUNK
    tri = np.tril(np.ones((ck, ck), np.float32))
    lt = jnp.asarray(np.kron(np.eye(GDN_BLK // ck, dtype=np.float32), tri))
    nq, co, eg = _gdn_intra(gqkv.reshape(b, s, 3 * GDN_W), small.reshape(b, s, LANES), gdn_conv_w,
                                        _lane_vec(gdn_a_log, SM_DECAY), _lane_vec(gdn_dt_bias, SM_DECAY), lt)
    o_gdn = _gdn_scan(nq, co, eg, ggate.reshape(b, s, GDN_W), gdn_norm_w.reshape(1, GDN_HEAD_DIM))

    x1, x1b = _merge(x2, o_nsa.reshape(m, NSA_Q_W), o_gdn.reshape(m, GDN_W), mgate,
                     w_nsa_out.astype(BF16), w_gdn_out.astype(BF16), w_o.astype(BF16),
                     ln1_g.reshape(1, d), ln1_b.reshape(1, d))
    out = _ffn(x1, x1b, ffn_w_up[:, :FFN_DIM].astype(BF16), ffn_w_up[:, FFN_DIM:].astype(BF16),
               ffn_conv_w[:, :FFN_DIM], ffn_conv_w[:, FFN_DIM:], ffn_w_down.astype(BF16),
               ln2_g.reshape(1, d), ln2_b.reshape(1, d), s)
    return out.reshape(b, s, d)


def kernel(x, w_in, nsa_cmp_pos, nsa_cmp_w1, nsa_cmp_w2, w_nsa_out, gdn_conv_w, gdn_a_log, gdn_dt_bias, gdn_norm_w, w_gdn_out, w_o, ln1_g, ln1_b, ffn_w_up, ffn_conv_w, ffn_w_down, ln2_g, ln2_b):
    for l in range(DEPTH):
        x = _layer(x, w_in[l], nsa_cmp_pos[l], nsa_cmp_w1[l], nsa_cmp_w2[l], w_nsa_out[l], gdn_conv_w[l],
                   gdn_a_log[l], gdn_dt_bias[l], gdn_norm_w[l], w_gdn_out[l], w_o[l], ln1_g[l], ln1_b[l],
                   ffn_w_up[l], ffn_conv_w[l], ffn_w_down[l], ln2_g[l], ln2_b[l])
    return x
```

```python
import functools

import numpy as np
import jax
import jax.numpy as jnp
from jax import lax
from jax.experimental import pallas as pl
from jax.experimental.pallas import tpu as pltpu

F32 = jnp.float32
BF16 = jnp.bfloat16

D_MODEL = 1024
NSA_HEADS = 8
NSA_KV_GROUPS = 2
NSA_REP = NSA_HEADS // NSA_KV_GROUPS
NSA_HEAD_DIM = 64
CMP_LEN = 32
CMP_STRIDE = 16
SLC_LEN = 64
SLC_TOPK = 8
WIN_LEN = 512
FORCE_SCORE = 1.0e4
NEG = -1.0e30
GDN_HEADS = 4
GDN_HEAD_DIM = 128
GDN_CONV = 4
GDN_CHUNK = 64
FFN_DIM = 2816
FFN_CONV = 3
DEPTH = 1
DEEPNORM_ALPHA = (2.0 * DEPTH) ** 0.25
LN_EPS = 1e-5
RMS_EPS = 1e-6

NSA_Q_W = NSA_HEADS * NSA_HEAD_DIM
NSA_KV_W = NSA_KV_GROUPS * NSA_HEAD_DIM
GDN_W = GDN_HEADS * GDN_HEAD_DIM

LANES = 128
VMEM_LIMIT_BYTES = 56 * 1024 * 1024

AUG_SEL0 = 64
AUG_POS_HI = 96
AUG_POS_LO = 97
AUG_PAD = 98
BIG = 2.0 ** 100
Q_TILE = 128
N_SLC = 32
V_ROWS = 80
SEL_KC = 512

NT_DIMS = (((1,), (1,)), ((), ()))


def _dot(a, b, **kw):
    return jnp.dot(a, b, preferred_element_type=F32, **kw)


def _dot_nt(a, b, **kw):
    return lax.dot_general(a, b, NT_DIMS, preferred_element_type=F32, **kw)


def _cparams(*sem):
    return pltpu.CompilerParams(dimension_semantics=sem, vmem_limit_bytes=VMEM_LIMIT_BYTES)


def _const_spec(shape):
    nd = len(shape)
    return pl.BlockSpec(shape, lambda *_: (0,) * nd, pipeline_mode=pl.Buffered(1))


_IN_WIDTHS = (NSA_Q_W,) + (NSA_KV_W,) * 6 + (3 * NSA_HEADS, 3 * GDN_W, GDN_HEADS, GDN_HEADS, GDN_W, 2 * D_MODEL)
(_C_Q, _C_CK, _C_CV, _C_SK, _C_SV, _C_WK, _C_WV, _C_GATE, _C_GQKV, _C_BETA, _C_DECAY, _C_GGATE, _C_MERGE,
 IN_WIDTH) = (int(v) for v in np.cumsum((0,) + _IN_WIDTHS))
_C_SMALL = _C_BETA // LANES * LANES
SM_BETA = _C_BETA - _C_SMALL
SM_DECAY = _C_DECAY - _C_SMALL
_INPROJ_GROUPS = (("keys", 256, BF16), ("cmp", 256, BF16), ("small", LANES, F32), ("gqkv", 3 * GDN_W, F32),
                  ("ggate", GDN_W, BF16), ("merge", 2 * D_MODEL, BF16))
_INPROJ_WIDTH = sum(w for _, w, _ in _INPROJ_GROUPS)
_INPROJ_T_ROWS = NSA_Q_W + 4 * NSA_HEAD_DIM
_GATE_T_ROWS = 32
INPROJ_TM = 512
WPREP_TK = 128


def _wprep_body(w_ref, rows_ref, wt_ref, wg_ref):
    hd = NSA_HEAD_DIM
    cols = lambda c0, n: w_ref[:, c0:c0 + n]
    for j, c0 in enumerate((_C_SK, _C_WK, _C_SK + hd, _C_WK + hd)):
        rows_ref[:, j * hd:(j + 1) * hd] = cols(c0, hd).astype(BF16)
    c = 4 * hd
    for c0, n in ((_C_CK, 2 * NSA_KV_W), (_C_SMALL, LANES), (_C_GQKV, 3 * GDN_W), (_C_GGATE, GDN_W),
                  (_C_MERGE, 2 * D_MODEL)):
        rows_ref[:, c:c + n] = cols(c0, n).astype(BF16)
        c += n
    for j in range(NSA_Q_W // LANES):
        wt_ref[j * LANES:(j + 1) * LANES, :] = cols(_C_Q + j * LANES, LANES).T.astype(BF16)
    sv_t = cols(_C_SV, NSA_KV_W).T
    wv_t = cols(_C_WV, NSA_KV_W).T
    for j, t in enumerate((sv_t[:hd], wv_t[:hd], sv_t[hd:], wv_t[hd:])):
        wt_ref[NSA_Q_W + j * hd:NSA_Q_W + (j + 1) * hd, :] = t.astype(BF16)
    wg_ref[...] = cols(_C_GATE, LANES).T[:_GATE_T_ROWS].astype(BF16)


def _wprep(w_in):
    k = w_in.shape[0]
    tk = WPREP_TK
    return pl.pallas_call(
        _wprep_body,
        grid=(k // tk,),
        in_specs=[pl.BlockSpec((tk, IN_WIDTH), lambda i: (i, 0))],
        out_specs=[pl.BlockSpec((tk, _INPROJ_WIDTH), lambda i: (i, 0)),
                   pl.BlockSpec((_INPROJ_T_ROWS, tk), lambda i: (0, i)),
                   pl.BlockSpec((_GATE_T_ROWS, tk), lambda i: (0, i))],
        out_shape=[jax.ShapeDtypeStruct((k, _INPROJ_WIDTH), BF16), jax.ShapeDtypeStruct((_INPROJ_T_ROWS, k), BF16),
                   jax.ShapeDtypeStruct((_GATE_T_ROWS, k), BF16)],
        compiler_params=_cparams("parallel"),
        name="wprep",
    )(w_in)


def _inproj_body(x_ref, w_ref, wt_ref, wg_ref, keys_ref, cmp_ref, small_ref, gqkv_ref, ggate_ref, merge_ref,
                 qvt_ref, gt_ref, cmp_s):
    x = x_ref[...].astype(BF16)
    outs = (keys_ref, None, small_ref, gqkv_ref, ggate_ref, merge_ref)
    c0 = 0
    for ref, (name, width, _) in zip(outs, _INPROJ_GROUPS):
        for s in range(0, width, 512):
            e = min(s + 512, width)
            res = _dot(x, w_ref[:, c0 + s:c0 + e])
            if name == "cmp":
                for j in range(width // LANES):
                    cmp_s[j] = res[:, j * LANES:(j + 1) * LANES]
            else:
                ref[:, s:e] = res.astype(ref.dtype)
        c0 += width
    nblk = cmp_ref.shape[0]
    for l in range(CMP_STRIDE):
        for j in range(cmp_s.shape[0]):
            cmp_ref[:, l * 256 + j * LANES:l * 256 + (j + 1) * LANES] = (
                cmp_s[j, pl.ds(l, nblk, stride=CMP_STRIDE), :].astype(BF16))
    for s in range(0, _INPROJ_T_ROWS, 256):
        qvt_ref[s:s + 256, :] = _dot_nt(wt_ref[s:s + 256, :], x).astype(qvt_ref.dtype)
    gt_ref[...] = _dot_nt(wg_ref[...], x)


def _inproj(x2, w_rows, w_t, w_g):
    m = x2.shape[0]
    tm = INPROJ_TM
    row_major = [(n, wd, dt) for n, wd, dt in _INPROJ_GROUPS if n != "cmp"]
    specs = {n: (pl.BlockSpec((tm, wd), lambda i: (i, 0)), jax.ShapeDtypeStruct((m, wd), dt))
             for n, wd, dt in row_major}
    specs["cmp"] = (pl.BlockSpec((tm // CMP_STRIDE, CMP_STRIDE * 256), lambda i: (i, 0)),
                    jax.ShapeDtypeStruct((m // CMP_STRIDE, CMP_STRIDE * 256), BF16))
    order = [n for n, _, _ in _INPROJ_GROUPS]
    return pl.pallas_call(
        _inproj_body,
        grid=(m // tm,),
        in_specs=[pl.BlockSpec((tm, D_MODEL), lambda i: (i, 0)), _const_spec(w_rows.shape),
                  _const_spec(w_t.shape), _const_spec(w_g.shape)],
        out_specs=[specs[n][0] for n in order]
        + [pl.BlockSpec((_INPROJ_T_ROWS, tm), lambda i: (0, i)), pl.BlockSpec((_GATE_T_ROWS, tm), lambda i: (0, i))],
        out_shape=[specs[n][1] for n in order]
        + [jax.ShapeDtypeStruct((_INPROJ_T_ROWS, m), BF16), jax.ShapeDtypeStruct((_GATE_T_ROWS, m), F32)],
        scratch_shapes=[pltpu.VMEM((256 // LANES, tm, LANES), F32)],
        compiler_params=_cparams("parallel"),
        name="inproj",
    )(x2, w_rows, w_t, w_g)


def _compress_weights(cmp_pos, cmp_w2):
    hd, half = NSA_HEAD_DIM, CMP_LEN // 2
    posr = cmp_pos.reshape(2, 2, half, hd)
    post = jnp.broadcast_to(posr.transpose(1, 2, 0, 3)[:, :, :, None, :], (2, half, 2, 2, hd))
    post = jnp.concatenate([post.reshape(2, half * 4 * hd), jnp.zeros((6, half * 4 * hd), cmp_pos.dtype)], axis=0)
    w2sel = jnp.zeros((2, 2, 2 * hd, LANES), cmp_w2.dtype)
    for g in range(2):
        w2sel = w2sel.at[:, g, g * hd:(g + 1) * hd, :hd].set(cmp_w2)
    return post.astype(BF16), w2sel.reshape(4, 2 * hd, LANES).astype(BF16)


def _compress_body(t_ref, w1_ref, pos_ref, w2_ref, aug_ref, out_ref, w1e_s):
    hd, half_len = NSA_HEAD_DIM, CMP_LEN // 2

    @pl.when(pl.program_id(0) == 0)
    def _():
        w1e_s[...] = jnp.zeros(w1e_s.shape, BF16)
        for which in range(2):
            for half in range(2):
                for l in range(half_len):
                    blk = w1_ref[which, (half * half_len + l) * hd:(half * half_len + l + 1) * hd, :].astype(BF16)
                    for g in range(NSA_KV_GROUPS):
                        r0 = l * 256 + which * LANES + g * hd
                        c0 = half * 256 + which * LANES + g * hd
                        w1e_s[r0:r0 + hd, c0:c0 + hd] = blk

    p = _dot(t_ref[0], w1e_s[...])
    pp = _dot(pos_ref[...], w1e_s[...])
    nxt = pltpu.roll(p[:, 256:], p.shape[0] - 1, axis=0)
    pre = p[:, :256] + nxt + pp[0:1, :256] + pp[1:2, 256:]
    h = jax.nn.gelu(pre).astype(BF16)
    n_idx = lax.broadcasted_iota(jnp.int32, (p.shape[0], LANES), 0)
    real = n_idx < p.shape[0] - 1
    for which in range(2):
        hw = h[:, which * LANES:(which + 1) * LANES]
        for g in range(2):
            o = jnp.where(real, _dot(hw, w2_ref[which * 2 + g]) + aug_ref[which], 0.0)
            out_ref[0, which * 2 + g] = (o if which == 0 else o.T).astype(out_ref.dtype)


def _compress(t2, w1, post, w2sel, aug):
    b, nblk, _ = t2.shape
    return pl.pallas_call(
        _compress_body,
        grid=(b,),
        in_specs=[pl.BlockSpec((1, nblk, CMP_STRIDE * 256), lambda i: (i, 0, 0)),
                  _const_spec(w1.shape), _const_spec(post.shape), _const_spec(w2sel.shape), _const_spec(aug.shape)],
        out_specs=pl.BlockSpec((1, 4, nblk, LANES), lambda i: (i, 0, 0, 0)),
        out_shape=jax.ShapeDtypeStruct((b, 4, nblk, LANES), BF16),
        scratch_shapes=[pltpu.VMEM((CMP_STRIDE * 256, 512), BF16)],
        compiler_params=_cparams("arbitrary"),
        name="compress",
    )(t2, w1, post, w2sel, aug)


def _nsa_consts(s):
    hd, rep = NSA_HEAD_DIM, NSA_REP
    t = np.arange(s)
    kx_win = np.zeros((s + WIN_LEN, hd), np.float32)
    kx_win[WIN_LEN + t, AUG_POS_HI - hd] = t // 256
    kx_win[WIN_LEN + t, AUG_POS_LO - hd] = t % 256
    kx_win[:WIN_LEN, AUG_PAD - hd] = 1.0
    kx_sel = kx_win.copy()
    kx_sel[WIN_LEN + t, t // SLC_LEN] = 1.0
    vx_win = np.zeros((V_ROWS - hd, s + WIN_LEN), np.float32)
    vx_win[0, WIN_LEN:] = 1.0
    vx_sel = vx_win
    n_cmp = s // CMP_STRIDE
    cmp_aug = np.zeros((2, n_cmp, LANES), np.float32)
    end = np.arange(n_cmp) * CMP_STRIDE + CMP_LEN - 1
    cmp_aug[0, :, AUG_POS_HI] = end // 256
    cmp_aug[0, :, AUG_POS_LO] = end % 256
    qx = np.zeros((NSA_KV_GROUPS, LANES - AUG_POS_HI, rep * Q_TILE), np.float32)
    for h in range(NSA_HEADS):
        slope = 2.0 ** (-8.0 * (h + 1) / NSA_HEADS)
        lanes = slice((h % rep) * Q_TILE, (h % rep + 1) * Q_TILE)
        qx[h // rep, 0, lanes] = slope * 256.0
        qx[h // rep, 1, lanes] = slope
        qx[h // rep, AUG_PAD - AUG_POS_HI, lanes] = -BIG
    c0 = np.arange(n_cmp)[None, :] * CMP_STRIDE
    s0 = np.arange(s // SLC_LEN)[:, None] * SLC_LEN
    ov_t = ((c0 < s0 + SLC_LEN) & (c0 + CMP_LEN > s0)).astype(np.float32)
    ov_t[:, (s - CMP_LEN) // CMP_STRIDE + 1:] = 0.0
    kk = np.arange(Q_TILE)[:, None]
    qq = np.arange(Q_TILE)[None, :]
    causal = np.tile(np.where(kk <= qq, 0.0, NEG).astype(np.float32), (1, rep))
    after = np.tile(np.where(kk > qq, 0.0, NEG).astype(np.float32), (1, rep))
    j = jnp.asarray
    return dict(kx_sel=j(kx_sel, BF16), kx_win=j(kx_win, BF16), vx_sel=j(vx_sel, BF16), vx_win=j(vx_win, BF16),
                cmp_aug=j(cmp_aug), qx=j(qx), ov_t=j(ov_t), causal=j(causal), after=j(after))


def _nsa_body(qt_ref, k_ref, vt_ref, kc_ref, vct_ref, gt_ref, kxs_ref, kxw_ref, vxs_ref, vxw_ref, qx_ref, ovt_ref,
              causal_ref, after_ref, out_ref, ks_s, kw_s, vs_s, vw_s):
    hd, rep, tq = NSA_HEAD_DIM, NSA_REP, Q_TILE
    nq = rep * tq
    i = pl.program_id(2)

    @pl.when(i == 0)
    def _():
        keys = k_ref[0]
        ks_s[:WIN_LEN, :hd] = jnp.zeros((WIN_LEN, hd), BF16)
        ks_s[WIN_LEN:, :hd] = keys[:, :hd]
        ks_s[:, hd:] = kxs_ref[...]
        kw_s[:WIN_LEN, :hd] = jnp.zeros((WIN_LEN, hd), BF16)
        kw_s[WIN_LEN:, :hd] = keys[:, hd:]
        kw_s[:, hd:] = kxw_ref[...]
        vals = vt_ref[...]
        vs_s[:hd, :WIN_LEN] = jnp.zeros((hd, WIN_LEN), BF16)
        vs_s[:hd, WIN_LEN:] = vals[:hd]
        vs_s[hd:, :] = vxs_ref[...]
        vw_s[:hd, :WIN_LEN] = jnp.zeros((hd, WIN_LEN), BF16)
        vw_s[:hd, WIN_LEN:] = vals[hd:]
        vw_s[hd:, :] = vxw_ref[...]

    qt = qt_ref[...]
    q64 = jnp.concatenate([qt[r * hd:(r + 1) * hd, :] for r in range(rep)], axis=1).astype(F32) * (hd ** -0.5)
    qx = qx_ref[0]

    def q_aug(sel_rows):
        return jnp.concatenate([q64, sel_rows, qx], axis=0).astype(BF16)

    n_row = lax.broadcasted_iota(jnp.int32, (LANES, nq), 0)
    t_lane = i * tq + (lax.broadcasted_iota(jnp.int32, (LANES, nq), 1) & (tq - 1))
    valid = t_lane >= n_row * CMP_STRIDE + (CMP_LEN - 1)
    qa0 = q_aug(jnp.zeros((N_SLC, nq), F32))
    sc = jnp.where(valid, _dot(kc_ref[0, 0], qa0), NEG)
    mc = jnp.max(sc, axis=0, keepdims=True)
    ec = jnp.where(valid, jnp.exp(sc - mc), 0.0)
    lc = jnp.sum(ec, axis=0, keepdims=True)
    pc = ec * jnp.where(lc > 0.0, 1.0 / lc, 0.0)
    o_cmp = _dot(vct_ref[0, 0], pc.astype(BF16))[:hd]
    psum = pc[:, 0:tq] + pc[:, tq:2 * tq] + pc[:, 2 * tq:3 * tq] + pc[:, 3 * tq:4 * tq]
    score_t = _dot(ovt_ref[...], psum, precision=lax.Precision.HIGHEST)

    w0 = pl.multiple_of(i * tq, tq)
    s_w = _dot(kw_s[pl.ds(w0, WIN_LEN + tq), :], qa0)
    s_w = jnp.concatenate([s_w[:tq] + after_ref[...], s_w[tq:WIN_LEN], s_w[WIN_LEN:] + causal_ref[...]], axis=0)
    p_w = jnp.exp(s_w - jnp.max(s_w, axis=0, keepdims=True))
    acc_w = _dot(vw_s[:, pl.ds(w0, WIN_LEN + tq)], p_w.astype(BF16))
    o_win = acc_w[:hd] * (1.0 / acc_w[hd:hd + 1])

    jb = lax.broadcasted_iota(jnp.int32, (N_SLC, tq), 0)
    cur = (i * tq + lax.broadcasted_iota(jnp.int32, (N_SLC, tq), 1)) // SLC_LEN
    forced = (jb == 0) | (jb == cur) | (jb == cur - 1)
    score_t = jnp.where(forced, FORCE_SCORE, jnp.where(jb <= cur, score_t, -1.0))
    rank = jnp.zeros((N_SLC, tq), F32)
    for jp in range(N_SLC):
        other = score_t[jp:jp + 1, :]
        ge = jnp.where(other >= score_t, 1.0, 0.0)
        gt = jnp.where(other > score_t, 1.0, 0.0)
        rank = rank + jnp.where(jb > jp, ge, gt)
    sel = rank < float(SLC_TOPK)
    qa = q_aug(jnp.concatenate([jnp.where(sel, 0.0, -BIG)] * rep, axis=1))
    lo_blk = jnp.min(jnp.where(sel & (jb >= 2) & (jb <= cur), jb.astype(F32), float(N_SLC)))
    lo_key = (lo_blk.astype(jnp.int32) // 2) * tq

    e_key = i * tq - WIN_LEN
    t0 = pl.multiple_of(jnp.where(e_key > 0, WIN_LEN, 0), tq)
    s_main = _dot(ks_s[pl.ds(w0, WIN_LEN + tq), :], qa)
    s_s = jnp.concatenate([_dot(ks_s[pl.ds(t0, tq), :], qa), s_main[:WIN_LEN],
                           s_main[WIN_LEN:] + causal_ref[...]], axis=0)
    m_s = jnp.max(s_s, axis=0, keepdims=True)
    p_s = jnp.exp(s_s - m_s).astype(BF16)
    acc_s = _dot(vs_s[:, pl.ds(t0, tq)], p_s[:tq]) + _dot(vs_s[:, pl.ds(w0, WIN_LEN + tq)], p_s[tq:])

    def early_step(c, carry):
        m, acc = carry
        k0 = tq + c * SEL_KC
        start = pl.multiple_of(WIN_LEN + k0, tq)
        k_abs = k0 + lax.broadcasted_iota(jnp.int32, (SEL_KC, nq), 0)
        s = jnp.where(k_abs < e_key, _dot(ks_s[pl.ds(start, SEL_KC), :], qa), NEG)
        m_new = jnp.maximum(m, jnp.max(s, axis=0, keepdims=True))
        p = jnp.exp(s - m_new).astype(BF16)
        return m_new, acc * jnp.exp(m - m_new) + _dot(vs_s[:, pl.ds(start, SEL_KC)], p)

    c_hi = (e_key - tq + SEL_KC - 1) // SEL_KC
    c_lo = jnp.where(lo_key < e_key, (lo_key - tq) // SEL_KC, c_hi)
    _, acc_s = lax.fori_loop(c_lo, c_hi, early_step, (m_s, acc_s))
    o_slc = acc_s[:hd] * (1.0 / acc_s[hd:hd + 1])

    sg = jax.nn.sigmoid(gt_ref[...])
    grp = pl.program_id(1)
    gate = lambda br, r: jnp.where(grp == 0, sg[br * NSA_HEADS + r:br * NSA_HEADS + r + 1],
                                   sg[br * NSA_HEADS + rep + r:br * NSA_HEADS + rep + r + 1])
    for pair in range(rep // 2):
        halves = []
        for r in (2 * pair, 2 * pair + 1):
            lanes = slice(r * tq, (r + 1) * tq)
            halves.append(gate(0, r) * o_cmp[:, lanes] + gate(1, r) * o_slc[:, lanes] + gate(2, r) * o_win[:, lanes])
        out_ref[0, :, pair * LANES:(pair + 1) * LANES] = jnp.concatenate(halves, axis=0).T.astype(out_ref.dtype)


def _nsa(qvt, keys, cmp_kv, gt, consts, b, s):
    nt = s // Q_TILE
    c = consts
    in_specs = [
        pl.BlockSpec((2 * LANES, Q_TILE), lambda bi, g, i: (g, bi * nt + i)),
        pl.BlockSpec((1, s, LANES), lambda bi, g, i: (bi, 0, g)),
        pl.BlockSpec((LANES, s), lambda bi, g, i: (NSA_Q_W // LANES + g, bi)),
        pl.BlockSpec((1, 1, s // CMP_STRIDE, LANES), lambda bi, g, i: (bi, g, 0, 0)),
        pl.BlockSpec((1, 1, s // CMP_STRIDE, LANES), lambda bi, g, i: (bi, 2 + g, 0, 0)),
        pl.BlockSpec((_GATE_T_ROWS, Q_TILE), lambda bi, g, i: (0, bi * nt + i)),
        _const_spec(c["kx_sel"].shape), _const_spec(c["kx_win"].shape), _const_spec(c["vx_sel"].shape),
        _const_spec(c["vx_win"].shape),
        pl.BlockSpec((1,) + c["qx"].shape[1:], lambda bi, g, i: (g, 0, 0)),
        _const_spec(c["ov_t"].shape), _const_spec(c["causal"].shape), _const_spec(c["after"].shape),
    ]
    return pl.pallas_call(
        _nsa_body,
        grid=(b, NSA_KV_GROUPS, nt),
        in_specs=in_specs,
        out_specs=pl.BlockSpec((1, Q_TILE, 2 * LANES), lambda bi, g, i: (bi, i, g)),
        out_shape=jax.ShapeDtypeStruct((b, s, NSA_Q_W), BF16),
        scratch_shapes=[pltpu.VMEM((s + WIN_LEN, LANES), BF16), pltpu.VMEM((s + WIN_LEN, LANES), BF16),
                        pltpu.VMEM((V_ROWS, s + WIN_LEN), BF16), pltpu.VMEM((V_ROWS, s + WIN_LEN), BF16)],
        compiler_params=_cparams("parallel", "parallel", "arbitrary"),
        name="nsa",
    )(qvt, keys, qvt, cmp_kv, cmp_kv, gt, c["kx_sel"], c["kx_win"], c["vx_sel"], c["vx_win"], c["qx"], c["ov_t"],
      c["causal"], c["after"])


GDN_TS = 512
GDN_BLK = 128
GDN_HALO = 8
GDN_SCAN_ROWS = GDN_HEAD_DIM + GDN_CHUNK


def _gdn_intra_body(x_ref, prev_ref, small_ref, cw_ref, alog_ref, dtb_ref, lt_ref,
                    nq_ref, co_ref, eg_ref, xp_s, x_s, p_s, rhs_s, qg_s, aqk_s, kdt_s):
    ts, dh, nh, blk = GDN_TS, GDN_HEAD_DIM, GDN_HEADS, GDN_BLK
    i = pl.program_id(1)
    xp_s[0:GDN_HALO, :] = jnp.where(i == 0, 0.0, prev_ref[0])
    xp_s[GDN_HALO:, :] = x_ref[0]
    conv = cw_ref[0:1, :] * xp_s[pl.ds(GDN_HALO - GDN_CONV + 1, ts), :]
    for j in range(1, GDN_CONV):
        conv = conv + cw_ref[j:j + 1, :] * xp_s[pl.ds(GDN_HALO - GDN_CONV + 1 + j, ts), :]
    act = jax.nn.silu(conv)

    sm = small_ref[0]
    beta = jax.nn.sigmoid(sm)
    g = -jnp.exp(alog_ref[...]) * jax.nn.softplus(sm + dtb_ref[...])
    gcum = jnp.concatenate([_dot(lt_ref[...], g[r:r + blk], precision=lax.Precision.HIGHEST)
                            for r in range(0, ts, blk)], axis=0)
    eg = jnp.exp(gcum)
    eg_ref[0] = eg

    ri = lax.broadcasted_iota(jnp.int32, (blk, blk), 0)
    ci = lax.broadcasted_iota(jnp.int32, (blk, blk), 1)
    same = (ri // GDN_CHUNK) == (ci // GDN_CHUNK)
    causal = same & (ri >= ci)
    strict = same & (ri > ci)
    eye = (ri == ci).astype(F32)

    for pb in range(ts // blk):
        rows = slice(pb * blk, (pb + 1) * blk)
        gc = gcum[rows]
        gc_t = gc.T
        first = lax.broadcasted_iota(jnp.int32, (blk, LANES), 0) < GDN_CHUNK
        g_last = jnp.where(first, gc[GDN_CHUNK - 1:GDN_CHUNK, :], gc[blk - 1:blk, :])
        e_dec = jnp.exp(g_last - gc)
        for h in range(nh):
            q = act[rows, h * dh:(h + 1) * dh]
            k = act[rows, GDN_W + h * dh:GDN_W + (h + 1) * dh]
            v = act[rows, 2 * GDN_W + h * dh:2 * GDN_W + (h + 1) * dh]
            q = q * lax.rsqrt(jnp.sum(q * q, axis=1, keepdims=True) + RMS_EPS) * (dh ** -0.5)
            k = k * lax.rsqrt(jnp.sum(k * k, axis=1, keepdims=True) + RMS_EPS)
            b_col = beta[rows, SM_BETA + h:SM_BETA + h + 1]
            eg_col = eg[rows, SM_DECAY + h:SM_DECAY + h + 1]
            gdiff = gc[:, SM_DECAY + h:SM_DECAY + h + 1] - gc_t[SM_DECAY + h:SM_DECAY + h + 1, :]
            decay = jnp.exp(jnp.where(causal, gdiff, NEG))
            kb = k * b_col
            kbf, kf, qf = kb.astype(BF16), k.astype(BF16), q.astype(BF16)
            a = jnp.where(strict, -_dot_nt(kbf, kf) * decay, 0.0)
            c = pb * nh + h
            x_s[c] = a.astype(BF16)
            p_s[c] = eye + a
            rhs_s[c] = jnp.concatenate([v * b_col, kb * eg_col], axis=1).astype(BF16)
            qg_s[c] = q * eg_col
            aqk_s[c] = jnp.where(causal, _dot_nt(qf, kf) * decay, 0.0).astype(BF16)
            kdt_s[c] = (k * e_dec[:, SM_DECAY + h:SM_DECAY + h + 1]).T.astype(BF16)

    n_chain = (ts // blk) * nh
    for _ in range(5):
        for c in range(n_chain):
            xb = x_s[c]
            xn = _dot(xb, xb).astype(BF16)
            x_s[c] = xn
            p = p_s[c]
            p_s[c] = p + _dot(p.astype(BF16), xn)

    tok_half = lax.broadcasted_iota(jnp.int32, (blk, blk), 1) // GDN_CHUNK
    for c in range(n_chain):
        pb, h = divmod(c, nh)
        uw = _dot(p_s[c].astype(BF16), rhs_s[c]).astype(BF16)
        a1 = _dot(aqk_s[c], uw)
        q_loc = qg_s[c] - a1[:, dh:]
        kdt = kdt_s[c]
        for half in range(blk // GDN_CHUNK):
            k1 = _dot(jnp.where(tok_half == half, kdt, jnp.zeros_like(kdt)), uw)
            n = pb * (blk // GDN_CHUNK) + half
            rows = slice(half * GDN_CHUNK, (half + 1) * GDN_CHUNK)
            nq_ref[0, h, n, :dh, :] = (-k1[:, dh:]).astype(BF16)
            nq_ref[0, h, n, dh:, :] = q_loc[rows].astype(BF16)
            co_ref[0, h, n, :dh, :] = k1[:, :dh].astype(BF16)
            co_ref[0, h, n, dh:, :] = a1[rows, :dh].astype(BF16)


def _gdn_intra(gqkv, small, conv_w, alog_l, dtb_l, lt):
    b, s, _ = gqkv.shape
    ts, nh, dh = GDN_TS, GDN_HEADS, GDN_HEAD_DIM
    n_chain = (ts // GDN_BLK) * nh
    sq = lambda dt: pltpu.VMEM((n_chain, GDN_BLK, GDN_BLK), dt)
    cspec = lambda: pl.BlockSpec((1, nh, ts // GDN_CHUNK, GDN_SCAN_ROWS, dh), lambda bi, i: (bi, 0, i, 0, 0))
    cshape = jax.ShapeDtypeStruct((b, nh, s // GDN_CHUNK, GDN_SCAN_ROWS, dh), BF16)
    return pl.pallas_call(
        _gdn_intra_body,
        grid=(b, s // ts),
        in_specs=[
            pl.BlockSpec((1, ts, 3 * GDN_W), lambda bi, i: (bi, i, 0)),
            pl.BlockSpec((1, GDN_HALO, 3 * GDN_W), lambda bi, i: (bi, jnp.maximum(i * (ts // GDN_HALO) - 1, 0), 0)),
            pl.BlockSpec((1, ts, LANES), lambda bi, i: (bi, i, 0)),
            _const_spec(conv_w.shape), _const_spec(alog_l.shape), _const_spec(dtb_l.shape), _const_spec(lt.shape),
        ],
        out_specs=[cspec(), cspec(), pl.BlockSpec((1, ts, LANES), lambda bi, i: (bi, i, 0))],
        out_shape=[cshape, cshape, jax.ShapeDtypeStruct((b, s, LANES), F32)],
        scratch_shapes=[pltpu.VMEM((ts + GDN_HALO, 3 * GDN_W), F32), sq(BF16), sq(F32),
                        pltpu.VMEM((n_chain, GDN_BLK, 2 * dh), BF16), sq(F32), sq(BF16), sq(BF16)],
        compiler_params=_cparams("parallel", "parallel"),
        name="gdn_intra",
    )(gqkv, gqkv, small, conv_w, alog_l, dtb_l, lt)


def _gdn_scan_body(nq_ref, co_ref, eg_ref, gate_ref, nw_ref, out_ref, st_s):
    nh, dh, ck = GDN_HEADS, GDN_HEAD_DIM, GDN_CHUNK
    st_s[...] = jnp.zeros(st_s.shape, F32)

    def chunk(n, carry):
        r0 = pl.multiple_of(n * ck, ck)
        d_row = eg_ref[0, pl.ds(r0 + ck - 1, 1), :]
        for h in range(nh):
            st = st_s[h]
            res = _dot(nq_ref[0, h, n], st.astype(BF16)) + co_ref[0, h, n].astype(F32)
            st_s[h] = st * d_row[:, SM_DECAY + h:SM_DECAY + h + 1] + res[:dh]
            o = res[dh:]
            ms = jnp.mean(o * o, axis=1, keepdims=True)
            gt = gate_ref[0, pl.ds(r0, ck), h * dh:(h + 1) * dh].astype(F32)
            out_ref[0, pl.ds(r0, ck), h * dh:(h + 1) * dh] = (
                o * lax.rsqrt(ms + RMS_EPS) * nw_ref[...] * jax.nn.silu(gt)).astype(out_ref.dtype)
        return carry

    lax.fori_loop(0, nq_ref.shape[2], chunk, 0, unroll=2)


def _gdn_scan(nq, co, eg, ggate, norm_w):
    b, nh, nc, rows, dh = nq.shape
    s = nc * GDN_CHUNK
    cspec = lambda: pl.BlockSpec((1, nh, nc, rows, dh), lambda bi: (bi, 0, 0, 0, 0))
    return pl.pallas_call(
        _gdn_scan_body,
        grid=(b,),
        in_specs=[cspec(), cspec(),
                  pl.BlockSpec((1, s, LANES), lambda bi: (bi, 0, 0)),
                  pl.BlockSpec((1, s, GDN_W), lambda bi: (bi, 0, 0)),
                  _const_spec(norm_w.shape)],
        out_specs=pl.BlockSpec((1, s, GDN_W), lambda bi: (bi, 0, 0)),
        out_shape=jax.ShapeDtypeStruct((b, s, GDN_W), BF16),
        scratch_shapes=[pltpu.VMEM((nh, dh, dh), F32)],
        compiler_params=_cparams("parallel"),
        name="gdn_scan",
    )(nq, co, eg, ggate, norm_w)


MERGE_TM = 512


def _layer_norm(y, g, b):
    mu = jnp.mean(y, axis=1, keepdims=True)
    d = y - mu
    var = jnp.mean(d * d, axis=1, keepdims=True)
    return d * lax.rsqrt(var + LN_EPS) * g + b


def _merge_body(x_ref, oa_ref, ob_ref, ga_ref, gb_ref, wa_ref, wb_ref, wo_ref, g_ref, b_ref, y_ref, yb_ref):
    ya = _dot(oa_ref[...], wa_ref[...])
    yb = _dot(ob_ref[...], wb_ref[...])
    mixin = jax.nn.sigmoid(ga_ref[...].astype(F32)) * ya + jax.nn.sigmoid(gb_ref[...].astype(F32)) * yb
    mix = _dot(mixin.astype(BF16), wo_ref[...])
    y = _layer_norm(DEEPNORM_ALPHA * x_ref[...] + mix, g_ref[...], b_ref[...])
    y_ref[...] = y
    yb_ref[...] = y.astype(BF16)


def _merge(x2, oa, ob, mgate, wa, wb, wo, g, b):
    m = x2.shape[0]
    tm, d = MERGE_TM, D_MODEL
    row = lambda wd, col=0: pl.BlockSpec((tm, wd), lambda i, col=col: (i, col))
    return pl.pallas_call(
        _merge_body,
        grid=(m // tm,),
        in_specs=[row(d), row(NSA_Q_W), row(GDN_W), row(d, 0), row(d, 1),
                  _const_spec(wa.shape), _const_spec(wb.shape), _const_spec(wo.shape),
                  _const_spec(g.shape), _const_spec(b.shape)],
        out_specs=[row(d), row(d)],
        out_shape=[jax.ShapeDtypeStruct((m, d), F32), jax.ShapeDtypeStruct((m, d), BF16)],
        compiler_params=_cparams("parallel"),
        name="merge",
    )(x2, oa, ob, mgate, mgate, wa, wb, wo, g, b)


FFN_TM = 512
FFN_HALO = 16
FFN_CK = 256


def _ffn_body(x_ref, xb_ref, prev_ref, wu_ref, cw_ref, wd_ref, g_ref, b_ref, out_ref, act_s,
              *, tiles_per_seq):
    i = pl.program_id(0)
    prev = prev_ref[...]
    prev = jnp.where(i % tiles_per_seq == 0, jnp.zeros_like(prev), prev)
    xc = jnp.concatenate([prev, xb_ref[...]], axis=0)

    def conv(h, c0):
        out = cw_ref[FFN_CONV - 1:FFN_CONV, c0:c0 + FFN_CK] * h[FFN_HALO:]
        for j in range(FFN_CONV - 1):
            shifted = pltpu.roll(h, FFN_CONV - 1 - j, axis=0)[FFN_HALO:]
            out = out + cw_ref[j:j + 1, c0:c0 + FFN_CK] * shifted
        return out

    for c in range(FFN_DIM // FFN_CK):
        c0 = c * FFN_CK
        hg = conv(_dot(xc, wu_ref[:, c0:c0 + FFN_CK]), c0)
        hv = conv(_dot(xc, wu_ref[:, FFN_DIM + c0:FFN_DIM + c0 + FFN_CK]), FFN_DIM + c0)
        act_s[:, c0:c0 + FFN_CK] = (jax.nn.silu(hg) * hv).astype(BF16)
    f = _dot(act_s[...], wd_ref[...])
    out_ref[...] = _layer_norm(DEEPNORM_ALPHA * x_ref[...] + f, g_ref[...], b_ref[...])


def _ffn(x1, x1b, wu, cw, wd, g, b, seq):
    m = x1.shape[0]
    tm, d = FFN_TM, D_MODEL
    return pl.pallas_call(
        functools.partial(_ffn_body, tiles_per_seq=seq // tm),
        grid=(m // tm,),
        in_specs=[pl.BlockSpec((tm, d), lambda i: (i, 0)),
                  pl.BlockSpec((tm, d), lambda i: (i, 0)),
                  pl.BlockSpec((FFN_HALO, d), lambda i: (jnp.maximum(i * (tm // FFN_HALO) - 1, 0), 0)),
                  _const_spec(wu.shape), _const_spec(cw.shape), _const_spec(wd.shape), _const_spec(g.shape), _const_spec(b.shape)],
        out_specs=pl.BlockSpec((tm, d), lambda i: (i, 0)),
        out_shape=jax.ShapeDtypeStruct((m, d), F32),
        scratch_shapes=[pltpu.VMEM((tm, FFN_DIM), BF16)],
        compiler_params=_cparams("parallel"),
        name="ffn",
    )(x1, x1b, x1b, wu, cw, wd, g, b)


def _lane_vec(vals, lane0):
    return jnp.zeros((1, LANES), F32).at[0, lane0:lane0 + vals.shape[0]].set(vals.astype(F32))


def _layer(x, w_in, cmp_pos, cmp_w1, cmp_w2, w_nsa_out, gdn_conv_w, gdn_a_log, gdn_dt_bias, gdn_norm_w,
           w_gdn_out, w_o, ln1_g, ln1_b, ffn_w_up, ffn_conv_w, ffn_w_down, ln2_g, ln2_b):
    b, s, d = x.shape
    m = b * s
    x2 = x.reshape(m, d)
    keys, cmpkv, small, gqkv, ggate, mgate, qvt, gt = _inproj(x2, *_wprep(w_in))

    consts = _nsa_consts(s)
    post, w2sel = _compress_weights(cmp_pos, cmp_w2)
    cmp_kv = _compress(cmpkv.reshape(b, s // CMP_STRIDE, CMP_STRIDE * 256), cmp_w1, post, w2sel, consts["cmp_aug"])
    o_nsa = _nsa(qvt, keys.reshape(b, s, 256), cmp_kv, gt, consts, b, s)

    ck = GDN_CHUNK
    tri = np.tril(np.ones((ck, ck), np.float32))
    lt = jnp.asarray(np.kron(np.eye(GDN_BLK // ck, dtype=np.float32), tri))
    nq, co, eg = _gdn_intra(gqkv.reshape(b, s, 3 * GDN_W), small.reshape(b, s, LANES), gdn_conv_w,
                                        _lane_vec(gdn_a_log, SM_DECAY), _lane_vec(gdn_dt_bias, SM_DECAY), lt)
    o_gdn = _gdn_scan(nq, co, eg, ggate.reshape(b, s, GDN_W), gdn_norm_w.reshape(1, GDN_HEAD_DIM))

    x1, x1b = _merge(x2, o_nsa.reshape(m, NSA_Q_W), o_gdn.reshape(m, GDN_W), mgate,
                     w_nsa_out.astype(BF16), w_gdn_out.astype(BF16), w_o.astype(BF16),
                     ln1_g.reshape(1, d), ln1_b.reshape(1, d))
    out = _ffn(x1, x1b, ffn_w_up.astype(BF16), ffn_conv_w, ffn_w_down.astype(BF16),
               ln2_g.reshape(1, d), ln2_b.reshape(1, d), s)
    return out.reshape(b, s, d)


def kernel(x, w_in, nsa_cmp_pos, nsa_cmp_w1, nsa_cmp_w2, w_nsa_out, gdn_conv_w, gdn_a_log, gdn_dt_bias, gdn_norm_w, w_gdn_out, w_o, ln1_g, ln1_b, ffn_w_up, ffn_conv_w, ffn_w_down, ln2_g, ln2_b):
    for l in range(DEPTH):
        x = _layer(x, w_in[l], nsa_cmp_pos[l], nsa_cmp_w1[l], nsa_cmp_w2[l], w_nsa_out[l], gdn_conv_w[l],
                   gdn_a_log[l], gdn_dt_bias[l], gdn_norm_w[l], w_gdn_out[l], w_o[l], ln1_g[l], ln1_b[l],
                   ffn_w_up[l], ffn_conv_w[l], ffn_w_down[l], ln2_g[l], ln2_b[l])
    return x
```

```python
import functools

import numpy as np
import jax
import jax.numpy as jnp
from jax import lax
from jax.experimental import pallas as pl
from jax.experimental.pallas import tpu as pltpu

F32 = jnp.float32
BF16 = jnp.bfloat16

D_MODEL = 1024
NSA_HEADS = 8
NSA_KV_GROUPS = 2
NSA_REP = NSA_HEADS // NSA_KV_GROUPS
NSA_HEAD_DIM = 64
CMP_LEN = 32
CMP_STRIDE = 16
SLC_LEN = 64
SLC_TOPK = 8
WIN_LEN = 512
FORCE_SCORE = 1.0e4
NEG = -1.0e30
GDN_HEADS = 4
GDN_HEAD_DIM = 128
GDN_CONV = 4
GDN_CHUNK = 64
FFN_DIM = 2816
FFN_CONV = 3
DEPTH = 1
DEEPNORM_ALPHA = (2.0 * DEPTH) ** 0.25
LN_EPS = 1e-5
RMS_EPS = 1e-6

NSA_Q_W = NSA_HEADS * NSA_HEAD_DIM
NSA_KV_W = NSA_KV_GROUPS * NSA_HEAD_DIM
GDN_W = GDN_HEADS * GDN_HEAD_DIM

LANES = 128
VMEM_LIMIT_BYTES = 56 * 1024 * 1024

AUG_SEL0 = 64
AUG_POS_HI = 96
AUG_POS_LO = 97
AUG_PAD = 98
BIG = 2.0 ** 100
Q_TILE = 128
N_SLC = 32
V_ROWS = 80
SEL_KC = 512
NSA_SUB = 2

NT_DIMS = (((1,), (1,)), ((), ()))


def _dot(a, b, **kw):
    return jnp.dot(a, b, preferred_element_type=F32, **kw)


def _dot_nt(a, b, **kw):
    return lax.dot_general(a, b, NT_DIMS, preferred_element_type=F32, **kw)


def _cparams(*sem):
    return pltpu.CompilerParams(dimension_semantics=sem, vmem_limit_bytes=VMEM_LIMIT_BYTES)


def _const_spec(shape):
    nd = len(shape)
    return pl.BlockSpec(shape, lambda *_: (0,) * nd, pipeline_mode=pl.Buffered(1))


_IN_WIDTHS = (NSA_Q_W,) + (NSA_KV_W,) * 6 + (3 * NSA_HEADS, 3 * GDN_W, GDN_HEADS, GDN_HEADS, GDN_W, 2 * D_MODEL)
(_C_Q, _C_CK, _C_CV, _C_SK, _C_SV, _C_WK, _C_WV, _C_GATE, _C_GQKV, _C_BETA, _C_DECAY, _C_GGATE, _C_MERGE,
 IN_WIDTH) = (int(v) for v in np.cumsum((0,) + _IN_WIDTHS))
_C_SMALL = _C_BETA // LANES * LANES
SM_BETA = _C_BETA - _C_SMALL
SM_DECAY = _C_DECAY - _C_SMALL
_INPROJ_GROUPS = (("keys", 256, BF16), ("cmp", 256, BF16), ("small", LANES, F32), ("gqkv", 3 * GDN_W, F32),
                  ("ggate", GDN_W, BF16), ("merge", 2 * D_MODEL, BF16))
_INPROJ_WIDTH = sum(w for _, w, _ in _INPROJ_GROUPS)
_INPROJ_T_ROWS = NSA_Q_W + 4 * NSA_HEAD_DIM
_GATE_T_ROWS = 32
INPROJ_TM = 512
WPREP_TK = 128


def _wprep_body(w_ref, rows_ref, wt_ref, wg_ref):
    hd = NSA_HEAD_DIM
    cols = lambda c0, n: w_ref[:, c0:c0 + n]
    for j, c0 in enumerate((_C_SK, _C_WK, _C_SK + hd, _C_WK + hd)):
        rows_ref[:, j * hd:(j + 1) * hd] = cols(c0, hd).astype(BF16)
    c = 4 * hd
    for c0, n in ((_C_CK, 2 * NSA_KV_W), (_C_SMALL, LANES), (_C_GQKV, 3 * GDN_W), (_C_GGATE, GDN_W),
                  (_C_MERGE, 2 * D_MODEL)):
        rows_ref[:, c:c + n] = cols(c0, n).astype(BF16)
        c += n
    for j in range(NSA_Q_W // LANES):
        wt_ref[j * LANES:(j + 1) * LANES, :] = cols(_C_Q + j * LANES, LANES).T.astype(BF16)
    sv_t = cols(_C_SV, NSA_KV_W).T
    wv_t = cols(_C_WV, NSA_KV_W).T
    for j, t in enumerate((sv_t[:hd], wv_t[:hd], sv_t[hd:], wv_t[hd:])):
        wt_ref[NSA_Q_W + j * hd:NSA_Q_W + (j + 1) * hd, :] = t.astype(BF16)
    wg_ref[...] = cols(_C_GATE, LANES).T[:_GATE_T_ROWS].astype(BF16)


def _wprep(w_in):
    k = w_in.shape[0]
    tk = WPREP_TK
    return pl.pallas_call(
        _wprep_body,
        grid=(k // tk,),
        in_specs=[pl.BlockSpec((tk, IN_WIDTH), lambda i: (i, 0))],
        out_specs=[pl.BlockSpec((tk, _INPROJ_WIDTH), lambda i: (i, 0)),
                   pl.BlockSpec((_INPROJ_T_ROWS, tk), lambda i: (0, i)),
                   pl.BlockSpec((_GATE_T_ROWS, tk), lambda i: (0, i))],
        out_shape=[jax.ShapeDtypeStruct((k, _INPROJ_WIDTH), BF16), jax.ShapeDtypeStruct((_INPROJ_T_ROWS, k), BF16),
                   jax.ShapeDtypeStruct((_GATE_T_ROWS, k), BF16)],
        compiler_params=_cparams("parallel"),
        name="wprep",
    )(w_in)


def _inproj_body(x_ref, w_ref, wt_ref, wg_ref, keys_ref, cmp_ref, small_ref, gqkv_ref, ggate_ref, merge_ref,
                 qvt_ref, gt_ref, cmp_s):
    x = x_ref[...].astype(BF16)
    outs = (keys_ref, None, small_ref, gqkv_ref, ggate_ref, merge_ref)
    c0 = 0
    for ref, (name, width, _) in zip(outs, _INPROJ_GROUPS):
        for s in range(0, width, 512):
            e = min(s + 512, width)
            res = _dot(x, w_ref[:, c0 + s:c0 + e])
            if name == "cmp":
                for j in range(width // LANES):
                    cmp_s[j] = res[:, j * LANES:(j + 1) * LANES]
            else:
                ref[:, s:e] = res.astype(ref.dtype)
        c0 += width
    nblk = cmp_ref.shape[0]
    for l in range(CMP_STRIDE):
        for j in range(cmp_s.shape[0]):
            cmp_ref[:, l * 256 + j * LANES:l * 256 + (j + 1) * LANES] = (
                cmp_s[j, pl.ds(l, nblk, stride=CMP_STRIDE), :].astype(BF16))
    for s in range(0, _INPROJ_T_ROWS, 256):
        qvt_ref[s:s + 256, :] = _dot_nt(wt_ref[s:s + 256, :], x).astype(qvt_ref.dtype)
    gt_ref[...] = _dot_nt(wg_ref[...], x)


def _inproj(x2, w_rows, w_t, w_g):
    m = x2.shape[0]
    tm = INPROJ_TM
    row_major = [(n, wd, dt) for n, wd, dt in _INPROJ_GROUPS if n != "cmp"]
    specs = {n: (pl.BlockSpec((tm, wd), lambda i: (i, 0)), jax.ShapeDtypeStruct((m, wd), dt))
             for n, wd, dt in row_major}
    specs["cmp"] = (pl.BlockSpec((tm // CMP_STRIDE, CMP_STRIDE * 256), lambda i: (i, 0)),
                    jax.ShapeDtypeStruct((m // CMP_STRIDE, CMP_STRIDE * 256), BF16))
    order = [n for n, _, _ in _INPROJ_GROUPS]
    return pl.pallas_call(
        _inproj_body,
        grid=(m // tm,),
        in_specs=[pl.BlockSpec((tm, D_MODEL), lambda i: (i, 0)), _const_spec(w_rows.shape),
                  _const_spec(w_t.shape), _const_spec(w_g.shape)],
        out_specs=[specs[n][0] for n in order]
        + [pl.BlockSpec((_INPROJ_T_ROWS, tm), lambda i: (0, i)), pl.BlockSpec((_GATE_T_ROWS, tm), lambda i: (0, i))],
        out_shape=[specs[n][1] for n in order]
        + [jax.ShapeDtypeStruct((_INPROJ_T_ROWS, m), BF16), jax.ShapeDtypeStruct((_GATE_T_ROWS, m), F32)],
        scratch_shapes=[pltpu.VMEM((256 // LANES, tm, LANES), F32)],
        compiler_params=_cparams("parallel"),
        name="inproj",
    )(x2, w_rows, w_t, w_g)


def _compress_weights(cmp_pos, cmp_w2):
    hd, half = NSA_HEAD_DIM, CMP_LEN // 2
    posr = cmp_pos.reshape(2, 2, half, hd)
    post = jnp.broadcast_to(posr.transpose(1, 2, 0, 3)[:, :, :, None, :], (2, half, 2, 2, hd))
    post = jnp.concatenate([post.reshape(2, half * 4 * hd), jnp.zeros((6, half * 4 * hd), cmp_pos.dtype)], axis=0)
    w2sel = jnp.zeros((2, 2, 2 * hd, LANES), cmp_w2.dtype)
    for g in range(2):
        w2sel = w2sel.at[:, g, g * hd:(g + 1) * hd, :hd].set(cmp_w2)
    return post.astype(BF16), w2sel.reshape(4, 2 * hd, LANES).astype(BF16)


def _compress_body(t_ref, w1_ref, pos_ref, w2_ref, aug_ref, out_ref, w1e_s):
    hd, half_len = NSA_HEAD_DIM, CMP_LEN // 2

    @pl.when(pl.program_id(0) == 0)
    def _():
        w1e_s[...] = jnp.zeros(w1e_s.shape, BF16)
        for which in range(2):
            for half in range(2):
                for l in range(half_len):
                    blk = w1_ref[which, (half * half_len + l) * hd:(half * half_len + l + 1) * hd, :].astype(BF16)
                    for g in range(NSA_KV_GROUPS):
                        r0 = l * 256 + which * LANES + g * hd
                        c0 = half * 256 + which * LANES + g * hd
                        w1e_s[r0:r0 + hd, c0:c0 + hd] = blk

    p = _dot(t_ref[0], w1e_s[...])
    pp = _dot(pos_ref[...], w1e_s[...])
    nxt = pltpu.roll(p[:, 256:], p.shape[0] - 1, axis=0)
    pre = p[:, :256] + nxt + pp[0:1, :256] + pp[1:2, 256:]
    h = jax.nn.gelu(pre).astype(BF16)
    n_idx = lax.broadcasted_iota(jnp.int32, (p.shape[0], LANES), 0)
    real = n_idx < p.shape[0] - 1
    for which in range(2):
        hw = h[:, which * LANES:(which + 1) * LANES]
        for g in range(2):
            o = jnp.where(real, _dot(hw, w2_ref[which * 2 + g]) + aug_ref[which], 0.0)
            out_ref[0, which * 2 + g] = (o if which == 0 else o.T).astype(out_ref.dtype)


def _compress(t2, w1, post, w2sel, aug):
    b, nblk, _ = t2.shape
    return pl.pallas_call(
        _compress_body,
        grid=(b,),
        in_specs=[pl.BlockSpec((1, nblk, CMP_STRIDE * 256), lambda i: (i, 0, 0)),
                  _const_spec(w1.shape), _const_spec(post.shape), _const_spec(w2sel.shape), _const_spec(aug.shape)],
        out_specs=pl.BlockSpec((1, 4, nblk, LANES), lambda i: (i, 0, 0, 0)),
        out_shape=jax.ShapeDtypeStruct((b, 4, nblk, LANES), BF16),
        scratch_shapes=[pltpu.VMEM((CMP_STRIDE * 256, 512), BF16)],
        compiler_params=_cparams("arbitrary"),
        name="compress",
    )(t2, w1, post, w2sel, aug)


def _nsa_consts(s):
    hd, rep = NSA_HEAD_DIM, NSA_REP
    t = np.arange(s)
    kx_win = np.zeros((s + WIN_LEN, hd), np.float32)
    kx_win[WIN_LEN + t, AUG_POS_HI - hd] = t // 256
    kx_win[WIN_LEN + t, AUG_POS_LO - hd] = t % 256
    kx_win[:WIN_LEN, AUG_PAD - hd] = 1.0
    kx_sel = kx_win.copy()
    kx_sel[WIN_LEN + t, t // SLC_LEN] = 1.0
    vx_win = np.zeros((V_ROWS - hd, s + WIN_LEN), np.float32)
    vx_win[0, WIN_LEN:] = 1.0
    vx_sel = vx_win
    n_cmp = s // CMP_STRIDE
    cmp_aug = np.zeros((2, n_cmp, LANES), np.float32)
    end = np.arange(n_cmp) * CMP_STRIDE + CMP_LEN - 1
    cmp_aug[0, :, AUG_POS_HI] = end // 256
    cmp_aug[0, :, AUG_POS_LO] = end % 256
    qx = np.zeros((NSA_KV_GROUPS, LANES - AUG_POS_HI, rep * Q_TILE), np.float32)
    for h in range(NSA_HEADS):
        slope = 2.0 ** (-8.0 * (h + 1) / NSA_HEADS)
        lanes = slice((h % rep) * Q_TILE, (h % rep + 1) * Q_TILE)
        qx[h // rep, 0, lanes] = slope * 256.0
        qx[h // rep, 1, lanes] = slope
        qx[h // rep, AUG_PAD - AUG_POS_HI, lanes] = -BIG
    c0 = np.arange(n_cmp)[None, :] * CMP_STRIDE
    s0 = np.arange(s // SLC_LEN)[:, None] * SLC_LEN
    ov_t = ((c0 < s0 + SLC_LEN) & (c0 + CMP_LEN > s0)).astype(np.float32)
    ov_t[:, (s - CMP_LEN) // CMP_STRIDE + 1:] = 0.0
    kk = np.arange(Q_TILE)[:, None]
    qq = np.arange(Q_TILE)[None, :]
    causal = np.tile(np.where(kk <= qq, 0.0, NEG).astype(np.float32), (1, rep))
    after = np.tile(np.where(kk > qq, 0.0, NEG).astype(np.float32), (1, rep))
    j = jnp.asarray
    return dict(kx_sel=j(kx_sel, BF16), kx_win=j(kx_win, BF16), vx_sel=j(vx_sel, BF16), vx_win=j(vx_win, BF16),
                cmp_aug=j(cmp_aug), qx=j(qx), ov_t=j(ov_t), causal=j(causal), after=j(after))


def _nsa_body(qt_ref, k_ref, vt_ref, kc_ref, vct_ref, gt_ref, kxs_ref, kxw_ref, vxs_ref, vxw_ref, qx_ref, ovt_ref,
              causal_ref, after_ref, out_ref, ks_s, kw_s, vs_s, vw_s):
    hd, rep, tq = NSA_HEAD_DIM, NSA_REP, Q_TILE
    nq = rep * tq
    i = pl.program_id(2)

    @pl.when(i == 0)
    def _():
        keys = k_ref[0]
        ks_s[:WIN_LEN, :hd] = jnp.zeros((WIN_LEN, hd), BF16)
        ks_s[WIN_LEN:, :hd] = keys[:, :hd]
        ks_s[:, hd:] = kxs_ref[...]
        kw_s[:WIN_LEN, :hd] = jnp.zeros((WIN_LEN, hd), BF16)
        kw_s[WIN_LEN:, :hd] = keys[:, hd:]
        kw_s[:, hd:] = kxw_ref[...]
        vals = vt_ref[...]
        vs_s[:hd, :WIN_LEN] = jnp.zeros((hd, WIN_LEN), BF16)
        vs_s[:hd, WIN_LEN:] = vals[:hd]
        vs_s[hd:, :] = vxs_ref[...]
        vw_s[:hd, :WIN_LEN] = jnp.zeros((hd, WIN_LEN), BF16)
        vw_s[:hd, WIN_LEN:] = vals[hd:]
        vw_s[hd:, :] = vxw_ref[...]

    qx = qx_ref[0]
    sg_all = jax.nn.sigmoid(gt_ref[...])
    grp = pl.program_id(1)

    def front(sub):
        it = i * NSA_SUB + sub
        qt = qt_ref[:, sub * tq:(sub + 1) * tq]
        q64 = jnp.concatenate([qt[r * hd:(r + 1) * hd, :] for r in range(rep)], axis=1).astype(F32) * (hd ** -0.5)

        def q_aug(sel_rows):
            return jnp.concatenate([q64, sel_rows, qx], axis=0).astype(BF16)

        n_row = lax.broadcasted_iota(jnp.int32, (LANES, nq), 0)
        t_lane = it * tq + (lax.broadcasted_iota(jnp.int32, (LANES, nq), 1) & (tq - 1))
        valid = t_lane >= n_row * CMP_STRIDE + (CMP_LEN - 1)
        qa0 = q_aug(jnp.zeros((N_SLC, nq), F32))
        sc = jnp.where(valid, _dot(kc_ref[0, 0], qa0), NEG)
        mc = jnp.max(sc, axis=0, keepdims=True)
        ec = jnp.where(valid, jnp.exp(sc - mc), 0.0)
        lc = jnp.sum(ec, axis=0, keepdims=True)
        pc = ec * jnp.where(lc > 0.0, 1.0 / lc, 0.0)
        o_cmp = _dot(vct_ref[0, 0], pc.astype(BF16))[:hd]
        psum = pc[:, 0:tq] + pc[:, tq:2 * tq] + pc[:, 2 * tq:3 * tq] + pc[:, 3 * tq:4 * tq]
        score_t = _dot(ovt_ref[...], psum, precision=lax.Precision.HIGHEST)

        w0 = pl.multiple_of(it * tq, tq)
        s_w = _dot(kw_s[pl.ds(w0, WIN_LEN + tq), :], qa0)
        s_w = jnp.concatenate([s_w[:tq] + after_ref[...], s_w[tq:WIN_LEN], s_w[WIN_LEN:] + causal_ref[...]],
                              axis=0)
        p_w = jnp.exp(s_w - jnp.max(s_w, axis=0, keepdims=True))
        acc_w = _dot(vw_s[:, pl.ds(w0, WIN_LEN + tq)], p_w.astype(BF16))
        o_win = acc_w[:hd] * (1.0 / acc_w[hd:hd + 1])

        jb = lax.broadcasted_iota(jnp.int32, (N_SLC, tq), 0)
        cur = (it * tq + lax.broadcasted_iota(jnp.int32, (N_SLC, tq), 1)) // SLC_LEN
        forced = (jb == 0) | (jb == cur) | (jb == cur - 1)
        score_t = jnp.where(forced, FORCE_SCORE, jnp.where(jb <= cur, score_t, -1.0))
        rank = jnp.zeros((N_SLC, tq), F32)
        for jp in range(N_SLC):
            other = score_t[jp:jp + 1, :]
            ge = jnp.where(other >= score_t, 1.0, 0.0)
            gt = jnp.where(other > score_t, 1.0, 0.0)
            rank = rank + jnp.where(jb > jp, ge, gt)
        sel = rank < float(SLC_TOPK)
        qa = q_aug(jnp.concatenate([jnp.where(sel, 0.0, -BIG)] * rep, axis=1))
        lo_blk = jnp.min(jnp.where(sel & (jb >= 2) & (jb <= cur), jb.astype(F32), float(N_SLC)))
        lo_key = (lo_blk.astype(jnp.int32) // 2) * tq

        e_key = it * tq - WIN_LEN
        t0 = pl.multiple_of(jnp.where(e_key > 0, WIN_LEN, 0), tq)
        s_main = _dot(ks_s[pl.ds(w0, WIN_LEN + tq), :], qa)
        s_s = jnp.concatenate([_dot(ks_s[pl.ds(t0, tq), :], qa), s_main[:WIN_LEN],
                               s_main[WIN_LEN:] + causal_ref[...]], axis=0)
        m_s = jnp.max(s_s, axis=0, keepdims=True)
        p_s = jnp.exp(s_s - m_s).astype(BF16)
        acc_s = _dot(vs_s[:, pl.ds(t0, tq)], p_s[:tq]) + _dot(vs_s[:, pl.ds(w0, WIN_LEN + tq)], p_s[tq:])
        c_hi = (e_key - tq + SEL_KC - 1) // SEL_KC
        c_lo = jnp.where(lo_key < e_key, (lo_key - tq) // SEL_KC, c_hi)
        return dict(qa=qa, e_key=e_key, c_lo=c_lo, c_hi=c_hi, m_s=m_s, acc_s=acc_s, o_cmp=o_cmp, o_win=o_win)

    def tail(sub, f):
        qa, e_key = f["qa"], f["e_key"]

        def early_step(c, carry):
            m, acc = carry
            k0 = tq + c * SEL_KC
            start = pl.multiple_of(WIN_LEN + k0, tq)
            k_abs = k0 + lax.broadcasted_iota(jnp.int32, (SEL_KC, nq), 0)
            s = jnp.where(k_abs < e_key, _dot(ks_s[pl.ds(start, SEL_KC), :], qa), NEG)
            m_new = jnp.maximum(m, jnp.max(s, axis=0, keepdims=True))
            p = jnp.exp(s - m_new).astype(BF16)
            return m_new, acc * jnp.exp(m - m_new) + _dot(vs_s[:, pl.ds(start, SEL_KC)], p)

        _, acc_s = lax.fori_loop(f["c_lo"], f["c_hi"], early_step, (f["m_s"], f["acc_s"]))
        o_slc = acc_s[:hd] * (1.0 / acc_s[hd:hd + 1])

        sg = sg_all[:, sub * tq:(sub + 1) * tq]
        gate = lambda br, r: jnp.where(grp == 0, sg[br * NSA_HEADS + r:br * NSA_HEADS + r + 1],
                                       sg[br * NSA_HEADS + rep + r:br * NSA_HEADS + rep + r + 1])
        for pair in range(rep // 2):
            halves = []
            for r in (2 * pair, 2 * pair + 1):
                lanes = slice(r * tq, (r + 1) * tq)
                halves.append(gate(0, r) * f["o_cmp"][:, lanes] + gate(1, r) * o_slc[:, lanes]
                              + gate(2, r) * f["o_win"][:, lanes])
            out_ref[0, sub * tq:(sub + 1) * tq, pair * LANES:(pair + 1) * LANES] = (
                jnp.concatenate(halves, axis=0).T.astype(out_ref.dtype))

    fronts = [front(sub) for sub in range(NSA_SUB)]
    for sub in range(NSA_SUB):
        tail(sub, fronts[sub])


def _nsa(qvt, keys, cmp_kv, gt, consts, b, s):
    tqs = NSA_SUB * Q_TILE
    nt = s // tqs
    c = consts
    in_specs = [
        pl.BlockSpec((2 * LANES, tqs), lambda bi, g, i: (g, bi * nt + i)),
        pl.BlockSpec((1, s, LANES), lambda bi, g, i: (bi, 0, g)),
        pl.BlockSpec((LANES, s), lambda bi, g, i: (NSA_Q_W // LANES + g, bi)),
        pl.BlockSpec((1, 1, s // CMP_STRIDE, LANES), lambda bi, g, i: (bi, g, 0, 0)),
        pl.BlockSpec((1, 1, s // CMP_STRIDE, LANES), lambda bi, g, i: (bi, 2 + g, 0, 0)),
        pl.BlockSpec((_GATE_T_ROWS, tqs), lambda bi, g, i: (0, bi * nt + i)),
        _const_spec(c["kx_sel"].shape), _const_spec(c["kx_win"].shape), _const_spec(c["vx_sel"].shape),
        _const_spec(c["vx_win"].shape),
        pl.BlockSpec((1,) + c["qx"].shape[1:], lambda bi, g, i: (g, 0, 0)),
        _const_spec(c["ov_t"].shape), _const_spec(c["causal"].shape), _const_spec(c["after"].shape),
    ]
    return pl.pallas_call(
        _nsa_body,
        grid=(b, NSA_KV_GROUPS, nt),
        in_specs=in_specs,
        out_specs=pl.BlockSpec((1, tqs, 2 * LANES), lambda bi, g, i: (bi, i, g)),
        out_shape=jax.ShapeDtypeStruct((b, s, NSA_Q_W), BF16),
        scratch_shapes=[pltpu.VMEM((s + WIN_LEN, LANES), BF16), pltpu.VMEM((s + WIN_LEN, LANES), BF16),
                        pltpu.VMEM((V_ROWS, s + WIN_LEN), BF16), pltpu.VMEM((V_ROWS, s + WIN_LEN), BF16)],
        compiler_params=_cparams("parallel", "parallel", "arbitrary"),
        name="nsa",
    )(qvt, keys, qvt, cmp_kv, cmp_kv, gt, c["kx_sel"], c["kx_win"], c["vx_sel"], c["vx_win"], c["qx"], c["ov_t"],
      c["causal"], c["after"])


GDN_TS = 512
GDN_BLK = 128
GDN_HALO = 8
GDN_SCAN_ROWS = GDN_HEAD_DIM + GDN_CHUNK


def _gdn_intra_body(x_ref, prev_ref, small_ref, cw_ref, alog_ref, dtb_ref, lt_ref,
                    nq_ref, co_ref, eg_ref, xp_s, x_s, p_s, rhs_s, qg_s, aqk_s, kdt_s):
    ts, dh, nh, blk = GDN_TS, GDN_HEAD_DIM, GDN_HEADS, GDN_BLK
    i = pl.program_id(1)
    xp_s[0:GDN_HALO, :] = jnp.where(i == 0, 0.0, prev_ref[0])
    xp_s[GDN_HALO:, :] = x_ref[0]
    conv = cw_ref[0:1, :] * xp_s[pl.ds(GDN_HALO - GDN_CONV + 1, ts), :]
    for j in range(1, GDN_CONV):
        conv = conv + cw_ref[j:j + 1, :] * xp_s[pl.ds(GDN_HALO - GDN_CONV + 1 + j, ts), :]
    act = jax.nn.silu(conv)

    sm = small_ref[0]
    beta = jax.nn.sigmoid(sm)
    g = -jnp.exp(alog_ref[...]) * jax.nn.softplus(sm + dtb_ref[...])
    gcum = jnp.concatenate([_dot(lt_ref[...], g[r:r + blk], precision=lax.Precision.HIGHEST)
                            for r in range(0, ts, blk)], axis=0)
    eg = jnp.exp(gcum)
    eg_ref[0] = eg

    ri = lax.broadcasted_iota(jnp.int32, (blk, blk), 0)
    ci = lax.broadcasted_iota(jnp.int32, (blk, blk), 1)
    same = (ri // GDN_CHUNK) == (ci // GDN_CHUNK)
    causal = same & (ri >= ci)
    strict = same & (ri > ci)
    eye = (ri == ci).astype(F32)

    for pb in range(ts // blk):
        rows = slice(pb * blk, (pb + 1) * blk)
        gc = gcum[rows]
        gc_t = gc.T
        first = lax.broadcasted_iota(jnp.int32, (blk, LANES), 0) < GDN_CHUNK
        g_last = jnp.where(first, gc[GDN_CHUNK - 1:GDN_CHUNK, :], gc[blk - 1:blk, :])
        e_dec = jnp.exp(g_last - gc)
        for h in range(nh):
            q = act[rows, h * dh:(h + 1) * dh]
            k = act[rows, GDN_W + h * dh:GDN_W + (h + 1) * dh]
            v = act[rows, 2 * GDN_W + h * dh:2 * GDN_W + (h + 1) * dh]
            q = q * lax.rsqrt(jnp.sum(q * q, axis=1, keepdims=True) + RMS_EPS) * (dh ** -0.5)
            k = k * lax.rsqrt(jnp.sum(k * k, axis=1, keepdims=True) + RMS_EPS)
            b_col = beta[rows, SM_BETA + h:SM_BETA + h + 1]
            eg_col = eg[rows, SM_DECAY + h:SM_DECAY + h + 1]
            gdiff = gc[:, SM_DECAY + h:SM_DECAY + h + 1] - gc_t[SM_DECAY + h:SM_DECAY + h + 1, :]
            decay = jnp.exp(jnp.where(causal, gdiff, NEG))
            kb = k * b_col
            kbf, kf, qf = kb.astype(BF16), k.astype(BF16), q.astype(BF16)
            a = jnp.where(strict, -_dot_nt(kbf, kf) * decay, 0.0)
            c = pb * nh + h
            x_s[c] = a.astype(BF16)
            p_s[c] = eye + a
            rhs_s[c] = jnp.concatenate([v * b_col, kb * eg_col], axis=1).astype(BF16)
            qg_s[c] = q * eg_col
            aqk_s[c] = jnp.where(causal, _dot_nt(qf, kf) * decay, 0.0).astype(BF16)
            kdt_s[c] = (k * e_dec[:, SM_DECAY + h:SM_DECAY + h + 1]).T.astype(BF16)

    n_chain = (ts // blk) * nh
    for _ in range(5):
        for c in range(n_chain):
            xb = x_s[c]
            xn = _dot(xb, xb).astype(BF16)
            x_s[c] = xn
            p = p_s[c]
            p_s[c] = p + _dot(p.astype(BF16), xn)

    tok_half = lax.broadcasted_iota(jnp.int32, (blk, blk), 1) // GDN_CHUNK
    for c in range(n_chain):
        pb, h = divmod(c, nh)
        uw = _dot(p_s[c].astype(BF16), rhs_s[c]).astype(BF16)
        a1 = _dot(aqk_s[c], uw)
        q_loc = qg_s[c] - a1[:, dh:]
        kdt = kdt_s[c]
        for half in range(blk // GDN_CHUNK):
            k1 = _dot(jnp.where(tok_half == half, kdt, jnp.zeros_like(kdt)), uw)
            n = pb * (blk // GDN_CHUNK) + half
            rows = slice(half * GDN_CHUNK, (half + 1) * GDN_CHUNK)
            nq_ref[0, h, n, :dh, :] = (-k1[:, dh:]).astype(BF16)
            nq_ref[0, h, n, dh:, :] = q_loc[rows].astype(BF16)
            co_ref[0, h, n, :dh, :] = k1[:, :dh].astype(BF16)
            co_ref[0, h, n, dh:, :] = a1[rows, :dh].astype(BF16)


def _gdn_intra(gqkv, small, conv_w, alog_l, dtb_l, lt):
    b, s, _ = gqkv.shape
    ts, nh, dh = GDN_TS, GDN_HEADS, GDN_HEAD_DIM
    n_chain = (ts // GDN_BLK) * nh
    sq = lambda dt: pltpu.VMEM((n_chain, GDN_BLK, GDN_BLK), dt)
    cspec = lambda: pl.BlockSpec((1, nh, ts // GDN_CHUNK, GDN_SCAN_ROWS, dh), lambda bi, i: (bi, 0, i, 0, 0))
    cshape = jax.ShapeDtypeStruct((b, nh, s // GDN_CHUNK, GDN_SCAN_ROWS, dh), BF16)
    return pl.pallas_call(
        _gdn_intra_body,
        grid=(b, s // ts),
        in_specs=[
            pl.BlockSpec((1, ts, 3 * GDN_W), lambda bi, i: (bi, i, 0)),
            pl.BlockSpec((1, GDN_HALO, 3 * GDN_W), lambda bi, i: (bi, jnp.maximum(i * (ts // GDN_HALO) - 1, 0), 0)),
            pl.BlockSpec((1, ts, LANES), lambda bi, i: (bi, i, 0)),
            _const_spec(conv_w.shape), _const_spec(alog_l.shape), _const_spec(dtb_l.shape), _const_spec(lt.shape),
        ],
        out_specs=[cspec(), cspec(), pl.BlockSpec((1, ts, LANES), lambda bi, i: (bi, i, 0))],
        out_shape=[cshape, cshape, jax.ShapeDtypeStruct((b, s, LANES), F32)],
        scratch_shapes=[pltpu.VMEM((ts + GDN_HALO, 3 * GDN_W), F32), sq(BF16), sq(F32),
                        pltpu.VMEM((n_chain, GDN_BLK, 2 * dh), BF16), sq(F32), sq(BF16), sq(BF16)],
        compiler_params=_cparams("parallel", "parallel"),
        name="gdn_intra",
    )(gqkv, gqkv, small, conv_w, alog_l, dtb_l, lt)


def _gdn_scan_body(nq_ref, co_ref, eg_ref, gate_ref, nw_ref, out_ref, st_s):
    nh, dh, ck = GDN_HEADS, GDN_HEAD_DIM, GDN_CHUNK
    st_s[...] = jnp.zeros(st_s.shape, F32)

    def chunk(n, carry):
        r0 = pl.multiple_of(n * ck, ck)
        d_row = eg_ref[0, pl.ds(r0 + ck - 1, 1), :]
        for h in range(nh):
            st = st_s[h]
            res = _dot(nq_ref[0, h, n], st.astype(BF16)) + co_ref[0, h, n].astype(F32)
            st_s[h] = st * d_row[:, SM_DECAY + h:SM_DECAY + h + 1] + res[:dh]
            o = res[dh:]
            ms = jnp.mean(o * o, axis=1, keepdims=True)
            gt = gate_ref[0, pl.ds(r0, ck), h * dh:(h + 1) * dh].astype(F32)
            out_ref[0, pl.ds(r0, ck), h * dh:(h + 1) * dh] = (
                o * lax.rsqrt(ms + RMS_EPS) * nw_ref[...] * jax.nn.silu(gt)).astype(out_ref.dtype)
        return carry

    lax.fori_loop(0, nq_ref.shape[2], chunk, 0, unroll=2)


def _gdn_scan(nq, co, eg, ggate, norm_w):
    b, nh, nc, rows, dh = nq.shape
    s = nc * GDN_CHUNK
    cspec = lambda: pl.BlockSpec((1, nh, nc, rows, dh), lambda bi: (bi, 0, 0, 0, 0))
    return pl.pallas_call(
        _gdn_scan_body,
        grid=(b,),
        in_specs=[cspec(), cspec(),
                  pl.BlockSpec((1, s, LANES), lambda bi: (bi, 0, 0)),
                  pl.BlockSpec((1, s, GDN_W), lambda bi: (bi, 0, 0)),
                  _const_spec(norm_w.shape)],
        out_specs=pl.BlockSpec((1, s, GDN_W), lambda bi: (bi, 0, 0)),
        out_shape=jax.ShapeDtypeStruct((b, s, GDN_W), BF16),
        scratch_shapes=[pltpu.VMEM((nh, dh, dh), F32)],
        compiler_params=_cparams("parallel"),
        name="gdn_scan",
    )(nq, co, eg, ggate, norm_w)


MERGE_TM = 512


def _layer_norm(y, g, b):
    mu = jnp.mean(y, axis=1, keepdims=True)
    d = y - mu
    var = jnp.mean(d * d, axis=1, keepdims=True)
    return d * lax.rsqrt(var + LN_EPS) * g + b


def _merge_body(x_ref, oa_ref, ob_ref, ga_ref, gb_ref, wa_ref, wb_ref, wo_ref, g_ref, b_ref, y_ref, yb_ref):
    ya = _dot(oa_ref[...], wa_ref[...])
    yb = _dot(ob_ref[...], wb_ref[...])
    mixin = jax.nn.sigmoid(ga_ref[...].astype(F32)) * ya + jax.nn.sigmoid(gb_ref[...].astype(F32)) * yb
    mix = _dot(mixin.astype(BF16), wo_ref[...])
    y = _layer_norm(DEEPNORM_ALPHA * x_ref[...] + mix, g_ref[...], b_ref[...])
    y_ref[...] = y
    yb_ref[...] = y.astype(BF16)


def _merge(x2, oa, ob, mgate, wa, wb, wo, g, b):
    m = x2.shape[0]
    tm, d = MERGE_TM, D_MODEL
    row = lambda wd, col=0: pl.BlockSpec((tm, wd), lambda i, col=col: (i, col))
    return pl.pallas_call(
        _merge_body,
        grid=(m // tm,),
        in_specs=[row(d), row(NSA_Q_W), row(GDN_W), row(d, 0), row(d, 1),
                  _const_spec(wa.shape), _const_spec(wb.shape), _const_spec(wo.shape),
                  _const_spec(g.shape), _const_spec(b.shape)],
        out_specs=[row(d), row(d)],
        out_shape=[jax.ShapeDtypeStruct((m, d), F32), jax.ShapeDtypeStruct((m, d), BF16)],
        compiler_params=_cparams("parallel"),
        name="merge",
    )(x2, oa, ob, mgate, mgate, wa, wb, wo, g, b)


FFN_TM = 512
FFN_HALO = 16
FFN_CK = 256


def _ffn_body(x_ref, xb_ref, prev_ref, wu_ref, cw_ref, wd_ref, g_ref, b_ref, out_ref, act_s,
              *, tiles_per_seq):
    i = pl.program_id(0)
    prev = prev_ref[...]
    prev = jnp.where(i % tiles_per_seq == 0, jnp.zeros_like(prev), prev)
    xc = jnp.concatenate([prev, xb_ref[...]], axis=0)

    def conv(h, c0):
        out = cw_ref[FFN_CONV - 1:FFN_CONV, c0:c0 + FFN_CK] * h[FFN_HALO:]
        for j in range(FFN_CONV - 1):
            shifted = pltpu.roll(h, FFN_CONV - 1 - j, axis=0)[FFN_HALO:]
            out = out + cw_ref[j:j + 1, c0:c0 + FFN_CK] * shifted
        return out

    for c in range(FFN_DIM // FFN_CK):
        c0 = c * FFN_CK
        hg = conv(_dot(xc, wu_ref[:, c0:c0 + FFN_CK]), c0)
        hv = conv(_dot(xc, wu_ref[:, FFN_DIM + c0:FFN_DIM + c0 + FFN_CK]), FFN_DIM + c0)
        act_s[:, c0:c0 + FFN_CK] = (jax.nn.silu(hg) * hv).astype(BF16)
    f = _dot(act_s[...], wd_ref[...])
    out_ref[...] = _layer_norm(DEEPNORM_ALPHA * x_ref[...] + f, g_ref[...], b_ref[...])


def _ffn(x1, x1b, wu, cw, wd, g, b, seq):
    m = x1.shape[0]
    tm, d = FFN_TM, D_MODEL
    return pl.pallas_call(
        functools.partial(_ffn_body, tiles_per_seq=seq // tm),
        grid=(m // tm,),
        in_specs=[pl.BlockSpec((tm, d), lambda i: (i, 0)),
                  pl.BlockSpec((tm, d), lambda i: (i, 0)),
                  pl.BlockSpec((FFN_HALO, d), lambda i: (jnp.maximum(i * (tm // FFN_HALO) - 1, 0), 0)),
                  _const_spec(wu.shape), _const_spec(cw.shape), _const_spec(wd.shape), _const_spec(g.shape), _const_spec(b.shape)],
        out_specs=pl.BlockSpec((tm, d), lambda i: (i, 0)),
        out_shape=jax.ShapeDtypeStruct((m, d), F32),
        scratch_shapes=[pltpu.VMEM((tm, FFN_DIM), BF16)],
        compiler_params=_cparams("parallel"),
        name="ffn",
    )(x1, x1b, x1b, wu, cw, wd, g, b)


def _lane_vec(vals, lane0):
    return jnp.zeros((1, LANES), F32).at[0, lane0:lane0 + vals.shape[0]].set(vals.astype(F32))


def _layer(x, w_in, cmp_pos, cmp_w1, cmp_w2, w_nsa_out, gdn_conv_w, gdn_a_log, gdn_dt_bias, gdn_norm_w,
           w_gdn_out, w_o, ln1_g, ln1_b, ffn_w_up, ffn_conv_w, ffn_w_down, ln2_g, ln2_b):
    b, s, d = x.shape
    m = b * s
    x2 = x.reshape(m, d)
    keys, cmpkv, small, gqkv, ggate, mgate, qvt, gt = _inproj(x2, *_wprep(w_in))

    consts = _nsa_consts(s)
    post, w2sel = _compress_weights(cmp_pos, cmp_w2)
    cmp_kv = _compress(cmpkv.reshape(b, s // CMP_STRIDE, CMP_STRIDE * 256), cmp_w1, post, w2sel, consts["cmp_aug"])
    o_nsa = _nsa(qvt, keys.reshape(b, s, 256), cmp_kv, gt, consts, b, s)

    ck = GDN_CHUNK
    tri = np.tril(np.ones((ck, ck), np.float32))
    lt = jnp.asarray(np.kron(np.eye(GDN_BLK // ck, dtype=np.float32), tri))
    nq, co, eg = _gdn_intra(gqkv.reshape(b, s, 3 * GDN_W), small.reshape(b, s, LANES), gdn_conv_w,
                                        _lane_vec(gdn_a_log, SM_DECAY), _lane_vec(gdn_dt_bias, SM_DECAY), lt)
    o_gdn = _gdn_scan(nq, co, eg, ggate.reshape(b, s, GDN_W), gdn_norm_w.reshape(1, GDN_HEAD_DIM))

    x1, x1b = _merge(x2, o_nsa.reshape(m, NSA_Q_W), o_gdn.reshape(m, GDN_W), mgate,
                     w_nsa_out.astype(BF16), w_gdn_out.astype(BF16), w_o.astype(BF16),
                     ln1_g.reshape(1, d), ln1_b.reshape(1, d))
    out = _ffn(x1, x1b, ffn_w_up.astype(BF16), ffn_conv_w, ffn_w_down.astype(BF16),
               ln2_g.reshape(1, d), ln2_b.reshape(1, d), s)
    return out.reshape(b, s, d)


def kernel(x, w_in, nsa_cmp_pos, nsa_cmp_w1, nsa_cmp_w2, w_nsa_out, gdn_conv_w, gdn_a_log, gdn_dt_bias, gdn_norm_w, w_gdn_out, w_o, ln1_g, ln1_b, ffn_w_up, ffn_conv_w, ffn_w_down, ln2_g, ln2_b):
    for l in range(DEPTH):
        x = _layer(x, w_in[l], nsa_cmp_pos[l], nsa_cmp_w1[l], nsa_cmp_w2[l], w_nsa_out[l], gdn_conv_w[l],
                   gdn_a_log[l], gdn_dt_bias[l], gdn_norm_w[l], w_gdn_out[l], w_o[l], ln1_g[l], ln1_b[l],
                   ffn_w_up[l], ffn_conv_w[l], ffn_w_down[l], ln2_g[l], ln2_b[l])
    return x
```

```python
import functools

import numpy as np
import jax
import jax.numpy as jnp
from jax import lax
from jax.experimental import pallas as pl
from jax.experimental.pallas import tpu as pltpu

F32 = jnp.float32
BF16 = jnp.bfloat16

D_MODEL = 1024
NSA_HEADS = 8
NSA_KV_GROUPS = 2
NSA_REP = NSA_HEADS // NSA_KV_GROUPS
NSA_HEAD_DIM = 64
CMP_LEN = 32
CMP_STRIDE = 16
SLC_LEN = 64
SLC_TOPK = 8
WIN_LEN = 512
FORCE_SCORE = 1.0e4
NEG = -1.0e30
GDN_HEADS = 4
GDN_HEAD_DIM = 128
GDN_CONV = 4
GDN_CHUNK = 64
FFN_DIM = 2816
FFN_CONV = 3
DEPTH = 1
DEEPNORM_ALPHA = (2.0 * DEPTH) ** 0.25
LN_EPS = 1e-5
RMS_EPS = 1e-6

NSA_Q_W = NSA_HEADS * NSA_HEAD_DIM
NSA_KV_W = NSA_KV_GROUPS * NSA_HEAD_DIM
GDN_W = GDN_HEADS * GDN_HEAD_DIM

LANES = 128
VMEM_LIMIT_BYTES = 56 * 1024 * 1024

AUG_SEL0 = 64
AUG_POS_HI = 96
AUG_POS_LO = 97
AUG_PAD = 98
BIG = 2.0 ** 100
Q_TILE = 128
N_SLC = 32
V_ROWS = 80
SEL_KC = 512
NSA_SUB = 2

NT_DIMS = (((1,), (1,)), ((), ()))


def _dot(a, b, **kw):
    return jnp.dot(a, b, preferred_element_type=F32, **kw)


def _dot_nt(a, b, **kw):
    return lax.dot_general(a, b, NT_DIMS, preferred_element_type=F32, **kw)


def _cparams(*sem):
    return pltpu.CompilerParams(dimension_semantics=sem, vmem_limit_bytes=VMEM_LIMIT_BYTES)


def _const_spec(shape):
    nd = len(shape)
    return pl.BlockSpec(shape, lambda *_: (0,) * nd, pipeline_mode=pl.Buffered(1))


_IN_WIDTHS = (NSA_Q_W,) + (NSA_KV_W,) * 6 + (3 * NSA_HEADS, 3 * GDN_W, GDN_HEADS, GDN_HEADS, GDN_W, 2 * D_MODEL)
(_C_Q, _C_CK, _C_CV, _C_SK, _C_SV, _C_WK, _C_WV, _C_GATE, _C_GQKV, _C_BETA, _C_DECAY, _C_GGATE, _C_MERGE,
 IN_WIDTH) = (int(v) for v in np.cumsum((0,) + _IN_WIDTHS))
_C_SMALL = _C_BETA // LANES * LANES
SM_BETA = _C_BETA - _C_SMALL
SM_DECAY = _C_DECAY - _C_SMALL
_INPROJ_GROUPS = (("keys", 256, BF16), ("cmp", 256, BF16), ("ggate", GDN_W, BF16), ("merge", 2 * D_MODEL, BF16))
_INPROJ_WIDTH = sum(w for _, w, _ in _INPROJ_GROUPS)
_GDN_PROJ_WIDTH = 3 * GDN_W + LANES
_INPROJ_T_ROWS = NSA_Q_W + 4 * NSA_HEAD_DIM
_GATE_T_ROWS = 32
INPROJ_TM = 512
WPREP_TK = 128


def _wprep_body(w_ref, rows_ref, gdn_ref, wt_ref, wg_ref):
    hd = NSA_HEAD_DIM
    cols = lambda c0, n: w_ref[:, c0:c0 + n]
    for j, c0 in enumerate((_C_SK, _C_WK, _C_SK + hd, _C_WK + hd)):
        rows_ref[:, j * hd:(j + 1) * hd] = cols(c0, hd).astype(BF16)
    c = 4 * hd
    for c0, n in ((_C_CK, 2 * NSA_KV_W), (_C_GGATE, GDN_W), (_C_MERGE, 2 * D_MODEL)):
        rows_ref[:, c:c + n] = cols(c0, n).astype(BF16)
        c += n
    gdn_ref[:, :3 * GDN_W] = cols(_C_GQKV, 3 * GDN_W).astype(BF16)
    gdn_ref[:, 3 * GDN_W:] = cols(_C_SMALL, LANES).astype(BF16)
    for j in range(NSA_Q_W // LANES):
        wt_ref[j * LANES:(j + 1) * LANES, :] = cols(_C_Q + j * LANES, LANES).T.astype(BF16)
    sv_t = cols(_C_SV, NSA_KV_W).T
    wv_t = cols(_C_WV, NSA_KV_W).T
    for j, t in enumerate((sv_t[:hd], wv_t[:hd], sv_t[hd:], wv_t[hd:])):
        wt_ref[NSA_Q_W + j * hd:NSA_Q_W + (j + 1) * hd, :] = t.astype(BF16)
    wg_ref[...] = cols(_C_GATE, LANES).T[:_GATE_T_ROWS].astype(BF16)


def _wprep(w_in):
    k = w_in.shape[0]
    tk = WPREP_TK
    return pl.pallas_call(
        _wprep_body,
        grid=(k // tk,),
        in_specs=[pl.BlockSpec((tk, IN_WIDTH), lambda i: (i, 0))],
        out_specs=[pl.BlockSpec((tk, _INPROJ_WIDTH), lambda i: (i, 0)),
                   pl.BlockSpec((tk, _GDN_PROJ_WIDTH), lambda i: (i, 0)),
                   pl.BlockSpec((_INPROJ_T_ROWS, tk), lambda i: (0, i)),
                   pl.BlockSpec((_GATE_T_ROWS, tk), lambda i: (0, i))],
        out_shape=[jax.ShapeDtypeStruct((k, _INPROJ_WIDTH), BF16), jax.ShapeDtypeStruct((k, _GDN_PROJ_WIDTH), BF16),
                   jax.ShapeDtypeStruct((_INPROJ_T_ROWS, k), BF16), jax.ShapeDtypeStruct((_GATE_T_ROWS, k), BF16)],
        compiler_params=_cparams("parallel"),
        name="wprep",
    )(w_in)


def _inproj_body(x_ref, w_ref, wt_ref, wg_ref, keys_ref, cmp_ref, ggate_ref, merge_ref, qvt_ref, gt_ref, cmp_s):
    x = x_ref[...].astype(BF16)
    outs = (keys_ref, None, ggate_ref, merge_ref)
    c0 = 0
    for ref, (name, width, _) in zip(outs, _INPROJ_GROUPS):
        for s in range(0, width, 512):
            e = min(s + 512, width)
            res = _dot(x, w_ref[:, c0 + s:c0 + e])
            if name == "cmp":
                for j in range(width // LANES):
                    cmp_s[j] = res[:, j * LANES:(j + 1) * LANES]
            else:
                ref[:, s:e] = res.astype(ref.dtype)
        c0 += width
    nblk = cmp_ref.shape[0]
    for l in range(CMP_STRIDE):
        for j in range(cmp_s.shape[0]):
            cmp_ref[:, l * 256 + j * LANES:l * 256 + (j + 1) * LANES] = (
                cmp_s[j, pl.ds(l, nblk, stride=CMP_STRIDE), :].astype(BF16))
    for s in range(0, _INPROJ_T_ROWS, 256):
        qvt_ref[s:s + 256, :] = _dot_nt(wt_ref[s:s + 256, :], x).astype(qvt_ref.dtype)
    gt_ref[...] = _dot_nt(wg_ref[...], x)


def _inproj(x2, w_rows, w_t, w_g):
    m = x2.shape[0]
    tm = INPROJ_TM
    row_major = [(n, wd, dt) for n, wd, dt in _INPROJ_GROUPS if n != "cmp"]
    specs = {n: (pl.BlockSpec((tm, wd), lambda i: (i, 0)), jax.ShapeDtypeStruct((m, wd), dt))
             for n, wd, dt in row_major}
    specs["cmp"] = (pl.BlockSpec((tm // CMP_STRIDE, CMP_STRIDE * 256), lambda i: (i, 0)),
                    jax.ShapeDtypeStruct((m // CMP_STRIDE, CMP_STRIDE * 256), BF16))
    order = [n for n, _, _ in _INPROJ_GROUPS]
    return pl.pallas_call(
        _inproj_body,
        grid=(m // tm,),
        in_specs=[pl.BlockSpec((tm, D_MODEL), lambda i: (i, 0)), _const_spec(w_rows.shape),
                  _const_spec(w_t.shape), _const_spec(w_g.shape)],
        out_specs=[specs[n][0] for n in order]
        + [pl.BlockSpec((_INPROJ_T_ROWS, tm), lambda i: (0, i)), pl.BlockSpec((_GATE_T_ROWS, tm), lambda i: (0, i))],
        out_shape=[specs[n][1] for n in order]
        + [jax.ShapeDtypeStruct((_INPROJ_T_ROWS, m), BF16), jax.ShapeDtypeStruct((_GATE_T_ROWS, m), F32)],
        scratch_shapes=[pltpu.VMEM((256 // LANES, tm, LANES), F32)],
        compiler_params=_cparams("parallel"),
        name="inproj",
    )(x2, w_rows, w_t, w_g)


def _compress_weights(cmp_pos, cmp_w2):
    hd, half = NSA_HEAD_DIM, CMP_LEN // 2
    posr = cmp_pos.reshape(2, 2, half, hd)
    post = jnp.broadcast_to(posr.transpose(1, 2, 0, 3)[:, :, :, None, :], (2, half, 2, 2, hd))
    post = jnp.concatenate([post.reshape(2, half * 4 * hd), jnp.zeros((6, half * 4 * hd), cmp_pos.dtype)], axis=0)
    w2sel = jnp.zeros((2, 2, 2 * hd, LANES), cmp_w2.dtype)
    for g in range(2):
        w2sel = w2sel.at[:, g, g * hd:(g + 1) * hd, :hd].set(cmp_w2)
    return post.astype(BF16), w2sel.reshape(4, 2 * hd, LANES).astype(BF16)


def _compress_body(t_ref, w1_ref, pos_ref, w2_ref, aug_ref, out_ref, w1e_s):
    hd, half_len = NSA_HEAD_DIM, CMP_LEN // 2

    @pl.when(pl.program_id(0) == 0)
    def _():
        w1e_s[...] = jnp.zeros(w1e_s.shape, BF16)
        for which in range(2):
            for half in range(2):
                for l in range(half_len):
                    blk = w1_ref[which, (half * half_len + l) * hd:(half * half_len + l + 1) * hd, :].astype(BF16)
                    for g in range(NSA_KV_GROUPS):
                        r0 = l * 256 + which * LANES + g * hd
                        c0 = half * 256 + which * LANES + g * hd
                        w1e_s[r0:r0 + hd, c0:c0 + hd] = blk

    p = _dot(t_ref[0], w1e_s[...])
    pp = _dot(pos_ref[...], w1e_s[...])
    nxt = pltpu.roll(p[:, 256:], p.shape[0] - 1, axis=0)
    pre = p[:, :256] + nxt + pp[0:1, :256] + pp[1:2, 256:]
    h = jax.nn.gelu(pre).astype(BF16)
    n_idx = lax.broadcasted_iota(jnp.int32, (p.shape[0], LANES), 0)
    real = n_idx < p.shape[0] - 1
    for which in range(2):
        hw = h[:, which * LANES:(which + 1) * LANES]
        for g in range(2):
            o = jnp.where(real, _dot(hw, w2_ref[which * 2 + g]) + aug_ref[which], 0.0)
            out_ref[0, which * 2 + g] = (o if which == 0 else o.T).astype(out_ref.dtype)


def _compress(t2, w1, post, w2sel, aug):
    b, nblk, _ = t2.shape
    return pl.pallas_call(
        _compress_body,
        grid=(b,),
        in_specs=[pl.BlockSpec((1, nblk, CMP_STRIDE * 256), lambda i: (i, 0, 0)),
                  _const_spec(w1.shape), _const_spec(post.shape), _const_spec(w2sel.shape), _const_spec(aug.shape)],
        out_specs=pl.BlockSpec((1, 4, nblk, LANES), lambda i: (i, 0, 0, 0)),
        out_shape=jax.ShapeDtypeStruct((b, 4, nblk, LANES), BF16),
        scratch_shapes=[pltpu.VMEM((CMP_STRIDE * 256, 512), BF16)],
        compiler_params=_cparams("arbitrary"),
        name="compress",
    )(t2, w1, post, w2sel, aug)


def _nsa_consts(s):
    hd, rep = NSA_HEAD_DIM, NSA_REP
    t = np.arange(s)
    kx_win = np.zeros((s + WIN_LEN, hd), np.float32)
    kx_win[WIN_LEN + t, AUG_POS_HI - hd] = t // 256
    kx_win[WIN_LEN + t, AUG_POS_LO - hd] = t % 256
    kx_win[:WIN_LEN, AUG_PAD - hd] = 1.0
    kx_sel = kx_win.copy()
    kx_sel[WIN_LEN + t, t // SLC_LEN] = 1.0
    vx_win = np.zeros((V_ROWS - hd, s + WIN_LEN), np.float32)
    vx_win[0, WIN_LEN:] = 1.0
    vx_sel = vx_win
    n_cmp = s // CMP_STRIDE
    cmp_aug = np.zeros((2, n_cmp, LANES), np.float32)
    end = np.arange(n_cmp) * CMP_STRIDE + CMP_LEN - 1
    cmp_aug[0, :, AUG_POS_HI] = end // 256
    cmp_aug[0, :, AUG_POS_LO] = end % 256
    qx = np.zeros((NSA_KV_GROUPS, LANES - AUG_POS_HI, rep * Q_TILE), np.float32)
    for h in range(NSA_HEADS):
        slope = 2.0 ** (-8.0 * (h + 1) / NSA_HEADS)
        lanes = slice((h % rep) * Q_TILE, (h % rep + 1) * Q_TILE)
        qx[h // rep, 0, lanes] = slope * 256.0
        qx[h // rep, 1, lanes] = slope
        qx[h // rep, AUG_PAD - AUG_POS_HI, lanes] = -BIG
    c0 = np.arange(n_cmp)[None, :] * CMP_STRIDE
    s0 = np.arange(s // SLC_LEN)[:, None] * SLC_LEN
    ov_t = ((c0 < s0 + SLC_LEN) & (c0 + CMP_LEN > s0)).astype(np.float32)
    ov_t[:, (s - CMP_LEN) // CMP_STRIDE + 1:] = 0.0
    kk = np.arange(Q_TILE)[:, None]
    qq = np.arange(Q_TILE)[None, :]
    causal = np.tile(np.where(kk <= qq, 0.0, NEG).astype(np.float32), (1, rep))
    after = np.tile(np.where(kk > qq, 0.0, NEG).astype(np.float32), (1, rep))
    j = jnp.asarray
    return dict(kx_sel=j(kx_sel, BF16), kx_win=j(kx_win, BF16), vx_sel=j(vx_sel, BF16), vx_win=j(vx_win, BF16),
                cmp_aug=j(cmp_aug), qx=j(qx), ov_t=j(ov_t), causal=j(causal), after=j(after))


def _nsa_body(qt_ref, k_ref, vt_ref, kc_ref, vct_ref, gt_ref, kxs_ref, kxw_ref, vxs_ref, vxw_ref, qx_ref, ovt_ref,
              causal_ref, after_ref, out_ref, ks_s, kw_s, vs_s, vw_s):
    hd, rep, tq = NSA_HEAD_DIM, NSA_REP, Q_TILE
    nq = rep * tq
    i = pl.program_id(2)

    @pl.when(i == 0)
    def _():
        keys = k_ref[0]
        ks_s[:WIN_LEN, :hd] = jnp.zeros((WIN_LEN, hd), BF16)
        ks_s[WIN_LEN:, :hd] = keys[:, :hd]
        ks_s[:, hd:] = kxs_ref[...]
        kw_s[:WIN_LEN, :hd] = jnp.zeros((WIN_LEN, hd), BF16)
        kw_s[WIN_LEN:, :hd] = keys[:, hd:]
        kw_s[:, hd:] = kxw_ref[...]
        vals = vt_ref[...]
        vs_s[:hd, :WIN_LEN] = jnp.zeros((hd, WIN_LEN), BF16)
        vs_s[:hd, WIN_LEN:] = vals[:hd]
        vs_s[hd:, :] = vxs_ref[...]
        vw_s[:hd, :WIN_LEN] = jnp.zeros((hd, WIN_LEN), BF16)
        vw_s[:hd, WIN_LEN:] = vals[hd:]
        vw_s[hd:, :] = vxw_ref[...]

    qx = qx_ref[0]
    sg_all = jax.nn.sigmoid(gt_ref[...])
    grp = pl.program_id(1)

    def front(sub):
        it = i * NSA_SUB + sub
        qt = qt_ref[:, sub * tq:(sub + 1) * tq]
        q64 = jnp.concatenate([qt[r * hd:(r + 1) * hd, :] for r in range(rep)], axis=1).astype(F32) * (hd ** -0.5)

        def q_aug(sel_rows):
            return jnp.concatenate([q64, sel_rows, qx], axis=0).astype(BF16)

        n_row = lax.broadcasted_iota(jnp.int32, (LANES, nq), 0)
        t_lane = it * tq + (lax.broadcasted_iota(jnp.int32, (LANES, nq), 1) & (tq - 1))
        valid = t_lane >= n_row * CMP_STRIDE + (CMP_LEN - 1)
        qa0 = q_aug(jnp.zeros((N_SLC, nq), F32))
        sc = jnp.where(valid, _dot(kc_ref[0, 0], qa0), NEG)
        mc = jnp.max(sc, axis=0, keepdims=True)
        ec = jnp.where(valid, jnp.exp(sc - mc), 0.0)
        lc = jnp.sum(ec, axis=0, keepdims=True)
        pc = ec * jnp.where(lc > 0.0, 1.0 / lc, 0.0)
        o_cmp = _dot(vct_ref[0, 0], pc.astype(BF16))[:hd]
        psum = pc[:, 0:tq] + pc[:, tq:2 * tq] + pc[:, 2 * tq:3 * tq] + pc[:, 3 * tq:4 * tq]
        score_t = _dot(ovt_ref[...], psum, precision=lax.Precision.HIGHEST)

        w0 = pl.multiple_of(it * tq, tq)
        s_w = _dot(kw_s[pl.ds(w0, WIN_LEN + tq), :], qa0)
        s_w = jnp.concatenate([s_w[:tq] + after_ref[...], s_w[tq:WIN_LEN], s_w[WIN_LEN:] + causal_ref[...]],
                              axis=0)
        p_w = jnp.exp(s_w - jnp.max(s_w, axis=0, keepdims=True))
        acc_w = _dot(vw_s[:, pl.ds(w0, WIN_LEN + tq)], p_w.astype(BF16))
        o_win = acc_w[:hd] * (1.0 / acc_w[hd:hd + 1])

        jb = lax.broadcasted_iota(jnp.int32, (N_SLC, tq), 0)
        cur = (it * tq + lax.broadcasted_iota(jnp.int32, (N_SLC, tq), 1)) // SLC_LEN
        forced = (jb == 0) | (jb == cur) | (jb == cur - 1)
        score_t = jnp.where(forced, FORCE_SCORE, jnp.where(jb <= cur, score_t, -1.0))
        rank = jnp.zeros((N_SLC, tq), F32)
        for jp in range(N_SLC):
            other = score_t[jp:jp + 1, :]
            ge = jnp.where(other >= score_t, 1.0, 0.0)
            gt = jnp.where(other > score_t, 1.0, 0.0)
            rank = rank + jnp.where(jb > jp, ge, gt)
        sel = rank < float(SLC_TOPK)
        qa = q_aug(jnp.concatenate([jnp.where(sel, 0.0, -BIG)] * rep, axis=1))
        lo_blk = jnp.min(jnp.where(sel & (jb >= 2) & (jb <= cur), jb.astype(F32), float(N_SLC)))
        lo_key = (lo_blk.astype(jnp.int32) // 2) * tq

        e_key = it * tq - WIN_LEN
        t0 = pl.multiple_of(jnp.where(e_key > 0, WIN_LEN, 0), tq)
        s_main = _dot(ks_s[pl.ds(w0, WIN_LEN + tq), :], qa)
        s_s = jnp.concatenate([_dot(ks_s[pl.ds(t0, tq), :], qa), s_main[:WIN_LEN],
                               s_main[WIN_LEN:] + causal_ref[...]], axis=0)
        m_s = jnp.max(s_s, axis=0, keepdims=True)
        p_s = jnp.exp(s_s - m_s).astype(BF16)
        acc_s = _dot(vs_s[:, pl.ds(t0, tq)], p_s[:tq]) + _dot(vs_s[:, pl.ds(w0, WIN_LEN + tq)], p_s[tq:])
        c_hi = (e_key - tq + SEL_KC - 1) // SEL_KC
        c_lo = jnp.where(lo_key < e_key, (lo_key - tq) // SEL_KC, c_hi)
        return dict(qa=qa, e_key=e_key, c_lo=c_lo, c_hi=c_hi, m_s=m_s, acc_s=acc_s, o_cmp=o_cmp, o_win=o_win)

    def tail(sub, f):
        qa, e_key = f["qa"], f["e_key"]

        def early_step(c, carry):
            m, acc = carry
            k0 = tq + c * SEL_KC
            start = pl.multiple_of(WIN_LEN + k0, tq)
            k_abs = k0 + lax.broadcasted_iota(jnp.int32, (SEL_KC, nq), 0)
            s = jnp.where(k_abs < e_key, _dot(ks_s[pl.ds(start, SEL_KC), :], qa), NEG)
            m_new = jnp.maximum(m, jnp.max(s, axis=0, keepdims=True))
            p = jnp.exp(s - m_new).astype(BF16)
            return m_new, acc * jnp.exp(m - m_new) + _dot(vs_s[:, pl.ds(start, SEL_KC)], p)

        _, acc_s = lax.fori_loop(f["c_lo"], f["c_hi"], early_step, (f["m_s"], f["acc_s"]))
        o_slc = acc_s[:hd] * (1.0 / acc_s[hd:hd + 1])

        sg = sg_all[:, sub * tq:(sub + 1) * tq]
        gate = lambda br, r: jnp.where(grp == 0, sg[br * NSA_HEADS + r:br * NSA_HEADS + r + 1],
                                       sg[br * NSA_HEADS + rep + r:br * NSA_HEADS + rep + r + 1])
        for pair in range(rep // 2):
            halves = []
            for r in (2 * pair, 2 * pair + 1):
                lanes = slice(r * tq, (r + 1) * tq)
                halves.append(gate(0, r) * f["o_cmp"][:, lanes] + gate(1, r) * o_slc[:, lanes]
                              + gate(2, r) * f["o_win"][:, lanes])
            out_ref[0, sub * tq:(sub + 1) * tq, pair * LANES:(pair + 1) * LANES] = (
                jnp.concatenate(halves, axis=0).T.astype(out_ref.dtype))

    fronts = [front(sub) for sub in range(NSA_SUB)]
    for sub in range(NSA_SUB):
        tail(sub, fronts[sub])


def _nsa(qvt, keys, cmp_kv, gt, consts, b, s):
    tqs = NSA_SUB * Q_TILE
    nt = s // tqs
    c = consts
    in_specs = [
        pl.BlockSpec((2 * LANES, tqs), lambda bi, g, i: (g, bi * nt + i)),
        pl.BlockSpec((1, s, LANES), lambda bi, g, i: (bi, 0, g)),
        pl.BlockSpec((LANES, s), lambda bi, g, i: (NSA_Q_W // LANES + g, bi)),
        pl.BlockSpec((1, 1, s // CMP_STRIDE, LANES), lambda bi, g, i: (bi, g, 0, 0)),
        pl.BlockSpec((1, 1, s // CMP_STRIDE, LANES), lambda bi, g, i: (bi, 2 + g, 0, 0)),
        pl.BlockSpec((_GATE_T_ROWS, tqs), lambda bi, g, i: (0, bi * nt + i)),
        _const_spec(c["kx_sel"].shape), _const_spec(c["kx_win"].shape), _const_spec(c["vx_sel"].shape),
        _const_spec(c["vx_win"].shape),
        pl.BlockSpec((1,) + c["qx"].shape[1:], lambda bi, g, i: (g, 0, 0)),
        _const_spec(c["ov_t"].shape), _const_spec(c["causal"].shape), _const_spec(c["after"].shape),
    ]
    return pl.pallas_call(
        _nsa_body,
        grid=(b, NSA_KV_GROUPS, nt),
        in_specs=in_specs,
        out_specs=pl.BlockSpec((1, tqs, 2 * LANES), lambda bi, g, i: (bi, i, g)),
        out_shape=jax.ShapeDtypeStruct((b, s, NSA_Q_W), BF16),
        scratch_shapes=[pltpu.VMEM((s + WIN_LEN, LANES), BF16), pltpu.VMEM((s + WIN_LEN, LANES), BF16),
                        pltpu.VMEM((V_ROWS, s + WIN_LEN), BF16), pltpu.VMEM((V_ROWS, s + WIN_LEN), BF16)],
        compiler_params=_cparams("parallel", "parallel", "arbitrary"),
        name="nsa",
    )(qvt, keys, qvt, cmp_kv, cmp_kv, gt, c["kx_sel"], c["kx_win"], c["vx_sel"], c["vx_win"], c["qx"], c["ov_t"],
      c["causal"], c["after"])


GDN_TS = 512
GDN_BLK = 128
GDN_HALO = 8
GDN_XHALO = 16
GDN_SCAN_ROWS = GDN_HEAD_DIM + GDN_CHUNK


def _gdn_intra_body(x_ref, prev_ref, w_ref, cw_ref, alog_ref, dtb_ref, lt_ref,
                    nq_ref, co_ref, eg_ref, xp_s, x_s, p_s, rhs_s, qg_s, aqk_s, kdt_s):
    ts, dh, nh, blk = GDN_TS, GDN_HEAD_DIM, GDN_HEADS, GDN_BLK
    i = pl.program_id(1)
    proj = _dot(x_ref[0].astype(BF16), w_ref[...])
    hist = _dot(prev_ref[0].astype(BF16), w_ref[:, :3 * GDN_W])[GDN_XHALO - GDN_HALO:]
    xp_s[0:GDN_HALO, :] = jnp.where(i == 0, 0.0, hist)
    xp_s[GDN_HALO:, :] = proj[:, :3 * GDN_W]
    xp = xp_s[...]
    conv = cw_ref[0:1, :] * xp
    for j in range(1, GDN_CONV):
        conv = pltpu.roll(conv, 1, axis=0) + cw_ref[j:j + 1, :] * xp
    act = jax.nn.silu(conv[GDN_HALO:])

    sm = proj[:, 3 * GDN_W:]
    beta = jax.nn.sigmoid(sm)
    g = -jnp.exp(alog_ref[...]) * jax.nn.softplus(sm + dtb_ref[...])
    gcum = jnp.concatenate([_dot(lt_ref[...], g[r:r + blk], precision=lax.Precision.HIGHEST)
                            for r in range(0, ts, blk)], axis=0)
    eg = jnp.exp(gcum)
    eg_ref[0] = eg

    ri = lax.broadcasted_iota(jnp.int32, (blk, blk), 0)
    ci = lax.broadcasted_iota(jnp.int32, (blk, blk), 1)
    same = (ri // GDN_CHUNK) == (ci // GDN_CHUNK)
    causal = same & (ri >= ci)
    strict = same & (ri > ci)
    eye = (ri == ci).astype(F32)

    for pb in range(ts // blk):
        rows = slice(pb * blk, (pb + 1) * blk)
        gc = gcum[rows]
        gc_t = gc.T
        first = lax.broadcasted_iota(jnp.int32, (blk, LANES), 0) < GDN_CHUNK
        g_last = jnp.where(first, gc[GDN_CHUNK - 1:GDN_CHUNK, :], gc[blk - 1:blk, :])
        e_dec = jnp.exp(g_last - gc)
        for h in range(nh):
            q = act[rows, h * dh:(h + 1) * dh]
            k = act[rows, GDN_W + h * dh:GDN_W + (h + 1) * dh]
            v = act[rows, 2 * GDN_W + h * dh:2 * GDN_W + (h + 1) * dh]
            q = q * lax.rsqrt(jnp.sum(q * q, axis=1, keepdims=True) + RMS_EPS) * (dh ** -0.5)
            k = k * lax.rsqrt(jnp.sum(k * k, axis=1, keepdims=True) + RMS_EPS)
            b_col = beta[rows, SM_BETA + h:SM_BETA + h + 1]
            eg_col = eg[rows, SM_DECAY + h:SM_DECAY + h + 1]
            gdiff = gc[:, SM_DECAY + h:SM_DECAY + h + 1] - gc_t[SM_DECAY + h:SM_DECAY + h + 1, :]
            decay = jnp.exp(jnp.where(causal, gdiff, NEG))
            kb = k * b_col
            kbf, kf, qf = kb.astype(BF16), k.astype(BF16), q.astype(BF16)
            a = jnp.where(strict, -_dot_nt(kbf, kf) * decay, 0.0)
            c = pb * nh + h
            x_s[c] = a.astype(BF16)
            p_s[c] = eye + a
            rhs_s[c] = jnp.concatenate([v * b_col, kb * eg_col], axis=1).astype(BF16)
            qg_s[c] = q * eg_col
            aqk_s[c] = jnp.where(causal, _dot_nt(qf, kf) * decay, 0.0).astype(BF16)
            kdt_s[c] = (k * e_dec[:, SM_DECAY + h:SM_DECAY + h + 1]).T.astype(BF16)

    n_chain = (ts // blk) * nh
    for _ in range(5):
        for c in range(n_chain):
            xb = x_s[c]
            xn = _dot(xb, xb).astype(BF16)
            x_s[c] = xn
            p = p_s[c]
            p_s[c] = p + _dot(p.astype(BF16), xn)

    tok_half = lax.broadcasted_iota(jnp.int32, (blk, blk), 1) // GDN_CHUNK
    for c in range(n_chain):
        pb, h = divmod(c, nh)
        uw = _dot(p_s[c].astype(BF16), rhs_s[c]).astype(BF16)
        a1 = _dot(aqk_s[c], uw)
        q_loc = qg_s[c] - a1[:, dh:]
        kdt = kdt_s[c]
        for half in range(blk // GDN_CHUNK):
            k1 = _dot(jnp.where(tok_half == half, kdt, jnp.zeros_like(kdt)), uw)
            n = pb * (blk // GDN_CHUNK) + half
            rows = slice(half * GDN_CHUNK, (half + 1) * GDN_CHUNK)
            nq_ref[0, h, n, :dh, :] = (-k1[:, dh:]).astype(BF16)
            nq_ref[0, h, n, dh:, :] = q_loc[rows].astype(BF16)
            co_ref[0, h, n, :dh, :] = k1[:, :dh].astype(BF16)
            co_ref[0, h, n, dh:, :] = a1[rows, :dh].astype(BF16)


def _gdn_intra(x, w_gdn, conv_w, alog_l, dtb_l, lt):
    b, s, d = x.shape
    ts, nh, dh = GDN_TS, GDN_HEADS, GDN_HEAD_DIM
    n_chain = (ts // GDN_BLK) * nh
    sq = lambda dt: pltpu.VMEM((n_chain, GDN_BLK, GDN_BLK), dt)
    cspec = lambda: pl.BlockSpec((1, nh, ts // GDN_CHUNK, GDN_SCAN_ROWS, dh), lambda bi, i: (bi, 0, i, 0, 0))
    cshape = jax.ShapeDtypeStruct((b, nh, s // GDN_CHUNK, GDN_SCAN_ROWS, dh), BF16)
    return pl.pallas_call(
        _gdn_intra_body,
        grid=(b, s // ts),
        in_specs=[
            pl.BlockSpec((1, ts, d), lambda bi, i: (bi, i, 0)),
            pl.BlockSpec((1, GDN_XHALO, d), lambda bi, i: (bi, jnp.maximum(i * (ts // GDN_XHALO) - 1, 0), 0)),
            _const_spec(w_gdn.shape), _const_spec(conv_w.shape), _const_spec(alog_l.shape), _const_spec(dtb_l.shape), _const_spec(lt.shape),
        ],
        out_specs=[cspec(), cspec(), pl.BlockSpec((1, ts, LANES), lambda bi, i: (bi, i, 0))],
        out_shape=[cshape, cshape, jax.ShapeDtypeStruct((b, s, LANES), F32)],
        scratch_shapes=[pltpu.VMEM((ts + GDN_HALO, 3 * GDN_W), F32), sq(BF16), sq(F32),
                        pltpu.VMEM((n_chain, GDN_BLK, 2 * dh), BF16), sq(F32), sq(BF16), sq(BF16)],
        compiler_params=_cparams("parallel", "parallel"),
        name="gdn_intra",
    )(x, x, w_gdn, conv_w, alog_l, dtb_l, lt)


def _gdn_scan_body(nq_ref, co_ref, eg_ref, gate_ref, nw_ref, out_ref, st_s):
    nh, dh, ck = GDN_HEADS, GDN_HEAD_DIM, GDN_CHUNK
    st_s[...] = jnp.zeros(st_s.shape, F32)

    def chunk(n, carry):
        r0 = pl.multiple_of(n * ck, ck)
        d_row = eg_ref[0, pl.ds(r0 + ck - 1, 1), :]
        for h in range(nh):
            st = st_s[h]
            res = _dot(nq_ref[0, h, n], st.astype(BF16)) + co_ref[0, h, n].astype(F32)
            st_s[h] = st * d_row[:, SM_DECAY + h:SM_DECAY + h + 1] + res[:dh]
            o = res[dh:]
            ms = jnp.mean(o * o, axis=1, keepdims=True)
            gt = gate_ref[0, pl.ds(r0, ck), h * dh:(h + 1) * dh].astype(F32)
            out_ref[0, pl.ds(r0, ck), h * dh:(h + 1) * dh] = (
                o * lax.rsqrt(ms + RMS_EPS) * nw_ref[...] * jax.nn.silu(gt)).astype(out_ref.dtype)
        return carry

    lax.fori_loop(0, nq_ref.shape[2], chunk, 0, unroll=2)


def _gdn_scan(nq, co, eg, ggate, norm_w):
    b, nh, nc, rows, dh = nq.shape
    s = nc * GDN_CHUNK
    cspec = lambda: pl.BlockSpec((1, nh, nc, rows, dh), lambda bi: (bi, 0, 0, 0, 0))
    return pl.pallas_call(
        _gdn_scan_body,
        grid=(b,),
        in_specs=[cspec(), cspec(),
                  pl.BlockSpec((1, s, LANES), lambda bi: (bi, 0, 0)),
                  pl.BlockSpec((1, s, GDN_W), lambda bi: (bi, 0, 0)),
                  _const_spec(norm_w.shape)],
        out_specs=pl.BlockSpec((1, s, GDN_W), lambda bi: (bi, 0, 0)),
        out_shape=jax.ShapeDtypeStruct((b, s, GDN_W), BF16),
        scratch_shapes=[pltpu.VMEM((nh, dh, dh), F32)],
        compiler_params=_cparams("parallel"),
        name="gdn_scan",
    )(nq, co, eg, ggate, norm_w)


MERGE_TM = 512


def _layer_norm(y, g, b):
    mu = jnp.mean(y, axis=1, keepdims=True)
    d = y - mu
    var = jnp.mean(d * d, axis=1, keepdims=True)
    return d * lax.rsqrt(var + LN_EPS) * g + b


def _merge_body(x_ref, oa_ref, ob_ref, ga_ref, gb_ref, wa_ref, wb_ref, wo_ref, g_ref, b_ref, y_ref, yb_ref):
    ya = _dot(oa_ref[...], wa_ref[...])
    yb = _dot(ob_ref[...], wb_ref[...])
    mixin = jax.nn.sigmoid(ga_ref[...].astype(F32)) * ya + jax.nn.sigmoid(gb_ref[...].astype(F32)) * yb
    mix = _dot(mixin.astype(BF16), wo_ref[...])
    y = _layer_norm(DEEPNORM_ALPHA * x_ref[...] + mix, g_ref[...], b_ref[...])
    y_ref[...] = y
    yb_ref[...] = y.astype(BF16)


def _merge(x2, oa, ob, mgate, wa, wb, wo, g, b):
    m = x2.shape[0]
    tm, d = MERGE_TM, D_MODEL
    row = lambda wd, col=0: pl.BlockSpec((tm, wd), lambda i, col=col: (i, col))
    return pl.pallas_call(
        _merge_body,
        grid=(m // tm,),
        in_specs=[row(d), row(NSA_Q_W), row(GDN_W), row(d, 0), row(d, 1),
                  _const_spec(wa.shape), _const_spec(wb.shape), _const_spec(wo.shape),
                  _const_spec(g.shape), _const_spec(b.shape)],
        out_specs=[row(d), row(d)],
        out_shape=[jax.ShapeDtypeStruct((m, d), F32), jax.ShapeDtypeStruct((m, d), BF16)],
        compiler_params=_cparams("parallel"),
        name="merge",
    )(x2, oa, ob, mgate, mgate, wa, wb, wo, g, b)


FFN_TM = 512
FFN_HALO = 16
FFN_CK = 256


def _ffn_body(x_ref, xb_ref, prev_ref, wu_ref, cw_ref, wd_ref, g_ref, b_ref, out_ref, act_s,
              *, tiles_per_seq):
    i = pl.program_id(0)
    prev = prev_ref[...]
    prev = jnp.where(i % tiles_per_seq == 0, jnp.zeros_like(prev), prev)
    xc = jnp.concatenate([prev, xb_ref[...]], axis=0)

    def conv(h, c0):
        out = cw_ref[FFN_CONV - 1:FFN_CONV, c0:c0 + FFN_CK] * h[FFN_HALO:]
        for j in range(FFN_CONV - 1):
            shifted = pltpu.roll(h, FFN_CONV - 1 - j, axis=0)[FFN_HALO:]
            out = out + cw_ref[j:j + 1, c0:c0 + FFN_CK] * shifted
        return out

    for c in range(FFN_DIM // FFN_CK):
        c0 = c * FFN_CK
        hg = conv(_dot(xc, wu_ref[:, c0:c0 + FFN_CK]), c0)
        hv = conv(_dot(xc, wu_ref[:, FFN_DIM + c0:FFN_DIM + c0 + FFN_CK]), FFN_DIM + c0)
        act_s[:, c0:c0 + FFN_CK] = (jax.nn.silu(hg) * hv).astype(BF16)
    f = _dot(act_s[...], wd_ref[...])
    out_ref[...] = _layer_norm(DEEPNORM_ALPHA * x_ref[...] + f, g_ref[...], b_ref[...])


def _ffn(x1, x1b, wu, cw, wd, g, b, seq):
    m = x1.shape[0]
    tm, d = FFN_TM, D_MODEL
    return pl.pallas_call(
        functools.partial(_ffn_body, tiles_per_seq=seq // tm),
        grid=(m // tm,),
        in_specs=[pl.BlockSpec((tm, d), lambda i: (i, 0)),
                  pl.BlockSpec((tm, d), lambda i: (i, 0)),
                  pl.BlockSpec((FFN_HALO, d), lambda i: (jnp.maximum(i * (tm // FFN_HALO) - 1, 0), 0)),
                  _const_spec(wu.shape), _const_spec(cw.shape), _const_spec(wd.shape), _const_spec(g.shape), _const_spec(b.shape)],
        out_specs=pl.BlockSpec((tm, d), lambda i: (i, 0)),
        out_shape=jax.ShapeDtypeStruct((m, d), F32),
        scratch_shapes=[pltpu.VMEM((tm, FFN_DIM), BF16)],
        compiler_params=_cparams("parallel"),
        name="ffn",
    )(x1, x1b, x1b, wu, cw, wd, g, b)


def _lane_vec(vals, lane0):
    return jnp.zeros((1, LANES), F32).at[0, lane0:lane0 + vals.shape[0]].set(vals.astype(F32))


def _layer(x, w_in, cmp_pos, cmp_w1, cmp_w2, w_nsa_out, gdn_conv_w, gdn_a_log, gdn_dt_bias, gdn_norm_w,
           w_gdn_out, w_o, ln1_g, ln1_b, ffn_w_up, ffn_conv_w, ffn_w_down, ln2_g, ln2_b):
    b, s, d = x.shape
    m = b * s
    x2 = x.reshape(m, d)
    w_rows, w_gdn, w_t, w_g = _wprep(w_in)
    keys, cmpkv, ggate, mgate, qvt, gt = _inproj(x2, w_rows, w_t, w_g)

    consts = _nsa_consts(s)
    post, w2sel = _compress_weights(cmp_pos, cmp_w2)
    cmp_kv = _compress(cmpkv.reshape(b, s // CMP_STRIDE, CMP_STRIDE * 256), cmp_w1, post, w2sel, consts["cmp_aug"])
    o_nsa = _nsa(qvt, keys.reshape(b, s, 256), cmp_kv, gt, consts, b, s)

    ck = GDN_CHUNK
    tri = np.tril(np.ones((ck, ck), np.float32))
    lt = jnp.asarray(np.kron(np.eye(GDN_BLK // ck, dtype=np.float32), tri))
    nq, co, eg = _gdn_intra(x, w_gdn, gdn_conv_w,
                                        _lane_vec(gdn_a_log, SM_DECAY), _lane_vec(gdn_dt_bias, SM_DECAY), lt)
    o_gdn = _gdn_scan(nq, co, eg, ggate.reshape(b, s, GDN_W), gdn_norm_w.reshape(1, GDN_HEAD_DIM))

    x1, x1b = _merge(x2, o_nsa.reshape(m, NSA_Q_W), o_gdn.reshape(m, GDN_W), mgate,
                     w_nsa_out.astype(BF16), w_gdn_out.astype(BF16), w_o.astype(BF16),
                     ln1_g.reshape(1, d), ln1_b.reshape(1, d))
    out = _ffn(x1, x1b, ffn_w_up.astype(BF16), ffn_conv_w, ffn_w_down.astype(BF16),
               ln2_g.reshape(1, d), ln2_b.reshape(1, d), s)
    return out.reshape(b, s, d)


def kernel(x, w_in, nsa_cmp_pos, nsa_cmp_w1, nsa_cmp_w2, w_nsa_out, gdn_conv_w, gdn_a_log, gdn_dt_bias, gdn_norm_w, w_gdn_out, w_o, ln1_g, ln1_b, ffn_w_up, ffn_conv_w, ffn_w_down, ln2_g, ln2_b):
    for l in range(DEPTH):
        x = _layer(x, w_in[l], nsa_cmp_pos[l], nsa_cmp_w1[l], nsa_cmp_w2[l], w_nsa_out[l], gdn_conv_w[l],
                   gdn_a_log[l], gdn_dt_bias[l], gdn_norm_w[l], w_gdn_out[l], w_o[l], ln1_g[l], ln1_b[l],
                   ffn_w_up[l], ffn_conv_w[l], ffn_w_down[l], ln2_g[l], ln2_b[l])
    return x
```

```python
import functools

import numpy as np
import jax
import jax.numpy as jnp
from jax import lax
from jax.experimental import pallas as pl
from jax.experimental.pallas import tpu as pltpu

F32 = jnp.float32
BF16 = jnp.bfloat16

D_MODEL = 1024
NSA_HEADS = 8
NSA_KV_GROUPS = 2
NSA_REP = NSA_HEADS // NSA_KV_GROUPS
NSA_HEAD_DIM = 64
CMP_LEN = 32
CMP_STRIDE = 16
SLC_LEN = 64
SLC_TOPK = 8
WIN_LEN = 512
FORCE_SCORE = 1.0e4
NEG = -1.0e30
GDN_HEADS = 4
GDN_HEAD_DIM = 128
GDN_CONV = 4
GDN_CHUNK = 64
FFN_DIM = 2816
FFN_CONV = 3
DEPTH = 1
DEEPNORM_ALPHA = (2.0 * DEPTH) ** 0.25
LN_EPS = 1e-5
RMS_EPS = 1e-6

NSA_Q_W = NSA_HEADS * NSA_HEAD_DIM
NSA_KV_W = NSA_KV_GROUPS * NSA_HEAD_DIM
GDN_W = GDN_HEADS * GDN_HEAD_DIM

LANES = 128
VMEM_LIMIT_BYTES = 56 * 1024 * 1024

AUG_SEL0 = 64
AUG_POS_HI = 96
AUG_POS_LO = 97
AUG_PAD = 98
BIG = 2.0 ** 100
Q_TILE = 128
N_SLC = 32
V_ROWS = 80
SEL_KC = 512
NSA_SUB = 2

NT_DIMS = (((1,), (1,)), ((), ()))


def _dot(a, b, **kw):
    return jnp.dot(a, b, preferred_element_type=F32, **kw)


def _dot_nt(a, b, **kw):
    return lax.dot_general(a, b, NT_DIMS, preferred_element_type=F32, **kw)


def _cparams(*sem):
    return pltpu.CompilerParams(dimension_semantics=sem, vmem_limit_bytes=VMEM_LIMIT_BYTES)


def _const_spec(shape):
    nd = len(shape)
    return pl.BlockSpec(shape, lambda *_: (0,) * nd, pipeline_mode=pl.Buffered(1))


_IN_WIDTHS = (NSA_Q_W,) + (NSA_KV_W,) * 6 + (3 * NSA_HEADS, 3 * GDN_W, GDN_HEADS, GDN_HEADS, GDN_W, 2 * D_MODEL)
(_C_Q, _C_CK, _C_CV, _C_SK, _C_SV, _C_WK, _C_WV, _C_GATE, _C_GQKV, _C_BETA, _C_DECAY, _C_GGATE, _C_MERGE,
 IN_WIDTH) = (int(v) for v in np.cumsum((0,) + _IN_WIDTHS))
_C_SMALL = _C_BETA // LANES * LANES
SM_BETA = _C_BETA - _C_SMALL
SM_DECAY = _C_DECAY - _C_SMALL
_INPROJ_GROUPS = (("keys", 256, BF16), ("cmp", 256, BF16), ("ggate", GDN_W, BF16), ("merge", 2 * D_MODEL, BF16))
_INPROJ_WIDTH = sum(w for _, w, _ in _INPROJ_GROUPS)
_GDN_PROJ_WIDTH = 3 * GDN_W + LANES
_INPROJ_T_ROWS = NSA_Q_W + 4 * NSA_HEAD_DIM
_GATE_T_ROWS = 32
INPROJ_TM = 512
WPREP_TK = 128


def _wprep_body(w_ref, rows_ref, gdn_ref, wt_ref, wg_ref):
    hd = NSA_HEAD_DIM
    cols = lambda c0, n: w_ref[:, c0:c0 + n]
    for j, c0 in enumerate((_C_SK, _C_WK, _C_SK + hd, _C_WK + hd)):
        rows_ref[:, j * hd:(j + 1) * hd] = cols(c0, hd).astype(BF16)
    c = 4 * hd
    for c0, n in ((_C_CK, 2 * NSA_KV_W), (_C_GGATE, GDN_W), (_C_MERGE, 2 * D_MODEL)):
        rows_ref[:, c:c + n] = cols(c0, n).astype(BF16)
        c += n
    gdn_ref[:, :3 * GDN_W] = cols(_C_GQKV, 3 * GDN_W).astype(BF16)
    gdn_ref[:, 3 * GDN_W:] = cols(_C_SMALL, LANES).astype(BF16)
    for j in range(NSA_Q_W // LANES):
        wt_ref[j * LANES:(j + 1) * LANES, :] = cols(_C_Q + j * LANES, LANES).T.astype(BF16)
    sv_t = cols(_C_SV, NSA_KV_W).T
    wv_t = cols(_C_WV, NSA_KV_W).T
    for j, t in enumerate((sv_t[:hd], wv_t[:hd], sv_t[hd:], wv_t[hd:])):
        wt_ref[NSA_Q_W + j * hd:NSA_Q_W + (j + 1) * hd, :] = t.astype(BF16)
    wg_ref[...] = cols(_C_GATE, LANES).T[:_GATE_T_ROWS].astype(BF16)


def _wprep(w_in):
    k = w_in.shape[0]
    tk = WPREP_TK
    return pl.pallas_call(
        _wprep_body,
        grid=(k // tk,),
        in_specs=[pl.BlockSpec((tk, IN_WIDTH), lambda i: (i, 0))],
        out_specs=[pl.BlockSpec((tk, _INPROJ_WIDTH), lambda i: (i, 0)),
                   pl.BlockSpec((tk, _GDN_PROJ_WIDTH), lambda i: (i, 0)),
                   pl.BlockSpec((_INPROJ_T_ROWS, tk), lambda i: (0, i)),
                   pl.BlockSpec((_GATE_T_ROWS, tk), lambda i: (0, i))],
        out_shape=[jax.ShapeDtypeStruct((k, _INPROJ_WIDTH), BF16), jax.ShapeDtypeStruct((k, _GDN_PROJ_WIDTH), BF16),
                   jax.ShapeDtypeStruct((_INPROJ_T_ROWS, k), BF16), jax.ShapeDtypeStruct((_GATE_T_ROWS, k), BF16)],
        compiler_params=_cparams("parallel"),
        name="wprep",
    )(w_in)


def _inproj_body(x_ref, w_ref, wt_ref, wg_ref, keys_ref, cmp_ref, ggate_ref, merge_ref, qvt_ref, gt_ref, cmp_s):
    x = x_ref[...].astype(BF16)
    outs = (keys_ref, None, ggate_ref, merge_ref)
    c0 = 0
    for ref, (name, width, _) in zip(outs, _INPROJ_GROUPS):
        for s in range(0, width, 512):
            e = min(s + 512, width)
            res = _dot(x, w_ref[:, c0 + s:c0 + e])
            if name == "cmp":
                for j in range(width // LANES):
                    cmp_s[j] = res[:, j * LANES:(j + 1) * LANES]
            else:
                ref[:, s:e] = res.astype(ref.dtype)
        c0 += width
    nblk = cmp_ref.shape[0]
    for l in range(CMP_STRIDE):
        for j in range(cmp_s.shape[0]):
            cmp_ref[:, l * 256 + j * LANES:l * 256 + (j + 1) * LANES] = (
                cmp_s[j, pl.ds(l, nblk, stride=CMP_STRIDE), :].astype(BF16))
    for s in range(0, _INPROJ_T_ROWS, 256):
        qvt_ref[s:s + 256, :] = _dot_nt(wt_ref[s:s + 256, :], x).astype(qvt_ref.dtype)
    gt_ref[...] = _dot_nt(wg_ref[...], x)


def _inproj(x2, w_rows, w_t, w_g):
    m = x2.shape[0]
    tm = INPROJ_TM
    row_major = [(n, wd, dt) for n, wd, dt in _INPROJ_GROUPS if n != "cmp"]
    specs = {n: (pl.BlockSpec((tm, wd), lambda i: (i, 0)), jax.ShapeDtypeStruct((m, wd), dt))
             for n, wd, dt in row_major}
    specs["cmp"] = (pl.BlockSpec((tm // CMP_STRIDE, CMP_STRIDE * 256), lambda i: (i, 0)),
                    jax.ShapeDtypeStruct((m // CMP_STRIDE, CMP_STRIDE * 256), BF16))
    order = [n for n, _, _ in _INPROJ_GROUPS]
    return pl.pallas_call(
        _inproj_body,
        grid=(m // tm,),
        in_specs=[pl.BlockSpec((tm, D_MODEL), lambda i: (i, 0)), _const_spec(w_rows.shape),
                  _const_spec(w_t.shape), _const_spec(w_g.shape)],
        out_specs=[specs[n][0] for n in order]
        + [pl.BlockSpec((_INPROJ_T_ROWS, tm), lambda i: (0, i)), pl.BlockSpec((_GATE_T_ROWS, tm), lambda i: (0, i))],
        out_shape=[specs[n][1] for n in order]
        + [jax.ShapeDtypeStruct((_INPROJ_T_ROWS, m), BF16), jax.ShapeDtypeStruct((_GATE_T_ROWS, m), F32)],
        scratch_shapes=[pltpu.VMEM((256 // LANES, tm, LANES), F32)],
        compiler_params=_cparams("parallel"),
        name="inproj",
    )(x2, w_rows, w_t, w_g)


def _compress_weights(cmp_pos, cmp_w2):
    hd, half = NSA_HEAD_DIM, CMP_LEN // 2
    posr = cmp_pos.reshape(2, 2, half, hd)
    post = jnp.broadcast_to(posr.transpose(1, 2, 0, 3)[:, :, :, None, :], (2, half, 2, 2, hd))
    post = jnp.concatenate([post.reshape(2, half * 4 * hd), jnp.zeros((6, half * 4 * hd), cmp_pos.dtype)], axis=0)
    w2sel = jnp.zeros((2, 2, 2 * hd, LANES), cmp_w2.dtype)
    for g in range(2):
        w2sel = w2sel.at[:, g, g * hd:(g + 1) * hd, :hd].set(cmp_w2)
    return post.astype(BF16), w2sel.reshape(4, 2 * hd, LANES).astype(BF16)


def _compress_body(t_ref, w1_ref, pos_ref, w2_ref, aug_ref, out_ref, w1e_s):
    hd, half_len = NSA_HEAD_DIM, CMP_LEN // 2

    @pl.when(pl.program_id(0) == 0)
    def _():
        w1e_s[...] = jnp.zeros(w1e_s.shape, BF16)
        for which in range(2):
            for half in range(2):
                for l in range(half_len):
                    blk = w1_ref[which, (half * half_len + l) * hd:(half * half_len + l + 1) * hd, :].astype(BF16)
                    for g in range(NSA_KV_GROUPS):
                        r0 = l * 256 + which * LANES + g * hd
                        c0 = half * 256 + which * LANES + g * hd
                        w1e_s[r0:r0 + hd, c0:c0 + hd] = blk

    p = _dot(t_ref[0], w1e_s[...])
    pp = _dot(pos_ref[...], w1e_s[...])
    nxt = pltpu.roll(p[:, 256:], p.shape[0] - 1, axis=0)
    pre = p[:, :256] + nxt + pp[0:1, :256] + pp[1:2, 256:]
    h = jax.nn.gelu(pre).astype(BF16)
    n_idx = lax.broadcasted_iota(jnp.int32, (p.shape[0], LANES), 0)
    real = n_idx < p.shape[0] - 1
    for which in range(2):
        hw = h[:, which * LANES:(which + 1) * LANES]
        for g in range(2):
            o = jnp.where(real, _dot(hw, w2_ref[which * 2 + g]) + aug_ref[which], 0.0)
            out_ref[0, which * 2 + g] = (o if which == 0 else o.T).astype(out_ref.dtype)


def _compress(t2, w1, post, w2sel, aug):
    b, nblk, _ = t2.shape
    return pl.pallas_call(
        _compress_body,
        grid=(b,),
        in_specs=[pl.BlockSpec((1, nblk, CMP_STRIDE * 256), lambda i: (i, 0, 0)),
                  _const_spec(w1.shape), _const_spec(post.shape), _const_spec(w2sel.shape), _const_spec(aug.shape)],
        out_specs=pl.BlockSpec((1, 4, nblk, LANES), lambda i: (i, 0, 0, 0)),
        out_shape=jax.ShapeDtypeStruct((b, 4, nblk, LANES), BF16),
        scratch_shapes=[pltpu.VMEM((CMP_STRIDE * 256, 512), BF16)],
        compiler_params=_cparams("arbitrary"),
        name="compress",
    )(t2, w1, post, w2sel, aug)


def _nsa_consts(s):
    hd, rep = NSA_HEAD_DIM, NSA_REP
    t = np.arange(s)
    kx_win = np.zeros((s + WIN_LEN, hd), np.float32)
    kx_win[WIN_LEN + t, AUG_POS_HI - hd] = t // 256
    kx_win[WIN_LEN + t, AUG_POS_LO - hd] = t % 256
    kx_win[:WIN_LEN, AUG_PAD - hd] = 1.0
    kx_sel = kx_win.copy()
    kx_sel[WIN_LEN + t, t // SLC_LEN] = 1.0
    vx_win = np.zeros((V_ROWS - hd, s + WIN_LEN), np.float32)
    vx_win[0, WIN_LEN:] = 1.0
    vx_sel = vx_win
    n_cmp = s // CMP_STRIDE
    cmp_aug = np.zeros((2, n_cmp, LANES), np.float32)
    end = np.arange(n_cmp) * CMP_STRIDE + CMP_LEN - 1
    cmp_aug[0, :, AUG_POS_HI] = end // 256
    cmp_aug[0, :, AUG_POS_LO] = end % 256
    qx = np.zeros((NSA_KV_GROUPS, LANES - AUG_POS_HI, rep * Q_TILE), np.float32)
    for h in range(NSA_HEADS):
        slope = 2.0 ** (-8.0 * (h + 1) / NSA_HEADS)
        lanes = slice((h % rep) * Q_TILE, (h % rep + 1) * Q_TILE)
        qx[h // rep, 0, lanes] = slope * 256.0
        qx[h // rep, 1, lanes] = slope
        qx[h // rep, AUG_PAD - AUG_POS_HI, lanes] = -BIG
    c0 = np.arange(n_cmp)[None, :] * CMP_STRIDE
    s0 = np.arange(s // SLC_LEN)[:, None] * SLC_LEN
    ov_t = ((c0 < s0 + SLC_LEN) & (c0 + CMP_LEN > s0)).astype(np.float32)
    ov_t[:, (s - CMP_LEN) // CMP_STRIDE + 1:] = 0.0
    kk = np.arange(Q_TILE)[:, None]
    qq = np.arange(Q_TILE)[None, :]
    causal = np.tile(np.where(kk <= qq, 0.0, NEG).astype(np.float32), (1, rep))
    after = np.tile(np.where(kk > qq, 0.0, NEG).astype(np.float32), (1, rep))
    j = jnp.asarray
    return dict(kx_sel=j(kx_sel, BF16), kx_win=j(kx_win, BF16), vx_sel=j(vx_sel, BF16), vx_win=j(vx_win, BF16),
                cmp_aug=j(cmp_aug), qx=j(qx), ov_t=j(ov_t), causal=j(causal), after=j(after))


def _nsa_body(qt_ref, k_ref, vt_ref, kc_ref, vct_ref, gt_ref, kxs_ref, kxw_ref, vxs_ref, vxw_ref, qx_ref, ovt_ref,
              causal_ref, after_ref, out_ref, ks_s, kw_s, vs_s, vw_s):
    hd, rep, tq = NSA_HEAD_DIM, NSA_REP, Q_TILE
    nq = rep * tq
    i = pl.program_id(2)

    @pl.when(i == 0)
    def _():
        keys = k_ref[0]
        ks_s[:WIN_LEN, :hd] = jnp.zeros((WIN_LEN, hd), BF16)
        ks_s[WIN_LEN:, :hd] = keys[:, :hd]
        ks_s[:, hd:] = kxs_ref[...]
        kw_s[:WIN_LEN, :hd] = jnp.zeros((WIN_LEN, hd), BF16)
        kw_s[WIN_LEN:, :hd] = keys[:, hd:]
        kw_s[:, hd:] = kxw_ref[...]
        vals = vt_ref[...]
        vs_s[:hd, :WIN_LEN] = jnp.zeros((hd, WIN_LEN), BF16)
        vs_s[:hd, WIN_LEN:] = vals[:hd]
        vs_s[hd:, :] = vxs_ref[...]
        vw_s[:hd, :WIN_LEN] = jnp.zeros((hd, WIN_LEN), BF16)
        vw_s[:hd, WIN_LEN:] = vals[hd:]
        vw_s[hd:, :] = vxw_ref[...]

    qx = qx_ref[0]
    sg_all = jax.nn.sigmoid(gt_ref[...])
    grp = pl.program_id(1)

    def front(sub, res):
        it = i * NSA_SUB + sub
        qt = qt_ref[:, sub * tq:(sub + 1) * tq]
        q64 = jnp.concatenate([qt[r * hd:(r + 1) * hd, :] for r in range(rep)], axis=1).astype(F32) * (hd ** -0.5)

        def q_aug(sel_rows):
            return jnp.concatenate([q64, sel_rows, qx], axis=0).astype(BF16)

        n_row = lax.broadcasted_iota(jnp.int32, (LANES, nq), 0)
        t_lane = it * tq + (lax.broadcasted_iota(jnp.int32, (LANES, nq), 1) & (tq - 1))
        valid = t_lane >= n_row * CMP_STRIDE + (CMP_LEN - 1)
        qa0 = q_aug(jnp.zeros((N_SLC, nq), F32))
        sc = jnp.where(valid, _dot(kc_ref[0, 0], qa0), NEG)
        mc = jnp.max(sc, axis=0, keepdims=True)
        ec = jnp.where(valid, jnp.exp(sc - mc), 0.0)
        lc = jnp.sum(ec, axis=0, keepdims=True)
        pc = ec * jnp.where(lc > 0.0, 1.0 / lc, 0.0)
        o_cmp = _dot(vct_ref[0, 0], pc.astype(BF16))[:hd]
        psum = pc[:, 0:tq] + pc[:, tq:2 * tq] + pc[:, 2 * tq:3 * tq] + pc[:, 3 * tq:4 * tq]
        score_t = _dot(ovt_ref[...], psum, precision=lax.Precision.HIGHEST)
        yield

        w0 = pl.multiple_of(it * tq, tq)
        s_w = _dot(kw_s[pl.ds(w0, WIN_LEN + tq), :], qa0)
        s_w = jnp.concatenate([s_w[:tq] + after_ref[...], s_w[tq:WIN_LEN], s_w[WIN_LEN:] + causal_ref[...]],
                              axis=0)
        p_w = jnp.exp(s_w - jnp.max(s_w, axis=0, keepdims=True))
        acc_w = _dot(vw_s[:, pl.ds(w0, WIN_LEN + tq)], p_w.astype(BF16))
        o_win = acc_w[:hd] * (1.0 / acc_w[hd:hd + 1])
        yield

        jb = lax.broadcasted_iota(jnp.int32, (N_SLC, tq), 0)
        cur = (it * tq + lax.broadcasted_iota(jnp.int32, (N_SLC, tq), 1)) // SLC_LEN
        forced = (jb == 0) | (jb == cur) | (jb == cur - 1)
        score_t = jnp.where(forced, FORCE_SCORE, jnp.where(jb <= cur, score_t, -1.0))
        rank = jnp.zeros((N_SLC, tq), F32)
        for jp in range(N_SLC):
            other = score_t[jp:jp + 1, :]
            ge = jnp.where(other >= score_t, 1.0, 0.0)
            gt = jnp.where(other > score_t, 1.0, 0.0)
            rank = rank + jnp.where(jb > jp, ge, gt)
        sel = rank < float(SLC_TOPK)
        qa = q_aug(jnp.concatenate([jnp.where(sel, 0.0, -BIG)] * rep, axis=1))
        lo_blk = jnp.min(jnp.where(sel & (jb >= 2) & (jb <= cur), jb.astype(F32), float(N_SLC)))
        lo_key = (lo_blk.astype(jnp.int32) // 2) * tq
        yield

        e_key = it * tq - WIN_LEN
        t0 = pl.multiple_of(jnp.where(e_key > 0, WIN_LEN, 0), tq)
        s_main = _dot(ks_s[pl.ds(w0, WIN_LEN + tq), :], qa)
        s_s = jnp.concatenate([_dot(ks_s[pl.ds(t0, tq), :], qa), s_main[:WIN_LEN],
                               s_main[WIN_LEN:] + causal_ref[...]], axis=0)
        m_s = jnp.max(s_s, axis=0, keepdims=True)
        p_s = jnp.exp(s_s - m_s).astype(BF16)
        acc_s = _dot(vs_s[:, pl.ds(t0, tq)], p_s[:tq]) + _dot(vs_s[:, pl.ds(w0, WIN_LEN + tq)], p_s[tq:])
        c_hi = (e_key - tq + SEL_KC - 1) // SEL_KC
        c_lo = jnp.where(lo_key < e_key, (lo_key - tq) // SEL_KC, c_hi)
        res.update(qa=qa, e_key=e_key, c_lo=c_lo, c_hi=c_hi, m_s=m_s, acc_s=acc_s, o_cmp=o_cmp, o_win=o_win)

    def tail(sub, f):
        qa, e_key = f["qa"], f["e_key"]

        def early_step(c, carry):
            m, acc = carry
            k0 = tq + c * SEL_KC
            start = pl.multiple_of(WIN_LEN + k0, tq)
            k_abs = k0 + lax.broadcasted_iota(jnp.int32, (SEL_KC, nq), 0)
            s = jnp.where(k_abs < e_key, _dot(ks_s[pl.ds(start, SEL_KC), :], qa), NEG)
            m_new = jnp.maximum(m, jnp.max(s, axis=0, keepdims=True))
            p = jnp.exp(s - m_new).astype(BF16)
            return m_new, acc * jnp.exp(m - m_new) + _dot(vs_s[:, pl.ds(start, SEL_KC)], p)

        _, acc_s = lax.fori_loop(f["c_lo"], f["c_hi"], early_step, (f["m_s"], f["acc_s"]))
        o_slc = acc_s[:hd] * (1.0 / acc_s[hd:hd + 1])

        sg = sg_all[:, sub * tq:(sub + 1) * tq]
        gate = lambda br, r: jnp.where(grp == 0, sg[br * NSA_HEADS + r:br * NSA_HEADS + r + 1],
                                       sg[br * NSA_HEADS + rep + r:br * NSA_HEADS + rep + r + 1])
        for pair in range(rep // 2):
            halves = []
            for r in (2 * pair, 2 * pair + 1):
                lanes = slice(r * tq, (r + 1) * tq)
                halves.append(gate(0, r) * f["o_cmp"][:, lanes] + gate(1, r) * o_slc[:, lanes]
                              + gate(2, r) * f["o_win"][:, lanes])
            out_ref[0, sub * tq:(sub + 1) * tq, pair * LANES:(pair + 1) * LANES] = (
                jnp.concatenate(halves, axis=0).T.astype(out_ref.dtype))

    fronts = [{} for _ in range(NSA_SUB)]
    for sub in range(NSA_SUB):
        for _ in front(sub, fronts[sub]):
            pass
    for sub in range(NSA_SUB):
        tail(sub, fronts[sub])


def _nsa(qvt, keys, cmp_kv, gt, consts, b, s):
    tqs = NSA_SUB * Q_TILE
    nt = s // tqs
    c = consts
    in_specs = [
        pl.BlockSpec((2 * LANES, tqs), lambda bi, g, i: (g, bi * nt + i)),
        pl.BlockSpec((1, s, LANES), lambda bi, g, i: (bi, 0, g)),
        pl.BlockSpec((LANES, s), lambda bi, g, i: (NSA_Q_W // LANES + g, bi)),
        pl.BlockSpec((1, 1, s // CMP_STRIDE, LANES), lambda bi, g, i: (bi, g, 0, 0)),
        pl.BlockSpec((1, 1, s // CMP_STRIDE, LANES), lambda bi, g, i: (bi, 2 + g, 0, 0)),
        pl.BlockSpec((_GATE_T_ROWS, tqs), lambda bi, g, i: (0, bi * nt + i)),
        _const_spec(c["kx_sel"].shape), _const_spec(c["kx_win"].shape), _const_spec(c["vx_sel"].shape),
        _const_spec(c["vx_win"].shape),
        pl.BlockSpec((1,) + c["qx"].shape[1:], lambda bi, g, i: (g, 0, 0)),
        _const_spec(c["ov_t"].shape), _const_spec(c["causal"].shape), _const_spec(c["after"].shape),
    ]
    return pl.pallas_call(
        _nsa_body,
        grid=(b, NSA_KV_GROUPS, nt),
        in_specs=in_specs,
        out_specs=pl.BlockSpec((1, tqs, 2 * LANES), lambda bi, g, i: (bi, i, g)),
        out_shape=jax.ShapeDtypeStruct((b, s, NSA_Q_W), BF16),
        scratch_shapes=[pltpu.VMEM((s + WIN_LEN, LANES), BF16), pltpu.VMEM((s + WIN_LEN, LANES), BF16),
                        pltpu.VMEM((V_ROWS, s + WIN_LEN), BF16), pltpu.VMEM((V_ROWS, s + WIN_LEN), BF16)],
        compiler_params=_cparams("parallel", "parallel", "arbitrary"),
        name="nsa",
    )(qvt, keys, qvt, cmp_kv, cmp_kv, gt, c["kx_sel"], c["kx_win"], c["vx_sel"], c["vx_win"], c["qx"], c["ov_t"],
      c["causal"], c["after"])


GDN_TS = 512
GDN_BLK = 128
GDN_HALO = 8
GDN_XHALO = 16
GDN_CHAIN_UNITS_PER_PREP_UNIT = 5
GDN_SCAN_ROWS = GDN_HEAD_DIM + GDN_CHUNK


def _gdn_prep(x_ref, prev_ref, w_ref, cw_ref, alog_ref, dtb_ref, lt_ref, first_tile, xp_s, buf):
    x_s, p_s, rhs_s, qg_s, aqk_s, kdt_s, eg_s = buf
    ts, dh, nh, blk = GDN_TS, GDN_HEAD_DIM, GDN_HEADS, GDN_BLK
    proj = _dot(x_ref[0].astype(BF16), w_ref[...])
    hist = _dot(prev_ref[0].astype(BF16), w_ref[:, :3 * GDN_W])[GDN_XHALO - GDN_HALO:]
    xp_s[0:GDN_HALO, :] = jnp.where(first_tile, 0.0, hist)
    xp_s[GDN_HALO:, :] = proj[:, :3 * GDN_W]
    yield
    act = []
    for blk_i in range(3 * nh):
        lanes = slice(blk_i * dh, (blk_i + 1) * dh)
        xp = xp_s[:, lanes]
        conv = cw_ref[0:1, lanes] * xp
        for j in range(1, GDN_CONV):
            conv = pltpu.roll(conv, 1, axis=0) + cw_ref[j:j + 1, lanes] * xp
        act.append(jax.nn.silu(conv[GDN_HALO:]))
        yield

    sm = proj[:, 3 * GDN_W:]
    beta = jax.nn.sigmoid(sm)
    g = -jnp.exp(alog_ref[...]) * jax.nn.softplus(sm + dtb_ref[...])
    gcum = jnp.concatenate([_dot(lt_ref[...], g[r:r + blk], precision=lax.Precision.HIGHEST)
                            for r in range(0, ts, blk)], axis=0)
    eg = jnp.exp(gcum)
    eg_s[...] = eg

    ri = lax.broadcasted_iota(jnp.int32, (blk, blk), 0)
    ci = lax.broadcasted_iota(jnp.int32, (blk, blk), 1)
    same = (ri // GDN_CHUNK) == (ci // GDN_CHUNK)
    causal = same & (ri >= ci)
    strict = same & (ri > ci)
    eye = (ri == ci).astype(F32)

    for pb in range(ts // blk):
        rows = slice(pb * blk, (pb + 1) * blk)
        gc = gcum[rows]
        gc_t = gc.T
        first = lax.broadcasted_iota(jnp.int32, (blk, LANES), 0) < GDN_CHUNK
        g_last = jnp.where(first, gc[GDN_CHUNK - 1:GDN_CHUNK, :], gc[blk - 1:blk, :])
        e_dec = jnp.exp(g_last - gc)
        for h in range(nh):
            q, k, v = act[h][rows], act[nh + h][rows], act[2 * nh + h][rows]
            q = q * lax.rsqrt(jnp.sum(q * q, axis=1, keepdims=True) + RMS_EPS) * (dh ** -0.5)
            k = k * lax.rsqrt(jnp.sum(k * k, axis=1, keepdims=True) + RMS_EPS)
            b_col = beta[rows, SM_BETA + h:SM_BETA + h + 1]
            eg_col = eg[rows, SM_DECAY + h:SM_DECAY + h + 1]
            gdiff = gc[:, SM_DECAY + h:SM_DECAY + h + 1] - gc_t[SM_DECAY + h:SM_DECAY + h + 1, :]
            decay = jnp.exp(jnp.where(causal, gdiff, NEG))
            kb = k * b_col
            kbf, kf, qf = kb.astype(BF16), k.astype(BF16), q.astype(BF16)
            a = jnp.where(strict, -_dot_nt(kbf, kf) * decay, 0.0)
            c = pb * nh + h
            x_s[c] = a.astype(BF16)
            p_s[c] = eye + a
            rhs_s[c] = jnp.concatenate([v * b_col, kb * eg_col], axis=1).astype(BF16)
            qg_s[c] = q * eg_col
            aqk_s[c] = jnp.where(causal, _dot_nt(qf, kf) * decay, 0.0).astype(BF16)
            kdt_s[c] = (k * e_dec[:, SM_DECAY + h:SM_DECAY + h + 1]).T.astype(BF16)
            yield


def _gdn_chains(buf, nq_ref, co_ref, eg_ref):
    x_s, p_s, rhs_s, qg_s, aqk_s, kdt_s, eg_s = buf
    dh, nh, blk = GDN_HEAD_DIM, GDN_HEADS, GDN_BLK
    n_chain = x_s.shape[0]
    eg_ref[0] = eg_s[...]
    for c in range(n_chain):
        y = x_s[c]
        x_s[c] = _dot(y, y).astype(BF16)
        yield
    for sweep in range(5):
        for c in range(n_chain):
            y = x_s[c]
            p = p_s[c]
            p_s[c] = p + _dot(p.astype(BF16), y)
            if sweep < 4:
                x_s[c] = _dot(y, y).astype(BF16)
            yield

    tok_half = lax.broadcasted_iota(jnp.int32, (blk, blk), 1) // GDN_CHUNK
    for c in range(n_chain):
        rhs_s[c] = _dot(p_s[c].astype(BF16), rhs_s[c]).astype(BF16)
        yield
    for c in range(n_chain):
        pb, h = divmod(c, nh)
        uw = rhs_s[c]
        a1 = _dot(aqk_s[c], uw)
        q_loc = qg_s[c] - a1[:, dh:]
        kdt = kdt_s[c]
        for half in range(blk // GDN_CHUNK):
            k1 = _dot(jnp.where(tok_half == half, kdt, jnp.zeros_like(kdt)), uw)
            n = pb * (blk // GDN_CHUNK) + half
            rows = slice(half * GDN_CHUNK, (half + 1) * GDN_CHUNK)
            nq_ref[0, h, n, :dh, :] = (-k1[:, dh:]).astype(BF16)
            nq_ref[0, h, n, dh:, :] = q_loc[rows].astype(BF16)
            co_ref[0, h, n, :dh, :] = k1[:, :dh].astype(BF16)
            co_ref[0, h, n, dh:, :] = a1[rows, :dh].astype(BF16)
        yield


def _interleave(major, minor, minor_per_major):
    for _ in major:
        for _ in range(minor_per_major):
            next(minor, None)
    for _ in minor:
        pass


def _gdn_intra_body(x_ref, prev_ref, w_ref, cw_ref, alog_ref, dtb_ref, lt_ref, nq_ref, co_ref, eg_ref, xp_s, *bufs,
                    tiles_per_seq, n_buf):
    j = pl.program_id(0)
    sets = (bufs[:n_buf], bufs[n_buf:])

    @pl.when(j == 0)
    def _():
        for ref in sets[1]:
            ref[...] = jnp.zeros(ref.shape, ref.dtype)

    first_tile = (j % tiles_per_seq) == 0
    for parity in range(2):
        @pl.when(j % 2 == parity)
        def _(parity=parity):
            chains = _gdn_chains(sets[1 - parity], nq_ref, co_ref, eg_ref)
            prep = _gdn_prep(x_ref, prev_ref, w_ref, cw_ref, alog_ref, dtb_ref, lt_ref, first_tile, xp_s,
                             sets[parity])
            _interleave(prep, chains, GDN_CHAIN_UNITS_PER_PREP_UNIT)


def _gdn_intra(x, w_gdn, conv_w, alog_l, dtb_l, lt):
    b, s, d = x.shape
    ts, nh, dh = GDN_TS, GDN_HEADS, GDN_HEAD_DIM
    tps = s // ts
    nt = b * tps
    n_chain = (ts // GDN_BLK) * nh
    sq = lambda dt: pltpu.VMEM((n_chain, GDN_BLK, GDN_BLK), dt)
    buf = lambda: [sq(BF16), sq(F32), pltpu.VMEM((n_chain, GDN_BLK, 2 * dh), BF16), sq(F32), sq(BF16), sq(BF16),
                   pltpu.VMEM((ts, LANES), F32)]
    src = lambda j: jnp.minimum(j, nt - 1)
    dst = lambda j: jnp.maximum(j - 1, 0)
    cspec = lambda: pl.BlockSpec((1, nh, ts // GDN_CHUNK, GDN_SCAN_ROWS, dh),
                                 lambda j: (dst(j) // tps, 0, dst(j) % tps, 0, 0))
    cshape = jax.ShapeDtypeStruct((b, nh, s // GDN_CHUNK, GDN_SCAN_ROWS, dh), BF16)
    return pl.pallas_call(
        functools.partial(_gdn_intra_body, tiles_per_seq=tps, n_buf=len(buf())),
        grid=(nt + 1,),
        in_specs=[
            pl.BlockSpec((1, ts, d), lambda j: (src(j) // tps, src(j) % tps, 0)),
            pl.BlockSpec((1, GDN_XHALO, d),
                         lambda j: (src(j) // tps, jnp.maximum((src(j) % tps) * (ts // GDN_XHALO) - 1, 0), 0)),
            _const_spec(w_gdn.shape), _const_spec(conv_w.shape), _const_spec(alog_l.shape), _const_spec(dtb_l.shape),
            _const_spec(lt.shape),
        ],
        out_specs=[cspec(), cspec(), pl.BlockSpec((1, ts, LANES), lambda j: (dst(j) // tps, dst(j) % tps, 0))],
        out_shape=[cshape, cshape, jax.ShapeDtypeStruct((b, s, LANES), F32)],
        scratch_shapes=[pltpu.VMEM((ts + GDN_HALO, 3 * GDN_W), F32)] + buf() + buf(),
        compiler_params=_cparams("arbitrary"),
        name="gdn_intra",
    )(x, x, w_gdn, conv_w, alog_l, dtb_l, lt)


def _gdn_scan_body(nq_ref, co_ref, eg_ref, gate_ref, nw_ref, out_ref, st_s):
    nh, dh, ck = GDN_HEADS, GDN_HEAD_DIM, GDN_CHUNK
    st_s[...] = jnp.zeros(st_s.shape, F32)

    def chunk(n, carry):
        r0 = pl.multiple_of(n * ck, ck)
        d_row = eg_ref[0, pl.ds(r0 + ck - 1, 1), :]
        for h in range(nh):
            st = st_s[h]
            res = _dot(nq_ref[0, h, n], st.astype(BF16)) + co_ref[0, h, n].astype(F32)
            st_s[h] = st * d_row[:, SM_DECAY + h:SM_DECAY + h + 1] + res[:dh]
            o = res[dh:]
            ms = jnp.mean(o * o, axis=1, keepdims=True)
            gt = gate_ref[0, pl.ds(r0, ck), h * dh:(h + 1) * dh].astype(F32)
            out_ref[0, pl.ds(r0, ck), h * dh:(h + 1) * dh] = (
                o * lax.rsqrt(ms + RMS_EPS) * nw_ref[...] * jax.nn.silu(gt)).astype(out_ref.dtype)
        return carry

    lax.fori_loop(0, nq_ref.shape[2], chunk, 0, unroll=2)


def _gdn_scan(nq, co, eg, ggate, norm_w):
    b, nh, nc, rows, dh = nq.shape
    s = nc * GDN_CHUNK
    cspec = lambda: pl.BlockSpec((1, nh, nc, rows, dh), lambda bi: (bi, 0, 0, 0, 0))
    return pl.pallas_call(
        _gdn_scan_body,
        grid=(b,),
        in_specs=[cspec(), cspec(),
                  pl.BlockSpec((1, s, LANES), lambda bi: (bi, 0, 0)),
                  pl.BlockSpec((1, s, GDN_W), lambda bi: (bi, 0, 0)),
                  _const_spec(norm_w.shape)],
        out_specs=pl.BlockSpec((1, s, GDN_W), lambda bi: (bi, 0, 0)),
        out_shape=jax.ShapeDtypeStruct((b, s, GDN_W), BF16),
        scratch_shapes=[pltpu.VMEM((nh, dh, dh), F32)],
        compiler_params=_cparams("parallel"),
        name="gdn_scan",
    )(nq, co, eg, ggate, norm_w)


MERGE_TM = 512


def _layer_norm(y, g, b):
    mu = jnp.mean(y, axis=1, keepdims=True)
    d = y - mu
    var = jnp.mean(d * d, axis=1, keepdims=True)
    return d * lax.rsqrt(var + LN_EPS) * g + b


def _merge_body(x_ref, oa_ref, ob_ref, ga_ref, gb_ref, wa_ref, wb_ref, wo_ref, g_ref, b_ref, y_ref, yb_ref):
    ya = _dot(oa_ref[...], wa_ref[...])
    yb = _dot(ob_ref[...], wb_ref[...])
    mixin = jax.nn.sigmoid(ga_ref[...].astype(F32)) * ya + jax.nn.sigmoid(gb_ref[...].astype(F32)) * yb
    mix = _dot(mixin.astype(BF16), wo_ref[...])
    y = _layer_norm(DEEPNORM_ALPHA * x_ref[...] + mix, g_ref[...], b_ref[...])
    y_ref[...] = y
    yb_ref[...] = y.astype(BF16)


def _merge(x2, oa, ob, mgate, wa, wb, wo, g, b):
    m = x2.shape[0]
    tm, d = MERGE_TM, D_MODEL
    row = lambda wd, col=0: pl.BlockSpec((tm, wd), lambda i, col=col: (i, col))
    return pl.pallas_call(
        _merge_body,
        grid=(m // tm,),
        in_specs=[row(d), row(NSA_Q_W), row(GDN_W), row(d, 0), row(d, 1),
                  _const_spec(wa.shape), _const_spec(wb.shape), _const_spec(wo.shape),
                  _const_spec(g.shape), _const_spec(b.shape)],
        out_specs=[row(d), row(d)],
        out_shape=[jax.ShapeDtypeStruct((m, d), F32), jax.ShapeDtypeStruct((m, d), BF16)],
        compiler_params=_cparams("parallel"),
        name="merge",
    )(x2, oa, ob, mgate, mgate, wa, wb, wo, g, b)


FFN_TM = 512
FFN_HALO = 16
FFN_CK = 256


def _ffn_body(x_ref, xb_ref, prev_ref, wu_ref, cw_ref, wd_ref, g_ref, b_ref, out_ref, act_s,
              *, tiles_per_seq):
    i = pl.program_id(0)
    prev = prev_ref[...]
    prev = jnp.where(i % tiles_per_seq == 0, jnp.zeros_like(prev), prev)
    xc = jnp.concatenate([prev, xb_ref[...]], axis=0)

    def conv(h, c0):
        out = cw_ref[FFN_CONV - 1:FFN_CONV, c0:c0 + FFN_CK] * h[FFN_HALO:]
        for j in range(FFN_CONV - 1):
            shifted = pltpu.roll(h, FFN_CONV - 1 - j, axis=0)[FFN_HALO:]
            out = out + cw_ref[j:j + 1, c0:c0 + FFN_CK] * shifted
        return out

    for c in range(FFN_DIM // FFN_CK):
        c0 = c * FFN_CK
        hg = conv(_dot(xc, wu_ref[:, c0:c0 + FFN_CK]), c0)
        hv = conv(_dot(xc, wu_ref[:, FFN_DIM + c0:FFN_DIM + c0 + FFN_CK]), FFN_DIM + c0)
        act_s[:, c0:c0 + FFN_CK] = (jax.nn.silu(hg) * hv).astype(BF16)
    f = _dot(act_s[...], wd_ref[...])
    out_ref[...] = _layer_norm(DEEPNORM_ALPHA * x_ref[...] + f, g_ref[...], b_ref[...])


def _ffn(x1, x1b, wu, cw, wd, g, b, seq):
    m = x1.shape[0]
    tm, d = FFN_TM, D_MODEL
    return pl.pallas_call(
        functools.partial(_ffn_body, tiles_per_seq=seq // tm),
        grid=(m // tm,),
        in_specs=[pl.BlockSpec((tm, d), lambda i: (i, 0)),
                  pl.BlockSpec((tm, d), lambda i: (i, 0)),
                  pl.BlockSpec((FFN_HALO, d), lambda i: (jnp.maximum(i * (tm // FFN_HALO) - 1, 0), 0)),
                  _const_spec(wu.shape), _const_spec(cw.shape), _const_spec(wd.shape), _const_spec(g.shape), _const_spec(b.shape)],
        out_specs=pl.BlockSpec((tm, d), lambda i: (i, 0)),
        out_shape=jax.ShapeDtypeStruct((m, d), F32),
        scratch_shapes=[pltpu.VMEM((tm, FFN_DIM), BF16)],
        compiler_params=_cparams("parallel"),
        name="ffn",
    )(x1, x1b, x1b, wu, cw, wd, g, b)


def _lane_vec(vals, lane0):
    return jnp.zeros((1, LANES), F32).at[0, lane0:lane0 + vals.shape[0]].set(vals.astype(F32))


def _layer(x, w_in, cmp_pos, cmp_w1, cmp_w2, w_nsa_out, gdn_conv_w, gdn_a_log, gdn_dt_bias, gdn_norm_w,
           w_gdn_out, w_o, ln1_g, ln1_b, ffn_w_up, ffn_conv_w, ffn_w_down, ln2_g, ln2_b):
    b, s, d = x.shape
    m = b * s
    x2 = x.reshape(m, d)
    w_rows, w_gdn, w_t, w_g = _wprep(w_in)
    keys, cmpkv, ggate, mgate, qvt, gt = _inproj(x2, w_rows, w_t, w_g)

    consts = _nsa_consts(s)
    post, w2sel = _compress_weights(cmp_pos, cmp_w2)
    cmp_kv = _compress(cmpkv.reshape(b, s // CMP_STRIDE, CMP_STRIDE * 256), cmp_w1, post, w2sel, consts["cmp_aug"])
    o_nsa = _nsa(qvt, keys.reshape(b, s, 256), cmp_kv, gt, consts, b, s)

    ck = GDN_CHUNK
    tri = np.tril(np.ones((ck, ck), np.float32))
    lt = jnp.asarray(np.kron(np.eye(GDN_BLK // ck, dtype=np.float32), tri))
    nq, co, eg = _gdn_intra(x, w_gdn, gdn_conv_w,
                                        _lane_vec(gdn_a_log, SM_DECAY), _lane_vec(gdn_dt_bias, SM_DECAY), lt)
    o_gdn = _gdn_scan(nq, co, eg, ggate.reshape(b, s, GDN_W), gdn_norm_w.reshape(1, GDN_HEAD_DIM))

    x1, x1b = _merge(x2, o_nsa.reshape(m, NSA_Q_W), o_gdn.reshape(m, GDN_W), mgate,
                     w_nsa_out.astype(BF16), w_gdn_out.astype(BF16), w_o.astype(BF16),
                     ln1_g.reshape(1, d), ln1_b.reshape(1, d))
    out = _ffn(x1, x1b, ffn_w_up.astype(BF16), ffn_conv_w, ffn_w_down.astype(BF16),
               ln2_g.reshape(1, d), ln2_b.reshape(1, d), s)
    return out.reshape(b, s, d)


def kernel(x, w_in, nsa_cmp_pos, nsa_cmp_w1, nsa_cmp_w2, w_nsa_out, gdn_conv_w, gdn_a_log, gdn_dt_bias, gdn_norm_w, w_gdn_out, w_o, ln1_g, ln1_b, ffn_w_up, ffn_conv_w, ffn_w_down, ln2_g, ln2_b):
    for l in range(DEPTH):
        x = _layer(x, w_in[l], nsa_cmp_pos[l], nsa_cmp_w1[l], nsa_cmp_w2[l], w_nsa_out[l], gdn_conv_w[l],
                   gdn_a_log[l], gdn_dt_bias[l], gdn_norm_w[l], w_gdn_out[l], w_o[l], ln1_g[l], ln1_b[l],
                   ffn_w_up[l], ffn_conv_w[l], ffn_w_down[l], ln2_g[l], ln2_b[l])
    return x
```

```python
import functools

import numpy as np
import jax
import jax.numpy as jnp
from jax import lax
from jax.experimental import pallas as pl
from jax.experimental.pallas import tpu as pltpu

F32 = jnp.float32
BF16 = jnp.bfloat16

D_MODEL = 1024
NSA_HEADS = 8
NSA_KV_GROUPS = 2
NSA_REP = NSA_HEADS // NSA_KV_GROUPS
NSA_HEAD_DIM = 64
CMP_LEN = 32
CMP_STRIDE = 16
SLC_LEN = 64
SLC_TOPK = 8
WIN_LEN = 512
FORCE_SCORE = 1.0e4
NEG = -1.0e30
GDN_HEADS = 4
GDN_HEAD_DIM = 128
GDN_CONV = 4
GDN_CHUNK = 64
FFN_DIM = 2816
FFN_CONV = 3
DEPTH = 1
DEEPNORM_ALPHA = (2.0 * DEPTH) ** 0.25
LN_EPS = 1e-5
RMS_EPS = 1e-6

NSA_Q_W = NSA_HEADS * NSA_HEAD_DIM
NSA_KV_W = NSA_KV_GROUPS * NSA_HEAD_DIM
GDN_W = GDN_HEADS * GDN_HEAD_DIM

LANES = 128
VMEM_LIMIT_BYTES = 56 * 1024 * 1024

AUG_SEL0 = 64
AUG_POS_HI = 96
AUG_POS_LO = 97
AUG_PAD = 98
BIG = 2.0 ** 100
Q_TILE = 128
N_SLC = 32
V_ROWS = 80
SEL_KC = 512
NSA_SUB = 2

NT_DIMS = (((1,), (1,)), ((), ()))


def _dot(a, b, **kw):
    return jnp.dot(a, b, preferred_element_type=F32, **kw)


def _dot_nt(a, b, **kw):
    return lax.dot_general(a, b, NT_DIMS, preferred_element_type=F32, **kw)


def _cparams(*sem):
    return pltpu.CompilerParams(dimension_semantics=sem, vmem_limit_bytes=VMEM_LIMIT_BYTES)


def _const_spec(shape):
    nd = len(shape)
    return pl.BlockSpec(shape, lambda *_: (0,) * nd, pipeline_mode=pl.Buffered(1))


_IN_WIDTHS = (NSA_Q_W,) + (NSA_KV_W,) * 6 + (3 * NSA_HEADS, 3 * GDN_W, GDN_HEADS, GDN_HEADS, GDN_W, 2 * D_MODEL)
(_C_Q, _C_CK, _C_CV, _C_SK, _C_SV, _C_WK, _C_WV, _C_GATE, _C_GQKV, _C_BETA, _C_DECAY, _C_GGATE, _C_MERGE,
 IN_WIDTH) = (int(v) for v in np.cumsum((0,) + _IN_WIDTHS))
_C_SMALL = _C_BETA // LANES * LANES
SM_BETA = _C_BETA - _C_SMALL
SM_DECAY = _C_DECAY - _C_SMALL
_INPROJ_GROUPS = (("keys", 256, BF16), ("cmp", 256, BF16), ("ggate", GDN_W, BF16), ("merge", 2 * D_MODEL, BF16))
_INPROJ_WIDTH = sum(w for _, w, _ in _INPROJ_GROUPS)
_GDN_PROJ_WIDTH = 3 * GDN_W + LANES
_INPROJ_T_ROWS = NSA_Q_W + 4 * NSA_HEAD_DIM
_GATE_T_ROWS = 32
INPROJ_TM = 512
WPREP_TK = 128


def _wprep_body(w_ref, rows_ref, gdn_ref, wt_ref, wg_ref):
    hd = NSA_HEAD_DIM
    feat = lambda c0, n: w_ref[0, c0:c0 + n, :]

    def put_t(ref, col, src):
        for r in range(0, src.shape[0], LANES):
            ref[:, col + r:col + r + LANES] = src[r:r + LANES].T.astype(BF16)

    for g in range(NSA_KV_GROUPS):
        put_t(rows_ref, 2 * g * hd, jnp.concatenate([feat(_C_SK + g * hd, hd), feat(_C_WK + g * hd, hd)], axis=0))
    c = 4 * hd
    for c0, n in ((_C_CK, 2 * NSA_KV_W), (_C_GGATE, GDN_W), (_C_MERGE, 2 * D_MODEL)):
        put_t(rows_ref, c, feat(c0, n))
        c += n
    put_t(gdn_ref, 0, feat(_C_GQKV, 3 * GDN_W))
    put_t(gdn_ref, 3 * GDN_W, feat(_C_SMALL, LANES))
    wt_ref[:NSA_Q_W, :] = feat(_C_Q, NSA_Q_W).astype(BF16)
    for j, c0 in enumerate((_C_SV, _C_WV, _C_SV + hd, _C_WV + hd)):
        wt_ref[NSA_Q_W + j * hd:NSA_Q_W + (j + 1) * hd, :] = feat(c0, hd).astype(BF16)
    wg_ref[...] = feat(_C_GATE, _GATE_T_ROWS).astype(BF16)


def _wprep(w_in_t, layer):
    k = w_in_t.shape[2]
    tk = WPREP_TK
    return pl.pallas_call(
        _wprep_body,
        grid=(k // tk,),
        in_specs=[pl.BlockSpec((1, IN_WIDTH, tk), lambda i: (layer, 0, i))],
        out_specs=[pl.BlockSpec((tk, _INPROJ_WIDTH), lambda i: (i, 0)),
                   pl.BlockSpec((tk, _GDN_PROJ_WIDTH), lambda i: (i, 0)),
                   pl.BlockSpec((_INPROJ_T_ROWS, tk), lambda i: (0, i)),
                   pl.BlockSpec((_GATE_T_ROWS, tk), lambda i: (0, i))],
        out_shape=[jax.ShapeDtypeStruct((k, _INPROJ_WIDTH), BF16), jax.ShapeDtypeStruct((k, _GDN_PROJ_WIDTH), BF16),
                   jax.ShapeDtypeStruct((_INPROJ_T_ROWS, k), BF16), jax.ShapeDtypeStruct((_GATE_T_ROWS, k), BF16)],
        compiler_params=_cparams("parallel"),
        name="wprep",
    )(w_in_t)


def _inproj_body(x_ref, w_ref, wt_ref, wg_ref, keys_ref, cmp_ref, ggate_ref, merge_ref, qvt_ref, gt_ref, cmp_s):
    x = x_ref[...].astype(BF16)
    outs = (keys_ref, None, ggate_ref, merge_ref)
    c0 = 0
    for ref, (name, width, _) in zip(outs, _INPROJ_GROUPS):
        for s in range(0, width, 512):
            e = min(s + 512, width)
            res = _dot(x, w_ref[:, c0 + s:c0 + e])
            if name == "cmp":
                for j in range(width // LANES):
                    cmp_s[j] = res[:, j * LANES:(j + 1) * LANES]
            else:
                ref[:, s:e] = res.astype(ref.dtype)
        c0 += width
    nblk = cmp_ref.shape[0]
    for l in range(CMP_STRIDE):
        for j in range(cmp_s.shape[0]):
            cmp_ref[:, l * 256 + j * LANES:l * 256 + (j + 1) * LANES] = (
                cmp_s[j, pl.ds(l, nblk, stride=CMP_STRIDE), :].astype(BF16))
    for s in range(0, _INPROJ_T_ROWS, 256):
        qvt_ref[s:s + 256, :] = _dot_nt(wt_ref[s:s + 256, :], x).astype(qvt_ref.dtype)
    gt_ref[...] = _dot_nt(wg_ref[...], x)


def _inproj(x2, w_rows, w_t, w_g):
    m = x2.shape[0]
    tm = INPROJ_TM
    row_major = [(n, wd, dt) for n, wd, dt in _INPROJ_GROUPS if n != "cmp"]
    specs = {n: (pl.BlockSpec((tm, wd), lambda i: (i, 0)), jax.ShapeDtypeStruct((m, wd), dt))
             for n, wd, dt in row_major}
    specs["cmp"] = (pl.BlockSpec((tm // CMP_STRIDE, CMP_STRIDE * 256), lambda i: (i, 0)),
                    jax.ShapeDtypeStruct((m // CMP_STRIDE, CMP_STRIDE * 256), BF16))
    order = [n for n, _, _ in _INPROJ_GROUPS]
    return pl.pallas_call(
        _inproj_body,
        grid=(m // tm,),
        in_specs=[pl.BlockSpec((tm, D_MODEL), lambda i: (i, 0)), _const_spec(w_rows.shape),
                  _const_spec(w_t.shape), _const_spec(w_g.shape)],
        out_specs=[specs[n][0] for n in order]
        + [pl.BlockSpec((_INPROJ_T_ROWS, tm), lambda i: (0, i)), pl.BlockSpec((_GATE_T_ROWS, tm), lambda i: (0, i))],
        out_shape=[specs[n][1] for n in order]
        + [jax.ShapeDtypeStruct((_INPROJ_T_ROWS, m), BF16), jax.ShapeDtypeStruct((_GATE_T_ROWS, m), F32)],
        scratch_shapes=[pltpu.VMEM((256 // LANES, tm, LANES), F32)],
        compiler_params=_cparams("parallel"),
        name="inproj",
    )(x2, w_rows, w_t, w_g)


def _compress_weights(cmp_pos, cmp_w2):
    hd, half = NSA_HEAD_DIM, CMP_LEN // 2
    posr = cmp_pos.reshape(2, 2, half, hd)
    post = jnp.broadcast_to(posr.transpose(1, 2, 0, 3)[:, :, :, None, :], (2, half, 2, 2, hd))
    post = jnp.concatenate([post.reshape(2, half * 4 * hd), jnp.zeros((6, half * 4 * hd), cmp_pos.dtype)], axis=0)
    w2sel = jnp.zeros((2, 2, 2 * hd, LANES), cmp_w2.dtype)
    for g in range(2):
        w2sel = w2sel.at[:, g, g * hd:(g + 1) * hd, :hd].set(cmp_w2)
    return post.astype(BF16), w2sel.reshape(4, 2 * hd, LANES).astype(BF16)


def _compress_body(t_ref, w1_ref, pos_ref, w2_ref, aug_ref, out_ref, w1e_s):
    hd, half_len = NSA_HEAD_DIM, CMP_LEN // 2

    @pl.when(pl.program_id(0) == 0)
    def _():
        w1e_s[...] = jnp.zeros(w1e_s.shape, BF16)
        for which in range(2):
            for half in range(2):
                for l in range(half_len):
                    blk = w1_ref[which, (half * half_len + l) * hd:(half * half_len + l + 1) * hd, :].astype(BF16)
                    for g in range(NSA_KV_GROUPS):
                        r0 = l * 256 + which * LANES + g * hd
                        c0 = half * 256 + which * LANES + g * hd
                        w1e_s[r0:r0 + hd, c0:c0 + hd] = blk

    p = _dot(t_ref[0], w1e_s[...])
    pp = _dot(pos_ref[...], w1e_s[...])
    nxt = pltpu.roll(p[:, 256:], p.shape[0] - 1, axis=0)
    pre = p[:, :256] + nxt + pp[0:1, :256] + pp[1:2, 256:]
    h = jax.nn.gelu(pre).astype(BF16)
    n_idx = lax.broadcasted_iota(jnp.int32, (p.shape[0], LANES), 0)
    real = n_idx < p.shape[0] - 1
    for which in range(2):
        hw = h[:, which * LANES:(which + 1) * LANES]
        for g in range(2):
            o = jnp.where(real, _dot(hw, w2_ref[which * 2 + g]) + aug_ref[which], 0.0)
            out_ref[0, which * 2 + g] = (o if which == 0 else o.T).astype(out_ref.dtype)


def _compress(t2, w1, post, w2sel, aug):
    b, nblk, _ = t2.shape
    return pl.pallas_call(
        _compress_body,
        grid=(b,),
        in_specs=[pl.BlockSpec((1, nblk, CMP_STRIDE * 256), lambda i: (i, 0, 0)),
                  _const_spec(w1.shape), _const_spec(post.shape), _const_spec(w2sel.shape), _const_spec(aug.shape)],
        out_specs=pl.BlockSpec((1, 4, nblk, LANES), lambda i: (i, 0, 0, 0)),
        out_shape=jax.ShapeDtypeStruct((b, 4, nblk, LANES), BF16),
        scratch_shapes=[pltpu.VMEM((CMP_STRIDE * 256, 512), BF16)],
        compiler_params=_cparams("arbitrary"),
        name="compress",
    )(t2, w1, post, w2sel, aug)


def _nsa_consts(s):
    hd, rep = NSA_HEAD_DIM, NSA_REP
    t = np.arange(s)
    kx_win = np.zeros((s + WIN_LEN, hd), np.float32)
    kx_win[WIN_LEN + t, AUG_POS_HI - hd] = t // 256
    kx_win[WIN_LEN + t, AUG_POS_LO - hd] = t % 256
    kx_win[:WIN_LEN, AUG_PAD - hd] = 1.0
    kx_sel = kx_win.copy()
    kx_sel[WIN_LEN + t, t // SLC_LEN] = 1.0
    vx_win = np.zeros((V_ROWS - hd, s + WIN_LEN), np.float32)
    vx_win[0, WIN_LEN:] = 1.0
    vx_sel = vx_win
    n_cmp = s // CMP_STRIDE
    cmp_aug = np.zeros((2, n_cmp, LANES), np.float32)
    end = np.arange(n_cmp) * CMP_STRIDE + CMP_LEN - 1
    cmp_aug[0, :, AUG_POS_HI] = end // 256
    cmp_aug[0, :, AUG_POS_LO] = end % 256
    qx = np.zeros((NSA_KV_GROUPS, LANES - AUG_POS_HI, rep * Q_TILE), np.float32)
    for h in range(NSA_HEADS):
        slope = 2.0 ** (-8.0 * (h + 1) / NSA_HEADS)
        lanes = slice((h % rep) * Q_TILE, (h % rep + 1) * Q_TILE)
        qx[h // rep, 0, lanes] = slope * 256.0
        qx[h // rep, 1, lanes] = slope
        qx[h // rep, AUG_PAD - AUG_POS_HI, lanes] = -BIG
    c0 = np.arange(n_cmp)[None, :] * CMP_STRIDE
    s0 = np.arange(s // SLC_LEN)[:, None] * SLC_LEN
    ov_t = ((c0 < s0 + SLC_LEN) & (c0 + CMP_LEN > s0)).astype(np.float32)
    ov_t[:, (s - CMP_LEN) // CMP_STRIDE + 1:] = 0.0
    kk = np.arange(Q_TILE)[:, None]
    qq = np.arange(Q_TILE)[None, :]
    causal = np.tile(np.where(kk <= qq, 0.0, NEG).astype(np.float32), (1, rep))
    after = np.tile(np.where(kk > qq, 0.0, NEG).astype(np.float32), (1, rep))
    j = jnp.asarray
    return dict(kx_sel=j(kx_sel, BF16), kx_win=j(kx_win, BF16), vx_sel=j(vx_sel, BF16), vx_win=j(vx_win, BF16),
                cmp_aug=j(cmp_aug), qx=j(qx), ov_t=j(ov_t), causal=j(causal), after=j(after))


def _nsa_body(qt_ref, k_ref, vt_ref, kc_ref, vct_ref, gt_ref, kxs_ref, kxw_ref, vxs_ref, vxw_ref, qx_ref, ovt_ref,
              causal_ref, after_ref, out_ref, ks_s, kw_s, vs_s, vw_s):
    hd, rep, tq = NSA_HEAD_DIM, NSA_REP, Q_TILE
    nq = rep * tq
    i = pl.program_id(2)

    @pl.when(i == 0)
    def _():
        keys = k_ref[0]
        ks_s[:WIN_LEN, :hd] = jnp.zeros((WIN_LEN, hd), BF16)
        ks_s[WIN_LEN:, :hd] = keys[:, :hd]
        ks_s[:, hd:] = kxs_ref[...]
        kw_s[:WIN_LEN, :hd] = jnp.zeros((WIN_LEN, hd), BF16)
        kw_s[WIN_LEN:, :hd] = keys[:, hd:]
        kw_s[:, hd:] = kxw_ref[...]
        vals = vt_ref[...]
        vs_s[:hd, :WIN_LEN] = jnp.zeros((hd, WIN_LEN), BF16)
        vs_s[:hd, WIN_LEN:] = vals[:hd]
        vs_s[hd:, :] = vxs_ref[...]
        vw_s[:hd, :WIN_LEN] = jnp.zeros((hd, WIN_LEN), BF16)
        vw_s[:hd, WIN_LEN:] = vals[hd:]
        vw_s[hd:, :] = vxw_ref[...]

    qx = qx_ref[0]
    sg_all = jax.nn.sigmoid(gt_ref[...])
    grp = pl.program_id(1)

    def front(sub, res):
        it = i * NSA_SUB + sub
        qt = qt_ref[:, sub * tq:(sub + 1) * tq]
        q64 = jnp.concatenate([qt[r * hd:(r + 1) * hd, :] for r in range(rep)], axis=1).astype(F32) * (hd ** -0.5)

        def q_aug(sel_rows):
            return jnp.concatenate([q64, sel_rows, qx], axis=0).astype(BF16)

        n_row = lax.broadcasted_iota(jnp.int32, (LANES, nq), 0)
        t_lane = it * tq + (lax.broadcasted_iota(jnp.int32, (LANES, nq), 1) & (tq - 1))
        valid = t_lane >= n_row * CMP_STRIDE + (CMP_LEN - 1)
        qa0 = q_aug(jnp.zeros((N_SLC, nq), F32))
        sc = jnp.where(valid, _dot(kc_ref[0, 0], qa0), NEG)
        mc = jnp.max(sc, axis=0, keepdims=True)
        ec = jnp.where(valid, jnp.exp(sc - mc), 0.0)
        lc = jnp.sum(ec, axis=0, keepdims=True)
        pc = ec * jnp.where(lc > 0.0, 1.0 / lc, 0.0)
        o_cmp = _dot(vct_ref[0, 0], pc.astype(BF16))[:hd]
        psum = pc[:, 0:tq] + pc[:, tq:2 * tq] + pc[:, 2 * tq:3 * tq] + pc[:, 3 * tq:4 * tq]
        score_t = _dot(ovt_ref[...], psum, precision=lax.Precision.HIGHEST)
        yield

        w0 = pl.multiple_of(it * tq, tq)
        s_w = _dot(kw_s[pl.ds(w0, WIN_LEN + tq), :], qa0)
        s_w = jnp.concatenate([s_w[:tq] + after_ref[...], s_w[tq:WIN_LEN], s_w[WIN_LEN:] + causal_ref[...]],
                              axis=0)
        p_w = jnp.exp(s_w - jnp.max(s_w, axis=0, keepdims=True))
        acc_w = _dot(vw_s[:, pl.ds(w0, WIN_LEN + tq)], p_w.astype(BF16))
        o_win = acc_w[:hd] * (1.0 / acc_w[hd:hd + 1])
        yield

        jb = lax.broadcasted_iota(jnp.int32, (N_SLC, tq), 0)
        cur = (it * tq + lax.broadcasted_iota(jnp.int32, (N_SLC, tq), 1)) // SLC_LEN
        forced = (jb == 0) | (jb == cur) | (jb == cur - 1)
        score_t = jnp.where(forced, FORCE_SCORE, jnp.where(jb <= cur, score_t, -1.0))
        rank = jnp.zeros((N_SLC, tq), F32)
        for jp in range(N_SLC):
            other = score_t[jp:jp + 1, :]
            ge = jnp.where(other >= score_t, 1.0, 0.0)
            gt = jnp.where(other > score_t, 1.0, 0.0)
            rank = rank + jnp.where(jb > jp, ge, gt)
        sel = rank < float(SLC_TOPK)
        qa = q_aug(jnp.concatenate([jnp.where(sel, 0.0, -BIG)] * rep, axis=1))
        lo_blk = jnp.min(jnp.where(sel & (jb >= 2) & (jb <= cur), jb.astype(F32), float(N_SLC)))
        lo_key = (lo_blk.astype(jnp.int32) // 2) * tq
        yield

        e_key = it * tq - WIN_LEN
        t0 = pl.multiple_of(jnp.where(e_key > 0, WIN_LEN, 0), tq)
        s_main = _dot(ks_s[pl.ds(w0, WIN_LEN + tq), :], qa)
        s_s = jnp.concatenate([_dot(ks_s[pl.ds(t0, tq), :], qa), s_main[:WIN_LEN],
                               s_main[WIN_LEN:] + causal_ref[...]], axis=0)
        m_s = jnp.max(s_s, axis=0, keepdims=True)
        p_s = jnp.exp(s_s - m_s).astype(BF16)
        acc_s = _dot(vs_s[:, pl.ds(t0, tq)], p_s[:tq]) + _dot(vs_s[:, pl.ds(w0, WIN_LEN + tq)], p_s[tq:])
        c_hi = (e_key - tq + SEL_KC - 1) // SEL_KC
        c_lo = jnp.where(lo_key < e_key, (lo_key - tq) // SEL_KC, c_hi)
        res.update(qa=qa, e_key=e_key, c_lo=c_lo, c_hi=c_hi, m_s=m_s, acc_s=acc_s, o_cmp=o_cmp, o_win=o_win)

    def tail(sub, f):
        qa, e_key = f["qa"], f["e_key"]

        def early_step(c, carry):
            m, acc = carry
            k0 = tq + c * SEL_KC
            start = pl.multiple_of(WIN_LEN + k0, tq)
            k_abs = k0 + lax.broadcasted_iota(jnp.int32, (SEL_KC, nq), 0)
            s = jnp.where(k_abs < e_key, _dot(ks_s[pl.ds(start, SEL_KC), :], qa), NEG)
            m_new = jnp.maximum(m, jnp.max(s, axis=0, keepdims=True))
            p = jnp.exp(s - m_new).astype(BF16)
            return m_new, acc * jnp.exp(m - m_new) + _dot(vs_s[:, pl.ds(start, SEL_KC)], p)

        _, acc_s = lax.fori_loop(f["c_lo"], f["c_hi"], early_step, (f["m_s"], f["acc_s"]))
        o_slc = acc_s[:hd] * (1.0 / acc_s[hd:hd + 1])

        sg = sg_all[:, sub * tq:(sub + 1) * tq]
        gate = lambda br, r: jnp.where(grp == 0, sg[br * NSA_HEADS + r:br * NSA_HEADS + r + 1],
                                       sg[br * NSA_HEADS + rep + r:br * NSA_HEADS + rep + r + 1])
        for pair in range(rep // 2):
            halves = []
            for r in (2 * pair, 2 * pair + 1):
                lanes = slice(r * tq, (r + 1) * tq)
                halves.append(gate(0, r) * f["o_cmp"][:, lanes] + gate(1, r) * o_slc[:, lanes]
                              + gate(2, r) * f["o_win"][:, lanes])
            out_ref[0, sub * tq:(sub + 1) * tq, pair * LANES:(pair + 1) * LANES] = (
                jnp.concatenate(halves, axis=0).T.astype(out_ref.dtype))

    fronts = [{} for _ in range(NSA_SUB)]
    for sub in range(NSA_SUB):
        for _ in front(sub, fronts[sub]):
            pass
    for sub in range(NSA_SUB):
        tail(sub, fronts[sub])


def _nsa(qvt, keys, cmp_kv, gt, consts, b, s):
    tqs = NSA_SUB * Q_TILE
    nt = s // tqs
    c = consts
    in_specs = [
        pl.BlockSpec((2 * LANES, tqs), lambda bi, g, i: (g, bi * nt + i)),
        pl.BlockSpec((1, s, LANES), lambda bi, g, i: (bi, 0, g)),
        pl.BlockSpec((LANES, s), lambda bi, g, i: (NSA_Q_W // LANES + g, bi)),
        pl.BlockSpec((1, 1, s // CMP_STRIDE, LANES), lambda bi, g, i: (bi, g, 0, 0)),
        pl.BlockSpec((1, 1, s // CMP_STRIDE, LANES), lambda bi, g, i: (bi, 2 + g, 0, 0)),
        pl.BlockSpec((_GATE_T_ROWS, tqs), lambda bi, g, i: (0, bi * nt + i)),
        _const_spec(c["kx_sel"].shape), _const_spec(c["kx_win"].shape), _const_spec(c["vx_sel"].shape),
        _const_spec(c["vx_win"].shape),
        pl.BlockSpec((1,) + c["qx"].shape[1:], lambda bi, g, i: (g, 0, 0)),
        _const_spec(c["ov_t"].shape), _const_spec(c["causal"].shape), _const_spec(c["after"].shape),
    ]
    return pl.pallas_call(
        _nsa_body,
        grid=(b, NSA_KV_GROUPS, nt),
        in_specs=in_specs,
        out_specs=pl.BlockSpec((1, tqs, 2 * LANES), lambda bi, g, i: (bi, i, g)),
        out_shape=jax.ShapeDtypeStruct((b, s, NSA_Q_W), BF16),
        scratch_shapes=[pltpu.VMEM((s + WIN_LEN, LANES), BF16), pltpu.VMEM((s + WIN_LEN, LANES), BF16),
                        pltpu.VMEM((V_ROWS, s + WIN_LEN), BF16), pltpu.VMEM((V_ROWS, s + WIN_LEN), BF16)],
        compiler_params=_cparams("parallel", "parallel", "arbitrary"),
        name="nsa",
    )(qvt, keys, qvt, cmp_kv, cmp_kv, gt, c["kx_sel"], c["kx_win"], c["vx_sel"], c["vx_win"], c["qx"], c["ov_t"],
      c["causal"], c["after"])


GDN_TS = 512
GDN_BLK = 128
GDN_HALO = 8
GDN_XHALO = 16
GDN_CHAIN_UNITS_PER_PREP_UNIT = 5
GDN_SCAN_ROWS = GDN_HEAD_DIM + GDN_CHUNK


def _gdn_prep(x_ref, prev_ref, w_ref, cw_ref, alog_ref, dtb_ref, lt_ref, first_tile, xp_s, buf):
    x_s, p_s, rhs_s, qg_s, aqk_s, kdt_s, eg_s = buf
    ts, dh, nh, blk = GDN_TS, GDN_HEAD_DIM, GDN_HEADS, GDN_BLK
    proj = _dot(x_ref[0].astype(BF16), w_ref[...])
    hist = _dot(prev_ref[0].astype(BF16), w_ref[:, :3 * GDN_W])[GDN_XHALO - GDN_HALO:]
    xp_s[0:GDN_HALO, :] = jnp.where(first_tile, 0.0, hist)
    xp_s[GDN_HALO:, :] = proj[:, :3 * GDN_W]
    yield
    act = []
    for blk_i in range(3 * nh):
        lanes = slice(blk_i * dh, (blk_i + 1) * dh)
        xp = xp_s[:, lanes]
        conv = cw_ref[0:1, lanes] * xp
        for j in range(1, GDN_CONV):
            conv = pltpu.roll(conv, 1, axis=0) + cw_ref[j:j + 1, lanes] * xp
        act.append(jax.nn.silu(conv[GDN_HALO:]))
        yield

    sm = proj[:, 3 * GDN_W:]
    beta = jax.nn.sigmoid(sm)
    g = -jnp.exp(alog_ref[...]) * jax.nn.softplus(sm + dtb_ref[...])
    gcum = jnp.concatenate([_dot(lt_ref[...], g[r:r + blk], precision=lax.Precision.HIGHEST)
                            for r in range(0, ts, blk)], axis=0)
    eg = jnp.exp(gcum)
    eg_s[...] = eg

    ri = lax.broadcasted_iota(jnp.int32, (blk, blk), 0)
    ci = lax.broadcasted_iota(jnp.int32, (blk, blk), 1)
    same = (ri // GDN_CHUNK) == (ci // GDN_CHUNK)
    causal = same & (ri >= ci)
    strict = same & (ri > ci)
    eye = (ri == ci).astype(F32)

    for pb in range(ts // blk):
        rows = slice(pb * blk, (pb + 1) * blk)
        gc = gcum[rows]
        gc_t = gc.T
        first = lax.broadcasted_iota(jnp.int32, (blk, LANES), 0) < GDN_CHUNK
        g_last = jnp.where(first, gc[GDN_CHUNK - 1:GDN_CHUNK, :], gc[blk - 1:blk, :])
        e_dec = jnp.exp(g_last - gc)
        for h in range(nh):
            q, k, v = act[h][rows], act[nh + h][rows], act[2 * nh + h][rows]
            q = q * lax.rsqrt(jnp.sum(q * q, axis=1, keepdims=True) + RMS_EPS) * (dh ** -0.5)
            k = k * lax.rsqrt(jnp.sum(k * k, axis=1, keepdims=True) + RMS_EPS)
            b_col = beta[rows, SM_BETA + h:SM_BETA + h + 1]
            eg_col = eg[rows, SM_DECAY + h:SM_DECAY + h + 1]
            gdiff = gc[:, SM_DECAY + h:SM_DECAY + h + 1] - gc_t[SM_DECAY + h:SM_DECAY + h + 1, :]
            decay = jnp.exp(jnp.where(causal, gdiff, NEG))
            kb = k * b_col
            kbf, kf, qf = kb.astype(BF16), k.astype(BF16), q.astype(BF16)
            a = jnp.where(strict, -_dot_nt(kbf, kf) * decay, 0.0)
            c = pb * nh + h
            x_s[c] = a.astype(BF16)
            p_s[c] = eye + a
            rhs_s[c] = jnp.concatenate([v * b_col, kb * eg_col], axis=1).astype(BF16)
            qg_s[c] = q * eg_col
            aqk_s[c] = jnp.where(causal, _dot_nt(qf, kf) * decay, 0.0).astype(BF16)
            kdt_s[c] = (k * e_dec[:, SM_DECAY + h:SM_DECAY + h + 1]).T.astype(BF16)
            yield


def _gdn_chains(buf, nq_ref, co_ref, eg_ref):
    x_s, p_s, rhs_s, qg_s, aqk_s, kdt_s, eg_s = buf
    dh, nh, blk = GDN_HEAD_DIM, GDN_HEADS, GDN_BLK
    n_chain = x_s.shape[0]
    eg_ref[0] = eg_s[...]
    for c in range(n_chain):
        y = x_s[c]
        x_s[c] = _dot(y, y).astype(BF16)
        yield
    for sweep in range(5):
        for c in range(n_chain):
            y = x_s[c]
            p = p_s[c]
            p_s[c] = p + _dot(p.astype(BF16), y)
            if sweep < 4:
                x_s[c] = _dot(y, y).astype(BF16)
            yield

    tok_half = lax.broadcasted_iota(jnp.int32, (blk, blk), 1) // GDN_CHUNK
    for c in range(n_chain):
        rhs_s[c] = _dot(p_s[c].astype(BF16), rhs_s[c]).astype(BF16)
        yield
    for c in range(n_chain):
        pb, h = divmod(c, nh)
        uw = rhs_s[c]
        a1 = _dot(aqk_s[c], uw)
        q_loc = qg_s[c] - a1[:, dh:]
        kdt = kdt_s[c]
        for half in range(blk // GDN_CHUNK):
            k1 = _dot(jnp.where(tok_half == half, kdt, jnp.zeros_like(kdt)), uw)
            n = pb * (blk // GDN_CHUNK) + half
            rows = slice(half * GDN_CHUNK, (half + 1) * GDN_CHUNK)
            nq_ref[0, h, n, :dh, :] = (-k1[:, dh:]).astype(BF16)
            nq_ref[0, h, n, dh:, :] = q_loc[rows].astype(BF16)
            co_ref[0, h, n, :dh, :] = k1[:, :dh].astype(BF16)
            co_ref[0, h, n, dh:, :] = a1[rows, :dh].astype(BF16)
        yield


def _interleave(major, minor, minor_per_major):
    for _ in major:
        for _ in range(minor_per_major):
            next(minor, None)
    for _ in minor:
        pass


def _gdn_intra_body(x_ref, prev_ref, w_ref, cw_ref, alog_ref, dtb_ref, lt_ref, nq_ref, co_ref, eg_ref, xp_s, *bufs,
                    tiles_per_seq, n_buf):
    j = pl.program_id(0)
    sets = (bufs[:n_buf], bufs[n_buf:])

    @pl.when(j == 0)
    def _():
        for ref in sets[1]:
            ref[...] = jnp.zeros(ref.shape, ref.dtype)

    first_tile = (j % tiles_per_seq) == 0
    for parity in range(2):
        @pl.when(j % 2 == parity)
        def _(parity=parity):
            chains = _gdn_chains(sets[1 - parity], nq_ref, co_ref, eg_ref)
            prep = _gdn_prep(x_ref, prev_ref, w_ref, cw_ref, alog_ref, dtb_ref, lt_ref, first_tile, xp_s,
                             sets[parity])
            _interleave(prep, chains, GDN_CHAIN_UNITS_PER_PREP_UNIT)


def _gdn_intra(x, w_gdn, conv_w, alog_l, dtb_l, lt):
    b, s, d = x.shape
    ts, nh, dh = GDN_TS, GDN_HEADS, GDN_HEAD_DIM
    tps = s // ts
    nt = b * tps
    n_chain = (ts // GDN_BLK) * nh
    sq = lambda dt: pltpu.VMEM((n_chain, GDN_BLK, GDN_BLK), dt)
    buf = lambda: [sq(BF16), sq(F32), pltpu.VMEM((n_chain, GDN_BLK, 2 * dh), BF16), sq(F32), sq(BF16), sq(BF16),
                   pltpu.VMEM((ts, LANES), F32)]
    src = lambda j: jnp.minimum(j, nt - 1)
    dst = lambda j: jnp.maximum(j - 1, 0)
    cspec = lambda: pl.BlockSpec((1, nh, ts // GDN_CHUNK, GDN_SCAN_ROWS, dh),
                                 lambda j: (dst(j) // tps, 0, dst(j) % tps, 0, 0))
    cshape = jax.ShapeDtypeStruct((b, nh, s // GDN_CHUNK, GDN_SCAN_ROWS, dh), BF16)
    return pl.pallas_call(
        functools.partial(_gdn_intra_body, tiles_per_seq=tps, n_buf=len(buf())),
        grid=(nt + 1,),
        in_specs=[
            pl.BlockSpec((1, ts, d), lambda j: (src(j) // tps, src(j) % tps, 0)),
            pl.BlockSpec((1, GDN_XHALO, d),
                         lambda j: (src(j) // tps, jnp.maximum((src(j) % tps) * (ts // GDN_XHALO) - 1, 0), 0)),
            _const_spec(w_gdn.shape), _const_spec(conv_w.shape), _const_spec(alog_l.shape), _const_spec(dtb_l.shape),
            _const_spec(lt.shape),
        ],
        out_specs=[cspec(), cspec(), pl.BlockSpec((1, ts, LANES), lambda j: (dst(j) // tps, dst(j) % tps, 0))],
        out_shape=[cshape, cshape, jax.ShapeDtypeStruct((b, s, LANES), F32)],
        scratch_shapes=[pltpu.VMEM((ts + GDN_HALO, 3 * GDN_W), F32)] + buf() + buf(),
        compiler_params=_cparams("arbitrary"),
        name="gdn_intra",
    )(x, x, w_gdn, conv_w, alog_l, dtb_l, lt)


def _gdn_scan_body(nq_ref, co_ref, eg_ref, gate_ref, nw_ref, out_ref, st_s):
    nh, dh, ck = GDN_HEADS, GDN_HEAD_DIM, GDN_CHUNK
    st_s[...] = jnp.zeros(st_s.shape, F32)

    def chunk(n, carry):
        r0 = pl.multiple_of(n * ck, ck)
        d_row = eg_ref[0, pl.ds(r0 + ck - 1, 1), :]
        for h in range(nh):
            st = st_s[h]
            res = _dot(nq_ref[0, h, n], st.astype(BF16)) + co_ref[0, h, n].astype(F32)
            st_s[h] = st * d_row[:, SM_DECAY + h:SM_DECAY + h + 1] + res[:dh]
            o = res[dh:]
            ms = jnp.mean(o * o, axis=1, keepdims=True)
            gt = gate_ref[0, pl.ds(r0, ck), h * dh:(h + 1) * dh].astype(F32)
            out_ref[0, pl.ds(r0, ck), h * dh:(h + 1) * dh] = (
                o * lax.rsqrt(ms + RMS_EPS) * nw_ref[...] * jax.nn.silu(gt)).astype(out_ref.dtype)
        return carry

    lax.fori_loop(0, nq_ref.shape[2], chunk, 0, unroll=2)


def _gdn_scan(nq, co, eg, ggate, norm_w):
    b, nh, nc, rows, dh = nq.shape
    s = nc * GDN_CHUNK
    cspec = lambda: pl.BlockSpec((1, nh, nc, rows, dh), lambda bi: (bi, 0, 0, 0, 0))
    return pl.pallas_call(
        _gdn_scan_body,
        grid=(b,),
        in_specs=[cspec(), cspec(),
                  pl.BlockSpec((1, s, LANES), lambda bi: (bi, 0, 0)),
                  pl.BlockSpec((1, s, GDN_W), lambda bi: (bi, 0, 0)),
                  _const_spec(norm_w.shape)],
        out_specs=pl.BlockSpec((1, s, GDN_W), lambda bi: (bi, 0, 0)),
        out_shape=jax.ShapeDtypeStruct((b, s, GDN_W), BF16),
        scratch_shapes=[pltpu.VMEM((nh, dh, dh), F32)],
        compiler_params=_cparams("parallel"),
        name="gdn_scan",
    )(nq, co, eg, ggate, norm_w)


MERGE_TM = 512


def _layer_norm(y, g, b):
    mu = jnp.mean(y, axis=1, keepdims=True)
    d = y - mu
    var = jnp.mean(d * d, axis=1, keepdims=True)
    return d * lax.rsqrt(var + LN_EPS) * g + b


def _merge_body(x_ref, oa_ref, ob_ref, ga_ref, gb_ref, wa_ref, wb_ref, wo_ref, g_ref, b_ref, y_ref, yb_ref):
    ya = _dot(oa_ref[...], wa_ref[...])
    yb = _dot(ob_ref[...], wb_ref[...])
    mixin = jax.nn.sigmoid(ga_ref[...].astype(F32)) * ya + jax.nn.sigmoid(gb_ref[...].astype(F32)) * yb
    mix = _dot(mixin.astype(BF16), wo_ref[...])
    y = _layer_norm(DEEPNORM_ALPHA * x_ref[...] + mix, g_ref[...], b_ref[...])
    y_ref[...] = y
    yb_ref[...] = y.astype(BF16)


def _merge(x2, oa, ob, mgate, wa, wb, wo, g, b):
    m = x2.shape[0]
    tm, d = MERGE_TM, D_MODEL
    row = lambda wd, col=0: pl.BlockSpec((tm, wd), lambda i, col=col: (i, col))
    return pl.pallas_call(
        _merge_body,
        grid=(m // tm,),
        in_specs=[row(d), row(NSA_Q_W), row(GDN_W), row(d, 0), row(d, 1),
                  _const_spec(wa.shape), _const_spec(wb.shape), _const_spec(wo.shape),
                  _const_spec(g.shape), _const_spec(b.shape)],
        out_specs=[row(d), row(d)],
        out_shape=[jax.ShapeDtypeStruct((m, d), F32), jax.ShapeDtypeStruct((m, d), BF16)],
        compiler_params=_cparams("parallel"),
        name="merge",
    )(x2, oa, ob, mgate, mgate, wa, wb, wo, g, b)


FFN_TM = 512
FFN_HALO = 16
FFN_CK = 256


def _ffn_body(x_ref, xb_ref, prev_ref, wu_ref, cw_ref, wd_ref, g_ref, b_ref, out_ref, act_s,
              *, tiles_per_seq):
    i = pl.program_id(0)
    prev = prev_ref[...]
    prev = jnp.where(i % tiles_per_seq == 0, jnp.zeros_like(prev), prev)
    xc = jnp.concatenate([prev, xb_ref[...]], axis=0)

    def conv(h, c0):
        out = cw_ref[FFN_CONV - 1:FFN_CONV, c0:c0 + FFN_CK] * h[FFN_HALO:]
        for j in range(FFN_CONV - 1):
            shifted = pltpu.roll(h, FFN_CONV - 1 - j, axis=0)[FFN_HALO:]
            out = out + cw_ref[j:j + 1, c0:c0 + FFN_CK] * shifted
        return out

    for c in range(FFN_DIM // FFN_CK):
        c0 = c * FFN_CK
        hg = conv(_dot(xc, wu_ref[:, c0:c0 + FFN_CK]), c0)
        hv = conv(_dot(xc, wu_ref[:, FFN_DIM + c0:FFN_DIM + c0 + FFN_CK]), FFN_DIM + c0)
        act_s[:, c0:c0 + FFN_CK] = (jax.nn.silu(hg) * hv).astype(BF16)
    f = _dot(act_s[...], wd_ref[...])
    out_ref[...] = _layer_norm(DEEPNORM_ALPHA * x_ref[...] + f, g_ref[...], b_ref[...])


def _ffn(x1, x1b, wu, cw, wd, g, b, seq):
    m = x1.shape[0]
    tm, d = FFN_TM, D_MODEL
    return pl.pallas_call(
        functools.partial(_ffn_body, tiles_per_seq=seq // tm),
        grid=(m // tm,),
        in_specs=[pl.BlockSpec((tm, d), lambda i: (i, 0)),
                  pl.BlockSpec((tm, d), lambda i: (i, 0)),
                  pl.BlockSpec((FFN_HALO, d), lambda i: (jnp.maximum(i * (tm // FFN_HALO) - 1, 0), 0)),
                  _const_spec(wu.shape), _const_spec(cw.shape), _const_spec(wd.shape), _const_spec(g.shape), _const_spec(b.shape)],
        out_specs=pl.BlockSpec((tm, d), lambda i: (i, 0)),
        out_shape=jax.ShapeDtypeStruct((m, d), F32),
        scratch_shapes=[pltpu.VMEM((tm, FFN_DIM), BF16)],
        compiler_params=_cparams("parallel"),
        name="ffn",
    )(x1, x1b, x1b, wu, cw, wd, g, b)


def _lane_vec(vals, lane0):
    return jnp.zeros((1, LANES), F32).at[0, lane0:lane0 + vals.shape[0]].set(vals.astype(F32))


def _layer(x, w_in, cmp_pos, cmp_w1, cmp_w2, w_nsa_out, gdn_conv_w, gdn_a_log, gdn_dt_bias, gdn_norm_w,
           w_gdn_out, w_o, ln1_g, ln1_b, ffn_w_up, ffn_conv_w, ffn_w_down, ln2_g, ln2_b):
    b, s, d = x.shape
    m = b * s
    x2 = x.reshape(m, d)
    w_rows, w_gdn, w_t, w_g = _wprep(*w_in)
    keys, cmpkv, ggate, mgate, qvt, gt = _inproj(x2, w_rows, w_t, w_g)

    consts = _nsa_consts(s)
    post, w2sel = _compress_weights(cmp_pos, cmp_w2)
    cmp_kv = _compress(cmpkv.reshape(b, s // CMP_STRIDE, CMP_STRIDE * 256), cmp_w1, post, w2sel, consts["cmp_aug"])
    o_nsa = _nsa(qvt, keys.reshape(b, s, 256), cmp_kv, gt, consts, b, s)

    ck = GDN_CHUNK
    tri = np.tril(np.ones((ck, ck), np.float32))
    lt = jnp.asarray(np.kron(np.eye(GDN_BLK // ck, dtype=np.float32), tri))
    nq, co, eg = _gdn_intra(x, w_gdn, gdn_conv_w,
                                        _lane_vec(gdn_a_log, SM_DECAY), _lane_vec(gdn_dt_bias, SM_DECAY), lt)
    o_gdn = _gdn_scan(nq, co, eg, ggate.reshape(b, s, GDN_W), gdn_norm_w.reshape(1, GDN_HEAD_DIM))

    x1, x1b = _merge(x2, o_nsa.reshape(m, NSA_Q_W), o_gdn.reshape(m, GDN_W), mgate,
                     w_nsa_out.astype(BF16), w_gdn_out.astype(BF16), w_o.astype(BF16),
                     ln1_g.reshape(1, d), ln1_b.reshape(1, d))
    out = _ffn(x1, x1b, ffn_w_up.astype(BF16), ffn_conv_w, ffn_w_down.astype(BF16),
               ln2_g.reshape(1, d), ln2_b.reshape(1, d), s)
    return out.reshape(b, s, d)


def kernel(x, w_in, nsa_cmp_pos, nsa_cmp_w1, nsa_cmp_w2, w_nsa_out, gdn_conv_w, gdn_a_log, gdn_dt_bias, gdn_norm_w, w_gdn_out, w_o, ln1_g, ln1_b, ffn_w_up, ffn_conv_w, ffn_w_down, ln2_g, ln2_b):
    w_in_t = jnp.swapaxes(w_in, 1, 2)
    for l in range(DEPTH):
        x = _layer(x, (w_in_t, l), nsa_cmp_pos[l], nsa_cmp_w1[l], nsa_cmp_w2[l], w_nsa_out[l], gdn_conv_w[l],
                   gdn_a_log[l], gdn_dt_bias[l], gdn_norm_w[l], w_gdn_out[l], w_o[l], ln1_g[l], ln1_b[l],
                   ffn_w_up[l], ffn_conv_w[l], ffn_w_down[l], ln2_g[l], ln2_b[l])
    return x
```

```python
import functools

import numpy as np
import jax
import jax.numpy as jnp
from jax import lax
from jax.experimental import pallas as pl
from jax.experimental.pallas import tpu as pltpu

F32 = jnp.float32
BF16 = jnp.bfloat16

D_MODEL = 1024
NSA_HEADS = 8
NSA_KV_GROUPS = 2
NSA_REP = NSA_HEADS // NSA_KV_GROUPS
NSA_HEAD_DIM = 64
CMP_LEN = 32
CMP_STRIDE = 16
SLC_LEN = 64
SLC_TOPK = 8
WIN_LEN = 512
FORCE_SCORE = 1.0e4
NEG = -1.0e30
GDN_HEADS = 4
GDN_HEAD_DIM = 128
GDN_CONV = 4
GDN_CHUNK = 64
FFN_DIM = 2816
FFN_CONV = 3
DEPTH = 1
DEEPNORM_ALPHA = (2.0 * DEPTH) ** 0.25
LN_EPS = 1e-5
RMS_EPS = 1e-6

NSA_Q_W = NSA_HEADS * NSA_HEAD_DIM
NSA_KV_W = NSA_KV_GROUPS * NSA_HEAD_DIM
GDN_W = GDN_HEADS * GDN_HEAD_DIM

LANES = 128
VMEM_LIMIT_BYTES = 56 * 1024 * 1024

AUG_SEL0 = 64
AUG_POS_HI = 96
AUG_POS_LO = 97
AUG_PAD = 98
BIG = 2.0 ** 100
Q_TILE = 128
N_SLC = 32
V_ROWS = 80
SEL_KC = 512
NSA_SUB = 2

NT_DIMS = (((1,), (1,)), ((), ()))


def _dot(a, b, **kw):
    return jnp.dot(a, b, preferred_element_type=F32, **kw)


def _dot_nt(a, b, **kw):
    return lax.dot_general(a, b, NT_DIMS, preferred_element_type=F32, **kw)


def _cparams(*sem):
    return pltpu.CompilerParams(dimension_semantics=sem, vmem_limit_bytes=VMEM_LIMIT_BYTES)


def _const_spec(shape):
    nd = len(shape)
    return pl.BlockSpec(shape, lambda *_: (0,) * nd, pipeline_mode=pl.Buffered(1))


_IN_WIDTHS = (NSA_Q_W,) + (NSA_KV_W,) * 6 + (3 * NSA_HEADS, 3 * GDN_W, GDN_HEADS, GDN_HEADS, GDN_W, 2 * D_MODEL)
(_C_Q, _C_CK, _C_CV, _C_SK, _C_SV, _C_WK, _C_WV, _C_GATE, _C_GQKV, _C_BETA, _C_DECAY, _C_GGATE, _C_MERGE,
 IN_WIDTH) = (int(v) for v in np.cumsum((0,) + _IN_WIDTHS))
_C_SMALL = _C_BETA // LANES * LANES
SM_BETA = _C_BETA - _C_SMALL
SM_DECAY = _C_DECAY - _C_SMALL
_INPROJ_GROUPS = (("keys", 256, BF16), ("cmp", 256, BF16), ("ggate", GDN_W, BF16), ("merge", 2 * D_MODEL, BF16))
_INPROJ_WIDTH = sum(w for _, w, _ in _INPROJ_GROUPS)
_GDN_PROJ_WIDTH = 3 * GDN_W + LANES
_INPROJ_T_ROWS = NSA_Q_W + 4 * NSA_HEAD_DIM
_GATE_T_ROWS = 32
INPROJ_TM = 512
WPREP_TK = 128


def _wprep_body(w_ref, rows_ref, gdn_ref, wt_ref, wg_ref):
    hd = NSA_HEAD_DIM
    feat = lambda c0, n: w_ref[0, c0:c0 + n, :]

    def put_t(ref, col, src):
        for r in range(0, src.shape[0], LANES):
            ref[:, col + r:col + r + LANES] = src[r:r + LANES].T.astype(BF16)

    for g in range(NSA_KV_GROUPS):
        put_t(rows_ref, 2 * g * hd, jnp.concatenate([feat(_C_SK + g * hd, hd), feat(_C_WK + g * hd, hd)], axis=0))
    c = 4 * hd
    for c0, n in ((_C_CK, 2 * NSA_KV_W), (_C_GGATE, GDN_W), (_C_MERGE, 2 * D_MODEL)):
        put_t(rows_ref, c, feat(c0, n))
        c += n
    put_t(gdn_ref, 0, feat(_C_GQKV, 3 * GDN_W))
    put_t(gdn_ref, 3 * GDN_W, feat(_C_SMALL, LANES))
    wt_ref[:NSA_Q_W, :] = feat(_C_Q, NSA_Q_W).astype(BF16)
    for j, c0 in enumerate((_C_SV, _C_WV, _C_SV + hd, _C_WV + hd)):
        wt_ref[NSA_Q_W + j * hd:NSA_Q_W + (j + 1) * hd, :] = feat(c0, hd).astype(BF16)
    wg_ref[...] = feat(_C_GATE, _GATE_T_ROWS).astype(BF16)


def _wprep(w_in_t, layer):
    k = w_in_t.shape[2]
    tk = WPREP_TK
    return pl.pallas_call(
        _wprep_body,
        grid=(k // tk,),
        in_specs=[pl.BlockSpec((1, IN_WIDTH, tk), lambda i: (layer, 0, i))],
        out_specs=[pl.BlockSpec((tk, _INPROJ_WIDTH), lambda i: (i, 0)),
                   pl.BlockSpec((tk, _GDN_PROJ_WIDTH), lambda i: (i, 0)),
                   pl.BlockSpec((_INPROJ_T_ROWS, tk), lambda i: (0, i)),
                   pl.BlockSpec((_GATE_T_ROWS, tk), lambda i: (0, i))],
        out_shape=[jax.ShapeDtypeStruct((k, _INPROJ_WIDTH), BF16), jax.ShapeDtypeStruct((k, _GDN_PROJ_WIDTH), BF16),
                   jax.ShapeDtypeStruct((_INPROJ_T_ROWS, k), BF16), jax.ShapeDtypeStruct((_GATE_T_ROWS, k), BF16)],
        compiler_params=_cparams("parallel"),
        name="wprep",
    )(w_in_t)


def _inproj_body(x_ref, w_ref, wt_ref, wg_ref, keys_ref, cmp_ref, ggate_ref, merge_ref, qvt_ref, gt_ref, cmp_s):
    x = x_ref[...].astype(BF16)
    outs = (keys_ref, None, ggate_ref, merge_ref)
    c0 = 0
    for ref, (name, width, _) in zip(outs, _INPROJ_GROUPS):
        for s in range(0, width, 512):
            e = min(s + 512, width)
            res = _dot(x, w_ref[:, c0 + s:c0 + e])
            if name == "cmp":
                for j in range(width // LANES):
                    cmp_s[j] = res[:, j * LANES:(j + 1) * LANES]
            else:
                ref[:, s:e] = res.astype(ref.dtype)
        c0 += width
    nblk = cmp_ref.shape[0]
    for l in range(CMP_STRIDE):
        for j in range(cmp_s.shape[0]):
            cmp_ref[:, l * 256 + j * LANES:l * 256 + (j + 1) * LANES] = (
                cmp_s[j, pl.ds(l, nblk, stride=CMP_STRIDE), :].astype(BF16))
    for s in range(0, _INPROJ_T_ROWS, 256):
        qvt_ref[s:s + 256, :] = _dot_nt(wt_ref[s:s + 256, :], x).astype(qvt_ref.dtype)
    gt_ref[...] = _dot_nt(wg_ref[...], x)


def _inproj(x2, w_rows, w_t, w_g):
    m = x2.shape[0]
    tm = INPROJ_TM
    row_major = [(n, wd, dt) for n, wd, dt in _INPROJ_GROUPS if n != "cmp"]
    specs = {n: (pl.BlockSpec((tm, wd), lambda i: (i, 0)), jax.ShapeDtypeStruct((m, wd), dt))
             for n, wd, dt in row_major}
    specs["cmp"] = (pl.BlockSpec((tm // CMP_STRIDE, CMP_STRIDE * 256), lambda i: (i, 0)),
                    jax.ShapeDtypeStruct((m // CMP_STRIDE, CMP_STRIDE * 256), BF16))
    order = [n for n, _, _ in _INPROJ_GROUPS]
    return pl.pallas_call(
        _inproj_body,
        grid=(m // tm,),
        in_specs=[pl.BlockSpec((tm, D_MODEL), lambda i: (i, 0)), _const_spec(w_rows.shape),
                  _const_spec(w_t.shape), _const_spec(w_g.shape)],
        out_specs=[specs[n][0] for n in order]
        + [pl.BlockSpec((_INPROJ_T_ROWS, tm), lambda i: (0, i)), pl.BlockSpec((_GATE_T_ROWS, tm), lambda i: (0, i))],
        out_shape=[specs[n][1] for n in order]
        + [jax.ShapeDtypeStruct((_INPROJ_T_ROWS, m), BF16), jax.ShapeDtypeStruct((_GATE_T_ROWS, m), F32)],
        scratch_shapes=[pltpu.VMEM((256 // LANES, tm, LANES), F32)],
        compiler_params=_cparams("parallel"),
        name="inproj",
    )(x2, w_rows, w_t, w_g)


def _compress_weights(cmp_pos, cmp_w2):
    hd, half = NSA_HEAD_DIM, CMP_LEN // 2
    posr = cmp_pos.reshape(2, 2, half, hd)
    post = jnp.broadcast_to(posr.transpose(1, 2, 0, 3)[:, :, :, None, :], (2, half, 2, 2, hd))
    post = jnp.concatenate([post.reshape(2, half * 4 * hd), jnp.zeros((6, half * 4 * hd), cmp_pos.dtype)], axis=0)
    w2sel = jnp.zeros((2, 2, 2 * hd, LANES), cmp_w2.dtype)
    for g in range(2):
        w2sel = w2sel.at[:, g, g * hd:(g + 1) * hd, :hd].set(cmp_w2)
    return post.astype(BF16), w2sel.reshape(4, 2 * hd, LANES).astype(BF16)


def _compress_body(t_ref, w1_ref, pos_ref, w2_ref, aug_ref, out_ref, w1e_s):
    hd, half_len = NSA_HEAD_DIM, CMP_LEN // 2

    @pl.when(pl.program_id(0) == 0)
    def _():
        w1e_s[...] = jnp.zeros(w1e_s.shape, BF16)
        for which in range(2):
            for half in range(2):
                for l in range(half_len):
                    blk = w1_ref[which, (half * half_len + l) * hd:(half * half_len + l + 1) * hd, :].astype(BF16)
                    for g in range(NSA_KV_GROUPS):
                        r0 = l * 256 + which * LANES + g * hd
                        c0 = half * 256 + which * LANES + g * hd
                        w1e_s[r0:r0 + hd, c0:c0 + hd] = blk

    p = _dot(t_ref[0], w1e_s[...])
    pp = _dot(pos_ref[...], w1e_s[...])
    nxt = pltpu.roll(p[:, 256:], p.shape[0] - 1, axis=0)
    pre = p[:, :256] + nxt + pp[0:1, :256] + pp[1:2, 256:]
    h = jax.nn.gelu(pre).astype(BF16)
    n_idx = lax.broadcasted_iota(jnp.int32, (p.shape[0], LANES), 0)
    real = n_idx < p.shape[0] - 1
    for which in range(2):
        hw = h[:, which * LANES:(which + 1) * LANES]
        for g in range(2):
            o = jnp.where(real, _dot(hw, w2_ref[which * 2 + g]) + aug_ref[which], 0.0)
            out_ref[0, which * 2 + g] = (o if which == 0 else o.T).astype(out_ref.dtype)


def _compress(t2, w1, post, w2sel, aug):
    b, nblk, _ = t2.shape
    return pl.pallas_call(
        _compress_body,
        grid=(b,),
        in_specs=[pl.BlockSpec((1, nblk, CMP_STRIDE * 256), lambda i: (i, 0, 0)),
                  _const_spec(w1.shape), _const_spec(post.shape), _const_spec(w2sel.shape), _const_spec(aug.shape)],
        out_specs=pl.BlockSpec((1, 4, nblk, LANES), lambda i: (i, 0, 0, 0)),
        out_shape=jax.ShapeDtypeStruct((b, 4, nblk, LANES), BF16),
        scratch_shapes=[pltpu.VMEM((CMP_STRIDE * 256, 512), BF16)],
        compiler_params=_cparams("arbitrary"),
        name="compress",
    )(t2, w1, post, w2sel, aug)


def _nsa_consts(s):
    hd, rep = NSA_HEAD_DIM, NSA_REP
    t = np.arange(s)
    kx_win = np.zeros((s + WIN_LEN, hd), np.float32)
    kx_win[WIN_LEN + t, AUG_POS_HI - hd] = t // 256
    kx_win[WIN_LEN + t, AUG_POS_LO - hd] = t % 256
    kx_win[:WIN_LEN, AUG_PAD - hd] = 1.0
    kx_sel = kx_win.copy()
    kx_sel[WIN_LEN + t, t // SLC_LEN] = 1.0
    vx_win = np.zeros((V_ROWS - hd, s + WIN_LEN), np.float32)
    vx_win[0, WIN_LEN:] = 1.0
    vx_sel = vx_win
    n_cmp = s // CMP_STRIDE
    cmp_aug = np.zeros((2, n_cmp, LANES), np.float32)
    end = np.arange(n_cmp) * CMP_STRIDE + CMP_LEN - 1
    cmp_aug[0, :, AUG_POS_HI] = end // 256
    cmp_aug[0, :, AUG_POS_LO] = end % 256
    qx = np.zeros((NSA_KV_GROUPS, LANES - AUG_POS_HI, rep * Q_TILE), np.float32)
    for h in range(NSA_HEADS):
        slope = 2.0 ** (-8.0 * (h + 1) / NSA_HEADS)
        lanes = slice((h % rep) * Q_TILE, (h % rep + 1) * Q_TILE)
        qx[h // rep, 0, lanes] = slope * 256.0
        qx[h // rep, 1, lanes] = slope
        qx[h // rep, AUG_PAD - AUG_POS_HI, lanes] = -BIG
    c0 = np.arange(n_cmp)[None, :] * CMP_STRIDE
    s0 = np.arange(s // SLC_LEN)[:, None] * SLC_LEN
    ov_t = ((c0 < s0 + SLC_LEN) & (c0 + CMP_LEN > s0)).astype(np.float32)
    ov_t[:, (s - CMP_LEN) // CMP_STRIDE + 1:] = 0.0
    kk = np.arange(Q_TILE)[:, None]
    qq = np.arange(Q_TILE)[None, :]
    causal = np.tile(np.where(kk <= qq, 0.0, NEG).astype(np.float32), (1, rep))
    after = np.tile(np.where(kk > qq, 0.0, NEG).astype(np.float32), (1, rep))
    j = jnp.asarray
    return dict(kx_sel=j(kx_sel, BF16), kx_win=j(kx_win, BF16), vx_sel=j(vx_sel, BF16), vx_win=j(vx_win, BF16),
                cmp_aug=j(cmp_aug), qx=j(qx), ov_t=j(ov_t), causal=j(causal), after=j(after))


def _nsa_body(qt_ref, k_ref, vt_ref, kc_ref, vct_ref, gt_ref, kxs_ref, kxw_ref, vxs_ref, vxw_ref, qx_ref, ovt_ref,
              causal_ref, after_ref, out_ref, ks_s, kw_s, vs_s, vw_s):
    hd, rep, tq = NSA_HEAD_DIM, NSA_REP, Q_TILE
    nq = rep * tq
    i = pl.program_id(2)

    @pl.when(i == 0)
    def _():
        keys = k_ref[0]
        ks_s[:WIN_LEN, :hd] = jnp.zeros((WIN_LEN, hd), BF16)
        ks_s[WIN_LEN:, :hd] = keys[:, :hd]
        ks_s[:, hd:] = kxs_ref[...]
        kw_s[:WIN_LEN, :hd] = jnp.zeros((WIN_LEN, hd), BF16)
        kw_s[WIN_LEN:, :hd] = keys[:, hd:]
        kw_s[:, hd:] = kxw_ref[...]
        vals = vt_ref[...]
        vs_s[:hd, :WIN_LEN] = jnp.zeros((hd, WIN_LEN), BF16)
        vs_s[:hd, WIN_LEN:] = vals[:hd]
        vs_s[hd:, :] = vxs_ref[...]
        vw_s[:hd, :WIN_LEN] = jnp.zeros((hd, WIN_LEN), BF16)
        vw_s[:hd, WIN_LEN:] = vals[hd:]
        vw_s[hd:, :] = vxw_ref[...]

    qx = qx_ref[0]
    sg_all = jax.nn.sigmoid(gt_ref[...])
    grp = pl.program_id(1)

    def front(sub, res):
        it = i * NSA_SUB + sub
        qt = qt_ref[:, sub * tq:(sub + 1) * tq]
        q64 = jnp.concatenate([qt[r * hd:(r + 1) * hd, :] for r in range(rep)], axis=1).astype(F32) * (hd ** -0.5)

        def q_aug(sel_rows):
            return jnp.concatenate([q64, sel_rows, qx], axis=0).astype(BF16)

        n_row = lax.broadcasted_iota(jnp.int32, (LANES, nq), 0)
        t_lane = it * tq + (lax.broadcasted_iota(jnp.int32, (LANES, nq), 1) & (tq - 1))
        valid = t_lane >= n_row * CMP_STRIDE + (CMP_LEN - 1)
        qa0 = q_aug(jnp.zeros((N_SLC, nq), F32))
        sc = jnp.where(valid, _dot(kc_ref[0, 0], qa0), NEG)
        mc = jnp.max(sc, axis=0, keepdims=True)
        ec = jnp.where(valid, jnp.exp(sc - mc), 0.0)
        lc = jnp.sum(ec, axis=0, keepdims=True)
        pc = ec * jnp.where(lc > 0.0, 1.0 / lc, 0.0)
        o_cmp = _dot(vct_ref[0, 0], pc.astype(BF16))[:hd]
        psum = pc[:, 0:tq] + pc[:, tq:2 * tq] + pc[:, 2 * tq:3 * tq] + pc[:, 3 * tq:4 * tq]
        score_t = _dot(ovt_ref[...], psum, precision=lax.Precision.HIGHEST)
        yield

        w0 = pl.multiple_of(it * tq, tq)
        s_w = _dot(kw_s[pl.ds(w0, WIN_LEN + tq), :], qa0)
        s_w = jnp.concatenate([s_w[:tq] + after_ref[...], s_w[tq:WIN_LEN], s_w[WIN_LEN:] + causal_ref[...]],
                              axis=0)
        p_w = jnp.exp(s_w - jnp.max(s_w, axis=0, keepdims=True))
        acc_w = _dot(vw_s[:, pl.ds(w0, WIN_LEN + tq)], p_w.astype(BF16))
        o_win = acc_w[:hd] * (1.0 / acc_w[hd:hd + 1])
        yield

        jb = lax.broadcasted_iota(jnp.int32, (N_SLC, tq), 0)
        cur = (it * tq + lax.broadcasted_iota(jnp.int32, (N_SLC, tq), 1)) // SLC_LEN
        forced = (jb == 0) | (jb == cur) | (jb == cur - 1)
        score_t = jnp.where(forced, FORCE_SCORE, jnp.where(jb <= cur, score_t, -1.0))
        rank = jnp.zeros((N_SLC, tq), F32)
        for jp in range(N_SLC):
            other = score_t[jp:jp + 1, :]
            ge = jnp.where(other >= score_t, 1.0, 0.0)
            gt = jnp.where(other > score_t, 1.0, 0.0)
            rank = rank + jnp.where(jb > jp, ge, gt)
        sel = rank < float(SLC_TOPK)
        qa = q_aug(jnp.concatenate([jnp.where(sel, 0.0, -BIG)] * rep, axis=1))
        lo_blk = jnp.min(jnp.where(sel & (jb >= 2) & (jb <= cur), jb.astype(F32), float(N_SLC)))
        lo_key = (lo_blk.astype(jnp.int32) // 2) * tq
        yield

        e_key = it * tq - WIN_LEN
        t0 = pl.multiple_of(jnp.where(e_key > 0, WIN_LEN, 0), tq)
        s_main = _dot(ks_s[pl.ds(w0, WIN_LEN + tq), :], qa)
        s_s = jnp.concatenate([_dot(ks_s[pl.ds(t0, tq), :], qa), s_main[:WIN_LEN],
                               s_main[WIN_LEN:] + causal_ref[...]], axis=0)
        m_s = jnp.max(s_s, axis=0, keepdims=True)
        p_s = jnp.exp(s_s - m_s).astype(BF16)
        acc_s = _dot(vs_s[:, pl.ds(t0, tq)], p_s[:tq]) + _dot(vs_s[:, pl.ds(w0, WIN_LEN + tq)], p_s[tq:])
        c_hi = (e_key - tq + SEL_KC - 1) // SEL_KC
        c_lo = jnp.where(lo_key < e_key, (lo_key - tq) // SEL_KC, c_hi)
        res.update(qa=qa, e_key=e_key, c_lo=c_lo, c_hi=c_hi, m_s=m_s, acc_s=acc_s, o_cmp=o_cmp, o_win=o_win)

    def tail(sub, f):
        qa, e_key = f["qa"], f["e_key"]

        def early_step(c, carry):
            m, acc = carry
            k0 = tq + c * SEL_KC
            start = pl.multiple_of(WIN_LEN + k0, tq)
            k_abs = k0 + lax.broadcasted_iota(jnp.int32, (SEL_KC, nq), 0)
            s = jnp.where(k_abs < e_key, _dot(ks_s[pl.ds(start, SEL_KC), :], qa), NEG)
            m_new = jnp.maximum(m, jnp.max(s, axis=0, keepdims=True))
            p = jnp.exp(s - m_new).astype(BF16)
            return m_new, acc * jnp.exp(m - m_new) + _dot(vs_s[:, pl.ds(start, SEL_KC)], p)

        _, acc_s = lax.fori_loop(f["c_lo"], f["c_hi"], early_step, (f["m_s"], f["acc_s"]))
        o_slc = acc_s[:hd] * (1.0 / acc_s[hd:hd + 1])

        sg = sg_all[:, sub * tq:(sub + 1) * tq]
        gate = lambda br, r: jnp.where(grp == 0, sg[br * NSA_HEADS + r:br * NSA_HEADS + r + 1],
                                       sg[br * NSA_HEADS + rep + r:br * NSA_HEADS + rep + r + 1])
        for pair in range(rep // 2):
            halves = []
            for r in (2 * pair, 2 * pair + 1):
                lanes = slice(r * tq, (r + 1) * tq)
                halves.append(gate(0, r) * f["o_cmp"][:, lanes] + gate(1, r) * o_slc[:, lanes]
                              + gate(2, r) * f["o_win"][:, lanes])
            out_ref[0, sub * tq:(sub + 1) * tq, pair * LANES:(pair + 1) * LANES] = (
                jnp.concatenate(halves, axis=0).T.astype(out_ref.dtype))

    fronts = [{} for _ in range(NSA_SUB)]
    for sub in range(NSA_SUB):
        for _ in front(sub, fronts[sub]):
            pass
    for sub in range(NSA_SUB):
        tail(sub, fronts[sub])


def _nsa(qvt, keys, cmp_kv, gt, consts, b, s):
    tqs = NSA_SUB * Q_TILE
    nt = s // tqs
    c = consts
    in_specs = [
        pl.BlockSpec((2 * LANES, tqs), lambda bi, g, i: (g, bi * nt + i)),
        pl.BlockSpec((1, s, LANES), lambda bi, g, i: (bi, 0, g)),
        pl.BlockSpec((LANES, s), lambda bi, g, i: (NSA_Q_W // LANES + g, bi)),
        pl.BlockSpec((1, 1, s // CMP_STRIDE, LANES), lambda bi, g, i: (bi, g, 0, 0)),
        pl.BlockSpec((1, 1, s // CMP_STRIDE, LANES), lambda bi, g, i: (bi, 2 + g, 0, 0)),
        pl.BlockSpec((_GATE_T_ROWS, tqs), lambda bi, g, i: (0, bi * nt + i)),
        _const_spec(c["kx_sel"].shape), _const_spec(c["kx_win"].shape), _const_spec(c["vx_sel"].shape),
        _const_spec(c["vx_win"].shape),
        pl.BlockSpec((1,) + c["qx"].shape[1:], lambda bi, g, i: (g, 0, 0)),
        _const_spec(c["ov_t"].shape), _const_spec(c["causal"].shape), _const_spec(c["after"].shape),
    ]
    return pl.pallas_call(
        _nsa_body,
        grid=(b, NSA_KV_GROUPS, nt),
        in_specs=in_specs,
        out_specs=pl.BlockSpec((1, tqs, 2 * LANES), lambda bi, g, i: (bi, i, g)),
        out_shape=jax.ShapeDtypeStruct((b, s, NSA_Q_W), BF16),
        scratch_shapes=[pltpu.VMEM((s + WIN_LEN, LANES), BF16), pltpu.VMEM((s + WIN_LEN, LANES), BF16),
                        pltpu.VMEM((V_ROWS, s + WIN_LEN), BF16), pltpu.VMEM((V_ROWS, s + WIN_LEN), BF16)],
        compiler_params=_cparams("parallel", "parallel", "arbitrary"),
        name="nsa",
    )(qvt, keys, qvt, cmp_kv, cmp_kv, gt, c["kx_sel"], c["kx_win"], c["vx_sel"], c["vx_win"], c["qx"], c["ov_t"],
      c["causal"], c["after"])


GDN_TS = 512
GDN_BLK = 128
GDN_HALO = 8
GDN_XHALO = 16
GDN_INV_BASE = 8
GDN_CHAIN_UNITS_PER_PREP_UNIT = 7
GDN_SCAN_ROWS = GDN_HEAD_DIM + GDN_CHUNK


def _gdn_prep(x_ref, prev_ref, w_ref, cw_ref, alog_ref, dtb_ref, lt_ref, first_tile, xp_s, buf):
    x_s, p_s, rhs_s, qg_s, aqk_s, kdt_s, a_s, eg_s = buf
    ts, dh, nh, blk = GDN_TS, GDN_HEAD_DIM, GDN_HEADS, GDN_BLK
    proj = _dot(x_ref[0].astype(BF16), w_ref[...])
    hist = _dot(prev_ref[0].astype(BF16), w_ref[:, :3 * GDN_W])[GDN_XHALO - GDN_HALO:]
    xp_s[0:GDN_HALO, :] = jnp.where(first_tile, 0.0, hist)
    xp_s[GDN_HALO:, :] = proj[:, :3 * GDN_W]
    yield
    act = []
    for blk_i in range(3 * nh):
        lanes = slice(blk_i * dh, (blk_i + 1) * dh)
        xp = xp_s[:, lanes]
        conv = cw_ref[0:1, lanes] * xp
        for j in range(1, GDN_CONV):
            conv = pltpu.roll(conv, 1, axis=0) + cw_ref[j:j + 1, lanes] * xp
        act.append(jax.nn.silu(conv[GDN_HALO:]))
        yield

    sm = proj[:, 3 * GDN_W:]
    beta = jax.nn.sigmoid(sm)
    g = -jnp.exp(alog_ref[...]) * jax.nn.softplus(sm + dtb_ref[...])
    gcum = jnp.concatenate([_dot(lt_ref[...], g[r:r + blk], precision=lax.Precision.HIGHEST)
                            for r in range(0, ts, blk)], axis=0)
    eg = jnp.exp(gcum)
    eg_s[...] = eg

    ri = lax.broadcasted_iota(jnp.int32, (blk, blk), 0)
    ci = lax.broadcasted_iota(jnp.int32, (blk, blk), 1)
    same = (ri // GDN_CHUNK) == (ci // GDN_CHUNK)
    causal = same & (ri >= ci)
    strict = same & (ri > ci)
    eye = (ri == ci).astype(F32)
    same_base = (ri // GDN_INV_BASE) == (ci // GDN_INV_BASE)

    for pb in range(ts // blk):
        rows = slice(pb * blk, (pb + 1) * blk)
        gc = gcum[rows]
        gc_t = gc.T
        first = lax.broadcasted_iota(jnp.int32, (blk, LANES), 0) < GDN_CHUNK
        g_last = jnp.where(first, gc[GDN_CHUNK - 1:GDN_CHUNK, :], gc[blk - 1:blk, :])
        e_dec = jnp.exp(g_last - gc)
        for h in range(nh):
            q, k, v = act[h][rows], act[nh + h][rows], act[2 * nh + h][rows]
            q = q * lax.rsqrt(jnp.sum(q * q, axis=1, keepdims=True) + RMS_EPS) * (dh ** -0.5)
            k = k * lax.rsqrt(jnp.sum(k * k, axis=1, keepdims=True) + RMS_EPS)
            b_col = beta[rows, SM_BETA + h:SM_BETA + h + 1]
            eg_col = eg[rows, SM_DECAY + h:SM_DECAY + h + 1]
            gdiff = gc[:, SM_DECAY + h:SM_DECAY + h + 1] - gc_t[SM_DECAY + h:SM_DECAY + h + 1, :]
            decay = jnp.exp(jnp.where(causal, gdiff, NEG))
            kb = k * b_col
            kbf, kf, qf = kb.astype(BF16), k.astype(BF16), q.astype(BF16)
            a = jnp.where(strict, -_dot_nt(kbf, kf) * decay, 0.0)
            c = pb * nh + h
            a_base = jnp.where(same_base, a, 0.0)
            a_s[c] = a.astype(BF16)
            x_s[c] = a_base.astype(BF16)
            p_s[c] = eye + a_base
            rhs_s[c] = jnp.concatenate([v * b_col, kb * eg_col], axis=1).astype(BF16)
            qg_s[c] = q * eg_col
            aqk_s[c] = jnp.where(causal, _dot_nt(qf, kf) * decay, 0.0).astype(BF16)
            kdt_s[c] = (k * e_dec[:, SM_DECAY + h:SM_DECAY + h + 1]).T.astype(BF16)
            yield


def _gdn_chains(buf, nq_ref, co_ref, eg_ref):
    x_s, p_s, rhs_s, qg_s, aqk_s, kdt_s, a_s, eg_s = buf
    dh, nh, blk = GDN_HEAD_DIM, GDN_HEADS, GDN_BLK
    n_chain = x_s.shape[0]
    eg_ref[0] = eg_s[...]
    ri = lax.broadcasted_iota(jnp.int32, (blk, blk), 0)
    ci = lax.broadcasted_iota(jnp.int32, (blk, blk), 1)
    same = lambda size: (ri // size) == (ci // size)
    for c in range(n_chain):
        y = x_s[c]
        x_s[c] = _dot(y, y).astype(BF16)
        yield
    for c in range(n_chain):
        y2 = x_s[c]
        p = p_s[c]
        p_s[c] = p + _dot(p.astype(BF16), y2)
        x_s[c] = _dot(y2, y2).astype(BF16)
        yield
    for c in range(n_chain):
        p = p_s[c]
        p_s[c] = p + _dot(p.astype(BF16), x_s[c])
        yield
    size = 2 * GDN_INV_BASE
    while size <= GDN_CHUNK:
        between = same(size) & jnp.logical_not(same(size // 2))
        for c in range(n_chain):
            a = a_s[c]
            x_s[c] = _dot(p_s[c].astype(BF16), jnp.where(between, a, jnp.zeros_like(a))).astype(BF16)
            yield
        for c in range(n_chain):
            t = p_s[c]
            p_s[c] = t + _dot(x_s[c], t.astype(BF16))
            yield
        size *= 2

    tok_half = lax.broadcasted_iota(jnp.int32, (blk, blk), 1) // GDN_CHUNK
    for c in range(n_chain):
        rhs_s[c] = _dot(p_s[c].astype(BF16), rhs_s[c]).astype(BF16)
        yield
    for c in range(n_chain):
        pb, h = divmod(c, nh)
        uw = rhs_s[c]
        a1 = _dot(aqk_s[c], uw)
        q_loc = qg_s[c] - a1[:, dh:]
        kdt = kdt_s[c]
        for half in range(blk // GDN_CHUNK):
            k1 = _dot(jnp.where(tok_half == half, kdt, jnp.zeros_like(kdt)), uw)
            n = pb * (blk // GDN_CHUNK) + half
            rows = slice(half * GDN_CHUNK, (half + 1) * GDN_CHUNK)
            nq_ref[0, h, n, :dh, :] = (-k1[:, dh:]).astype(BF16)
            nq_ref[0, h, n, dh:, :] = q_loc[rows].astype(BF16)
            co_ref[0, h, n, :dh, :] = k1[:, :dh].astype(BF16)
            co_ref[0, h, n, dh:, :] = a1[rows, :dh].astype(BF16)
        yield


def _interleave(major, minor, minor_per_major):
    for _ in major:
        for _ in range(minor_per_major):
            next(minor, None)
    for _ in minor:
        pass


def _gdn_intra_body(x_ref, prev_ref, w_ref, cw_ref, alog_ref, dtb_ref, lt_ref, nq_ref, co_ref, eg_ref, xp_s, *bufs,
                    tiles_per_seq, n_buf):
    j = pl.program_id(0)
    sets = (bufs[:n_buf], bufs[n_buf:])

    @pl.when(j == 0)
    def _():
        for ref in sets[1]:
            ref[...] = jnp.zeros(ref.shape, ref.dtype)

    first_tile = (j % tiles_per_seq) == 0
    for parity in range(2):
        @pl.when(j % 2 == parity)
        def _(parity=parity):
            chains = _gdn_chains(sets[1 - parity], nq_ref, co_ref, eg_ref)
            prep = _gdn_prep(x_ref, prev_ref, w_ref, cw_ref, alog_ref, dtb_ref, lt_ref, first_tile, xp_s,
                             sets[parity])
            _interleave(prep, chains, GDN_CHAIN_UNITS_PER_PREP_UNIT)


def _gdn_intra(x, w_gdn, conv_w, alog_l, dtb_l, lt):
    b, s, d = x.shape
    ts, nh, dh = GDN_TS, GDN_HEADS, GDN_HEAD_DIM
    tps = s // ts
    nt = b * tps
    n_chain = (ts // GDN_BLK) * nh
    sq = lambda dt: pltpu.VMEM((n_chain, GDN_BLK, GDN_BLK), dt)
    buf = lambda: [sq(BF16), sq(F32), pltpu.VMEM((n_chain, GDN_BLK, 2 * dh), BF16), sq(F32), sq(BF16), sq(BF16),
                   sq(BF16), pltpu.VMEM((ts, LANES), F32)]
    src = lambda j: jnp.minimum(j, nt - 1)
    dst = lambda j: jnp.maximum(j - 1, 0)
    cspec = lambda: pl.BlockSpec((1, nh, ts // GDN_CHUNK, GDN_SCAN_ROWS, dh),
                                 lambda j: (dst(j) // tps, 0, dst(j) % tps, 0, 0))
    cshape = jax.ShapeDtypeStruct((b, nh, s // GDN_CHUNK, GDN_SCAN_ROWS, dh), BF16)
    return pl.pallas_call(
        functools.partial(_gdn_intra_body, tiles_per_seq=tps, n_buf=len(buf())),
        grid=(nt + 1,),
        in_specs=[
            pl.BlockSpec((1, ts, d), lambda j: (src(j) // tps, src(j) % tps, 0)),
            pl.BlockSpec((1, GDN_XHALO, d),
                         lambda j: (src(j) // tps, jnp.maximum((src(j) % tps) * (ts // GDN_XHALO) - 1, 0), 0)),
            _const_spec(w_gdn.shape), _const_spec(conv_w.shape), _const_spec(alog_l.shape), _const_spec(dtb_l.shape),
            _const_spec(lt.shape),
        ],
        out_specs=[cspec(), cspec(), pl.BlockSpec((1, ts, LANES), lambda j: (dst(j) // tps, dst(j) % tps, 0))],
        out_shape=[cshape, cshape, jax.ShapeDtypeStruct((b, s, LANES), F32)],
        scratch_shapes=[pltpu.VMEM((ts + GDN_HALO, 3 * GDN_W), F32)] + buf() + buf(),
        compiler_params=_cparams("arbitrary"),
        name="gdn_intra",
    )(x, x, w_gdn, conv_w, alog_l, dtb_l, lt)


def _gdn_scan_body(nq_ref, co_ref, eg_ref, gate_ref, nw_ref, out_ref, st_s):
    nh, dh, ck = GDN_HEADS, GDN_HEAD_DIM, GDN_CHUNK
    st_s[...] = jnp.zeros(st_s.shape, F32)

    def chunk(n, carry):
        r0 = pl.multiple_of(n * ck, ck)
        d_row = eg_ref[0, pl.ds(r0 + ck - 1, 1), :]
        for h in range(nh):
            st = st_s[h]
            res = _dot(nq_ref[0, h, n], st.astype(BF16)) + co_ref[0, h, n].astype(F32)
            st_s[h] = st * d_row[:, SM_DECAY + h:SM_DECAY + h + 1] + res[:dh]
            o = res[dh:]
            ms = jnp.mean(o * o, axis=1, keepdims=True)
            gt = gate_ref[0, pl.ds(r0, ck), h * dh:(h + 1) * dh].astype(F32)
            out_ref[0, pl.ds(r0, ck), h * dh:(h + 1) * dh] = (
                o * lax.rsqrt(ms + RMS_EPS) * nw_ref[...] * jax.nn.silu(gt)).astype(out_ref.dtype)
        return carry

    lax.fori_loop(0, nq_ref.shape[2], chunk, 0, unroll=2)


def _gdn_scan(nq, co, eg, ggate, norm_w):
    b, nh, nc, rows, dh = nq.shape
    s = nc * GDN_CHUNK
    cspec = lambda: pl.BlockSpec((1, nh, nc, rows, dh), lambda bi: (bi, 0, 0, 0, 0))
    return pl.pallas_call(
        _gdn_scan_body,
        grid=(b,),
        in_specs=[cspec(), cspec(),
                  pl.BlockSpec((1, s, LANES), lambda bi: (bi, 0, 0)),
                  pl.BlockSpec((1, s, GDN_W), lambda bi: (bi, 0, 0)),
                  _const_spec(norm_w.shape)],
        out_specs=pl.BlockSpec((1, s, GDN_W), lambda bi: (bi, 0, 0)),
        out_shape=jax.ShapeDtypeStruct((b, s, GDN_W), BF16),
        scratch_shapes=[pltpu.VMEM((nh, dh, dh), F32)],
        compiler_params=_cparams("parallel"),
        name="gdn_scan",
    )(nq, co, eg, ggate, norm_w)


MERGE_TM = 512


def _layer_norm(y, g, b):
    mu = jnp.mean(y, axis=1, keepdims=True)
    d = y - mu
    var = jnp.mean(d * d, axis=1, keepdims=True)
    return d * lax.rsqrt(var + LN_EPS) * g + b


def _merge_body(x_ref, oa_ref, ob_ref, ga_ref, gb_ref, wa_ref, wb_ref, wo_ref, g_ref, b_ref, y_ref, yb_ref):
    ya = _dot(oa_ref[...], wa_ref[...])
    yb = _dot(ob_ref[...], wb_ref[...])
    mixin = jax.nn.sigmoid(ga_ref[...].astype(F32)) * ya + jax.nn.sigmoid(gb_ref[...].astype(F32)) * yb
    mix = _dot(mixin.astype(BF16), wo_ref[...])
    y = _layer_norm(DEEPNORM_ALPHA * x_ref[...] + mix, g_ref[...], b_ref[...])
    y_ref[...] = y
    yb_ref[...] = y.astype(BF16)


def _merge(x2, oa, ob, mgate, wa, wb, wo, g, b):
    m = x2.shape[0]
    tm, d = MERGE_TM, D_MODEL
    row = lambda wd, col=0: pl.BlockSpec((tm, wd), lambda i, col=col: (i, col))
    return pl.pallas_call(
        _merge_body,
        grid=(m // tm,),
        in_specs=[row(d), row(NSA_Q_W), row(GDN_W), row(d, 0), row(d, 1),
                  _const_spec(wa.shape), _const_spec(wb.shape), _const_spec(wo.shape),
                  _const_spec(g.shape), _const_spec(b.shape)],
        out_specs=[row(d), row(d)],
        out_shape=[jax.ShapeDtypeStruct((m, d), F32), jax.ShapeDtypeStruct((m, d), BF16)],
        compiler_params=_cparams("parallel"),
        name="merge",
    )(x2, oa, ob, mgate, mgate, wa, wb, wo, g, b)


FFN_TM = 512
FFN_HALO = 16
FFN_CK = 256


def _ffn_body(x_ref, xb_ref, prev_ref, wu_ref, cw_ref, wd_ref, g_ref, b_ref, out_ref, act_s,
              *, tiles_per_seq):
    i = pl.program_id(0)
    prev = prev_ref[...]
    prev = jnp.where(i % tiles_per_seq == 0, jnp.zeros_like(prev), prev)
    xc = jnp.concatenate([prev, xb_ref[...]], axis=0)

    def conv(h, c0):
        out = cw_ref[FFN_CONV - 1:FFN_CONV, c0:c0 + FFN_CK] * h[FFN_HALO:]
        for j in range(FFN_CONV - 1):
            shifted = pltpu.roll(h, FFN_CONV - 1 - j, axis=0)[FFN_HALO:]
            out = out + cw_ref[j:j + 1, c0:c0 + FFN_CK] * shifted
        return out

    for c in range(FFN_DIM // FFN_CK):
        c0 = c * FFN_CK
        hg = conv(_dot(xc, wu_ref[:, c0:c0 + FFN_CK]), c0)
        hv = conv(_dot(xc, wu_ref[:, FFN_DIM + c0:FFN_DIM + c0 + FFN_CK]), FFN_DIM + c0)
        act_s[:, c0:c0 + FFN_CK] = (jax.nn.silu(hg) * hv).astype(BF16)
    f = _dot(act_s[...], wd_ref[...])
    out_ref[...] = _layer_norm(DEEPNORM_ALPHA * x_ref[...] + f, g_ref[...], b_ref[...])


def _ffn(x1, x1b, wu, cw, wd, g, b, seq):
    m = x1.shape[0]
    tm, d = FFN_TM, D_MODEL
    return pl.pallas_call(
        functools.partial(_ffn_body, tiles_per_seq=seq // tm),
        grid=(m // tm,),
        in_specs=[pl.BlockSpec((tm, d), lambda i: (i, 0)),
                  pl.BlockSpec((tm, d), lambda i: (i, 0)),
                  pl.BlockSpec((FFN_HALO, d), lambda i: (jnp.maximum(i * (tm // FFN_HALO) - 1, 0), 0)),
                  _const_spec(wu.shape), _const_spec(cw.shape), _const_spec(wd.shape), _const_spec(g.shape), _const_spec(b.shape)],
        out_specs=pl.BlockSpec((tm, d), lambda i: (i, 0)),
        out_shape=jax.ShapeDtypeStruct((m, d), F32),
        scratch_shapes=[pltpu.VMEM((tm, FFN_DIM), BF16)],
        compiler_params=_cparams("parallel"),
        name="ffn",
    )(x1, x1b, x1b, wu, cw, wd, g, b)


def _lane_vec(vals, lane0):
    return jnp.zeros((1, LANES), F32).at[0, lane0:lane0 + vals.shape[0]].set(vals.astype(F32))


def _layer(x, w_in, cmp_pos, cmp_w1, cmp_w2, w_nsa_out, gdn_conv_w, gdn_a_log, gdn_dt_bias, gdn_norm_w,
           w_gdn_out, w_o, ln1_g, ln1_b, ffn_w_up, ffn_conv_w, ffn_w_down, ln2_g, ln2_b):
    b, s, d = x.shape
    m = b * s
    x2 = x.reshape(m, d)
    w_rows, w_gdn, w_t, w_g = _wprep(*w_in)
    keys, cmpkv, ggate, mgate, qvt, gt = _inproj(x2, w_rows, w_t, w_g)

    consts = _nsa_consts(s)
    post, w2sel = _compress_weights(cmp_pos, cmp_w2)
    cmp_kv = _compress(cmpkv.reshape(b, s // CMP_STRIDE, CMP_STRIDE * 256), cmp_w1, post, w2sel, consts["cmp_aug"])
    o_nsa = _nsa(qvt, keys.reshape(b, s, 256), cmp_kv, gt, consts, b, s)

    ck = GDN_CHUNK
    tri = np.tril(np.ones((ck, ck), np.float32))
    lt = jnp.asarray(np.kron(np.eye(GDN_BLK // ck, dtype=np.float32), tri))
    nq, co, eg = _gdn_intra(x, w_gdn, gdn_conv_w,
                                        _lane_vec(gdn_a_log, SM_DECAY), _lane_vec(gdn_dt_bias, SM_DECAY), lt)
    o_gdn = _gdn_scan(nq, co, eg, ggate.reshape(b, s, GDN_W), gdn_norm_w.reshape(1, GDN_HEAD_DIM))

    x1, x1b = _merge(x2, o_nsa.reshape(m, NSA_Q_W), o_gdn.reshape(m, GDN_W), mgate,
                     w_nsa_out.astype(BF16), w_gdn_out.astype(BF16), w_o.astype(BF16),
                     ln1_g.reshape(1, d), ln1_b.reshape(1, d))
    out = _ffn(x1, x1b, ffn_w_up.astype(BF16), ffn_conv_w, ffn_w_down.astype(BF16),
               ln2_g.reshape(1, d), ln2_b.reshape(1, d), s)
    return out.reshape(b, s, d)


def kernel(x, w_in, nsa_cmp_pos, nsa_cmp_w1, nsa_cmp_w2, w_nsa_out, gdn_conv_w, gdn_a_log, gdn_dt_bias, gdn_norm_w, w_gdn_out, w_o, ln1_g, ln1_b, ffn_w_up, ffn_conv_w, ffn_w_down, ln2_g, ln2_b):
    w_in_t = jnp.swapaxes(w_in, 1, 2)
    for l in range(DEPTH):
        x = _layer(x, (w_in_t, l), nsa_cmp_pos[l], nsa_cmp_w1[l], nsa_cmp_w2[l], w_nsa_out[l], gdn_conv_w[l],
                   gdn_a_log[l], gdn_dt_bias[l], gdn_norm_w[l], w_gdn_out[l], w_o[l], ln1_g[l], ln1_b[l],
                   ffn_w_up[l], ffn_conv_w[l], ffn_w_down[l], ln2_g[l], ln2_b[l])
    return x
```

```python
import functools

import numpy as np
import jax
import jax.numpy as jnp
from jax import lax
from jax.experimental import pallas as pl
from jax.experimental.pallas import tpu as pltpu

F32 = jnp.float32
BF16 = jnp.bfloat16

D_MODEL = 1024
NSA_HEADS = 8
NSA_KV_GROUPS = 2
NSA_REP = NSA_HEADS // NSA_KV_GROUPS
NSA_HEAD_DIM = 64
CMP_LEN = 32
CMP_STRIDE = 16
SLC_LEN = 64
SLC_TOPK = 8
WIN_LEN = 512
FORCE_SCORE = 1.0e4
NEG = -1.0e30
GDN_HEADS = 4
GDN_HEAD_DIM = 128
GDN_CONV = 4
GDN_CHUNK = 64
FFN_DIM = 2816
FFN_CONV = 3
DEPTH = 1
DEEPNORM_ALPHA = (2.0 * DEPTH) ** 0.25
LN_EPS = 1e-5
RMS_EPS = 1e-6

NSA_Q_W = NSA_HEADS * NSA_HEAD_DIM
NSA_KV_W = NSA_KV_GROUPS * NSA_HEAD_DIM
GDN_W = GDN_HEADS * GDN_HEAD_DIM

LANES = 128
SUBLANES = 8
VMEM_LIMIT_BYTES = 56 * 1024 * 1024

AUG_SEL0 = 64
AUG_POS_HI = 96
AUG_POS_LO = 97
AUG_PAD = 98
BIG = 2.0 ** 100
POS_SPLIT = 256
Q_TILE = 128
N_SLC = 32
V_ROWS = 80
SEL_KC = 512
NSA_SUB = 2

NT_DIMS = (((1,), (1,)), ((), ()))


def _dot(a, b, **kw):
    return jnp.dot(a, b, preferred_element_type=F32, **kw)


def _dot_nt(a, b, **kw):
    return lax.dot_general(a, b, NT_DIMS, preferred_element_type=F32, **kw)


def _cparams(*sem):
    return pltpu.CompilerParams(dimension_semantics=sem, vmem_limit_bytes=VMEM_LIMIT_BYTES)


def _const_spec(shape):
    nd = len(shape)
    return pl.BlockSpec(shape, lambda *_: (0,) * nd, pipeline_mode=pl.Buffered(1))


_IN_WIDTHS = (NSA_Q_W,) + (NSA_KV_W,) * 6 + (3 * NSA_HEADS, 3 * GDN_W, GDN_HEADS, GDN_HEADS, GDN_W, 2 * D_MODEL)
(_C_Q, _C_CK, _C_CV, _C_SK, _C_SV, _C_WK, _C_WV, _C_GATE, _C_GQKV, _C_BETA, _C_DECAY, _C_GGATE, _C_MERGE,
 IN_WIDTH) = (int(v) for v in np.cumsum((0,) + _IN_WIDTHS))
_C_SMALL = _C_BETA // LANES * LANES
SM_BETA = _C_BETA - _C_SMALL
SM_DECAY = _C_DECAY - _C_SMALL
KEYS_COLS = 2 * NSA_KV_W
CMP_COLS = 2 * NSA_KV_W
_INPROJ_GROUPS = (("keys", KEYS_COLS, BF16), ("cmp", CMP_COLS, BF16), ("ggate", GDN_W, BF16),
                  ("merge", 2 * D_MODEL, BF16))
_INPROJ_WIDTH = sum(w for _, w, _ in _INPROJ_GROUPS)
_GDN_PROJ_WIDTH = 3 * GDN_W + LANES
_INPROJ_T_ROWS = NSA_Q_W + 4 * NSA_HEAD_DIM
_GATE_T_ROWS = 32
INPROJ_TM = 512
INPROJ_TN = 512
WPREP_TK = 128


def _wprep_body(w_ref, rows_ref, gdn_ref, wt_ref, wg_ref):
    hd = NSA_HEAD_DIM
    feat = lambda c0, n: w_ref[0, c0:c0 + n, :]

    def put_t(ref, col, src):
        for r in range(0, src.shape[0], LANES):
            ref[:, col + r:col + r + LANES] = src[r:r + LANES].T.astype(BF16)

    for g in range(NSA_KV_GROUPS):
        put_t(rows_ref, 2 * g * hd, jnp.concatenate([feat(_C_SK + g * hd, hd), feat(_C_WK + g * hd, hd)], axis=0))
    c = 4 * hd
    for c0, n in ((_C_CK, 2 * NSA_KV_W), (_C_GGATE, GDN_W), (_C_MERGE, 2 * D_MODEL)):
        put_t(rows_ref, c, feat(c0, n))
        c += n
    put_t(gdn_ref, 0, feat(_C_GQKV, 3 * GDN_W))
    put_t(gdn_ref, 3 * GDN_W, feat(_C_SMALL, LANES))
    wt_ref[:NSA_Q_W, :] = feat(_C_Q, NSA_Q_W).astype(BF16)
    for j, c0 in enumerate((_C_SV, _C_WV, _C_SV + hd, _C_WV + hd)):
        wt_ref[NSA_Q_W + j * hd:NSA_Q_W + (j + 1) * hd, :] = feat(c0, hd).astype(BF16)
    wg_ref[...] = feat(_C_GATE, _GATE_T_ROWS).astype(BF16)


def _wprep(w_in_t, layer):
    k = w_in_t.shape[2]
    tk = WPREP_TK
    return pl.pallas_call(
        _wprep_body,
        grid=(k // tk,),
        in_specs=[pl.BlockSpec((1, IN_WIDTH, tk), lambda i: (layer, 0, i))],
        out_specs=[pl.BlockSpec((tk, _INPROJ_WIDTH), lambda i: (i, 0)),
                   pl.BlockSpec((tk, _GDN_PROJ_WIDTH), lambda i: (i, 0)),
                   pl.BlockSpec((_INPROJ_T_ROWS, tk), lambda i: (0, i)),
                   pl.BlockSpec((_GATE_T_ROWS, tk), lambda i: (0, i))],
        out_shape=[jax.ShapeDtypeStruct((k, _INPROJ_WIDTH), BF16), jax.ShapeDtypeStruct((k, _GDN_PROJ_WIDTH), BF16),
                   jax.ShapeDtypeStruct((_INPROJ_T_ROWS, k), BF16), jax.ShapeDtypeStruct((_GATE_T_ROWS, k), BF16)],
        compiler_params=_cparams("parallel"),
        name="wprep",
    )(w_in_t)


def _inproj_body(x_ref, w_ref, wt_ref, wg_ref, keys_ref, cmp_ref, ggate_ref, merge_ref, qvt_ref, gt_ref, cmp_s):
    x = x_ref[...].astype(BF16)
    outs = (keys_ref, None, ggate_ref, merge_ref)
    c0 = 0
    for ref, (name, width, _) in zip(outs, _INPROJ_GROUPS):
        for s in range(0, width, INPROJ_TN):
            e = min(s + INPROJ_TN, width)
            res = _dot(x, w_ref[:, c0 + s:c0 + e])
            if name == "cmp":
                for j in range(width // LANES):
                    cmp_s[j] = res[:, j * LANES:(j + 1) * LANES]
            else:
                ref[:, s:e] = res.astype(ref.dtype)
        c0 += width
    nblk = cmp_ref.shape[0]
    for l in range(CMP_STRIDE):
        for j in range(cmp_s.shape[0]):
            cmp_ref[:, l * CMP_COLS + j * LANES:l * CMP_COLS + (j + 1) * LANES] = (
                cmp_s[j, pl.ds(l, nblk, stride=CMP_STRIDE), :].astype(BF16))
    for s in range(0, _INPROJ_T_ROWS, 2 * LANES):
        qvt_ref[s:s + 2 * LANES, :] = _dot_nt(wt_ref[s:s + 2 * LANES, :], x).astype(qvt_ref.dtype)
    gt_ref[...] = _dot_nt(wg_ref[...], x)


def _inproj(x2, w_rows, w_t, w_g):
    m = x2.shape[0]
    tm = INPROJ_TM
    row_major = [(n, wd, dt) for n, wd, dt in _INPROJ_GROUPS if n != "cmp"]
    specs = {n: (pl.BlockSpec((tm, wd), lambda i: (i, 0)), jax.ShapeDtypeStruct((m, wd), dt))
             for n, wd, dt in row_major}
    specs["cmp"] = (pl.BlockSpec((tm // CMP_STRIDE, CMP_STRIDE * CMP_COLS), lambda i: (i, 0)),
                    jax.ShapeDtypeStruct((m // CMP_STRIDE, CMP_STRIDE * CMP_COLS), BF16))
    order = [n for n, _, _ in _INPROJ_GROUPS]
    return pl.pallas_call(
        _inproj_body,
        grid=(m // tm,),
        in_specs=[pl.BlockSpec((tm, D_MODEL), lambda i: (i, 0)), _const_spec(w_rows.shape),
                  _const_spec(w_t.shape), _const_spec(w_g.shape)],
        out_specs=[specs[n][0] for n in order]
        + [pl.BlockSpec((_INPROJ_T_ROWS, tm), lambda i: (0, i)), pl.BlockSpec((_GATE_T_ROWS, tm), lambda i: (0, i))],
        out_shape=[specs[n][1] for n in order]
        + [jax.ShapeDtypeStruct((_INPROJ_T_ROWS, m), BF16), jax.ShapeDtypeStruct((_GATE_T_ROWS, m), F32)],
        scratch_shapes=[pltpu.VMEM((CMP_COLS // LANES, tm, LANES), F32)],
        compiler_params=_cparams("parallel"),
        name="inproj",
    )(x2, w_rows, w_t, w_g)


def _compress_weights(cmp_pos, cmp_w2):
    hd, half = NSA_HEAD_DIM, CMP_LEN // 2
    posr = cmp_pos.reshape(2, 2, half, hd)
    post = jnp.broadcast_to(posr.transpose(1, 2, 0, 3)[:, :, :, None, :], (2, half, 2, 2, hd))
    post = jnp.concatenate([post.reshape(2, half * 4 * hd),
                            jnp.zeros((SUBLANES - 2, half * 4 * hd), cmp_pos.dtype)], axis=0)
    w2sel = jnp.zeros((2, 2, 2 * hd, LANES), cmp_w2.dtype)
    for g in range(2):
        w2sel = w2sel.at[:, g, g * hd:(g + 1) * hd, :hd].set(cmp_w2)
    return post.astype(BF16), w2sel.reshape(4, 2 * hd, LANES).astype(BF16)


def _compress_body(t_ref, w1_ref, pos_ref, w2_ref, aug_ref, out_ref, w1e_s):
    hd, half_len = NSA_HEAD_DIM, CMP_LEN // 2

    @pl.when(pl.program_id(0) == 0)
    def _():
        w1e_s[...] = jnp.zeros(w1e_s.shape, BF16)
        for which in range(2):
            for half in range(2):
                for l in range(half_len):
                    blk = w1_ref[which, (half * half_len + l) * hd:(half * half_len + l + 1) * hd, :].astype(BF16)
                    for g in range(NSA_KV_GROUPS):
                        r0 = l * CMP_COLS + which * LANES + g * hd
                        c0 = half * CMP_COLS + which * LANES + g * hd
                        w1e_s[r0:r0 + hd, c0:c0 + hd] = blk

    p = _dot(t_ref[0], w1e_s[...])
    pp = _dot(pos_ref[...], w1e_s[...])
    nxt = pltpu.roll(p[:, CMP_COLS:], p.shape[0] - 1, axis=0)
    pre = p[:, :CMP_COLS] + nxt + pp[0:1, :CMP_COLS] + pp[1:2, CMP_COLS:]
    h = jax.nn.gelu(pre).astype(BF16)
    n_idx = lax.broadcasted_iota(jnp.int32, (p.shape[0], LANES), 0)
    real = n_idx < p.shape[0] - 1
    for which in range(2):
        hw = h[:, which * LANES:(which + 1) * LANES]
        for g in range(2):
            o = jnp.where(real, _dot(hw, w2_ref[which * 2 + g]) + aug_ref[which], 0.0)
            out_ref[0, which * 2 + g] = (o if which == 0 else o.T).astype(out_ref.dtype)


def _compress(t2, w1, post, w2sel, aug):
    b, nblk, _ = t2.shape
    return pl.pallas_call(
        _compress_body,
        grid=(b,),
        in_specs=[pl.BlockSpec((1, nblk, CMP_STRIDE * CMP_COLS), lambda i: (i, 0, 0)),
                  _const_spec(w1.shape), _const_spec(post.shape), _const_spec(w2sel.shape), _const_spec(aug.shape)],
        out_specs=pl.BlockSpec((1, 4, nblk, LANES), lambda i: (i, 0, 0, 0)),
        out_shape=jax.ShapeDtypeStruct((b, 4, nblk, LANES), BF16),
        scratch_shapes=[pltpu.VMEM((CMP_STRIDE * CMP_COLS, 2 * CMP_COLS), BF16)],
        compiler_params=_cparams("arbitrary"),
        name="compress",
    )(t2, w1, post, w2sel, aug)


def _nsa_consts(s):
    hd, rep = NSA_HEAD_DIM, NSA_REP
    t = np.arange(s)
    kx_win = np.zeros((s + WIN_LEN, hd), np.float32)
    kx_win[WIN_LEN + t, AUG_POS_HI - hd] = t // POS_SPLIT
    kx_win[WIN_LEN + t, AUG_POS_LO - hd] = t % POS_SPLIT
    kx_win[:WIN_LEN, AUG_PAD - hd] = 1.0
    kx_sel = kx_win.copy()
    kx_sel[WIN_LEN + t, t // SLC_LEN] = 1.0
    vx_win = np.zeros((V_ROWS - hd, s + WIN_LEN), np.float32)
    vx_win[0, WIN_LEN:] = 1.0
    vx_sel = vx_win
    n_cmp = s // CMP_STRIDE
    cmp_aug = np.zeros((2, n_cmp, LANES), np.float32)
    end = np.arange(n_cmp) * CMP_STRIDE + CMP_LEN - 1
    cmp_aug[0, :, AUG_POS_HI] = end // POS_SPLIT
    cmp_aug[0, :, AUG_POS_LO] = end % POS_SPLIT
    qx = np.zeros((NSA_KV_GROUPS, LANES - AUG_POS_HI, rep * Q_TILE), np.float32)
    for h in range(NSA_HEADS):
        slope = 2.0 ** (-8.0 * (h + 1) / NSA_HEADS)
        lanes = slice((h % rep) * Q_TILE, (h % rep + 1) * Q_TILE)
        qx[h // rep, 0, lanes] = slope * POS_SPLIT
        qx[h // rep, 1, lanes] = slope
        qx[h // rep, AUG_PAD - AUG_POS_HI, lanes] = -BIG
    c0 = np.arange(n_cmp)[None, :] * CMP_STRIDE
    s0 = np.arange(s // SLC_LEN)[:, None] * SLC_LEN
    ov_t = ((c0 < s0 + SLC_LEN) & (c0 + CMP_LEN > s0)).astype(np.float32)
    ov_t[:, (s - CMP_LEN) // CMP_STRIDE + 1:] = 0.0
    kk = np.arange(Q_TILE)[:, None]
    qq = np.arange(Q_TILE)[None, :]
    causal = np.tile(np.where(kk <= qq, 0.0, NEG).astype(np.float32), (1, rep))
    after = np.tile(np.where(kk > qq, 0.0, NEG).astype(np.float32), (1, rep))
    j = jnp.asarray
    return dict(kx_sel=j(kx_sel, BF16), kx_win=j(kx_win, BF16), vx_sel=j(vx_sel, BF16), vx_win=j(vx_win, BF16),
                cmp_aug=j(cmp_aug), qx=j(qx), ov_t=j(ov_t), causal=j(causal), after=j(after))


def _nsa_body(qt_ref, k_ref, vt_ref, kc_ref, vct_ref, gt_ref, kxs_ref, kxw_ref, vxs_ref, vxw_ref, qx_ref, ovt_ref,
              causal_ref, after_ref, out_ref, ks_s, kw_s, vs_s, vw_s):
    hd, rep, tq = NSA_HEAD_DIM, NSA_REP, Q_TILE
    nq = rep * tq
    i = pl.program_id(2)

    @pl.when(i == 0)
    def _():
        keys = k_ref[0]
        ks_s[:WIN_LEN, :hd] = jnp.zeros((WIN_LEN, hd), BF16)
        ks_s[WIN_LEN:, :hd] = keys[:, :hd]
        ks_s[:, hd:] = kxs_ref[...]
        kw_s[:WIN_LEN, :hd] = jnp.zeros((WIN_LEN, hd), BF16)
        kw_s[WIN_LEN:, :hd] = keys[:, hd:]
        kw_s[:, hd:] = kxw_ref[...]
        vals = vt_ref[...]
        vs_s[:hd, :WIN_LEN] = jnp.zeros((hd, WIN_LEN), BF16)
        vs_s[:hd, WIN_LEN:] = vals[:hd]
        vs_s[hd:, :] = vxs_ref[...]
        vw_s[:hd, :WIN_LEN] = jnp.zeros((hd, WIN_LEN), BF16)
        vw_s[:hd, WIN_LEN:] = vals[hd:]
        vw_s[hd:, :] = vxw_ref[...]

    qx = qx_ref[0]
    sg_all = jax.nn.sigmoid(gt_ref[...])
    grp = pl.program_id(1)

    full_past = WIN_LEN // tq

    def front(sub, res, it, past):
        qt = qt_ref[:, sub * tq:(sub + 1) * tq]
        q64 = jnp.concatenate([qt[r * hd:(r + 1) * hd, :] for r in range(rep)], axis=1).astype(F32) * (hd ** -0.5)

        def q_aug(sel_rows):
            return jnp.concatenate([q64, sel_rows, qx], axis=0).astype(BF16)

        n_row = lax.broadcasted_iota(jnp.int32, (LANES, nq), 0)
        t_lane = it * tq + (lax.broadcasted_iota(jnp.int32, (LANES, nq), 1) & (tq - 1))
        valid = t_lane >= n_row * CMP_STRIDE + (CMP_LEN - 1)
        qa0 = q_aug(jnp.zeros((N_SLC, nq), F32))
        sc = jnp.where(valid, _dot(kc_ref[0, 0], qa0), NEG)
        mc = jnp.max(sc, axis=0, keepdims=True)
        ec = jnp.where(valid, jnp.exp(sc - mc), 0.0)
        lc = jnp.sum(ec, axis=0, keepdims=True)
        pc = ec * jnp.where(lc > 0.0, 1.0 / lc, 0.0)
        o_cmp = _dot(vct_ref[0, 0], pc.astype(BF16))[:hd]
        psum = pc[:, 0:tq] + pc[:, tq:2 * tq] + pc[:, 2 * tq:3 * tq] + pc[:, 3 * tq:4 * tq]
        score_t = _dot(ovt_ref[...], psum, precision=lax.Precision.HIGHEST)
        yield

        n_keys = (past + 1) * tq
        if past == full_past:
            w0 = pl.multiple_of(it * tq, tq)
        else:
            w0 = WIN_LEN
        diag = past * tq

        def one_shot_masks(s):
            parts = [s[:diag], s[diag:] + causal_ref[...]] if past else [s + causal_ref[...]]
            if past == full_past:
                parts = [s[:tq] + after_ref[...], s[tq:diag], parts[1]]
            return jnp.concatenate(parts, axis=0) if len(parts) > 1 else parts[0]

        s_w = one_shot_masks(_dot(kw_s[pl.ds(w0, n_keys), :], qa0))
        p_w = jnp.exp(s_w - jnp.max(s_w, axis=0, keepdims=True))
        acc_w = _dot(vw_s[:, pl.ds(w0, n_keys)], p_w.astype(BF16))
        o_win = acc_w[:hd] * (1.0 / acc_w[hd:hd + 1])
        yield

        jb = lax.broadcasted_iota(jnp.int32, (N_SLC, tq), 0)
        cur = (it * tq + lax.broadcasted_iota(jnp.int32, (N_SLC, tq), 1)) // SLC_LEN
        forced = (jb == 0) | (jb == cur) | (jb == cur - 1)
        score_t = jnp.where(forced, FORCE_SCORE, jnp.where(jb <= cur, score_t, -1.0))
        rank = jnp.zeros((N_SLC, tq), F32)
        for jp in range(N_SLC):
            other = score_t[jp:jp + 1, :]
            ge = jnp.where(other >= score_t, 1.0, 0.0)
            gt = jnp.where(other > score_t, 1.0, 0.0)
            rank = rank + jnp.where(jb > jp, ge, gt)
        sel = rank < float(SLC_TOPK)
        qa = q_aug(jnp.concatenate([jnp.where(sel, 0.0, -BIG)] * rep, axis=1))
        lo_blk = jnp.min(jnp.where(sel & (jb >= 2) & (jb <= cur), jb.astype(F32), float(N_SLC)))
        lo_key = (lo_blk.astype(jnp.int32) // 2) * tq
        yield

        s_main = _dot(ks_s[pl.ds(w0, n_keys), :], qa)
        if past < full_past:
            s_s = jnp.concatenate([s_main[:diag], s_main[diag:] + causal_ref[...]], axis=0) if past else (
                s_main + causal_ref[...])
            m_s = jnp.max(s_s, axis=0, keepdims=True)
            acc_s = _dot(vs_s[:, pl.ds(w0, n_keys)], jnp.exp(s_s - m_s).astype(BF16))
            res.update(early=False, acc_s=acc_s, o_cmp=o_cmp, o_win=o_win)
            return
        e_key = it * tq - WIN_LEN
        t0 = pl.multiple_of(jnp.where(e_key > 0, WIN_LEN, 0), tq)
        s_s = jnp.concatenate([_dot(ks_s[pl.ds(t0, tq), :], qa), s_main[:diag],
                               s_main[diag:] + causal_ref[...]], axis=0)
        m_s = jnp.max(s_s, axis=0, keepdims=True)
        p_s = jnp.exp(s_s - m_s).astype(BF16)
        acc_s = _dot(vs_s[:, pl.ds(t0, tq)], p_s[:tq]) + _dot(vs_s[:, pl.ds(w0, n_keys)], p_s[tq:])
        c_hi = (e_key - tq + SEL_KC - 1) // SEL_KC
        c_lo = jnp.where(lo_key < e_key, (lo_key - tq) // SEL_KC, c_hi)
        res.update(early=True, qa=qa, e_key=e_key, c_lo=c_lo, c_hi=c_hi, m_s=m_s, acc_s=acc_s, o_cmp=o_cmp,
                   o_win=o_win)

    def tail(sub, f):
        def early_step(c, carry):
            qa, e_key = f["qa"], f["e_key"]
            m, acc = carry
            k0 = tq + c * SEL_KC
            start = pl.multiple_of(WIN_LEN + k0, tq)
            k_abs = k0 + lax.broadcasted_iota(jnp.int32, (SEL_KC, nq), 0)
            s = jnp.where(k_abs < e_key, _dot(ks_s[pl.ds(start, SEL_KC), :], qa), NEG)
            m_new = jnp.maximum(m, jnp.max(s, axis=0, keepdims=True))
            p = jnp.exp(s - m_new).astype(BF16)
            return m_new, acc * jnp.exp(m - m_new) + _dot(vs_s[:, pl.ds(start, SEL_KC)], p)

        acc_s = f["acc_s"]
        if f["early"]:
            _, acc_s = lax.fori_loop(f["c_lo"], f["c_hi"], early_step, (f["m_s"], acc_s))
        o_slc = acc_s[:hd] * (1.0 / acc_s[hd:hd + 1])

        sg = sg_all[:, sub * tq:(sub + 1) * tq]
        gate = lambda br, r: jnp.where(grp == 0, sg[br * NSA_HEADS + r:br * NSA_HEADS + r + 1],
                                       sg[br * NSA_HEADS + rep + r:br * NSA_HEADS + rep + r + 1])
        for pair in range(rep // 2):
            halves = []
            for r in (2 * pair, 2 * pair + 1):
                lanes = slice(r * tq, (r + 1) * tq)
                halves.append(gate(0, r) * f["o_cmp"][:, lanes] + gate(1, r) * o_slc[:, lanes]
                              + gate(2, r) * f["o_win"][:, lanes])
            out_ref[0, sub * tq:(sub + 1) * tq, pair * LANES:(pair + 1) * LANES] = (
                jnp.concatenate(halves, axis=0).T.astype(out_ref.dtype))

    def run(tiles):
        fronts = [{} for _ in range(NSA_SUB)]
        for sub, (it, past) in enumerate(tiles):
            for _ in front(sub, fronts[sub], it, past):
                pass
        for sub in range(NSA_SUB):
            tail(sub, fronts[sub])

    n_short = -(-full_past // NSA_SUB)
    for step in range(n_short):
        @pl.when(i == step)
        def _(step=step):
            run([(step * NSA_SUB + sub, min(step * NSA_SUB + sub, full_past)) for sub in range(NSA_SUB)])

    @pl.when(i >= n_short)
    def _():
        run([(i * NSA_SUB + sub, full_past) for sub in range(NSA_SUB)])


def _nsa(qvt, keys, cmp_kv, gt, consts, b, s):
    tqs = NSA_SUB * Q_TILE
    nt = s // tqs
    c = consts
    in_specs = [
        pl.BlockSpec((2 * LANES, tqs), lambda bi, g, i: (g, bi * nt + i)),
        pl.BlockSpec((1, s, LANES), lambda bi, g, i: (bi, 0, g)),
        pl.BlockSpec((LANES, s), lambda bi, g, i: (NSA_Q_W // LANES + g, bi)),
        pl.BlockSpec((1, 1, s // CMP_STRIDE, LANES), lambda bi, g, i: (bi, g, 0, 0)),
        pl.BlockSpec((1, 1, s // CMP_STRIDE, LANES), lambda bi, g, i: (bi, 2 + g, 0, 0)),
        pl.BlockSpec((_GATE_T_ROWS, tqs), lambda bi, g, i: (0, bi * nt + i)),
        _const_spec(c["kx_sel"].shape), _const_spec(c["kx_win"].shape), _const_spec(c["vx_sel"].shape),
        _const_spec(c["vx_win"].shape),
        pl.BlockSpec((1,) + c["qx"].shape[1:], lambda bi, g, i: (g, 0, 0)),
        _const_spec(c["ov_t"].shape), _const_spec(c["causal"].shape), _const_spec(c["after"].shape),
    ]
    return pl.pallas_call(
        _nsa_body,
        grid=(b, NSA_KV_GROUPS, nt),
        in_specs=in_specs,
        out_specs=pl.BlockSpec((1, tqs, 2 * LANES), lambda bi, g, i: (bi, i, g)),
        out_shape=jax.ShapeDtypeStruct((b, s, NSA_Q_W), BF16),
        scratch_shapes=[pltpu.VMEM((s + WIN_LEN, LANES), BF16), pltpu.VMEM((s + WIN_LEN, LANES), BF16),
                        pltpu.VMEM((V_ROWS, s + WIN_LEN), BF16), pltpu.VMEM((V_ROWS, s + WIN_LEN), BF16)],
        compiler_params=_cparams("parallel", "parallel", "arbitrary"),
        name="nsa",
    )(qvt, keys, qvt, cmp_kv, cmp_kv, gt, c["kx_sel"], c["kx_win"], c["vx_sel"], c["vx_win"], c["qx"], c["ov_t"],
      c["causal"], c["after"])


GDN_TS = 512
GDN_BLK = 128
GDN_HALO = 8
GDN_XHALO = 16
GDN_INV_BASE = 8
GDN_CHAIN_UNITS_PER_PREP_UNIT = 4
GDN_SCAN_ROWS = GDN_HEAD_DIM + GDN_CHUNK


def _gdn_prep(x_ref, prev_ref, w_ref, cw_ref, alog_ref, dtb_ref, lt_ref, first_tile, xp_s, buf):
    x_s, p_s, rhs_s, qg_s, aqk_s, kdt_s, a_s, eg_s = buf
    ts, dh, nh, blk = GDN_TS, GDN_HEAD_DIM, GDN_HEADS, GDN_BLK
    proj = _dot(x_ref[0].astype(BF16), w_ref[...])
    hist = _dot(prev_ref[0].astype(BF16), w_ref[:, :3 * GDN_W])[GDN_XHALO - GDN_HALO:]
    xp_s[0:GDN_HALO, :] = jnp.where(first_tile, 0.0, hist)
    xp_s[GDN_HALO:, :] = proj[:, :3 * GDN_W]
    yield
    act = []
    for blk_i in range(3 * nh):
        lanes = slice(blk_i * dh, (blk_i + 1) * dh)
        xp = xp_s[:, lanes]
        conv = cw_ref[0:1, lanes] * xp
        for j in range(1, GDN_CONV):
            conv = pltpu.roll(conv, 1, axis=0) + cw_ref[j:j + 1, lanes] * xp
        act.append(jax.nn.silu(conv[GDN_HALO:]))
        yield

    sm = proj[:, 3 * GDN_W:]
    beta = jax.nn.sigmoid(sm)
    g = -jnp.exp(alog_ref[...]) * jax.nn.softplus(sm + dtb_ref[...])
    gcum = jnp.concatenate([_dot(lt_ref[...], g[r:r + blk], precision=lax.Precision.HIGHEST)
                            for r in range(0, ts, blk)], axis=0)
    eg = jnp.exp(gcum)
    eg_s[...] = eg

    ri = lax.broadcasted_iota(jnp.int32, (blk, blk), 0)
    ci = lax.broadcasted_iota(jnp.int32, (blk, blk), 1)
    same = (ri // GDN_CHUNK) == (ci // GDN_CHUNK)
    causal = same & (ri >= ci)
    strict = same & (ri > ci)
    eye = (ri == ci).astype(F32)
    same_base = (ri // GDN_INV_BASE) == (ci // GDN_INV_BASE)

    for pb in range(ts // blk):
        rows = slice(pb * blk, (pb + 1) * blk)
        gc = gcum[rows]
        gc_t = gc.T
        first = lax.broadcasted_iota(jnp.int32, (blk, LANES), 0) < GDN_CHUNK
        g_last = jnp.where(first, gc[GDN_CHUNK - 1:GDN_CHUNK, :], gc[blk - 1:blk, :])
        e_dec = jnp.exp(g_last - gc)
        for h in range(nh):
            q, k, v = act[h][rows], act[nh + h][rows], act[2 * nh + h][rows]
            q = q * lax.rsqrt(jnp.sum(q * q, axis=1, keepdims=True) + RMS_EPS) * (dh ** -0.5)
            k = k * lax.rsqrt(jnp.sum(k * k, axis=1, keepdims=True) + RMS_EPS)
            b_col = beta[rows, SM_BETA + h:SM_BETA + h + 1]
            eg_col = eg[rows, SM_DECAY + h:SM_DECAY + h + 1]
            gdiff = gc[:, SM_DECAY + h:SM_DECAY + h + 1] - gc_t[SM_DECAY + h:SM_DECAY + h + 1, :]
            decay = jnp.exp(jnp.where(causal, gdiff, NEG))
            kb = k * b_col
            kbf, kf, qf = kb.astype(BF16), k.astype(BF16), q.astype(BF16)
            a = jnp.where(strict, -_dot_nt(kbf, kf) * decay, 0.0)
            c = pb * nh + h
            a_base = jnp.where(same_base, a, 0.0)
            a_s[c] = a.astype(BF16)
            x_s[c] = a_base.astype(BF16)
            p_s[c] = eye + a_base
            rhs_s[c] = jnp.concatenate([v * b_col, kb * eg_col], axis=1).astype(BF16)
            qg_s[c] = q * eg_col
            aqk_s[c] = jnp.where(causal, _dot_nt(qf, kf) * decay, 0.0).astype(BF16)
            kdt_s[c] = (k * e_dec[:, SM_DECAY + h:SM_DECAY + h + 1]).T.astype(BF16)
            yield


def _gdn_chains(buf, nq_ref, co_ref, eg_ref):
    x_s, p_s, rhs_s, qg_s, aqk_s, kdt_s, a_s, eg_s = buf
    dh, nh, blk = GDN_HEAD_DIM, GDN_HEADS, GDN_BLK
    n_chain = x_s.shape[0]
    eg_ref[0] = eg_s[...]
    ri = lax.broadcasted_iota(jnp.int32, (blk, blk), 0)
    ci = lax.broadcasted_iota(jnp.int32, (blk, blk), 1)
    same = lambda size: (ri // size) == (ci // size)
    for c in range(n_chain):
        y = x_s[c]
        x_s[c] = _dot(y, y).astype(BF16)
        yield
    for c in range(n_chain):
        y2 = x_s[c]
        p = p_s[c]
        p_s[c] = p + _dot(p.astype(BF16), y2)
        x_s[c] = _dot(y2, y2).astype(BF16)
        yield
    for c in range(n_chain):
        p = p_s[c]
        p_s[c] = p + _dot(p.astype(BF16), x_s[c])
        yield
    size = 2 * GDN_INV_BASE
    while size <= GDN_CHUNK:
        between = same(size) & jnp.logical_not(same(size // 2))
        for c in range(n_chain):
            a = a_s[c]
            x_s[c] = _dot(p_s[c].astype(BF16), jnp.where(between, a, jnp.zeros_like(a))).astype(BF16)
            yield
        for c in range(n_chain):
            t = p_s[c]
            p_s[c] = t + _dot(x_s[c], t.astype(BF16))
            yield
        size *= 2

    tok_half = lax.broadcasted_iota(jnp.int32, (blk, blk), 1) // GDN_CHUNK
    for c in range(n_chain):
        rhs_s[c] = _dot(p_s[c].astype(BF16), rhs_s[c]).astype(BF16)
        yield
    for c in range(n_chain):
        pb, h = divmod(c, nh)
        uw = rhs_s[c]
        a1 = _dot(aqk_s[c], uw)
        q_loc = qg_s[c] - a1[:, dh:]
        kdt = kdt_s[c]
        for half in range(blk // GDN_CHUNK):
            k1 = _dot(jnp.where(tok_half == half, kdt, jnp.zeros_like(kdt)), uw)
            n = pb * (blk // GDN_CHUNK) + half
            rows = slice(half * GDN_CHUNK, (half + 1) * GDN_CHUNK)
            nq_ref[0, h, n, :dh, :] = (-k1[:, dh:]).astype(BF16)
            nq_ref[0, h, n, dh:, :] = q_loc[rows].astype(BF16)
            co_ref[0, h, n, :dh, :] = k1[:, :dh].astype(BF16)
            co_ref[0, h, n, dh:, :] = a1[rows, :dh].astype(BF16)
        yield


def _interleave(major, minor, minor_per_major):
    for _ in major:
        for _ in range(minor_per_major):
            next(minor, None)
    for _ in minor:
        pass


def _gdn_intra_body(x_ref, prev_ref, w_ref, cw_ref, alog_ref, dtb_ref, lt_ref, nq_ref, co_ref, eg_ref, xp_s, *bufs,
                    tiles_per_seq, n_buf):
    j = pl.program_id(0)
    sets = (bufs[:n_buf], bufs[n_buf:])

    @pl.when(j == 0)
    def _():
        for ref in sets[1]:
            ref[...] = jnp.zeros(ref.shape, ref.dtype)

    first_tile = (j % tiles_per_seq) == 0
    for parity in range(2):
        @pl.when(j % 2 == parity)
        def _(parity=parity):
            chains = _gdn_chains(sets[1 - parity], nq_ref, co_ref, eg_ref)
            prep = _gdn_prep(x_ref, prev_ref, w_ref, cw_ref, alog_ref, dtb_ref, lt_ref, first_tile, xp_s,
                             sets[parity])
            _interleave(prep, chains, GDN_CHAIN_UNITS_PER_PREP_UNIT)


def _gdn_intra(x, w_gdn, conv_w, alog_l, dtb_l, lt):
    b, s, d = x.shape
    ts, nh, dh = GDN_TS, GDN_HEADS, GDN_HEAD_DIM
    tps = s // ts
    nt = b * tps
    n_chain = (ts // GDN_BLK) * nh
    sq = lambda dt: pltpu.VMEM((n_chain, GDN_BLK, GDN_BLK), dt)
    buf = lambda: [sq(BF16), sq(F32), pltpu.VMEM((n_chain, GDN_BLK, 2 * dh), BF16), sq(F32), sq(BF16), sq(BF16),
                   sq(BF16), pltpu.VMEM((ts, LANES), F32)]
    src = lambda j: jnp.minimum(j, nt - 1)
    dst = lambda j: jnp.maximum(j - 1, 0)
    cspec = lambda: pl.BlockSpec((1, nh, ts // GDN_CHUNK, GDN_SCAN_ROWS, dh),
                                 lambda j: (dst(j) // tps, 0, dst(j) % tps, 0, 0))
    cshape = jax.ShapeDtypeStruct((b, nh, s // GDN_CHUNK, GDN_SCAN_ROWS, dh), BF16)
    return pl.pallas_call(
        functools.partial(_gdn_intra_body, tiles_per_seq=tps, n_buf=len(buf())),
        grid=(nt + 1,),
        in_specs=[
            pl.BlockSpec((1, ts, d), lambda j: (src(j) // tps, src(j) % tps, 0)),
            pl.BlockSpec((1, GDN_XHALO, d),
                         lambda j: (src(j) // tps, jnp.maximum((src(j) % tps) * (ts // GDN_XHALO) - 1, 0), 0)),
            _const_spec(w_gdn.shape), _const_spec(conv_w.shape), _const_spec(alog_l.shape), _const_spec(dtb_l.shape),
            _const_spec(lt.shape),
        ],
        out_specs=[cspec(), cspec(), pl.BlockSpec((1, ts, LANES), lambda j: (dst(j) // tps, dst(j) % tps, 0))],
        out_shape=[cshape, cshape, jax.ShapeDtypeStruct((b, s, LANES), F32)],
        scratch_shapes=[pltpu.VMEM((ts + GDN_HALO, 3 * GDN_W), F32)] + buf() + buf(),
        compiler_params=_cparams("arbitrary"),
        name="gdn_intra",
    )(x, x, w_gdn, conv_w, alog_l, dtb_l, lt)


def _gdn_scan_body(nq_ref, co_ref, eg_ref, gate_ref, nw_ref, out_ref, st_s):
    nh, dh, ck = GDN_HEADS, GDN_HEAD_DIM, GDN_CHUNK
    st_s[...] = jnp.zeros(st_s.shape, F32)

    def chunk(n, carry):
        r0 = pl.multiple_of(n * ck, ck)
        d_row = eg_ref[0, pl.ds(r0 + ck - 1, 1), :]
        for h in range(nh):
            st = st_s[h]
            res = _dot(nq_ref[0, h, n], st.astype(BF16)) + co_ref[0, h, n].astype(F32)
            st_s[h] = st * d_row[:, SM_DECAY + h:SM_DECAY + h + 1] + res[:dh]
            o = res[dh:]
            ms = jnp.mean(o * o, axis=1, keepdims=True)
            gt = gate_ref[0, pl.ds(r0, ck), h * dh:(h + 1) * dh].astype(F32)
            out_ref[0, pl.ds(r0, ck), h * dh:(h + 1) * dh] = (
                o * lax.rsqrt(ms + RMS_EPS) * nw_ref[...] * jax.nn.silu(gt)).astype(out_ref.dtype)
        return carry

    lax.fori_loop(0, nq_ref.shape[2], chunk, 0, unroll=8)


def _gdn_scan(nq, co, eg, ggate, norm_w):
    b, nh, nc, rows, dh = nq.shape
    s = nc * GDN_CHUNK
    cspec = lambda: pl.BlockSpec((1, nh, nc, rows, dh), lambda bi: (bi, 0, 0, 0, 0))
    return pl.pallas_call(
        _gdn_scan_body,
        grid=(b,),
        in_specs=[cspec(), cspec(),
                  pl.BlockSpec((1, s, LANES), lambda bi: (bi, 0, 0)),
                  pl.BlockSpec((1, s, GDN_W), lambda bi: (bi, 0, 0)),
                  _const_spec(norm_w.shape)],
        out_specs=pl.BlockSpec((1, s, GDN_W), lambda bi: (bi, 0, 0)),
        out_shape=jax.ShapeDtypeStruct((b, s, GDN_W), BF16),
        scratch_shapes=[pltpu.VMEM((nh, dh, dh), F32)],
        compiler_params=_cparams("parallel"),
        name="gdn_scan",
    )(nq, co, eg, ggate, norm_w)


MERGE_TM = 512


def _layer_norm(y, g, b):
    mu = jnp.mean(y, axis=1, keepdims=True)
    d = y - mu
    var = jnp.mean(d * d, axis=1, keepdims=True)
    return d * lax.rsqrt(var + LN_EPS) * g + b


def _merge_body(x_ref, oa_ref, ob_ref, ga_ref, gb_ref, wa_ref, wb_ref, wo_ref, g_ref, b_ref, y_ref, yb_ref):
    ya = _dot(oa_ref[...], wa_ref[...])
    yb = _dot(ob_ref[...], wb_ref[...])
    mixin = jax.nn.sigmoid(ga_ref[...].astype(F32)) * ya + jax.nn.sigmoid(gb_ref[...].astype(F32)) * yb
    mix = _dot(mixin.astype(BF16), wo_ref[...])
    y = _layer_norm(DEEPNORM_ALPHA * x_ref[...] + mix, g_ref[...], b_ref[...])
    y_ref[...] = y
    yb_ref[...] = y.astype(BF16)


def _merge(x2, oa, ob, mgate, wa, wb, wo, g, b):
    m = x2.shape[0]
    tm, d = MERGE_TM, D_MODEL
    row = lambda wd, col=0: pl.BlockSpec((tm, wd), lambda i, col=col: (i, col))
    return pl.pallas_call(
        _merge_body,
        grid=(m // tm,),
        in_specs=[row(d), row(NSA_Q_W), row(GDN_W), row(d, 0), row(d, 1),
                  _const_spec(wa.shape), _const_spec(wb.shape), _const_spec(wo.shape),
                  _const_spec(g.shape), _const_spec(b.shape)],
        out_specs=[row(d), row(d)],
        out_shape=[jax.ShapeDtypeStruct((m, d), F32), jax.ShapeDtypeStruct((m, d), BF16)],
        compiler_params=_cparams("parallel"),
        name="merge",
    )(x2, oa, ob, mgate, mgate, wa, wb, wo, g, b)


FFN_TM = 512
FFN_HALO = 16
FFN_CK = 256


def _ffn_body(x_ref, xb_ref, prev_ref, wu_ref, cw_ref, wd_ref, g_ref, b_ref, out_ref, act_s,
              *, tiles_per_seq):
    i = pl.program_id(0)
    prev = prev_ref[...]
    prev = jnp.where(i % tiles_per_seq == 0, jnp.zeros_like(prev), prev)
    xc = jnp.concatenate([prev, xb_ref[...]], axis=0)

    def conv(h, c0):
        out = cw_ref[FFN_CONV - 1:FFN_CONV, c0:c0 + FFN_CK] * h[FFN_HALO:]
        for j in range(FFN_CONV - 1):
            shifted = pltpu.roll(h, FFN_CONV - 1 - j, axis=0)[FFN_HALO:]
            out = out + cw_ref[j:j + 1, c0:c0 + FFN_CK] * shifted
        return out

    for c in range(FFN_DIM // FFN_CK):
        c0 = c * FFN_CK
        hg = conv(_dot(xc, wu_ref[:, c0:c0 + FFN_CK]), c0)
        hv = conv(_dot(xc, wu_ref[:, FFN_DIM + c0:FFN_DIM + c0 + FFN_CK]), FFN_DIM + c0)
        act_s[:, c0:c0 + FFN_CK] = (jax.nn.silu(hg) * hv).astype(BF16)
    f = _dot(act_s[...], wd_ref[...])
    out_ref[...] = _layer_norm(DEEPNORM_ALPHA * x_ref[...] + f, g_ref[...], b_ref[...])


def _ffn(x1, x1b, wu, cw, wd, g, b, seq):
    m = x1.shape[0]
    tm, d = FFN_TM, D_MODEL
    return pl.pallas_call(
        functools.partial(_ffn_body, tiles_per_seq=seq // tm),
        grid=(m // tm,),
        in_specs=[pl.BlockSpec((tm, d), lambda i: (i, 0)),
                  pl.BlockSpec((tm, d), lambda i: (i, 0)),
                  pl.BlockSpec((FFN_HALO, d), lambda i: (jnp.maximum(i * (tm // FFN_HALO) - 1, 0), 0)),
                  _const_spec(wu.shape), _const_spec(cw.shape), _const_spec(wd.shape), _const_spec(g.shape), _const_spec(b.shape)],
        out_specs=pl.BlockSpec((tm, d), lambda i: (i, 0)),
        out_shape=jax.ShapeDtypeStruct((m, d), F32),
        scratch_shapes=[pltpu.VMEM((tm, FFN_DIM), BF16)],
        compiler_params=_cparams("parallel"),
        name="ffn",
    )(x1, x1b, x1b, wu, cw, wd, g, b)


def _lane_vec(vals, lane0):
    return jnp.zeros((1, LANES), F32).at[0, lane0:lane0 + vals.shape[0]].set(vals.astype(F32))


def _layer(x, w_in, cmp_pos, cmp_w1, cmp_w2, w_nsa_out, gdn_conv_w, gdn_a_log, gdn_dt_bias, gdn_norm_w,
           w_gdn_out, w_o, ln1_g, ln1_b, ffn_w_up, ffn_conv_w, ffn_w_down, ln2_g, ln2_b):
    b, s, d = x.shape
    m = b * s
    x2 = x.reshape(m, d)
    w_rows, w_gdn, w_t, w_g = _wprep(*w_in)
    keys, cmpkv, ggate, mgate, qvt, gt = _inproj(x2, w_rows, w_t, w_g)

    consts = _nsa_consts(s)
    post, w2sel = _compress_weights(cmp_pos, cmp_w2)
    cmp_kv = _compress(cmpkv.reshape(b, s // CMP_STRIDE, CMP_STRIDE * CMP_COLS), cmp_w1, post, w2sel,
                       consts["cmp_aug"])
    o_nsa = _nsa(qvt, keys.reshape(b, s, KEYS_COLS), cmp_kv, gt, consts, b, s)

    ck = GDN_CHUNK
    tri = np.tril(np.ones((ck, ck), np.float32))
    lt = jnp.asarray(np.kron(np.eye(GDN_BLK // ck, dtype=np.float32), tri))
    nq, co, eg = _gdn_intra(x, w_gdn, gdn_conv_w, _lane_vec(gdn_a_log, SM_DECAY), _lane_vec(gdn_dt_bias, SM_DECAY),
                            lt)
    o_gdn = _gdn_scan(nq, co, eg, ggate.reshape(b, s, GDN_W), gdn_norm_w.reshape(1, GDN_HEAD_DIM))

    x1, x1b = _merge(x2, o_nsa.reshape(m, NSA_Q_W), o_gdn.reshape(m, GDN_W), mgate,
                     w_nsa_out.astype(BF16), w_gdn_out.astype(BF16), w_o.astype(BF16),
                     ln1_g.reshape(1, d), ln1_b.reshape(1, d))
    out = _ffn(x1, x1b, ffn_w_up.astype(BF16), ffn_conv_w, ffn_w_down.astype(BF16),
               ln2_g.reshape(1, d), ln2_b.reshape(1, d), s)
    return out.reshape(b, s, d)


def kernel(x, w_in, nsa_cmp_pos, nsa_cmp_w1, nsa_cmp_w2, w_nsa_out, gdn_conv_w, gdn_a_log, gdn_dt_bias, gdn_norm_w, w_gdn_out, w_o, ln1_g, ln1_b, ffn_w_up, ffn_conv_w, ffn_w_down, ln2_g, ln2_b):
    w_in_t = jnp.swapaxes(w_in, 1, 2)
    for l in range(DEPTH):
        x = _layer(x, (w_in_t, l), nsa_cmp_pos[l], nsa_cmp_w1[l], nsa_cmp_w2[l], w_nsa_out[l], gdn_conv_w[l],
                   gdn_a_log[l], gdn_dt_bias[l], gdn_norm_w[l], w_gdn_out[l], w_o[l], ln1_g[l], ln1_b[l],
                   ffn_w_up[l], ffn_conv_w[l], ffn_w_down[l], ln2_g[l], ln2_b[l])
    return x
```

```python
import functools

import numpy as np
import jax
import jax.numpy as jnp
from jax import lax
from jax.experimental import pallas as pl
from jax.experimental.pallas import tpu as pltpu

F32 = jnp.float32
BF16 = jnp.bfloat16

D_MODEL = 1024
NSA_HEADS = 8
NSA_KV_GROUPS = 2
NSA_REP = NSA_HEADS // NSA_KV_GROUPS
NSA_HEAD_DIM = 64
CMP_LEN = 32
CMP_STRIDE = 16
SLC_LEN = 64
SLC_TOPK = 8
WIN_LEN = 512
FORCE_SCORE = 1.0e4
NEG = -1.0e30
GDN_HEADS = 4
GDN_HEAD_DIM = 128
GDN_CONV = 4
GDN_CHUNK = 64
FFN_DIM = 2816
FFN_CONV = 3
DEPTH = 1
DEEPNORM_ALPHA = (2.0 * DEPTH) ** 0.25
LN_EPS = 1e-5
RMS_EPS = 1e-6

NSA_Q_W = NSA_HEADS * NSA_HEAD_DIM
NSA_KV_W = NSA_KV_GROUPS * NSA_HEAD_DIM
GDN_W = GDN_HEADS * GDN_HEAD_DIM

LANES = 128
SUBLANES = 8
VMEM_LIMIT_BYTES = 56 * 1024 * 1024

AUG_SEL0 = 64
AUG_POS_HI = 96
AUG_POS_LO = 97
AUG_PAD = 98
BIG = 2.0 ** 100
POS_SPLIT = 256
Q_TILE = 128
N_SLC = 32
V_ROWS = 80
SEL_KC = 512
NSA_SUB = 2

NT_DIMS = (((1,), (1,)), ((), ()))


def _dot(a, b, **kw):
    return jnp.dot(a, b, preferred_element_type=F32, **kw)


def _dot_nt(a, b, **kw):
    return lax.dot_general(a, b, NT_DIMS, preferred_element_type=F32, **kw)


def _cparams(*sem):
    return pltpu.CompilerParams(dimension_semantics=sem, vmem_limit_bytes=VMEM_LIMIT_BYTES)


def _const_spec(shape):
    nd = len(shape)
    return pl.BlockSpec(shape, lambda *_: (0,) * nd, pipeline_mode=pl.Buffered(1))


_IN_WIDTHS = (NSA_Q_W,) + (NSA_KV_W,) * 6 + (3 * NSA_HEADS, 3 * GDN_W, GDN_HEADS, GDN_HEADS, GDN_W, 2 * D_MODEL)
(_C_Q, _C_CK, _C_CV, _C_SK, _C_SV, _C_WK, _C_WV, _C_GATE, _C_GQKV, _C_BETA, _C_DECAY, _C_GGATE, _C_MERGE,
 IN_WIDTH) = (int(v) for v in np.cumsum((0,) + _IN_WIDTHS))
_C_SMALL = _C_BETA // LANES * LANES
SM_BETA = _C_BETA - _C_SMALL
SM_DECAY = _C_DECAY - _C_SMALL
KEYS_COLS = 2 * NSA_KV_W
CMP_COLS = 2 * NSA_KV_W
_INPROJ_GROUPS = (("keys", KEYS_COLS, BF16), ("cmp", CMP_COLS, BF16), ("ggate", GDN_W, BF16),
                  ("merge", 2 * D_MODEL, BF16))
_INPROJ_WIDTH = sum(w for _, w, _ in _INPROJ_GROUPS)
_GDN_PROJ_WIDTH = 3 * GDN_W + LANES
_INPROJ_T_ROWS = NSA_Q_W + 4 * NSA_HEAD_DIM
_GATE_T_ROWS = 32
INPROJ_TM = 1024
INPROJ_TN = 512
WPREP_TK = 128


def _wprep_body(w_ref, rows_ref, gdn_ref, wt_ref, wg_ref):
    hd = NSA_HEAD_DIM
    feat = lambda c0, n: w_ref[0, c0:c0 + n, :]

    def put_t(ref, col, src):
        for r in range(0, src.shape[0], LANES):
            ref[:, col + r:col + r + LANES] = src[r:r + LANES].T.astype(BF16)

    for g in range(NSA_KV_GROUPS):
        put_t(rows_ref, 2 * g * hd, jnp.concatenate([feat(_C_SK + g * hd, hd), feat(_C_WK + g * hd, hd)], axis=0))
    c = 4 * hd
    for c0, n in ((_C_CK, 2 * NSA_KV_W), (_C_GGATE, GDN_W), (_C_MERGE, 2 * D_MODEL)):
        put_t(rows_ref, c, feat(c0, n))
        c += n
    put_t(gdn_ref, 0, feat(_C_GQKV, 3 * GDN_W))
    put_t(gdn_ref, 3 * GDN_W, feat(_C_SMALL, LANES))
    wt_ref[:NSA_Q_W, :] = feat(_C_Q, NSA_Q_W).astype(BF16)
    for j, c0 in enumerate((_C_SV, _C_WV, _C_SV + hd, _C_WV + hd)):
        wt_ref[NSA_Q_W + j * hd:NSA_Q_W + (j + 1) * hd, :] = feat(c0, hd).astype(BF16)
    wg_ref[...] = feat(_C_GATE, _GATE_T_ROWS).astype(BF16)


def _wprep(w_in_t, layer):
    k = w_in_t.shape[2]
    tk = WPREP_TK
    return pl.pallas_call(
        _wprep_body,
        grid=(k // tk,),
        in_specs=[pl.BlockSpec((1, IN_WIDTH, tk), lambda i: (layer, 0, i))],
        out_specs=[pl.BlockSpec((tk, _INPROJ_WIDTH), lambda i: (i, 0)),
                   pl.BlockSpec((tk, _GDN_PROJ_WIDTH), lambda i: (i, 0)),
                   pl.BlockSpec((_INPROJ_T_ROWS, tk), lambda i: (0, i)),
                   pl.BlockSpec((_GATE_T_ROWS, tk), lambda i: (0, i))],
        out_shape=[jax.ShapeDtypeStruct((k, _INPROJ_WIDTH), BF16), jax.ShapeDtypeStruct((k, _GDN_PROJ_WIDTH), BF16),
                   jax.ShapeDtypeStruct((_INPROJ_T_ROWS, k), BF16), jax.ShapeDtypeStruct((_GATE_T_ROWS, k), BF16)],
        compiler_params=_cparams("parallel"),
        name="wprep",
    )(w_in_t)


def _inproj_body(x_ref, w_ref, wt_ref, wg_ref, keys_ref, cmp_ref, ggate_ref, merge_ref, qvt_ref, gt_ref, cmp_s):
    x = x_ref[...].astype(BF16)
    outs = (keys_ref, None, ggate_ref, merge_ref)
    c0 = 0
    for ref, (name, width, _) in zip(outs, _INPROJ_GROUPS):
        for s in range(0, width, INPROJ_TN):
            e = min(s + INPROJ_TN, width)
            res = _dot(x, w_ref[:, c0 + s:c0 + e])
            if name == "cmp":
                for j in range(width // LANES):
                    cmp_s[j] = res[:, j * LANES:(j + 1) * LANES]
            else:
                ref[:, s:e] = res.astype(ref.dtype)
        c0 += width
    nblk = cmp_ref.shape[0]
    for l in range(CMP_STRIDE):
        for j in range(cmp_s.shape[0]):
            cmp_ref[:, l * CMP_COLS + j * LANES:l * CMP_COLS + (j + 1) * LANES] = (
                cmp_s[j, pl.ds(l, nblk, stride=CMP_STRIDE), :].astype(BF16))
    for s in range(0, _INPROJ_T_ROWS, 2 * LANES):
        qvt_ref[s:s + 2 * LANES, :] = _dot_nt(wt_ref[s:s + 2 * LANES, :], x).astype(qvt_ref.dtype)
    gt_ref[...] = _dot_nt(wg_ref[...], x)


def _inproj(x2, w_rows, w_t, w_g):
    m = x2.shape[0]
    tm = INPROJ_TM
    row_major = [(n, wd, dt) for n, wd, dt in _INPROJ_GROUPS if n != "cmp"]
    specs = {n: (pl.BlockSpec((tm, wd), lambda i: (i, 0)), jax.ShapeDtypeStruct((m, wd), dt))
             for n, wd, dt in row_major}
    specs["cmp"] = (pl.BlockSpec((tm // CMP_STRIDE, CMP_STRIDE * CMP_COLS), lambda i: (i, 0)),
                    jax.ShapeDtypeStruct((m // CMP_STRIDE, CMP_STRIDE * CMP_COLS), BF16))
    order = [n for n, _, _ in _INPROJ_GROUPS]
    return pl.pallas_call(
        _inproj_body,
        grid=(m // tm,),
        in_specs=[pl.BlockSpec((tm, D_MODEL), lambda i: (i, 0)), _const_spec(w_rows.shape),
                  _const_spec(w_t.shape), _const_spec(w_g.shape)],
        out_specs=[specs[n][0] for n in order]
        + [pl.BlockSpec((_INPROJ_T_ROWS, tm), lambda i: (0, i)), pl.BlockSpec((_GATE_T_ROWS, tm), lambda i: (0, i))],
        out_shape=[specs[n][1] for n in order]
        + [jax.ShapeDtypeStruct((_INPROJ_T_ROWS, m), BF16), jax.ShapeDtypeStruct((_GATE_T_ROWS, m), F32)],
        scratch_shapes=[pltpu.VMEM((CMP_COLS // LANES, tm, LANES), F32)],
        compiler_params=_cparams("parallel"),
        name="inproj",
    )(x2, w_rows, w_t, w_g)


def _compress_weights(cmp_pos, cmp_w2):
    hd, half = NSA_HEAD_DIM, CMP_LEN // 2
    posr = cmp_pos.reshape(2, 2, half, hd)
    post = jnp.broadcast_to(posr.transpose(1, 2, 0, 3)[:, :, :, None, :], (2, half, 2, 2, hd))
    post = jnp.concatenate([post.reshape(2, half * 4 * hd),
                            jnp.zeros((SUBLANES - 2, half * 4 * hd), cmp_pos.dtype)], axis=0)
    w2sel = jnp.zeros((2, 2, 2 * hd, LANES), cmp_w2.dtype)
    for g in range(2):
        w2sel = w2sel.at[:, g, g * hd:(g + 1) * hd, :hd].set(cmp_w2)
    return post.astype(BF16), w2sel.reshape(4, 2 * hd, LANES).astype(BF16)


def _compress_body(t_ref, w1_ref, pos_ref, w2_ref, aug_ref, out_ref, w1e_s):
    hd, half_len = NSA_HEAD_DIM, CMP_LEN // 2

    @pl.when(pl.program_id(0) == 0)
    def _():
        w1e_s[...] = jnp.zeros(w1e_s.shape, BF16)
        for which in range(2):
            for half in range(2):
                for l in range(half_len):
                    blk = w1_ref[which, (half * half_len + l) * hd:(half * half_len + l + 1) * hd, :].astype(BF16)
                    for g in range(NSA_KV_GROUPS):
                        r0 = l * CMP_COLS + which * LANES + g * hd
                        c0 = half * CMP_COLS + which * LANES + g * hd
                        w1e_s[r0:r0 + hd, c0:c0 + hd] = blk

    p = _dot(t_ref[0], w1e_s[...])
    pp = _dot(pos_ref[...], w1e_s[...])
    nxt = pltpu.roll(p[:, CMP_COLS:], p.shape[0] - 1, axis=0)
    pre = p[:, :CMP_COLS] + nxt + pp[0:1, :CMP_COLS] + pp[1:2, CMP_COLS:]
    h = jax.nn.gelu(pre).astype(BF16)
    n_idx = lax.broadcasted_iota(jnp.int32, (p.shape[0], LANES), 0)
    real = n_idx < p.shape[0] - 1
    for which in range(2):
        hw = h[:, which * LANES:(which + 1) * LANES]
        for g in range(2):
            o = jnp.where(real, _dot(hw, w2_ref[which * 2 + g]) + aug_ref[which], 0.0)
            out_ref[0, which * 2 + g] = (o if which == 0 else o.T).astype(out_ref.dtype)


def _compress(t2, w1, post, w2sel, aug):
    b, nblk, _ = t2.shape
    return pl.pallas_call(
        _compress_body,
        grid=(b,),
        in_specs=[pl.BlockSpec((1, nblk, CMP_STRIDE * CMP_COLS), lambda i: (i, 0, 0)),
                  _const_spec(w1.shape), _const_spec(post.shape), _const_spec(w2sel.shape), _const_spec(aug.shape)],
        out_specs=pl.BlockSpec((1, 4, nblk, LANES), lambda i: (i, 0, 0, 0)),
        out_shape=jax.ShapeDtypeStruct((b, 4, nblk, LANES), BF16),
        scratch_shapes=[pltpu.VMEM((CMP_STRIDE * CMP_COLS, 2 * CMP_COLS), BF16)],
        compiler_params=_cparams("arbitrary"),
        name="compress",
    )(t2, w1, post, w2sel, aug)


def _nsa_consts(s):
    hd, rep = NSA_HEAD_DIM, NSA_REP
    t = np.arange(s)
    kx_win = np.zeros((s + WIN_LEN, hd), np.float32)
    kx_win[WIN_LEN + t, AUG_POS_HI - hd] = t // POS_SPLIT
    kx_win[WIN_LEN + t, AUG_POS_LO - hd] = t % POS_SPLIT
    kx_win[:WIN_LEN, AUG_PAD - hd] = 1.0
    kx_sel = kx_win.copy()
    kx_sel[WIN_LEN + t, t // SLC_LEN] = 1.0
    vx_win = np.zeros((V_ROWS - hd, s + WIN_LEN), np.float32)
    vx_win[0, WIN_LEN:] = 1.0
    vx_sel = vx_win
    n_cmp = s // CMP_STRIDE
    cmp_aug = np.zeros((2, n_cmp, LANES), np.float32)
    end = np.arange(n_cmp) * CMP_STRIDE + CMP_LEN - 1
    cmp_aug[0, :, AUG_POS_HI] = end // POS_SPLIT
    cmp_aug[0, :, AUG_POS_LO] = end % POS_SPLIT
    qx = np.zeros((NSA_KV_GROUPS, LANES - AUG_POS_HI, rep * Q_TILE), np.float32)
    for h in range(NSA_HEADS):
        slope = 2.0 ** (-8.0 * (h + 1) / NSA_HEADS)
        lanes = slice((h % rep) * Q_TILE, (h % rep + 1) * Q_TILE)
        qx[h // rep, 0, lanes] = slope * POS_SPLIT
        qx[h // rep, 1, lanes] = slope
        qx[h // rep, AUG_PAD - AUG_POS_HI, lanes] = -BIG
    c0 = np.arange(n_cmp)[None, :] * CMP_STRIDE
    s0 = np.arange(s // SLC_LEN)[:, None] * SLC_LEN
    ov_t = ((c0 < s0 + SLC_LEN) & (c0 + CMP_LEN > s0)).astype(np.float32)
    ov_t[:, (s - CMP_LEN) // CMP_STRIDE + 1:] = 0.0
    kk = np.arange(Q_TILE)[:, None]
    qq = np.arange(Q_TILE)[None, :]
    causal = np.tile(np.where(kk <= qq, 0.0, NEG).astype(np.float32), (1, rep))
    after = np.tile(np.where(kk > qq, 0.0, NEG).astype(np.float32), (1, rep))
    j = jnp.asarray
    return dict(kx_sel=j(kx_sel, BF16), kx_win=j(kx_win, BF16), vx_sel=j(vx_sel, BF16), vx_win=j(vx_win, BF16),
                cmp_aug=j(cmp_aug), qx=j(qx), ov_t=j(ov_t), causal=j(causal), after=j(after))


def _nsa_body(qt_ref, k_ref, vt_ref, kc_ref, vct_ref, gt_ref, kxs_ref, kxw_ref, vxs_ref, vxw_ref, qx_ref, ovt_ref,
              causal_ref, after_ref, out_ref, ks_s, kw_s, vs_s, vw_s):
    hd, rep, tq = NSA_HEAD_DIM, NSA_REP, Q_TILE
    nq = rep * tq
    i = pl.program_id(2)

    @pl.when(i == 0)
    def _():
        keys = k_ref[0]
        ks_s[:WIN_LEN, :hd] = jnp.zeros((WIN_LEN, hd), BF16)
        ks_s[WIN_LEN:, :hd] = keys[:, :hd]
        ks_s[:, hd:] = kxs_ref[...]
        kw_s[:WIN_LEN, :hd] = jnp.zeros((WIN_LEN, hd), BF16)
        kw_s[WIN_LEN:, :hd] = keys[:, hd:]
        kw_s[:, hd:] = kxw_ref[...]
        vals = vt_ref[...]
        vs_s[:hd, :WIN_LEN] = jnp.zeros((hd, WIN_LEN), BF16)
        vs_s[:hd, WIN_LEN:] = vals[:hd]
        vs_s[hd:, :] = vxs_ref[...]
        vw_s[:hd, :WIN_LEN] = jnp.zeros((hd, WIN_LEN), BF16)
        vw_s[:hd, WIN_LEN:] = vals[hd:]
        vw_s[hd:, :] = vxw_ref[...]

    qx = qx_ref[0]
    sg_all = jax.nn.sigmoid(gt_ref[...])
    grp = pl.program_id(1)

    full_past = WIN_LEN // tq

    def front(sub, res, it, past):
        qt = qt_ref[:, sub * tq:(sub + 1) * tq]
        q64 = jnp.concatenate([qt[r * hd:(r + 1) * hd, :] for r in range(rep)], axis=1).astype(F32) * (hd ** -0.5)

        def q_aug(sel_rows):
            return jnp.concatenate([q64, sel_rows, qx], axis=0).astype(BF16)

        n_row = lax.broadcasted_iota(jnp.int32, (LANES, nq), 0)
        t_lane = it * tq + (lax.broadcasted_iota(jnp.int32, (LANES, nq), 1) & (tq - 1))
        valid = t_lane >= n_row * CMP_STRIDE + (CMP_LEN - 1)
        qa0 = q_aug(jnp.zeros((N_SLC, nq), F32))
        sc = jnp.where(valid, _dot(kc_ref[0, 0], qa0), NEG)
        mc = jnp.max(sc, axis=0, keepdims=True)
        ec = jnp.where(valid, jnp.exp(sc - mc), 0.0)
        lc = jnp.sum(ec, axis=0, keepdims=True)
        pc = ec * jnp.where(lc > 0.0, 1.0 / lc, 0.0)
        o_cmp = _dot(vct_ref[0, 0], pc.astype(BF16))[:hd]
        psum = pc[:, 0:tq] + pc[:, tq:2 * tq] + pc[:, 2 * tq:3 * tq] + pc[:, 3 * tq:4 * tq]
        score_t = _dot(ovt_ref[...], psum, precision=lax.Precision.HIGHEST)
        yield

        n_keys = (past + 1) * tq
        if past == full_past:
            w0 = pl.multiple_of(it * tq, tq)
        else:
            w0 = WIN_LEN
        diag = past * tq

        def one_shot_masks(s):
            parts = [s[:diag], s[diag:] + causal_ref[...]] if past else [s + causal_ref[...]]
            if past == full_past:
                parts = [s[:tq] + after_ref[...], s[tq:diag], parts[1]]
            return jnp.concatenate(parts, axis=0) if len(parts) > 1 else parts[0]

        s_w = one_shot_masks(_dot(kw_s[pl.ds(w0, n_keys), :], qa0))
        p_w = jnp.exp(s_w - jnp.max(s_w, axis=0, keepdims=True))
        acc_w = _dot(vw_s[:, pl.ds(w0, n_keys)], p_w.astype(BF16))
        o_win = acc_w[:hd] * (1.0 / acc_w[hd:hd + 1])
        yield

        jb = lax.broadcasted_iota(jnp.int32, (N_SLC, tq), 0)
        cur = (it * tq + lax.broadcasted_iota(jnp.int32, (N_SLC, tq), 1)) // SLC_LEN
        forced = (jb == 0) | (jb == cur) | (jb == cur - 1)
        score_t = jnp.where(forced, FORCE_SCORE, jnp.where(jb <= cur, score_t, -1.0))
        rank = jnp.zeros((N_SLC, tq), F32)
        for jp in range(N_SLC):
            other = score_t[jp:jp + 1, :]
            ge = jnp.where(other >= score_t, 1.0, 0.0)
            gt = jnp.where(other > score_t, 1.0, 0.0)
            rank = rank + jnp.where(jb > jp, ge, gt)
        sel = rank < float(SLC_TOPK)
        qa = q_aug(jnp.concatenate([jnp.where(sel, 0.0, -BIG)] * rep, axis=1))
        lo_blk = jnp.min(jnp.where(sel & (jb >= 2) & (jb <= cur), jb.astype(F32), float(N_SLC)))
        lo_key = (lo_blk.astype(jnp.int32) // 2) * tq
        yield

        s_main = _dot(ks_s[pl.ds(w0, n_keys), :], qa)
        if past < full_past:
            s_s = jnp.concatenate([s_main[:diag], s_main[diag:] + causal_ref[...]], axis=0) if past else (
                s_main + causal_ref[...])
            m_s = jnp.max(s_s, axis=0, keepdims=True)
            acc_s = _dot(vs_s[:, pl.ds(w0, n_keys)], jnp.exp(s_s - m_s).astype(BF16))
            res.update(early=False, acc_s=acc_s, o_cmp=o_cmp, o_win=o_win)
            return
        e_key = it * tq - WIN_LEN
        t0 = pl.multiple_of(jnp.where(e_key > 0, WIN_LEN, 0), tq)
        s_s = jnp.concatenate([_dot(ks_s[pl.ds(t0, tq), :], qa), s_main[:diag],
                               s_main[diag:] + causal_ref[...]], axis=0)
        m_s = jnp.max(s_s, axis=0, keepdims=True)
        p_s = jnp.exp(s_s - m_s).astype(BF16)
        acc_s = _dot(vs_s[:, pl.ds(t0, tq)], p_s[:tq]) + _dot(vs_s[:, pl.ds(w0, n_keys)], p_s[tq:])
        c_hi = (e_key - tq + SEL_KC - 1) // SEL_KC
        c_lo = jnp.where(lo_key < e_key, (lo_key - tq) // SEL_KC, c_hi)
        res.update(early=True, qa=qa, e_key=e_key, c_lo=c_lo, c_hi=c_hi, m_s=m_s, acc_s=acc_s, o_cmp=o_cmp,
                   o_win=o_win)

    def tail(sub, f):
        def early_step(c, carry):
            qa, e_key = f["qa"], f["e_key"]
            m, acc = carry
            k0 = tq + c * SEL_KC
            start = pl.multiple_of(WIN_LEN + k0, tq)
            k_abs = k0 + lax.broadcasted_iota(jnp.int32, (SEL_KC, nq), 0)
            s = jnp.where(k_abs < e_key, _dot(ks_s[pl.ds(start, SEL_KC), :], qa), NEG)
            m_new = jnp.maximum(m, jnp.max(s, axis=0, keepdims=True))
            p = jnp.exp(s - m_new).astype(BF16)
            return m_new, acc * jnp.exp(m - m_new) + _dot(vs_s[:, pl.ds(start, SEL_KC)], p)

        acc_s = f["acc_s"]
        if f["early"]:
            _, acc_s = lax.fori_loop(f["c_lo"], f["c_hi"], early_step, (f["m_s"], acc_s))
        o_slc = acc_s[:hd] * (1.0 / acc_s[hd:hd + 1])

        sg = sg_all[:, sub * tq:(sub + 1) * tq]
        gate = lambda br, r: jnp.where(grp == 0, sg[br * NSA_HEADS + r:br * NSA_HEADS + r + 1],
                                       sg[br * NSA_HEADS + rep + r:br * NSA_HEADS + rep + r + 1])
        for pair in range(rep // 2):
            halves = []
            for r in (2 * pair, 2 * pair + 1):
                lanes = slice(r * tq, (r + 1) * tq)
                halves.append(gate(0, r) * f["o_cmp"][:, lanes] + gate(1, r) * o_slc[:, lanes]
                              + gate(2, r) * f["o_win"][:, lanes])
            out_ref[0, sub * tq:(sub + 1) * tq, pair * LANES:(pair + 1) * LANES] = (
                jnp.concatenate(halves, axis=0).T.astype(out_ref.dtype))

    def run(tiles):
        fronts = [{} for _ in range(NSA_SUB)]
        for sub, (it, past) in enumerate(tiles):
            for _ in front(sub, fronts[sub], it, past):
                pass
        for sub in range(NSA_SUB):
            tail(sub, fronts[sub])

    n_short = -(-full_past // NSA_SUB)
    for step in range(n_short):
        @pl.when(i == step)
        def _(step=step):
            run([(step * NSA_SUB + sub, min(step * NSA_SUB + sub, full_past)) for sub in range(NSA_SUB)])

    @pl.when(i >= n_short)
    def _():
        run([(i * NSA_SUB + sub, full_past) for sub in range(NSA_SUB)])


def _nsa(qvt, keys, cmp_kv, gt, consts, b, s):
    tqs = NSA_SUB * Q_TILE
    nt = s // tqs
    c = consts
    in_specs = [
        pl.BlockSpec((2 * LANES, tqs), lambda bi, g, i: (g, bi * nt + i)),
        pl.BlockSpec((1, s, LANES), lambda bi, g, i: (bi, 0, g)),
        pl.BlockSpec((LANES, s), lambda bi, g, i: (NSA_Q_W // LANES + g, bi)),
        pl.BlockSpec((1, 1, s // CMP_STRIDE, LANES), lambda bi, g, i: (bi, g, 0, 0)),
        pl.BlockSpec((1, 1, s // CMP_STRIDE, LANES), lambda bi, g, i: (bi, 2 + g, 0, 0)),
        pl.BlockSpec((_GATE_T_ROWS, tqs), lambda bi, g, i: (0, bi * nt + i)),
        _const_spec(c["kx_sel"].shape), _const_spec(c["kx_win"].shape), _const_spec(c["vx_sel"].shape),
        _const_spec(c["vx_win"].shape),
        pl.BlockSpec((1,) + c["qx"].shape[1:], lambda bi, g, i: (g, 0, 0)),
        _const_spec(c["ov_t"].shape), _const_spec(c["causal"].shape), _const_spec(c["after"].shape),
    ]
    return pl.pallas_call(
        _nsa_body,
        grid=(b, NSA_KV_GROUPS, nt),
        in_specs=in_specs,
        out_specs=pl.BlockSpec((1, tqs, 2 * LANES), lambda bi, g, i: (bi, i, g)),
        out_shape=jax.ShapeDtypeStruct((b, s, NSA_Q_W), BF16),
        scratch_shapes=[pltpu.VMEM((s + WIN_LEN, LANES), BF16), pltpu.VMEM((s + WIN_LEN, LANES), BF16),
                        pltpu.VMEM((V_ROWS, s + WIN_LEN), BF16), pltpu.VMEM((V_ROWS, s + WIN_LEN), BF16)],
        compiler_params=_cparams("parallel", "parallel", "arbitrary"),
        name="nsa",
    )(qvt, keys, qvt, cmp_kv, cmp_kv, gt, c["kx_sel"], c["kx_win"], c["vx_sel"], c["vx_win"], c["qx"], c["ov_t"],
      c["causal"], c["after"])


GDN_TS = 512
GDN_BLK = 128
GDN_HALO = 8
GDN_XHALO = 16
GDN_INV_BASE = 8
GDN_CHAIN_UNITS_PER_PREP_UNIT = 4
GDN_SCAN_ROWS = GDN_HEAD_DIM + GDN_CHUNK


def _gdn_prep(x_ref, prev_ref, w_ref, cw_ref, alog_ref, dtb_ref, lt_ref, first_tile, xp_s, buf):
    x_s, p_s, rhs_s, qg_s, aqk_s, kdt_s, a_s, eg_s = buf
    ts, dh, nh, blk = GDN_TS, GDN_HEAD_DIM, GDN_HEADS, GDN_BLK
    proj = _dot(x_ref[0].astype(BF16), w_ref[...])
    hist = _dot(prev_ref[0].astype(BF16), w_ref[:, :3 * GDN_W])[GDN_XHALO - GDN_HALO:]
    xp_s[0:GDN_HALO, :] = jnp.where(first_tile, 0.0, hist)
    xp_s[GDN_HALO:, :] = proj[:, :3 * GDN_W]
    yield
    act = []
    for blk_i in range(3 * nh):
        lanes = slice(blk_i * dh, (blk_i + 1) * dh)
        xp = xp_s[:, lanes]
        conv = cw_ref[0:1, lanes] * xp
        for j in range(1, GDN_CONV):
            conv = pltpu.roll(conv, 1, axis=0) + cw_ref[j:j + 1, lanes] * xp
        act.append(jax.nn.silu(conv[GDN_HALO:]))
        yield

    sm = proj[:, 3 * GDN_W:]
    beta = jax.nn.sigmoid(sm)
    g = -jnp.exp(alog_ref[...]) * jax.nn.softplus(sm + dtb_ref[...])
    gcum = jnp.concatenate([_dot(lt_ref[...], g[r:r + blk], precision=lax.Precision.HIGHEST)
                            for r in range(0, ts, blk)], axis=0)
    eg = jnp.exp(gcum)
    eg_s[...] = eg

    ri = lax.broadcasted_iota(jnp.int32, (blk, blk), 0)
    ci = lax.broadcasted_iota(jnp.int32, (blk, blk), 1)
    same = (ri // GDN_CHUNK) == (ci // GDN_CHUNK)
    causal = same & (ri >= ci)
    strict = same & (ri > ci)
    eye = (ri == ci).astype(F32)
    same_base = (ri // GDN_INV_BASE) == (ci // GDN_INV_BASE)

    for pb in range(ts // blk):
        rows = slice(pb * blk, (pb + 1) * blk)
        gc = gcum[rows]
        gc_t = gc.T
        first = lax.broadcasted_iota(jnp.int32, (blk, LANES), 0) < GDN_CHUNK
        g_last = jnp.where(first, gc[GDN_CHUNK - 1:GDN_CHUNK, :], gc[blk - 1:blk, :])
        e_dec = jnp.exp(g_last - gc)
        for h in range(nh):
            q, k, v = act[h][rows], act[nh + h][rows], act[2 * nh + h][rows]
            q = q * lax.rsqrt(jnp.sum(q * q, axis=1, keepdims=True) + RMS_EPS) * (dh ** -0.5)
            k = k * lax.rsqrt(jnp.sum(k * k, axis=1, keepdims=True) + RMS_EPS)
            b_col = beta[rows, SM_BETA + h:SM_BETA + h + 1]
            eg_col = eg[rows, SM_DECAY + h:SM_DECAY + h + 1]
            gdiff = gc[:, SM_DECAY + h:SM_DECAY + h + 1] - gc_t[SM_DECAY + h:SM_DECAY + h + 1, :]
            decay = jnp.exp(jnp.where(causal, gdiff, NEG))
            kb = k * b_col
            kbf, kf, qf = kb.astype(BF16), k.astype(BF16), q.astype(BF16)
            a = jnp.where(strict, -_dot_nt(kbf, kf) * decay, 0.0)
            c = pb * nh + h
            a_base = jnp.where(same_base, a, 0.0)
            a_s[c] = a.astype(BF16)
            x_s[c] = a_base.astype(BF16)
            p_s[c] = eye + a_base
            rhs_s[c] = jnp.concatenate([v * b_col, kb * eg_col], axis=1).astype(BF16)
            qg_s[c] = q * eg_col
            aqk_s[c] = jnp.where(causal, _dot_nt(qf, kf) * decay, 0.0).astype(BF16)
            kdt_s[c] = (k * e_dec[:, SM_DECAY + h:SM_DECAY + h + 1]).T.astype(BF16)
            yield


def _gdn_chains(buf, nq_ref, co_ref, eg_ref):
    x_s, p_s, rhs_s, qg_s, aqk_s, kdt_s, a_s, eg_s = buf
    dh, nh, blk = GDN_HEAD_DIM, GDN_HEADS, GDN_BLK
    n_chain = x_s.shape[0]
    eg_ref[0] = eg_s[...]
    ri = lax.broadcasted_iota(jnp.int32, (blk, blk), 0)
    ci = lax.broadcasted_iota(jnp.int32, (blk, blk), 1)
    same = lambda size: (ri // size) == (ci // size)
    for c in range(n_chain):
        y = x_s[c]
        x_s[c] = _dot(y, y).astype(BF16)
        yield
    for c in range(n_chain):
        y2 = x_s[c]
        p = p_s[c]
        p_s[c] = p + _dot(p.astype(BF16), y2)
        x_s[c] = _dot(y2, y2).astype(BF16)
        yield
    for c in range(n_chain):
        p = p_s[c]
        p_s[c] = p + _dot(p.astype(BF16), x_s[c])
        yield
    size = 2 * GDN_INV_BASE
    while size <= GDN_CHUNK:
        between = same(size) & jnp.logical_not(same(size // 2))
        for c in range(n_chain):
            a = a_s[c]
            x_s[c] = _dot(p_s[c].astype(BF16), jnp.where(between, a, jnp.zeros_like(a))).astype(BF16)
            yield
        for c in range(n_chain):
            t = p_s[c]
            p_s[c] = t + _dot(x_s[c], t.astype(BF16))
            yield
        size *= 2

    tok_half = lax.broadcasted_iota(jnp.int32, (blk, blk), 1) // GDN_CHUNK
    for c in range(n_chain):
        rhs_s[c] = _dot(p_s[c].astype(BF16), rhs_s[c]).astype(BF16)
        yield
    for c in range(n_chain):
        pb, h = divmod(c, nh)
        uw = rhs_s[c]
        a1 = _dot(aqk_s[c], uw)
        q_loc = qg_s[c] - a1[:, dh:]
        kdt = kdt_s[c]
        for half in range(blk // GDN_CHUNK):
            k1 = _dot(jnp.where(tok_half == half, kdt, jnp.zeros_like(kdt)), uw)
            n = pb * (blk // GDN_CHUNK) + half
            rows = slice(half * GDN_CHUNK, (half + 1) * GDN_CHUNK)
            nq_ref[0, h, n, :dh, :] = (-k1[:, dh:]).astype(BF16)
            nq_ref[0, h, n, dh:, :] = q_loc[rows].astype(BF16)
            co_ref[0, h, n, :dh, :] = k1[:, :dh].astype(BF16)
            co_ref[0, h, n, dh:, :] = a1[rows, :dh].astype(BF16)
        yield


def _interleave(major, minor, minor_per_major):
    for _ in major:
        for _ in range(minor_per_major):
            next(minor, None)
    for _ in minor:
        pass


def _gdn_intra_body(x_ref, prev_ref, w_ref, cw_ref, alog_ref, dtb_ref, lt_ref, nq_ref, co_ref, eg_ref, xp_s, *bufs,
                    tiles_per_seq, n_buf):
    j = pl.program_id(0)
    sets = (bufs[:n_buf], bufs[n_buf:])

    @pl.when(j == 0)
    def _():
        for ref in sets[1]:
            ref[...] = jnp.zeros(ref.shape, ref.dtype)

    first_tile = (j % tiles_per_seq) == 0
    for parity in range(2):
        @pl.when(j % 2 == parity)
        def _(parity=parity):
            chains = _gdn_chains(sets[1 - parity], nq_ref, co_ref, eg_ref)
            prep = _gdn_prep(x_ref, prev_ref, w_ref, cw_ref, alog_ref, dtb_ref, lt_ref, first_tile, xp_s,
                             sets[parity])
            _interleave(prep, chains, GDN_CHAIN_UNITS_PER_PREP_UNIT)


def _gdn_intra(x, w_gdn, conv_w, alog_l, dtb_l, lt):
    b, s, d = x.shape
    ts, nh, dh = GDN_TS, GDN_HEADS, GDN_HEAD_DIM
    tps = s // ts
    nt = b * tps
    n_chain = (ts // GDN_BLK) * nh
    sq = lambda dt: pltpu.VMEM((n_chain, GDN_BLK, GDN_BLK), dt)
    buf = lambda: [sq(BF16), sq(F32), pltpu.VMEM((n_chain, GDN_BLK, 2 * dh), BF16), sq(F32), sq(BF16), sq(BF16),
                   sq(BF16), pltpu.VMEM((ts, LANES), F32)]
    src = lambda j: jnp.minimum(j, nt - 1)
    dst = lambda j: jnp.maximum(j - 1, 0)
    cspec = lambda: pl.BlockSpec((1, nh, ts // GDN_CHUNK, GDN_SCAN_ROWS, dh),
                                 lambda j: (dst(j) // tps, 0, dst(j) % tps, 0, 0))
    cshape = jax.ShapeDtypeStruct((b, nh, s // GDN_CHUNK, GDN_SCAN_ROWS, dh), BF16)
    return pl.pallas_call(
        functools.partial(_gdn_intra_body, tiles_per_seq=tps, n_buf=len(buf())),
        grid=(nt + 1,),
        in_specs=[
            pl.BlockSpec((1, ts, d), lambda j: (src(j) // tps, src(j) % tps, 0)),
            pl.BlockSpec((1, GDN_XHALO, d),
                         lambda j: (src(j) // tps, jnp.maximum((src(j) % tps) * (ts // GDN_XHALO) - 1, 0), 0)),
            _const_spec(w_gdn.shape), _const_spec(conv_w.shape), _const_spec(alog_l.shape), _const_spec(dtb_l.shape),
            _const_spec(lt.shape),
        ],
        out_specs=[cspec(), cspec(), pl.BlockSpec((1, ts, LANES), lambda j: (dst(j) // tps, dst(j) % tps, 0))],
        out_shape=[cshape, cshape, jax.ShapeDtypeStruct((b, s, LANES), F32)],
        scratch_shapes=[pltpu.VMEM((ts + GDN_HALO, 3 * GDN_W), F32)] + buf() + buf(),
        compiler_params=_cparams("arbitrary"),
        name="gdn_intra",
    )(x, x, w_gdn, conv_w, alog_l, dtb_l, lt)


def _gdn_scan_body(nq_ref, co_ref, eg_ref, gate_ref, nw_ref, out_ref, st_s):
    nh, dh, ck = GDN_HEADS, GDN_HEAD_DIM, GDN_CHUNK
    st_s[...] = jnp.zeros(st_s.shape, F32)

    def chunk(n, carry):
        r0 = pl.multiple_of(n * ck, ck)
        d_row = eg_ref[0, pl.ds(r0 + ck - 1, 1), :]
        for h in range(nh):
            st = st_s[h]
            res = _dot(nq_ref[0, h, n], st.astype(BF16)) + co_ref[0, h, n].astype(F32)
            st_s[h] = st * d_row[:, SM_DECAY + h:SM_DECAY + h + 1] + res[:dh]
            o = res[dh:]
            ms = jnp.mean(o * o, axis=1, keepdims=True)
            gt = gate_ref[0, pl.ds(r0, ck), h * dh:(h + 1) * dh].astype(F32)
            out_ref[0, pl.ds(r0, ck), h * dh:(h + 1) * dh] = (
                o * lax.rsqrt(ms + RMS_EPS) * nw_ref[...] * jax.nn.silu(gt)).astype(out_ref.dtype)
        return carry

    lax.fori_loop(0, nq_ref.shape[2], chunk, 0, unroll=8)


def _gdn_scan(nq, co, eg, ggate, norm_w):
    b, nh, nc, rows, dh = nq.shape
    s = nc * GDN_CHUNK
    cspec = lambda: pl.BlockSpec((1, nh, nc, rows, dh), lambda bi: (bi, 0, 0, 0, 0))
    return pl.pallas_call(
        _gdn_scan_body,
        grid=(b,),
        in_specs=[cspec(), cspec(),
                  pl.BlockSpec((1, s, LANES), lambda bi: (bi, 0, 0)),
                  pl.BlockSpec((1, s, GDN_W), lambda bi: (bi, 0, 0)),
                  _const_spec(norm_w.shape)],
        out_specs=pl.BlockSpec((1, s, GDN_W), lambda bi: (bi, 0, 0)),
        out_shape=jax.ShapeDtypeStruct((b, s, GDN_W), BF16),
        scratch_shapes=[pltpu.VMEM((nh, dh, dh), F32)],
        compiler_params=_cparams("parallel"),
        name="gdn_scan",
    )(nq, co, eg, ggate, norm_w)


MERGE_TM = 1024


def _layer_norm(y, g, b):
    mu = jnp.mean(y, axis=1, keepdims=True)
    d = y - mu
    var = jnp.mean(d * d, axis=1, keepdims=True)
    return d * lax.rsqrt(var + LN_EPS) * g + b


def _merge_body(x_ref, oa_ref, ob_ref, ga_ref, gb_ref, wa_ref, wb_ref, wo_ref, g_ref, b_ref, y_ref, yb_ref):
    ya = _dot(oa_ref[...], wa_ref[...])
    yb = _dot(ob_ref[...], wb_ref[...])
    mixin = jax.nn.sigmoid(ga_ref[...].astype(F32)) * ya + jax.nn.sigmoid(gb_ref[...].astype(F32)) * yb
    mix = _dot(mixin.astype(BF16), wo_ref[...])
    y = _layer_norm(DEEPNORM_ALPHA * x_ref[...] + mix, g_ref[...], b_ref[...])
    y_ref[...] = y
    yb_ref[...] = y.astype(BF16)


def _merge(x2, oa, ob, mgate, wa, wb, wo, g, b):
    m = x2.shape[0]
    tm, d = MERGE_TM, D_MODEL
    row = lambda wd, col=0: pl.BlockSpec((tm, wd), lambda i, col=col: (i, col))
    return pl.pallas_call(
        _merge_body,
        grid=(m // tm,),
        in_specs=[row(d), row(NSA_Q_W), row(GDN_W), row(d, 0), row(d, 1),
                  _const_spec(wa.shape), _const_spec(wb.shape), _const_spec(wo.shape),
                  _const_spec(g.shape), _const_spec(b.shape)],
        out_specs=[row(d), row(d)],
        out_shape=[jax.ShapeDtypeStruct((m, d), F32), jax.ShapeDtypeStruct((m, d), BF16)],
        compiler_params=_cparams("parallel"),
        name="merge",
    )(x2, oa, ob, mgate, mgate, wa, wb, wo, g, b)


FFN_TM = 1024
FFN_HALO = 16
FFN_CK = 256


def _ffn_body(x_ref, xb_ref, prev_ref, wu_ref, cw_ref, wd_ref, g_ref, b_ref, out_ref, act_s,
              *, tiles_per_seq):
    i = pl.program_id(0)
    prev = prev_ref[...]
    prev = jnp.where(i % tiles_per_seq == 0, jnp.zeros_like(prev), prev)
    xc = jnp.concatenate([prev, xb_ref[...]], axis=0)

    def conv(h, c0):
        out = cw_ref[FFN_CONV - 1:FFN_CONV, c0:c0 + FFN_CK] * h[FFN_HALO:]
        for j in range(FFN_CONV - 1):
            shifted = pltpu.roll(h, FFN_CONV - 1 - j, axis=0)[FFN_HALO:]
            out = out + cw_ref[j:j + 1, c0:c0 + FFN_CK] * shifted
        return out

    for c in range(FFN_DIM // FFN_CK):
        c0 = c * FFN_CK
        hg = conv(_dot(xc, wu_ref[:, c0:c0 + FFN_CK]), c0)
        hv = conv(_dot(xc, wu_ref[:, FFN_DIM + c0:FFN_DIM + c0 + FFN_CK]), FFN_DIM + c0)
        act_s[:, c0:c0 + FFN_CK] = (jax.nn.silu(hg) * hv).astype(BF16)
    f = _dot(act_s[...], wd_ref[...])
    out_ref[...] = _layer_norm(DEEPNORM_ALPHA * x_ref[...] + f, g_ref[...], b_ref[...])


def _ffn(x1, x1b, wu, cw, wd, g, b, seq):
    m = x1.shape[0]
    tm, d = FFN_TM, D_MODEL
    return pl.pallas_call(
        functools.partial(_ffn_body, tiles_per_seq=seq // tm),
        grid=(m // tm,),
        in_specs=[pl.BlockSpec((tm, d), lambda i: (i, 0)),
                  pl.BlockSpec((tm, d), lambda i: (i, 0)),
                  pl.BlockSpec((FFN_HALO, d), lambda i: (jnp.maximum(i * (tm // FFN_HALO) - 1, 0), 0)),
                  _const_spec(wu.shape), _const_spec(cw.shape), _const_spec(wd.shape), _const_spec(g.shape), _const_spec(b.shape)],
        out_specs=pl.BlockSpec((tm, d), lambda i: (i, 0)),
        out_shape=jax.ShapeDtypeStruct((m, d), F32),
        scratch_shapes=[pltpu.VMEM((tm, FFN_DIM), BF16)],
        compiler_params=_cparams("parallel"),
        name="ffn",
    )(x1, x1b, x1b, wu, cw, wd, g, b)


def _lane_vec(vals, lane0):
    return jnp.zeros((1, LANES), F32).at[0, lane0:lane0 + vals.shape[0]].set(vals.astype(F32))


def _layer(x, w_in, cmp_pos, cmp_w1, cmp_w2, w_nsa_out, gdn_conv_w, gdn_a_log, gdn_dt_bias, gdn_norm_w,
           w_gdn_out, w_o, ln1_g, ln1_b, ffn_w_up, ffn_conv_w, ffn_w_down, ln2_g, ln2_b):
    b, s, d = x.shape
    m = b * s
    x2 = x.reshape(m, d)
    w_rows, w_gdn, w_t, w_g = _wprep(*w_in)
    keys, cmpkv, ggate, mgate, qvt, gt = _inproj(x2, w_rows, w_t, w_g)

    consts = _nsa_consts(s)
    post, w2sel = _compress_weights(cmp_pos, cmp_w2)
    cmp_kv = _compress(cmpkv.reshape(b, s // CMP_STRIDE, CMP_STRIDE * CMP_COLS), cmp_w1, post, w2sel,
                       consts["cmp_aug"])
    o_nsa = _nsa(qvt, keys.reshape(b, s, KEYS_COLS), cmp_kv, gt, consts, b, s)

    ck = GDN_CHUNK
    tri = np.tril(np.ones((ck, ck), np.float32))
    lt = jnp.asarray(np.kron(np.eye(GDN_BLK // ck, dtype=np.float32), tri))
    nq, co, eg = _gdn_intra(x, w_gdn, gdn_conv_w, _lane_vec(gdn_a_log, SM_DECAY), _lane_vec(gdn_dt_bias, SM_DECAY),
                            lt)
    o_gdn = _gdn_scan(nq, co, eg, ggate.reshape(b, s, GDN_W), gdn_norm_w.reshape(1, GDN_HEAD_DIM))

    x1, x1b = _merge(x2, o_nsa.reshape(m, NSA_Q_W), o_gdn.reshape(m, GDN_W), mgate,
                     w_nsa_out.astype(BF16), w_gdn_out.astype(BF16), w_o.astype(BF16),
                     ln1_g.reshape(1, d), ln1_b.reshape(1, d))
    out = _ffn(x1, x1b, ffn_w_up.astype(BF16), ffn_conv_w, ffn_w_down.astype(BF16),
               ln2_g.reshape(1, d), ln2_b.reshape(1, d), s)
    return out.reshape(b, s, d)


def kernel(x, w_in, nsa_cmp_pos, nsa_cmp_w1, nsa_cmp_w2, w_nsa_out, gdn_conv_w, gdn_a_log, gdn_dt_bias, gdn_norm_w, w_gdn_out, w_o, ln1_g, ln1_b, ffn_w_up, ffn_conv_w, ffn_w_down, ln2_g, ln2_b):
    w_in_t = jnp.swapaxes(w_in, 1, 2)
    for l in range(DEPTH):
        x = _layer(x, (w_in_t, l), nsa_cmp_pos[l], nsa_cmp_w1[l], nsa_cmp_w2[l], w_nsa_out[l], gdn_conv_w[l],
                   gdn_a_log[l], gdn_dt_bias[l], gdn_norm_w[l], w_gdn_out[l], w_o[l], ln1_g[l], ln1_b[l],
                   ffn_w_up[l], ffn_conv_w[l], ffn_w_down[l], ln2_g[l], ln2_b[l])
    return x
```

```python
import functools

import numpy as np
import jax
import jax.numpy as jnp
from jax import lax
from jax.experimental import pallas as pl
from jax.experimental.pallas import tpu as pltpu

F32 = jnp.float32
BF16 = jnp.bfloat16

D_MODEL = 1024
NSA_HEADS = 8
NSA_KV_GROUPS = 2
NSA_REP = NSA_HEADS // NSA_KV_GROUPS
NSA_HEAD_DIM = 64
CMP_LEN = 32
CMP_STRIDE = 16
SLC_LEN = 64
SLC_TOPK = 8
WIN_LEN = 512
FORCE_SCORE = 1.0e4
NEG = -1.0e30
GDN_HEADS = 4
GDN_HEAD_DIM = 128
GDN_CONV = 4
GDN_CHUNK = 64
FFN_DIM = 2816
FFN_CONV = 3
DEPTH = 1
DEEPNORM_ALPHA = (2.0 * DEPTH) ** 0.25
LN_EPS = 1e-5
RMS_EPS = 1e-6

NSA_Q_W = NSA_HEADS * NSA_HEAD_DIM
NSA_KV_W = NSA_KV_GROUPS * NSA_HEAD_DIM
GDN_W = GDN_HEADS * GDN_HEAD_DIM

LANES = 128
SUBLANES = 8
VMEM_LIMIT_BYTES = 56 * 1024 * 1024

AUG_SEL0 = 64
AUG_POS_HI = 96
AUG_POS_LO = 97
AUG_PAD = 98
BIG = 2.0 ** 100
POS_SPLIT = 256
Q_TILE = 128
N_SLC = 32
V_ROWS = 80
SEL_KC = 512
NSA_SUB = 4

NT_DIMS = (((1,), (1,)), ((), ()))


def _dot(a, b, **kw):
    return jnp.dot(a, b, preferred_element_type=F32, **kw)


def _dot_nt(a, b, **kw):
    return lax.dot_general(a, b, NT_DIMS, preferred_element_type=F32, **kw)


def _cparams(*sem):
    return pltpu.CompilerParams(dimension_semantics=sem, vmem_limit_bytes=VMEM_LIMIT_BYTES)


def _const_spec(shape):
    nd = len(shape)
    return pl.BlockSpec(shape, lambda *_: (0,) * nd, pipeline_mode=pl.Buffered(1))


_IN_WIDTHS = (NSA_Q_W,) + (NSA_KV_W,) * 6 + (3 * NSA_HEADS, 3 * GDN_W, GDN_HEADS, GDN_HEADS, GDN_W, 2 * D_MODEL)
(_C_Q, _C_CK, _C_CV, _C_SK, _C_SV, _C_WK, _C_WV, _C_GATE, _C_GQKV, _C_BETA, _C_DECAY, _C_GGATE, _C_MERGE,
 IN_WIDTH) = (int(v) for v in np.cumsum((0,) + _IN_WIDTHS))
_C_SMALL = _C_BETA // LANES * LANES
SM_BETA = _C_BETA - _C_SMALL
SM_DECAY = _C_DECAY - _C_SMALL
KEYS_COLS = 2 * NSA_KV_W
CMP_COLS = 2 * NSA_KV_W
_INPROJ_GROUPS = (("keys", KEYS_COLS, BF16), ("cmp", CMP_COLS, BF16), ("ggate", GDN_W, BF16),
                  ("merge", 2 * D_MODEL, BF16))
_INPROJ_WIDTH = sum(w for _, w, _ in _INPROJ_GROUPS)
_GDN_PROJ_WIDTH = 3 * GDN_W + LANES
_INPROJ_T_ROWS = NSA_Q_W + 4 * NSA_HEAD_DIM
_GATE_T_ROWS = 32
INPROJ_TM = 1024
INPROJ_TN = 512
WPREP_TK = 128


def _wprep_body(w_ref, rows_ref, gdn_ref, wt_ref, wg_ref):
    hd = NSA_HEAD_DIM
    feat = lambda c0, n: w_ref[0, c0:c0 + n, :]

    def put_t(ref, col, src):
        for r in range(0, src.shape[0], LANES):
            ref[:, col + r:col + r + LANES] = src[r:r + LANES].T.astype(BF16)

    for g in range(NSA_KV_GROUPS):
        put_t(rows_ref, 2 * g * hd, jnp.concatenate([feat(_C_SK + g * hd, hd), feat(_C_WK + g * hd, hd)], axis=0))
    c = 4 * hd
    for c0, n in ((_C_CK, 2 * NSA_KV_W), (_C_GGATE, GDN_W), (_C_MERGE, 2 * D_MODEL)):
        put_t(rows_ref, c, feat(c0, n))
        c += n
    put_t(gdn_ref, 0, feat(_C_GQKV, 3 * GDN_W))
    put_t(gdn_ref, 3 * GDN_W, feat(_C_SMALL, LANES))
    wt_ref[:NSA_Q_W, :] = feat(_C_Q, NSA_Q_W).astype(BF16)
    for j, c0 in enumerate((_C_SV, _C_WV, _C_SV + hd, _C_WV + hd)):
        wt_ref[NSA_Q_W + j * hd:NSA_Q_W + (j + 1) * hd, :] = feat(c0, hd).astype(BF16)
    wg_ref[...] = feat(_C_GATE, _GATE_T_ROWS).astype(BF16)


def _wprep(w_in_t, layer):
    k = w_in_t.shape[2]
    tk = WPREP_TK
    return pl.pallas_call(
        _wprep_body,
        grid=(k // tk,),
        in_specs=[pl.BlockSpec((1, IN_WIDTH, tk), lambda i: (layer, 0, i))],
        out_specs=[pl.BlockSpec((tk, _INPROJ_WIDTH), lambda i: (i, 0)),
                   pl.BlockSpec((tk, _GDN_PROJ_WIDTH), lambda i: (i, 0)),
                   pl.BlockSpec((_INPROJ_T_ROWS, tk), lambda i: (0, i)),
                   pl.BlockSpec((_GATE_T_ROWS, tk), lambda i: (0, i))],
        out_shape=[jax.ShapeDtypeStruct((k, _INPROJ_WIDTH), BF16), jax.ShapeDtypeStruct((k, _GDN_PROJ_WIDTH), BF16),
                   jax.ShapeDtypeStruct((_INPROJ_T_ROWS, k), BF16), jax.ShapeDtypeStruct((_GATE_T_ROWS, k), BF16)],
        compiler_params=_cparams("parallel"),
        name="wprep",
    )(w_in_t)


def _inproj_body(x_ref, w_ref, wt_ref, wg_ref, keys_ref, cmp_ref, ggate_ref, merge_ref, qvt_ref, gt_ref, cmp_s):
    x = x_ref[...].astype(BF16)
    outs = (keys_ref, None, ggate_ref, merge_ref)
    c0 = 0
    for ref, (name, width, _) in zip(outs, _INPROJ_GROUPS):
        for s in range(0, width, INPROJ_TN):
            e = min(s + INPROJ_TN, width)
            res = _dot(x, w_ref[:, c0 + s:c0 + e])
            if name == "cmp":
                for j in range(width // LANES):
                    cmp_s[j] = res[:, j * LANES:(j + 1) * LANES]
            else:
                ref[:, s:e] = res.astype(ref.dtype)
        c0 += width
    nblk = cmp_ref.shape[0]
    for l in range(CMP_STRIDE):
        for j in range(cmp_s.shape[0]):
            cmp_ref[:, l * CMP_COLS + j * LANES:l * CMP_COLS + (j + 1) * LANES] = (
                cmp_s[j, pl.ds(l, nblk, stride=CMP_STRIDE), :].astype(BF16))
    for s in range(0, _INPROJ_T_ROWS, 2 * LANES):
        qvt_ref[s:s + 2 * LANES, :] = _dot_nt(wt_ref[s:s + 2 * LANES, :], x).astype(qvt_ref.dtype)
    gt_ref[...] = _dot_nt(wg_ref[...], x)


def _inproj(x2, w_rows, w_t, w_g):
    m = x2.shape[0]
    tm = INPROJ_TM
    row_major = [(n, wd, dt) for n, wd, dt in _INPROJ_GROUPS if n != "cmp"]
    specs = {n: (pl.BlockSpec((tm, wd), lambda i: (i, 0)), jax.ShapeDtypeStruct((m, wd), dt))
             for n, wd, dt in row_major}
    specs["cmp"] = (pl.BlockSpec((tm // CMP_STRIDE, CMP_STRIDE * CMP_COLS), lambda i: (i, 0)),
                    jax.ShapeDtypeStruct((m // CMP_STRIDE, CMP_STRIDE * CMP_COLS), BF16))
    order = [n for n, _, _ in _INPROJ_GROUPS]
    return pl.pallas_call(
        _inproj_body,
        grid=(m // tm,),
        in_specs=[pl.BlockSpec((tm, D_MODEL), lambda i: (i, 0)), _const_spec(w_rows.shape),
                  _const_spec(w_t.shape), _const_spec(w_g.shape)],
        out_specs=[specs[n][0] for n in order]
        + [pl.BlockSpec((_INPROJ_T_ROWS, tm), lambda i: (0, i)), pl.BlockSpec((_GATE_T_ROWS, tm), lambda i: (0, i))],
        out_shape=[specs[n][1] for n in order]
        + [jax.ShapeDtypeStruct((_INPROJ_T_ROWS, m), BF16), jax.ShapeDtypeStruct((_GATE_T_ROWS, m), F32)],
        scratch_shapes=[pltpu.VMEM((CMP_COLS // LANES, tm, LANES), F32)],
        compiler_params=_cparams("parallel"),
        name="inproj",
    )(x2, w_rows, w_t, w_g)


def _compress_weights(cmp_pos, cmp_w2):
    hd, half = NSA_HEAD_DIM, CMP_LEN // 2
    posr = cmp_pos.reshape(2, 2, half, hd)
    post = jnp.broadcast_to(posr.transpose(1, 2, 0, 3)[:, :, :, None, :], (2, half, 2, 2, hd))
    post = jnp.concatenate([post.reshape(2, half * 4 * hd),
                            jnp.zeros((SUBLANES - 2, half * 4 * hd), cmp_pos.dtype)], axis=0)
    w2sel = jnp.zeros((2, 2, 2 * hd, LANES), cmp_w2.dtype)
    for g in range(2):
        w2sel = w2sel.at[:, g, g * hd:(g + 1) * hd, :hd].set(cmp_w2)
    return post.astype(BF16), w2sel.reshape(4, 2 * hd, LANES).astype(BF16)


def _compress_body(t_ref, w1_ref, pos_ref, w2_ref, aug_ref, out_ref, w1e_s):
    hd, half_len = NSA_HEAD_DIM, CMP_LEN // 2

    @pl.when(pl.program_id(0) == 0)
    def _():
        w1e_s[...] = jnp.zeros(w1e_s.shape, BF16)
        for which in range(2):
            for half in range(2):
                for l in range(half_len):
                    blk = w1_ref[which, (half * half_len + l) * hd:(half * half_len + l + 1) * hd, :].astype(BF16)
                    for g in range(NSA_KV_GROUPS):
                        r0 = l * CMP_COLS + which * LANES + g * hd
                        c0 = half * CMP_COLS + which * LANES + g * hd
                        w1e_s[r0:r0 + hd, c0:c0 + hd] = blk

    p = _dot(t_ref[0], w1e_s[...])
    pp = _dot(pos_ref[...], w1e_s[...])
    nxt = pltpu.roll(p[:, CMP_COLS:], p.shape[0] - 1, axis=0)
    pre = p[:, :CMP_COLS] + nxt + pp[0:1, :CMP_COLS] + pp[1:2, CMP_COLS:]
    h = jax.nn.gelu(pre).astype(BF16)
    n_idx = lax.broadcasted_iota(jnp.int32, (p.shape[0], LANES), 0)
    real = n_idx < p.shape[0] - 1
    for which in range(2):
        hw = h[:, which * LANES:(which + 1) * LANES]
        for g in range(2):
            o = jnp.where(real, _dot(hw, w2_ref[which * 2 + g]) + aug_ref[which], 0.0)
            out_ref[0, which * 2 + g] = (o if which == 0 else o.T).astype(out_ref.dtype)


def _compress(t2, w1, post, w2sel, aug):
    b, nblk, _ = t2.shape
    return pl.pallas_call(
        _compress_body,
        grid=(b,),
        in_specs=[pl.BlockSpec((1, nblk, CMP_STRIDE * CMP_COLS), lambda i: (i, 0, 0)),
                  _const_spec(w1.shape), _const_spec(post.shape), _const_spec(w2sel.shape), _const_spec(aug.shape)],
        out_specs=pl.BlockSpec((1, 4, nblk, LANES), lambda i: (i, 0, 0, 0)),
        out_shape=jax.ShapeDtypeStruct((b, 4, nblk, LANES), BF16),
        scratch_shapes=[pltpu.VMEM((CMP_STRIDE * CMP_COLS, 2 * CMP_COLS), BF16)],
        compiler_params=_cparams("arbitrary"),
        name="compress",
    )(t2, w1, post, w2sel, aug)


def _nsa_consts(s):
    hd, rep = NSA_HEAD_DIM, NSA_REP
    t = np.arange(s)
    kx_win = np.zeros((s + WIN_LEN, hd), np.float32)
    kx_win[WIN_LEN + t, AUG_POS_HI - hd] = t // POS_SPLIT
    kx_win[WIN_LEN + t, AUG_POS_LO - hd] = t % POS_SPLIT
    kx_win[:WIN_LEN, AUG_PAD - hd] = 1.0
    kx_sel = kx_win.copy()
    kx_sel[WIN_LEN + t, t // SLC_LEN] = 1.0
    vx_win = np.zeros((V_ROWS - hd, s + WIN_LEN), np.float32)
    vx_win[0, WIN_LEN:] = 1.0
    vx_sel = vx_win
    n_cmp = s // CMP_STRIDE
    cmp_aug = np.zeros((2, n_cmp, LANES), np.float32)
    end = np.arange(n_cmp) * CMP_STRIDE + CMP_LEN - 1
    cmp_aug[0, :, AUG_POS_HI] = end // POS_SPLIT
    cmp_aug[0, :, AUG_POS_LO] = end % POS_SPLIT
    qx = np.zeros((NSA_KV_GROUPS, LANES - AUG_POS_HI, rep * Q_TILE), np.float32)
    for h in range(NSA_HEADS):
        slope = 2.0 ** (-8.0 * (h + 1) / NSA_HEADS)
        lanes = slice((h % rep) * Q_TILE, (h % rep + 1) * Q_TILE)
        qx[h // rep, 0, lanes] = slope * POS_SPLIT
        qx[h // rep, 1, lanes] = slope
        qx[h // rep, AUG_PAD - AUG_POS_HI, lanes] = -BIG
    c0 = np.arange(n_cmp)[None, :] * CMP_STRIDE
    s0 = np.arange(s // SLC_LEN)[:, None] * SLC_LEN
    ov_t = ((c0 < s0 + SLC_LEN) & (c0 + CMP_LEN > s0)).astype(np.float32)
    ov_t[:, (s - CMP_LEN) // CMP_STRIDE + 1:] = 0.0
    kk = np.arange(Q_TILE)[:, None]
    qq = np.arange(Q_TILE)[None, :]
    causal = np.tile(np.where(kk <= qq, 0.0, NEG).astype(np.float32), (1, rep))
    after = np.tile(np.where(kk > qq, 0.0, NEG).astype(np.float32), (1, rep))
    j = jnp.asarray
    return dict(kx_sel=j(kx_sel, BF16), kx_win=j(kx_win, BF16), vx_sel=j(vx_sel, BF16), vx_win=j(vx_win, BF16),
                cmp_aug=j(cmp_aug), qx=j(qx), ov_t=j(ov_t), causal=j(causal), after=j(after))


def _nsa_body(qt_ref, k_ref, vt_ref, kc_ref, vct_ref, gt_ref, kxs_ref, kxw_ref, vxs_ref, vxw_ref, qx_ref, ovt_ref,
              causal_ref, after_ref, out_ref, ks_s, kw_s, vs_s, vw_s):
    hd, rep, tq = NSA_HEAD_DIM, NSA_REP, Q_TILE
    nq = rep * tq
    i = pl.program_id(2)

    @pl.when(i == 0)
    def _():
        keys = k_ref[0]
        ks_s[:WIN_LEN, :hd] = jnp.zeros((WIN_LEN, hd), BF16)
        ks_s[WIN_LEN:, :hd] = keys[:, :hd]
        ks_s[:, hd:] = kxs_ref[...]
        kw_s[:WIN_LEN, :hd] = jnp.zeros((WIN_LEN, hd), BF16)
        kw_s[WIN_LEN:, :hd] = keys[:, hd:]
        kw_s[:, hd:] = kxw_ref[...]
        vals = vt_ref[...]
        vs_s[:hd, :WIN_LEN] = jnp.zeros((hd, WIN_LEN), BF16)
        vs_s[:hd, WIN_LEN:] = vals[:hd]
        vs_s[hd:, :] = vxs_ref[...]
        vw_s[:hd, :WIN_LEN] = jnp.zeros((hd, WIN_LEN), BF16)
        vw_s[:hd, WIN_LEN:] = vals[hd:]
        vw_s[hd:, :] = vxw_ref[...]

    qx = qx_ref[0]
    sg_all = jax.nn.sigmoid(gt_ref[...])
    grp = pl.program_id(1)

    full_past = WIN_LEN // tq

    def front(sub, res, it, past):
        qt = qt_ref[:, sub * tq:(sub + 1) * tq]
        q64 = jnp.concatenate([qt[r * hd:(r + 1) * hd, :] for r in range(rep)], axis=1).astype(F32) * (hd ** -0.5)

        def q_aug(sel_rows):
            return jnp.concatenate([q64, sel_rows, qx], axis=0).astype(BF16)

        n_row = lax.broadcasted_iota(jnp.int32, (LANES, nq), 0)
        t_lane = it * tq + (lax.broadcasted_iota(jnp.int32, (LANES, nq), 1) & (tq - 1))
        valid = t_lane >= n_row * CMP_STRIDE + (CMP_LEN - 1)
        qa0 = q_aug(jnp.zeros((N_SLC, nq), F32))
        sc = jnp.where(valid, _dot(kc_ref[0, 0], qa0), NEG)
        mc = jnp.max(sc, axis=0, keepdims=True)
        ec = jnp.where(valid, jnp.exp(sc - mc), 0.0)
        lc = jnp.sum(ec, axis=0, keepdims=True)
        pc = ec * jnp.where(lc > 0.0, 1.0 / lc, 0.0)
        o_cmp = _dot(vct_ref[0, 0], pc.astype(BF16))[:hd]
        psum = pc[:, 0:tq] + pc[:, tq:2 * tq] + pc[:, 2 * tq:3 * tq] + pc[:, 3 * tq:4 * tq]
        score_t = _dot(ovt_ref[...], psum, precision=lax.Precision.HIGHEST)
        yield

        n_keys = (past + 1) * tq
        if past == full_past:
            w0 = pl.multiple_of(it * tq, tq)
        else:
            w0 = WIN_LEN
        diag = past * tq

        def one_shot_masks(s):
            parts = [s[:diag], s[diag:] + causal_ref[...]] if past else [s + causal_ref[...]]
            if past == full_past:
                parts = [s[:tq] + after_ref[...], s[tq:diag], parts[1]]
            return jnp.concatenate(parts, axis=0) if len(parts) > 1 else parts[0]

        s_w = one_shot_masks(_dot(kw_s[pl.ds(w0, n_keys), :], qa0))
        p_w = jnp.exp(s_w - jnp.max(s_w, axis=0, keepdims=True))
        acc_w = _dot(vw_s[:, pl.ds(w0, n_keys)], p_w.astype(BF16))
        o_win = acc_w[:hd] * (1.0 / acc_w[hd:hd + 1])
        yield

        jb = lax.broadcasted_iota(jnp.int32, (N_SLC, tq), 0)
        cur = (it * tq + lax.broadcasted_iota(jnp.int32, (N_SLC, tq), 1)) // SLC_LEN
        forced = (jb == 0) | (jb == cur) | (jb == cur - 1)
        score_t = jnp.where(forced, FORCE_SCORE, jnp.where(jb <= cur, score_t, -1.0))
        rank = jnp.zeros((N_SLC, tq), F32)
        for jp in range(N_SLC):
            other = score_t[jp:jp + 1, :]
            ge = jnp.where(other >= score_t, 1.0, 0.0)
            gt = jnp.where(other > score_t, 1.0, 0.0)
            rank = rank + jnp.where(jb > jp, ge, gt)
        sel = rank < float(SLC_TOPK)
        qa = q_aug(jnp.concatenate([jnp.where(sel, 0.0, -BIG)] * rep, axis=1))
        lo_blk = jnp.min(jnp.where(sel & (jb >= 2) & (jb <= cur), jb.astype(F32), float(N_SLC)))
        lo_key = (lo_blk.astype(jnp.int32) // 2) * tq
        yield

        s_main = _dot(ks_s[pl.ds(w0, n_keys), :], qa)
        if past < full_past:
            s_s = jnp.concatenate([s_main[:diag], s_main[diag:] + causal_ref[...]], axis=0) if past else (
                s_main + causal_ref[...])
            m_s = jnp.max(s_s, axis=0, keepdims=True)
            acc_s = _dot(vs_s[:, pl.ds(w0, n_keys)], jnp.exp(s_s - m_s).astype(BF16))
            res.update(early=False, acc_s=acc_s, o_cmp=o_cmp, o_win=o_win)
            return
        e_key = it * tq - WIN_LEN
        t0 = pl.multiple_of(jnp.where(e_key > 0, WIN_LEN, 0), tq)
        s_s = jnp.concatenate([_dot(ks_s[pl.ds(t0, tq), :], qa), s_main[:diag],
                               s_main[diag:] + causal_ref[...]], axis=0)
        m_s = jnp.max(s_s, axis=0, keepdims=True)
        p_s = jnp.exp(s_s - m_s).astype(BF16)
        acc_s = _dot(vs_s[:, pl.ds(t0, tq)], p_s[:tq]) + _dot(vs_s[:, pl.ds(w0, n_keys)], p_s[tq:])
        c_hi = (e_key - tq + SEL_KC - 1) // SEL_KC
        c_lo = jnp.where(lo_key < e_key, (lo_key - tq) // SEL_KC, c_hi)
        res.update(early=True, qa=qa, e_key=e_key, c_lo=c_lo, c_hi=c_hi, m_s=m_s, acc_s=acc_s, o_cmp=o_cmp,
                   o_win=o_win)

    def tail(sub, f):
        def early_step(c, carry):
            qa, e_key = f["qa"], f["e_key"]
            m, acc = carry
            k0 = tq + c * SEL_KC
            start = pl.multiple_of(WIN_LEN + k0, tq)
            k_abs = k0 + lax.broadcasted_iota(jnp.int32, (SEL_KC, nq), 0)
            s = jnp.where(k_abs < e_key, _dot(ks_s[pl.ds(start, SEL_KC), :], qa), NEG)
            m_new = jnp.maximum(m, jnp.max(s, axis=0, keepdims=True))
            p = jnp.exp(s - m_new).astype(BF16)
            return m_new, acc * jnp.exp(m - m_new) + _dot(vs_s[:, pl.ds(start, SEL_KC)], p)

        acc_s = f["acc_s"]
        if f["early"]:
            _, acc_s = lax.fori_loop(f["c_lo"], f["c_hi"], early_step, (f["m_s"], acc_s))
        o_slc = acc_s[:hd] * (1.0 / acc_s[hd:hd + 1])

        sg = sg_all[:, sub * tq:(sub + 1) * tq]
        gate = lambda br, r: jnp.where(grp == 0, sg[br * NSA_HEADS + r:br * NSA_HEADS + r + 1],
                                       sg[br * NSA_HEADS + rep + r:br * NSA_HEADS + rep + r + 1])
        for pair in range(rep // 2):
            halves = []
            for r in (2 * pair, 2 * pair + 1):
                lanes = slice(r * tq, (r + 1) * tq)
                halves.append(gate(0, r) * f["o_cmp"][:, lanes] + gate(1, r) * o_slc[:, lanes]
                              + gate(2, r) * f["o_win"][:, lanes])
            out_ref[0, sub * tq:(sub + 1) * tq, pair * LANES:(pair + 1) * LANES] = (
                jnp.concatenate(halves, axis=0).T.astype(out_ref.dtype))

    def run(tiles):
        fronts = [{} for _ in range(NSA_SUB)]
        for sub, (it, past) in enumerate(tiles):
            for _ in front(sub, fronts[sub], it, past):
                pass
        for sub in range(NSA_SUB):
            tail(sub, fronts[sub])

    n_short = -(-full_past // NSA_SUB)
    for step in range(n_short):
        @pl.when(i == step)
        def _(step=step):
            run([(step * NSA_SUB + sub, min(step * NSA_SUB + sub, full_past)) for sub in range(NSA_SUB)])

    @pl.when(i >= n_short)
    def _():
        run([(i * NSA_SUB + sub, full_past) for sub in range(NSA_SUB)])


def _nsa(qvt, keys, cmp_kv, gt, consts, b, s):
    tqs = NSA_SUB * Q_TILE
    nt = s // tqs
    c = consts
    in_specs = [
        pl.BlockSpec((2 * LANES, tqs), lambda bi, g, i: (g, bi * nt + i)),
        pl.BlockSpec((1, s, LANES), lambda bi, g, i: (bi, 0, g)),
        pl.BlockSpec((LANES, s), lambda bi, g, i: (NSA_Q_W // LANES + g, bi)),
        pl.BlockSpec((1, 1, s // CMP_STRIDE, LANES), lambda bi, g, i: (bi, g, 0, 0)),
        pl.BlockSpec((1, 1, s // CMP_STRIDE, LANES), lambda bi, g, i: (bi, 2 + g, 0, 0)),
        pl.BlockSpec((_GATE_T_ROWS, tqs), lambda bi, g, i: (0, bi * nt + i)),
        _const_spec(c["kx_sel"].shape), _const_spec(c["kx_win"].shape), _const_spec(c["vx_sel"].shape),
        _const_spec(c["vx_win"].shape),
        pl.BlockSpec((1,) + c["qx"].shape[1:], lambda bi, g, i: (g, 0, 0)),
        _const_spec(c["ov_t"].shape), _const_spec(c["causal"].shape), _const_spec(c["after"].shape),
    ]
    return pl.pallas_call(
        _nsa_body,
        grid=(b, NSA_KV_GROUPS, nt),
        in_specs=in_specs,
        out_specs=pl.BlockSpec((1, tqs, 2 * LANES), lambda bi, g, i: (bi, i, g)),
        out_shape=jax.ShapeDtypeStruct((b, s, NSA_Q_W), BF16),
        scratch_shapes=[pltpu.VMEM((s + WIN_LEN, LANES), BF16), pltpu.VMEM((s + WIN_LEN, LANES), BF16),
                        pltpu.VMEM((V_ROWS, s + WIN_LEN), BF16), pltpu.VMEM((V_ROWS, s + WIN_LEN), BF16)],
        compiler_params=_cparams("parallel", "parallel", "arbitrary"),
        name="nsa",
    )(qvt, keys, qvt, cmp_kv, cmp_kv, gt, c["kx_sel"], c["kx_win"], c["vx_sel"], c["vx_win"], c["qx"], c["ov_t"],
      c["causal"], c["after"])


GDN_TS = 512
GDN_BLK = 128
GDN_HALO = 8
GDN_XHALO = 16
GDN_INV_BASE = 8
GDN_CHAIN_UNITS_PER_PREP_UNIT = 4
GDN_SCAN_ROWS = GDN_HEAD_DIM + GDN_CHUNK


def _gdn_prep(x_ref, prev_ref, w_ref, cw_ref, alog_ref, dtb_ref, lt_ref, first_tile, xp_s, buf):
    x_s, p_s, rhs_s, qg_s, aqk_s, kdt_s, a_s, eg_s = buf
    ts, dh, nh, blk = GDN_TS, GDN_HEAD_DIM, GDN_HEADS, GDN_BLK
    proj = _dot(x_ref[0].astype(BF16), w_ref[...])
    hist = _dot(prev_ref[0].astype(BF16), w_ref[:, :3 * GDN_W])[GDN_XHALO - GDN_HALO:]
    xp_s[0:GDN_HALO, :] = jnp.where(first_tile, 0.0, hist)
    xp_s[GDN_HALO:, :] = proj[:, :3 * GDN_W]
    yield
    act = []
    for blk_i in range(3 * nh):
        lanes = slice(blk_i * dh, (blk_i + 1) * dh)
        xp = xp_s[:, lanes]
        conv = cw_ref[0:1, lanes] * xp
        for j in range(1, GDN_CONV):
            conv = pltpu.roll(conv, 1, axis=0) + cw_ref[j:j + 1, lanes] * xp
        act.append(jax.nn.silu(conv[GDN_HALO:]))
        yield

    sm = proj[:, 3 * GDN_W:]
    beta = jax.nn.sigmoid(sm)
    g = -jnp.exp(alog_ref[...]) * jax.nn.softplus(sm + dtb_ref[...])
    gcum = jnp.concatenate([_dot(lt_ref[...], g[r:r + blk], precision=lax.Precision.HIGHEST)
                            for r in range(0, ts, blk)], axis=0)
    eg = jnp.exp(gcum)
    eg_s[...] = eg

    ri = lax.broadcasted_iota(jnp.int32, (blk, blk), 0)
    ci = lax.broadcasted_iota(jnp.int32, (blk, blk), 1)
    same = (ri // GDN_CHUNK) == (ci // GDN_CHUNK)
    causal = same & (ri >= ci)
    strict = same & (ri > ci)
    eye = (ri == ci).astype(F32)
    same_base = (ri // GDN_INV_BASE) == (ci // GDN_INV_BASE)

    for pb in range(ts // blk):
        rows = slice(pb * blk, (pb + 1) * blk)
        gc = gcum[rows]
        gc_t = gc.T
        first = lax.broadcasted_iota(jnp.int32, (blk, LANES), 0) < GDN_CHUNK
        g_last = jnp.where(first, gc[GDN_CHUNK - 1:GDN_CHUNK, :], gc[blk - 1:blk, :])
        e_dec = jnp.exp(g_last - gc)
        for h in range(nh):
            q, k, v = act[h][rows], act[nh + h][rows], act[2 * nh + h][rows]
            q = q * lax.rsqrt(jnp.sum(q * q, axis=1, keepdims=True) + RMS_EPS) * (dh ** -0.5)
            k = k * lax.rsqrt(jnp.sum(k * k, axis=1, keepdims=True) + RMS_EPS)
            b_col = beta[rows, SM_BETA + h:SM_BETA + h + 1]
            eg_col = eg[rows, SM_DECAY + h:SM_DECAY + h + 1]
            gdiff = gc[:, SM_DECAY + h:SM_DECAY + h + 1] - gc_t[SM_DECAY + h:SM_DECAY + h + 1, :]
            decay = jnp.exp(jnp.where(causal, gdiff, NEG))
            kb = k * b_col
            kbf, kf, qf = kb.astype(BF16), k.astype(BF16), q.astype(BF16)
            a = jnp.where(strict, -_dot_nt(kbf, kf) * decay, 0.0)
            c = pb * nh + h
            a_base = jnp.where(same_base, a, 0.0)
            a_s[c] = a.astype(BF16)
            x_s[c] = a_base.astype(BF16)
            p_s[c] = eye + a_base
            rhs_s[c] = jnp.concatenate([v * b_col, kb * eg_col], axis=1).astype(BF16)
            qg_s[c] = q * eg_col
            aqk_s[c] = jnp.where(causal, _dot_nt(qf, kf) * decay, 0.0).astype(BF16)
            kdt_s[c] = (k * e_dec[:, SM_DECAY + h:SM_DECAY + h + 1]).T.astype(BF16)
            yield


def _gdn_chains(buf, nq_ref, co_ref, eg_ref):
    x_s, p_s, rhs_s, qg_s, aqk_s, kdt_s, a_s, eg_s = buf
    dh, nh, blk = GDN_HEAD_DIM, GDN_HEADS, GDN_BLK
    n_chain = x_s.shape[0]
    eg_ref[0] = eg_s[...]
    ri = lax.broadcasted_iota(jnp.int32, (blk, blk), 0)
    ci = lax.broadcasted_iota(jnp.int32, (blk, blk), 1)
    same = lambda size: (ri // size) == (ci // size)
    for c in range(n_chain):
        y = x_s[c]
        x_s[c] = _dot(y, y).astype(BF16)
        yield
    for c in range(n_chain):
        y2 = x_s[c]
        p = p_s[c]
        p_s[c] = p + _dot(p.astype(BF16), y2)
        x_s[c] = _dot(y2, y2).astype(BF16)
        yield
    for c in range(n_chain):
        p = p_s[c]
        p_s[c] = p + _dot(p.astype(BF16), x_s[c])
        yield
    size = 2 * GDN_INV_BASE
    while size <= GDN_CHUNK:
        between = same(size) & jnp.logical_not(same(size // 2))
        for c in range(n_chain):
            a = a_s[c]
            x_s[c] = _dot(p_s[c].astype(BF16), jnp.where(between, a, jnp.zeros_like(a))).astype(BF16)
            yield
        for c in range(n_chain):
            t = p_s[c]
            p_s[c] = t + _dot(x_s[c], t.astype(BF16))
            yield
        size *= 2

    tok_half = lax.broadcasted_iota(jnp.int32, (blk, blk), 1) // GDN_CHUNK
    for c in range(n_chain):
        rhs_s[c] = _dot(p_s[c].astype(BF16), rhs_s[c]).astype(BF16)
        yield
    for c in range(n_chain):
        pb, h = divmod(c, nh)
        uw = rhs_s[c]
        a1 = _dot(aqk_s[c], uw)
        q_loc = qg_s[c] - a1[:, dh:]
        kdt = kdt_s[c]
        for half in range(blk // GDN_CHUNK):
            k1 = _dot(jnp.where(tok_half == half, kdt, jnp.zeros_like(kdt)), uw)
            n = pb * (blk // GDN_CHUNK) + half
            rows = slice(half * GDN_CHUNK, (half + 1) * GDN_CHUNK)
            nq_ref[0, h, n, :dh, :] = (-k1[:, dh:]).astype(BF16)
            nq_ref[0, h, n, dh:, :] = q_loc[rows].astype(BF16)
            co_ref[0, h, n, :dh, :] = k1[:, :dh].astype(BF16)
            co_ref[0, h, n, dh:, :] = a1[rows, :dh].astype(BF16)
        yield


def _interleave(major, minor, minor_per_major):
    for _ in major:
        for _ in range(minor_per_major):
            next(minor, None)
    for _ in minor:
        pass


def _gdn_intra_body(x_ref, prev_ref, w_ref, cw_ref, alog_ref, dtb_ref, lt_ref, nq_ref, co_ref, eg_ref, xp_s, *bufs,
                    tiles_per_seq, n_buf):
    j = pl.program_id(0)
    sets = (bufs[:n_buf], bufs[n_buf:])

    @pl.when(j == 0)
    def _():
        for ref in sets[1]:
            ref[...] = jnp.zeros(ref.shape, ref.dtype)

    first_tile = (j % tiles_per_seq) == 0
    for parity in range(2):
        @pl.when(j % 2 == parity)
        def _(parity=parity):
            chains = _gdn_chains(sets[1 - parity], nq_ref, co_ref, eg_ref)
            prep = _gdn_prep(x_ref, prev_ref, w_ref, cw_ref, alog_ref, dtb_ref, lt_ref, first_tile, xp_s,
                             sets[parity])
            _interleave(prep, chains, GDN_CHAIN_UNITS_PER_PREP_UNIT)


def _gdn_intra(x, w_gdn, conv_w, alog_l, dtb_l, lt):
    b, s, d = x.shape
    ts, nh, dh = GDN_TS, GDN_HEADS, GDN_HEAD_DIM
    tps = s // ts
    nt = b * tps
    n_chain = (ts // GDN_BLK) * nh
    sq = lambda dt: pltpu.VMEM((n_chain, GDN_BLK, GDN_BLK), dt)
    buf = lambda: [sq(BF16), sq(F32), pltpu.VMEM((n_chain, GDN_BLK, 2 * dh), BF16), sq(F32), sq(BF16), sq(BF16),
                   sq(BF16), pltpu.VMEM((ts, LANES), F32)]
    src = lambda j: jnp.minimum(j, nt - 1)
    dst = lambda j: jnp.maximum(j - 1, 0)
    cspec = lambda: pl.BlockSpec((1, nh, ts // GDN_CHUNK, GDN_SCAN_ROWS, dh),
                                 lambda j: (dst(j) // tps, 0, dst(j) % tps, 0, 0))
    cshape = jax.ShapeDtypeStruct((b, nh, s // GDN_CHUNK, GDN_SCAN_ROWS, dh), BF16)
    return pl.pallas_call(
        functools.partial(_gdn_intra_body, tiles_per_seq=tps, n_buf=len(buf())),
        grid=(nt + 1,),
        in_specs=[
            pl.BlockSpec((1, ts, d), lambda j: (src(j) // tps, src(j) % tps, 0)),
            pl.BlockSpec((1, GDN_XHALO, d),
                         lambda j: (src(j) // tps, jnp.maximum((src(j) % tps) * (ts // GDN_XHALO) - 1, 0), 0)),
            _const_spec(w_gdn.shape), _const_spec(conv_w.shape), _const_spec(alog_l.shape), _const_spec(dtb_l.shape),
            _const_spec(lt.shape),
        ],
        out_specs=[cspec(), cspec(), pl.BlockSpec((1, ts, LANES), lambda j: (dst(j) // tps, dst(j) % tps, 0))],
        out_shape=[cshape, cshape, jax.ShapeDtypeStruct((b, s, LANES), F32)],
        scratch_shapes=[pltpu.VMEM((ts + GDN_HALO, 3 * GDN_W), F32)] + buf() + buf(),
        compiler_params=_cparams("arbitrary"),
        name="gdn_intra",
    )(x, x, w_gdn, conv_w, alog_l, dtb_l, lt)


def _gdn_scan_body(nq_ref, co_ref, eg_ref, gate_ref, nw_ref, out_ref, st_s):
    nh, dh, ck = GDN_HEADS, GDN_HEAD_DIM, GDN_CHUNK
    st_s[...] = jnp.zeros(st_s.shape, F32)

    def chunk(n, carry):
        r0 = pl.multiple_of(n * ck, ck)
        d_row = eg_ref[0, pl.ds(r0 + ck - 1, 1), :]
        for h in range(nh):
            st = st_s[h]
            res = _dot(nq_ref[0, h, n], st.astype(BF16)) + co_ref[0, h, n].astype(F32)
            st_s[h] = st * d_row[:, SM_DECAY + h:SM_DECAY + h + 1] + res[:dh]
            o = res[dh:]
            ms = jnp.mean(o * o, axis=1, keepdims=True)
            gt = gate_ref[0, pl.ds(r0, ck), h * dh:(h + 1) * dh].astype(F32)
            out_ref[0, pl.ds(r0, ck), h * dh:(h + 1) * dh] = (
                o * lax.rsqrt(ms + RMS_EPS) * nw_ref[...] * jax.nn.silu(gt)).astype(out_ref.dtype)
        return carry

    lax.fori_loop(0, nq_ref.shape[2], chunk, 0, unroll=16)


def _gdn_scan(nq, co, eg, ggate, norm_w):
    b, nh, nc, rows, dh = nq.shape
    s = nc * GDN_CHUNK
    cspec = lambda: pl.BlockSpec((1, nh, nc, rows, dh), lambda bi: (bi, 0, 0, 0, 0))
    return pl.pallas_call(
        _gdn_scan_body,
        grid=(b,),
        in_specs=[cspec(), cspec(),
                  pl.BlockSpec((1, s, LANES), lambda bi: (bi, 0, 0)),
                  pl.BlockSpec((1, s, GDN_W), lambda bi: (bi, 0, 0)),
                  _const_spec(norm_w.shape)],
        out_specs=pl.BlockSpec((1, s, GDN_W), lambda bi: (bi, 0, 0)),
        out_shape=jax.ShapeDtypeStruct((b, s, GDN_W), BF16),
        scratch_shapes=[pltpu.VMEM((nh, dh, dh), F32)],
        compiler_params=_cparams("parallel"),
        name="gdn_scan",
    )(nq, co, eg, ggate, norm_w)


MERGE_TM = 1024


def _layer_norm(y, g, b):
    mu = jnp.mean(y, axis=1, keepdims=True)
    d = y - mu
    var = jnp.mean(d * d, axis=1, keepdims=True)
    return d * lax.rsqrt(var + LN_EPS) * g + b


def _merge_body(x_ref, oa_ref, ob_ref, ga_ref, gb_ref, wa_ref, wb_ref, wo_ref, g_ref, b_ref, y_ref, yb_ref):
    ya = _dot(oa_ref[...], wa_ref[...])
    yb = _dot(ob_ref[...], wb_ref[...])
    mixin = jax.nn.sigmoid(ga_ref[...].astype(F32)) * ya + jax.nn.sigmoid(gb_ref[...].astype(F32)) * yb
    mix = _dot(mixin.astype(BF16), wo_ref[...])
    y = _layer_norm(DEEPNORM_ALPHA * x_ref[...] + mix, g_ref[...], b_ref[...])
    y_ref[...] = y
    yb_ref[...] = y.astype(BF16)


def _merge(x2, oa, ob, mgate, wa, wb, wo, g, b):
    m = x2.shape[0]
    tm, d = MERGE_TM, D_MODEL
    row = lambda wd, col=0: pl.BlockSpec((tm, wd), lambda i, col=col: (i, col))
    return pl.pallas_call(
        _merge_body,
        grid=(m // tm,),
        in_specs=[row(d), row(NSA_Q_W), row(GDN_W), row(d, 0), row(d, 1),
                  _const_spec(wa.shape), _const_spec(wb.shape), _const_spec(wo.shape),
                  _const_spec(g.shape), _const_spec(b.shape)],
        out_specs=[row(d), row(d)],
        out_shape=[jax.ShapeDtypeStruct((m, d), F32), jax.ShapeDtypeStruct((m, d), BF16)],
        compiler_params=_cparams("parallel"),
        name="merge",
    )(x2, oa, ob, mgate, mgate, wa, wb, wo, g, b)


FFN_TM = 1024
FFN_HALO = 16
FFN_CK = 256


def _ffn_body(x_ref, xb_ref, prev_ref, wu_ref, cw_ref, wd_ref, g_ref, b_ref, out_ref, act_s,
              *, tiles_per_seq):
    i = pl.program_id(0)
    prev = prev_ref[...]
    prev = jnp.where(i % tiles_per_seq == 0, jnp.zeros_like(prev), prev)
    xc = jnp.concatenate([prev, xb_ref[...]], axis=0)

    def conv(h, c0):
        out = cw_ref[FFN_CONV - 1:FFN_CONV, c0:c0 + FFN_CK] * h[FFN_HALO:]
        for j in range(FFN_CONV - 1):
            shifted = pltpu.roll(h, FFN_CONV - 1 - j, axis=0)[FFN_HALO:]
            out = out + cw_ref[j:j + 1, c0:c0 + FFN_CK] * shifted
        return out

    for c in range(FFN_DIM // FFN_CK):
        c0 = c * FFN_CK
        hg = conv(_dot(xc, wu_ref[:, c0:c0 + FFN_CK]), c0)
        hv = conv(_dot(xc, wu_ref[:, FFN_DIM + c0:FFN_DIM + c0 + FFN_CK]), FFN_DIM + c0)
        act_s[:, c0:c0 + FFN_CK] = (jax.nn.silu(hg) * hv).astype(BF16)
    f = _dot(act_s[...], wd_ref[...])
    out_ref[...] = _layer_norm(DEEPNORM_ALPHA * x_ref[...] + f, g_ref[...], b_ref[...])


def _ffn(x1, x1b, wu, cw, wd, g, b, seq):
    m = x1.shape[0]
    tm, d = FFN_TM, D_MODEL
    return pl.pallas_call(
        functools.partial(_ffn_body, tiles_per_seq=seq // tm),
        grid=(m // tm,),
        in_specs=[pl.BlockSpec((tm, d), lambda i: (i, 0)),
                  pl.BlockSpec((tm, d), lambda i: (i, 0)),
                  pl.BlockSpec((FFN_HALO, d), lambda i: (jnp.maximum(i * (tm // FFN_HALO) - 1, 0), 0)),
                  _const_spec(wu.shape), _const_spec(cw.shape), _const_spec(wd.shape), _const_spec(g.shape), _const_spec(b.shape)],
        out_specs=pl.BlockSpec((tm, d), lambda i: (i, 0)),
        out_shape=jax.ShapeDtypeStruct((m, d), F32),
        scratch_shapes=[pltpu.VMEM((tm, FFN_DIM), BF16)],
        compiler_params=_cparams("parallel"),
        name="ffn",
    )(x1, x1b, x1b, wu, cw, wd, g, b)


def _lane_vec(vals, lane0):
    return jnp.zeros((1, LANES), F32).at[0, lane0:lane0 + vals.shape[0]].set(vals.astype(F32))


def _layer(x, w_in, cmp_pos, cmp_w1, cmp_w2, w_nsa_out, gdn_conv_w, gdn_a_log, gdn_dt_bias, gdn_norm_w,
           w_gdn_out, w_o, ln1_g, ln1_b, ffn_w_up, ffn_conv_w, ffn_w_down, ln2_g, ln2_b):
    b, s, d = x.shape
    m = b * s
    x2 = x.reshape(m, d)
    w_rows, w_gdn, w_t, w_g = _wprep(*w_in)
    keys, cmpkv, ggate, mgate, qvt, gt = _inproj(x2, w_rows, w_t, w_g)

    consts = _nsa_consts(s)
    post, w2sel = _compress_weights(cmp_pos, cmp_w2)
    cmp_kv = _compress(cmpkv.reshape(b, s // CMP_STRIDE, CMP_STRIDE * CMP_COLS), cmp_w1, post, w2sel,
                       consts["cmp_aug"])
    o_nsa = _nsa(qvt, keys.reshape(b, s, KEYS_COLS), cmp_kv, gt, consts, b, s)

    ck = GDN_CHUNK
    tri = np.tril(np.ones((ck, ck), np.float32))
    lt = jnp.asarray(np.kron(np.eye(GDN_BLK // ck, dtype=np.float32), tri))
    nq, co, eg = _gdn_intra(x, w_gdn, gdn_conv_w, _lane_vec(gdn_a_log, SM_DECAY), _lane_vec(gdn_dt_bias, SM_DECAY),
                            lt)
    o_gdn = _gdn_scan(nq, co, eg, ggate.reshape(b, s, GDN_W), gdn_norm_w.reshape(1, GDN_HEAD_DIM))

    x1, x1b = _merge(x2, o_nsa.reshape(m, NSA_Q_W), o_gdn.reshape(m, GDN_W), mgate,
                     w_nsa_out.astype(BF16), w_gdn_out.astype(BF16), w_o.astype(BF16),
                     ln1_g.reshape(1, d), ln1_b.reshape(1, d))
    out = _ffn(x1, x1b, ffn_w_up.astype(BF16), ffn_conv_w, ffn_w_down.astype(BF16),
               ln2_g.reshape(1, d), ln2_b.reshape(1, d), s)
    return out.reshape(b, s, d)


def kernel(x, w_in, nsa_cmp_pos, nsa_cmp_w1, nsa_cmp_w2, w_nsa_out, gdn_conv_w, gdn_a_log, gdn_dt_bias, gdn_norm_w, w_gdn_out, w_o, ln1_g, ln1_b, ffn_w_up, ffn_conv_w, ffn_w_down, ln2_g, ln2_b):
    w_in_t = jnp.swapaxes(w_in, 1, 2)
    for l in range(DEPTH):
        x = _layer(x, (w_in_t, l), nsa_cmp_pos[l], nsa_cmp_w1[l], nsa_cmp_w2[l], w_nsa_out[l], gdn_conv_w[l],
                   gdn_a_log[l], gdn_dt_bias[l], gdn_norm_w[l], w_gdn_out[l], w_o[l], ln1_g[l], ln1_b[l],
                   ffn_w_up[l], ffn_conv_w[l], ffn_w_down[l], ln2_g[l], ln2_b[l])
    return x
```

```python
import functools

import numpy as np
import jax
import jax.numpy as jnp
from jax import lax
from jax.experimental import pallas as pl
from jax.experimental.pallas import tpu as pltpu

F32 = jnp.float32
BF16 = jnp.bfloat16

D_MODEL = 1024
NSA_HEADS = 8
NSA_KV_GROUPS = 2
NSA_REP = NSA_HEADS // NSA_KV_GROUPS
NSA_HEAD_DIM = 64
CMP_LEN = 32
CMP_STRIDE = 16
SLC_LEN = 64
SLC_TOPK = 8
WIN_LEN = 512
FORCE_SCORE = 1.0e4
NEG = -1.0e30
GDN_HEADS = 4
GDN_HEAD_DIM = 128
GDN_CONV = 4
GDN_CHUNK = 64
FFN_DIM = 2816
FFN_CONV = 3
DEPTH = 1
DEEPNORM_ALPHA = (2.0 * DEPTH) ** 0.25
LN_EPS = 1e-5
RMS_EPS = 1e-6

NSA_Q_W = NSA_HEADS * NSA_HEAD_DIM
NSA_KV_W = NSA_KV_GROUPS * NSA_HEAD_DIM
GDN_W = GDN_HEADS * GDN_HEAD_DIM

LANES = 128
SUBLANES = 8
VMEM_LIMIT_BYTES = 56 * 1024 * 1024

AUG_SEL0 = 64
AUG_POS_HI = 96
AUG_POS_LO = 97
AUG_PAD = 98
BIG = 2.0 ** 100
POS_SPLIT = 256
Q_TILE = 128
N_SLC = 32
V_ROWS = 80
SEL_KC = 512
NSA_SUB = 8

NT_DIMS = (((1,), (1,)), ((), ()))


def _dot(a, b, **kw):
    return jnp.dot(a, b, preferred_element_type=F32, **kw)


def _dot_nt(a, b, **kw):
    return lax.dot_general(a, b, NT_DIMS, preferred_element_type=F32, **kw)


def _cparams(*sem):
    return pltpu.CompilerParams(dimension_semantics=sem, vmem_limit_bytes=VMEM_LIMIT_BYTES)


def _const_spec(shape):
    nd = len(shape)
    return pl.BlockSpec(shape, lambda *_: (0,) * nd, pipeline_mode=pl.Buffered(1))


_IN_WIDTHS = (NSA_Q_W,) + (NSA_KV_W,) * 6 + (3 * NSA_HEADS, 3 * GDN_W, GDN_HEADS, GDN_HEADS, GDN_W, 2 * D_MODEL)
(_C_Q, _C_CK, _C_CV, _C_SK, _C_SV, _C_WK, _C_WV, _C_GATE, _C_GQKV, _C_BETA, _C_DECAY, _C_GGATE, _C_MERGE,
 IN_WIDTH) = (int(v) for v in np.cumsum((0,) + _IN_WIDTHS))
_C_SMALL = _C_BETA // LANES * LANES
SM_BETA = _C_BETA - _C_SMALL
SM_DECAY = _C_DECAY - _C_SMALL
KEYS_COLS = 2 * NSA_KV_W
CMP_COLS = 2 * NSA_KV_W
_INPROJ_GROUPS = (("keys", KEYS_COLS, BF16), ("cmp", CMP_COLS, BF16), ("ggate", GDN_W, BF16),
                  ("merge", 2 * D_MODEL, BF16))
_INPROJ_WIDTH = sum(w for _, w, _ in _INPROJ_GROUPS)
_GDN_PROJ_WIDTH = 3 * GDN_W + LANES
_INPROJ_T_ROWS = NSA_Q_W + 4 * NSA_HEAD_DIM
_GATE_T_ROWS = 32
INPROJ_TM = 1024
INPROJ_TN = 512
WPREP_TK = 128


def _wprep_body(w_ref, rows_ref, gdn_ref, wt_ref, wg_ref):
    hd = NSA_HEAD_DIM
    feat = lambda c0, n: w_ref[0, c0:c0 + n, :]

    def put_t(ref, col, src):
        for r in range(0, src.shape[0], LANES):
            ref[:, col + r:col + r + LANES] = src[r:r + LANES].T.astype(BF16)

    for g in range(NSA_KV_GROUPS):
        put_t(rows_ref, 2 * g * hd, jnp.concatenate([feat(_C_SK + g * hd, hd), feat(_C_WK + g * hd, hd)], axis=0))
    c = 4 * hd
    for c0, n in ((_C_CK, 2 * NSA_KV_W), (_C_GGATE, GDN_W), (_C_MERGE, 2 * D_MODEL)):
        put_t(rows_ref, c, feat(c0, n))
        c += n
    put_t(gdn_ref, 0, feat(_C_GQKV, 3 * GDN_W))
    put_t(gdn_ref, 3 * GDN_W, feat(_C_SMALL, LANES))
    wt_ref[:NSA_Q_W, :] = feat(_C_Q, NSA_Q_W).astype(BF16)
    for j, c0 in enumerate((_C_SV, _C_WV, _C_SV + hd, _C_WV + hd)):
        wt_ref[NSA_Q_W + j * hd:NSA_Q_W + (j + 1) * hd, :] = feat(c0, hd).astype(BF16)
    wg_ref[...] = feat(_C_GATE, _GATE_T_ROWS).astype(BF16)


def _wprep(w_in_t, layer):
    k = w_in_t.shape[2]
    tk = WPREP_TK
    return pl.pallas_call(
        _wprep_body,
        grid=(k // tk,),
        in_specs=[pl.BlockSpec((1, IN_WIDTH, tk), lambda i: (layer, 0, i))],
        out_specs=[pl.BlockSpec((tk, _INPROJ_WIDTH), lambda i: (i, 0)),
                   pl.BlockSpec((tk, _GDN_PROJ_WIDTH), lambda i: (i, 0)),
                   pl.BlockSpec((_INPROJ_T_ROWS, tk), lambda i: (0, i)),
                   pl.BlockSpec((_GATE_T_ROWS, tk), lambda i: (0, i))],
        out_shape=[jax.ShapeDtypeStruct((k, _INPROJ_WIDTH), BF16), jax.ShapeDtypeStruct((k, _GDN_PROJ_WIDTH), BF16),
                   jax.ShapeDtypeStruct((_INPROJ_T_ROWS, k), BF16), jax.ShapeDtypeStruct((_GATE_T_ROWS, k), BF16)],
        compiler_params=_cparams("parallel"),
        name="wprep",
    )(w_in_t)


def _inproj_body(x_ref, w_ref, wt_ref, wg_ref, keys_ref, cmp_ref, ggate_ref, merge_ref, qvt_ref, gt_ref, cmp_s):
    x = x_ref[...].astype(BF16)
    outs = (keys_ref, None, ggate_ref, merge_ref)
    c0 = 0
    for ref, (name, width, _) in zip(outs, _INPROJ_GROUPS):
        for s in range(0, width, INPROJ_TN):
            e = min(s + INPROJ_TN, width)
            res = _dot(x, w_ref[:, c0 + s:c0 + e])
            if name == "cmp":
                for j in range(width // LANES):
                    cmp_s[j] = res[:, j * LANES:(j + 1) * LANES]
            else:
                ref[:, s:e] = res.astype(ref.dtype)
        c0 += width
    nblk = cmp_ref.shape[0]
    for l in range(CMP_STRIDE):
        for j in range(cmp_s.shape[0]):
            cmp_ref[:, l * CMP_COLS + j * LANES:l * CMP_COLS + (j + 1) * LANES] = (
                cmp_s[j, pl.ds(l, nblk, stride=CMP_STRIDE), :].astype(BF16))
    for s in range(0, _INPROJ_T_ROWS, 2 * LANES):
        qvt_ref[s:s + 2 * LANES, :] = _dot_nt(wt_ref[s:s + 2 * LANES, :], x).astype(qvt_ref.dtype)
    gt_ref[...] = _dot_nt(wg_ref[...], x)


def _inproj(x2, w_rows, w_t, w_g):
    m = x2.shape[0]
    tm = INPROJ_TM
    row_major = [(n, wd, dt) for n, wd, dt in _INPROJ_GROUPS if n != "cmp"]
    specs = {n: (pl.BlockSpec((tm, wd), lambda i: (i, 0)), jax.ShapeDtypeStruct((m, wd), dt))
             for n, wd, dt in row_major}
    specs["cmp"] = (pl.BlockSpec((tm // CMP_STRIDE, CMP_STRIDE * CMP_COLS), lambda i: (i, 0)),
                    jax.ShapeDtypeStruct((m // CMP_STRIDE, CMP_STRIDE * CMP_COLS), BF16))
    order = [n for n, _, _ in _INPROJ_GROUPS]
    return pl.pallas_call(
        _inproj_body,
        grid=(m // tm,),
        in_specs=[pl.BlockSpec((tm, D_MODEL), lambda i: (i, 0)), _const_spec(w_rows.shape),
                  _const_spec(w_t.shape), _const_spec(w_g.shape)],
        out_specs=[specs[n][0] for n in order]
        + [pl.BlockSpec((_INPROJ_T_ROWS, tm), lambda i: (0, i)), pl.BlockSpec((_GATE_T_ROWS, tm), lambda i: (0, i))],
        out_shape=[specs[n][1] for n in order]
        + [jax.ShapeDtypeStruct((_INPROJ_T_ROWS, m), BF16), jax.ShapeDtypeStruct((_GATE_T_ROWS, m), F32)],
        scratch_shapes=[pltpu.VMEM((CMP_COLS // LANES, tm, LANES), F32)],
        compiler_params=_cparams("parallel"),
        name="inproj",
    )(x2, w_rows, w_t, w_g)


def _compress_weights(cmp_pos, cmp_w2):
    hd, half = NSA_HEAD_DIM, CMP_LEN // 2
    posr = cmp_pos.reshape(2, 2, half, hd)
    post = jnp.broadcast_to(posr.transpose(1, 2, 0, 3)[:, :, :, None, :], (2, half, 2, 2, hd))
    post = jnp.concatenate([post.reshape(2, half * 4 * hd),
                            jnp.zeros((SUBLANES - 2, half * 4 * hd), cmp_pos.dtype)], axis=0)
    w2sel = jnp.zeros((2, 2, 2 * hd, LANES), cmp_w2.dtype)
    for g in range(2):
        w2sel = w2sel.at[:, g, g * hd:(g + 1) * hd, :hd].set(cmp_w2)
    return post.astype(BF16), w2sel.reshape(4, 2 * hd, LANES).astype(BF16)


def _compress_body(t_ref, w1_ref, pos_ref, w2_ref, aug_ref, out_ref, w1e_s):
    hd, half_len = NSA_HEAD_DIM, CMP_LEN // 2

    @pl.when(pl.program_id(0) == 0)
    def _():
        w1e_s[...] = jnp.zeros(w1e_s.shape, BF16)
        for which in range(2):
            for half in range(2):
                for l in range(half_len):
                    blk = w1_ref[which, (half * half_len + l) * hd:(half * half_len + l + 1) * hd, :].astype(BF16)
                    for g in range(NSA_KV_GROUPS):
                        r0 = l * CMP_COLS + which * LANES + g * hd
                        c0 = half * CMP_COLS + which * LANES + g * hd
                        w1e_s[r0:r0 + hd, c0:c0 + hd] = blk

    p = _dot(t_ref[0], w1e_s[...])
    pp = _dot(pos_ref[...], w1e_s[...])
    nxt = pltpu.roll(p[:, CMP_COLS:], p.shape[0] - 1, axis=0)
    pre = p[:, :CMP_COLS] + nxt + pp[0:1, :CMP_COLS] + pp[1:2, CMP_COLS:]
    h = jax.nn.gelu(pre).astype(BF16)
    n_idx = lax.broadcasted_iota(jnp.int32, (p.shape[0], LANES), 0)
    real = n_idx < p.shape[0] - 1
    for which in range(2):
        hw = h[:, which * LANES:(which + 1) * LANES]
        for g in range(2):
            o = jnp.where(real, _dot(hw, w2_ref[which * 2 + g]) + aug_ref[which], 0.0)
            out_ref[0, which * 2 + g] = (o if which == 0 else o.T).astype(out_ref.dtype)


def _compress(t2, w1, post, w2sel, aug):
    b, nblk, _ = t2.shape
    return pl.pallas_call(
        _compress_body,
        grid=(b,),
        in_specs=[pl.BlockSpec((1, nblk, CMP_STRIDE * CMP_COLS), lambda i: (i, 0, 0)),
                  _const_spec(w1.shape), _const_spec(post.shape), _const_spec(w2sel.shape), _const_spec(aug.shape)],
        out_specs=pl.BlockSpec((1, 4, nblk, LANES), lambda i: (i, 0, 0, 0)),
        out_shape=jax.ShapeDtypeStruct((b, 4, nblk, LANES), BF16),
        scratch_shapes=[pltpu.VMEM((CMP_STRIDE * CMP_COLS, 2 * CMP_COLS), BF16)],
        compiler_params=_cparams("arbitrary"),
        name="compress",
    )(t2, w1, post, w2sel, aug)


def _nsa_consts(s):
    hd, rep = NSA_HEAD_DIM, NSA_REP
    t = np.arange(s)
    kx_win = np.zeros((s + WIN_LEN, hd), np.float32)
    kx_win[WIN_LEN + t, AUG_POS_HI - hd] = t // POS_SPLIT
    kx_win[WIN_LEN + t, AUG_POS_LO - hd] = t % POS_SPLIT
    kx_win[:WIN_LEN, AUG_PAD - hd] = 1.0
    kx_sel = kx_win.copy()
    kx_sel[WIN_LEN + t, t // SLC_LEN] = 1.0
    vx_win = np.zeros((V_ROWS - hd, s + WIN_LEN), np.float32)
    vx_win[0, WIN_LEN:] = 1.0
    vx_sel = vx_win
    n_cmp = s // CMP_STRIDE
    cmp_aug = np.zeros((2, n_cmp, LANES), np.float32)
    end = np.arange(n_cmp) * CMP_STRIDE + CMP_LEN - 1
    cmp_aug[0, :, AUG_POS_HI] = end // POS_SPLIT
    cmp_aug[0, :, AUG_POS_LO] = end % POS_SPLIT
    qx = np.zeros((NSA_KV_GROUPS, LANES - AUG_POS_HI, rep * Q_TILE), np.float32)
    for h in range(NSA_HEADS):
        slope = 2.0 ** (-8.0 * (h + 1) / NSA_HEADS)
        lanes = slice((h % rep) * Q_TILE, (h % rep + 1) * Q_TILE)
        qx[h // rep, 0, lanes] = slope * POS_SPLIT
        qx[h // rep, 1, lanes] = slope
        qx[h // rep, AUG_PAD - AUG_POS_HI, lanes] = -BIG
    c0 = np.arange(n_cmp)[None, :] * CMP_STRIDE
    s0 = np.arange(s // SLC_LEN)[:, None] * SLC_LEN
    ov_t = ((c0 < s0 + SLC_LEN) & (c0 + CMP_LEN > s0)).astype(np.float32)
    ov_t[:, (s - CMP_LEN) // CMP_STRIDE + 1:] = 0.0
    kk = np.arange(Q_TILE)[:, None]
    qq = np.arange(Q_TILE)[None, :]
    causal = np.tile(np.where(kk <= qq, 0.0, NEG).astype(np.float32), (1, rep))
    after = np.tile(np.where(kk > qq, 0.0, NEG).astype(np.float32), (1, rep))
    j = jnp.asarray
    return dict(kx_sel=j(kx_sel, BF16), kx_win=j(kx_win, BF16), vx_sel=j(vx_sel, BF16), vx_win=j(vx_win, BF16),
                cmp_aug=j(cmp_aug), qx=j(qx), ov_t=j(ov_t), causal=j(causal), after=j(after))


def _nsa_body(qt_ref, k_ref, vt_ref, kc_ref, vct_ref, gt_ref, kxs_ref, kxw_ref, vxs_ref, vxw_ref, qx_ref, ovt_ref,
              causal_ref, after_ref, out_ref, ks_s, kw_s, vs_s, vw_s):
    hd, rep, tq = NSA_HEAD_DIM, NSA_REP, Q_TILE
    nq = rep * tq
    i = pl.program_id(2)

    @pl.when(i == 0)
    def _():
        keys = k_ref[0]
        ks_s[:WIN_LEN, :hd] = jnp.zeros((WIN_LEN, hd), BF16)
        ks_s[WIN_LEN:, :hd] = keys[:, :hd]
        ks_s[:, hd:] = kxs_ref[...]
        kw_s[:WIN_LEN, :hd] = jnp.zeros((WIN_LEN, hd), BF16)
        kw_s[WIN_LEN:, :hd] = keys[:, hd:]
        kw_s[:, hd:] = kxw_ref[...]
        vals = vt_ref[...]
        vs_s[:hd, :WIN_LEN] = jnp.zeros((hd, WIN_LEN), BF16)
        vs_s[:hd, WIN_LEN:] = vals[:hd]
        vs_s[hd:, :] = vxs_ref[...]
        vw_s[:hd, :WIN_LEN] = jnp.zeros((hd, WIN_LEN), BF16)
        vw_s[:hd, WIN_LEN:] = vals[hd:]
        vw_s[hd:, :] = vxw_ref[...]

    qx = qx_ref[0]
    sg_all = jax.nn.sigmoid(gt_ref[...])
    grp = pl.program_id(1)

    full_past = WIN_LEN // tq

    def front(sub, res, it, past):
        qt = qt_ref[:, sub * tq:(sub + 1) * tq]
        q64 = jnp.concatenate([qt[r * hd:(r + 1) * hd, :] for r in range(rep)], axis=1).astype(F32) * (hd ** -0.5)

        def q_aug(sel_rows):
            return jnp.concatenate([q64, sel_rows, qx], axis=0).astype(BF16)

        n_row = lax.broadcasted_iota(jnp.int32, (LANES, nq), 0)
        t_lane = it * tq + (lax.broadcasted_iota(jnp.int32, (LANES, nq), 1) & (tq - 1))
        valid = t_lane >= n_row * CMP_STRIDE + (CMP_LEN - 1)
        qa0 = q_aug(jnp.zeros((N_SLC, nq), F32))
        sc = jnp.where(valid, _dot(kc_ref[0, 0], qa0), NEG)
        mc = jnp.max(sc, axis=0, keepdims=True)
        ec = jnp.where(valid, jnp.exp(sc - mc), 0.0)
        lc = jnp.sum(ec, axis=0, keepdims=True)
        pc = ec * jnp.where(lc > 0.0, 1.0 / lc, 0.0)
        o_cmp = _dot(vct_ref[0, 0], pc.astype(BF16))[:hd]
        psum = pc[:, 0:tq] + pc[:, tq:2 * tq] + pc[:, 2 * tq:3 * tq] + pc[:, 3 * tq:4 * tq]
        score_t = _dot(ovt_ref[...], psum, precision=lax.Precision.HIGHEST)
        yield

        n_keys = (past + 1) * tq
        if past == full_past:
            w0 = pl.multiple_of(it * tq, tq)
        else:
            w0 = WIN_LEN
        diag = past * tq

        def one_shot_masks(s):
            parts = [s[:diag], s[diag:] + causal_ref[...]] if past else [s + causal_ref[...]]
            if past == full_past:
                parts = [s[:tq] + after_ref[...], s[tq:diag], parts[1]]
            return jnp.concatenate(parts, axis=0) if len(parts) > 1 else parts[0]

        s_w = one_shot_masks(_dot(kw_s[pl.ds(w0, n_keys), :], qa0))
        p_w = jnp.exp(s_w - jnp.max(s_w, axis=0, keepdims=True))
        acc_w = _dot(vw_s[:, pl.ds(w0, n_keys)], p_w.astype(BF16))
        o_win = acc_w[:hd] * (1.0 / acc_w[hd:hd + 1])
        yield

        jb = lax.broadcasted_iota(jnp.int32, (N_SLC, tq), 0)
        cur = (it * tq + lax.broadcasted_iota(jnp.int32, (N_SLC, tq), 1)) // SLC_LEN
        forced = (jb == 0) | (jb == cur) | (jb == cur - 1)
        score_t = jnp.where(forced, FORCE_SCORE, jnp.where(jb <= cur, score_t, -1.0))
        rank = jnp.zeros((N_SLC, tq), F32)
        for jp in range(N_SLC):
            other = score_t[jp:jp + 1, :]
            ge = jnp.where(other >= score_t, 1.0, 0.0)
            gt = jnp.where(other > score_t, 1.0, 0.0)
            rank = rank + jnp.where(jb > jp, ge, gt)
        sel = rank < float(SLC_TOPK)
        qa = q_aug(jnp.concatenate([jnp.where(sel, 0.0, -BIG)] * rep, axis=1))
        lo_blk = jnp.min(jnp.where(sel & (jb >= 2) & (jb <= cur), jb.astype(F32), float(N_SLC)))
        lo_key = (lo_blk.astype(jnp.int32) // 2) * tq
        yield

        s_main = _dot(ks_s[pl.ds(w0, n_keys), :], qa)
        if past < full_past:
            s_s = jnp.concatenate([s_main[:diag], s_main[diag:] + causal_ref[...]], axis=0) if past else (
                s_main + causal_ref[...])
            m_s = jnp.max(s_s, axis=0, keepdims=True)
            acc_s = _dot(vs_s[:, pl.ds(w0, n_keys)], jnp.exp(s_s - m_s).astype(BF16))
            res.update(early=False, acc_s=acc_s, o_cmp=o_cmp, o_win=o_win)
            return
        e_key = it * tq - WIN_LEN
        t0 = pl.multiple_of(jnp.where(e_key > 0, WIN_LEN, 0), tq)
        s_s = jnp.concatenate([_dot(ks_s[pl.ds(t0, tq), :], qa), s_main[:diag],
                               s_main[diag:] + causal_ref[...]], axis=0)
        m_s = jnp.max(s_s, axis=0, keepdims=True)
        p_s = jnp.exp(s_s - m_s).astype(BF16)
        acc_s = _dot(vs_s[:, pl.ds(t0, tq)], p_s[:tq]) + _dot(vs_s[:, pl.ds(w0, n_keys)], p_s[tq:])
        c_hi = (e_key - tq + SEL_KC - 1) // SEL_KC
        c_lo = jnp.where(lo_key < e_key, (lo_key - tq) // SEL_KC, c_hi)
        res.update(early=True, qa=qa, e_key=e_key, c_lo=c_lo, c_hi=c_hi, m_s=m_s, acc_s=acc_s, o_cmp=o_cmp,
                   o_win=o_win)

    def tail(sub, f):
        def early_step(c, carry):
            qa, e_key = f["qa"], f["e_key"]
            m, acc = carry
            k0 = tq + c * SEL_KC
            start = pl.multiple_of(WIN_LEN + k0, tq)
            k_abs = k0 + lax.broadcasted_iota(jnp.int32, (SEL_KC, nq), 0)
            s = jnp.where(k_abs < e_key, _dot(ks_s[pl.ds(start, SEL_KC), :], qa), NEG)
            m_new = jnp.maximum(m, jnp.max(s, axis=0, keepdims=True))
            p = jnp.exp(s - m_new).astype(BF16)
            return m_new, acc * jnp.exp(m - m_new) + _dot(vs_s[:, pl.ds(start, SEL_KC)], p)

        acc_s = f["acc_s"]
        if f["early"]:
            _, acc_s = lax.fori_loop(f["c_lo"], f["c_hi"], early_step, (f["m_s"], acc_s))
        o_slc = acc_s[:hd] * (1.0 / acc_s[hd:hd + 1])

        sg = sg_all[:, sub * tq:(sub + 1) * tq]
        gate = lambda br, r: jnp.where(grp == 0, sg[br * NSA_HEADS + r:br * NSA_HEADS + r + 1],
                                       sg[br * NSA_HEADS + rep + r:br * NSA_HEADS + rep + r + 1])
        for pair in range(rep // 2):
            halves = []
            for r in (2 * pair, 2 * pair + 1):
                lanes = slice(r * tq, (r + 1) * tq)
                halves.append(gate(0, r) * f["o_cmp"][:, lanes] + gate(1, r) * o_slc[:, lanes]
                              + gate(2, r) * f["o_win"][:, lanes])
            out_ref[0, sub * tq:(sub + 1) * tq, pair * LANES:(pair + 1) * LANES] = (
                jnp.concatenate(halves, axis=0).T.astype(out_ref.dtype))

    def run(tiles):
        fronts = [{} for _ in range(NSA_SUB)]
        for sub, (it, past) in enumerate(tiles):
            for _ in front(sub, fronts[sub], it, past):
                pass
        for sub in range(NSA_SUB):
            tail(sub, fronts[sub])

    n_short = -(-full_past // NSA_SUB)
    for step in range(n_short):
        @pl.when(i == step)
        def _(step=step):
            run([(step * NSA_SUB + sub, min(step * NSA_SUB + sub, full_past)) for sub in range(NSA_SUB)])

    @pl.when(i >= n_short)
    def _():
        run([(i * NSA_SUB + sub, full_past) for sub in range(NSA_SUB)])


def _nsa(qvt, keys, cmp_kv, gt, consts, b, s):
    tqs = NSA_SUB * Q_TILE
    nt = s // tqs
    c = consts
    in_specs = [
        pl.BlockSpec((2 * LANES, tqs), lambda bi, g, i: (g, bi * nt + i)),
        pl.BlockSpec((1, s, LANES), lambda bi, g, i: (bi, 0, g)),
        pl.BlockSpec((LANES, s), lambda bi, g, i: (NSA_Q_W // LANES + g, bi)),
        pl.BlockSpec((1, 1, s // CMP_STRIDE, LANES), lambda bi, g, i: (bi, g, 0, 0)),
        pl.BlockSpec((1, 1, s // CMP_STRIDE, LANES), lambda bi, g, i: (bi, 2 + g, 0, 0)),
        pl.BlockSpec((_GATE_T_ROWS, tqs), lambda bi, g, i: (0, bi * nt + i)),
        _const_spec(c["kx_sel"].shape), _const_spec(c["kx_win"].shape), _const_spec(c["vx_sel"].shape),
        _const_spec(c["vx_win"].shape),
        pl.BlockSpec((1,) + c["qx"].shape[1:], lambda bi, g, i: (g, 0, 0)),
        _const_spec(c["ov_t"].shape), _const_spec(c["causal"].shape), _const_spec(c["after"].shape),
    ]
    return pl.pallas_call(
        _nsa_body,
        grid=(b, NSA_KV_GROUPS, nt),
        in_specs=in_specs,
        out_specs=pl.BlockSpec((1, tqs, 2 * LANES), lambda bi, g, i: (bi, i, g)),
        out_shape=jax.ShapeDtypeStruct((b, s, NSA_Q_W), BF16),
        scratch_shapes=[pltpu.VMEM((s + WIN_LEN, LANES), BF16), pltpu.VMEM((s + WIN_LEN, LANES), BF16),
                        pltpu.VMEM((V_ROWS, s + WIN_LEN), BF16), pltpu.VMEM((V_ROWS, s + WIN_LEN), BF16)],
        compiler_params=_cparams("parallel", "parallel", "arbitrary"),
        name="nsa",
    )(qvt, keys, qvt, cmp_kv, cmp_kv, gt, c["kx_sel"], c["kx_win"], c["vx_sel"], c["vx_win"], c["qx"], c["ov_t"],
      c["causal"], c["after"])


GDN_TS = 512
GDN_BLK = 128
GDN_HALO = 8
GDN_XHALO = 16
GDN_INV_BASE = 8
GDN_CHAIN_UNITS_PER_PREP_UNIT = 4
GDN_SCAN_ROWS = GDN_HEAD_DIM + GDN_CHUNK


def _gdn_prep(x_ref, prev_ref, w_ref, cw_ref, alog_ref, dtb_ref, lt_ref, first_tile, xp_s, buf):
    x_s, p_s, rhs_s, qg_s, aqk_s, kdt_s, a_s, eg_s = buf
    ts, dh, nh, blk = GDN_TS, GDN_HEAD_DIM, GDN_HEADS, GDN_BLK
    proj = _dot(x_ref[0].astype(BF16), w_ref[...])
    hist = _dot(prev_ref[0].astype(BF16), w_ref[:, :3 * GDN_W])[GDN_XHALO - GDN_HALO:]
    xp_s[0:GDN_HALO, :] = jnp.where(first_tile, 0.0, hist)
    xp_s[GDN_HALO:, :] = proj[:, :3 * GDN_W]
    yield
    act = []
    for blk_i in range(3 * nh):
        lanes = slice(blk_i * dh, (blk_i + 1) * dh)
        xp = xp_s[:, lanes]
        conv = cw_ref[0:1, lanes] * xp
        for j in range(1, GDN_CONV):
            conv = pltpu.roll(conv, 1, axis=0) + cw_ref[j:j + 1, lanes] * xp
        act.append(jax.nn.silu(conv[GDN_HALO:]))
        yield

    sm = proj[:, 3 * GDN_W:]
    beta = jax.nn.sigmoid(sm)
    g = -jnp.exp(alog_ref[...]) * jax.nn.softplus(sm + dtb_ref[...])
    gcum = jnp.concatenate([_dot(lt_ref[...], g[r:r + blk], precision=lax.Precision.HIGHEST)
                            for r in range(0, ts, blk)], axis=0)
    eg = jnp.exp(gcum)
    eg_s[...] = eg

    ri = lax.broadcasted_iota(jnp.int32, (blk, blk), 0)
    ci = lax.broadcasted_iota(jnp.int32, (blk, blk), 1)
    same = (ri // GDN_CHUNK) == (ci // GDN_CHUNK)
    causal = same & (ri >= ci)
    strict = same & (ri > ci)
    eye = (ri == ci).astype(F32)
    same_base = (ri // GDN_INV_BASE) == (ci // GDN_INV_BASE)

    for pb in range(ts // blk):
        rows = slice(pb * blk, (pb + 1) * blk)
        gc = gcum[rows]
        gc_t = gc.T
        first = lax.broadcasted_iota(jnp.int32, (blk, LANES), 0) < GDN_CHUNK
        g_last = jnp.where(first, gc[GDN_CHUNK - 1:GDN_CHUNK, :], gc[blk - 1:blk, :])
        e_dec = jnp.exp(g_last - gc)
        for h in range(nh):
            q, k, v = act[h][rows], act[nh + h][rows], act[2 * nh + h][rows]
            q = q * lax.rsqrt(jnp.sum(q * q, axis=1, keepdims=True) + RMS_EPS) * (dh ** -0.5)
            k = k * lax.rsqrt(jnp.sum(k * k, axis=1, keepdims=True) + RMS_EPS)
            b_col = beta[rows, SM_BETA + h:SM_BETA + h + 1]
            eg_col = eg[rows, SM_DECAY + h:SM_DECAY + h + 1]
            gdiff = gc[:, SM_DECAY + h:SM_DECAY + h + 1] - gc_t[SM_DECAY + h:SM_DECAY + h + 1, :]
            decay = jnp.exp(jnp.where(causal, gdiff, NEG))
            kb = k * b_col
            kbf, kf, qf = kb.astype(BF16), k.astype(BF16), q.astype(BF16)
            a = jnp.where(strict, -_dot_nt(kbf, kf) * decay, 0.0)
            c = pb * nh + h
            a_base = jnp.where(same_base, a, 0.0)
            a_s[c] = a.astype(BF16)
            x_s[c] = a_base.astype(BF16)
            p_s[c] = eye + a_base
            rhs_s[c] = jnp.concatenate([v * b_col, kb * eg_col], axis=1).astype(BF16)
            qg_s[c] = q * eg_col
            aqk_s[c] = jnp.where(causal, _dot_nt(qf, kf) * decay, 0.0).astype(BF16)
            kdt_s[c] = (k * e_dec[:, SM_DECAY + h:SM_DECAY + h + 1]).T.astype(BF16)
            yield


def _gdn_chains(buf, nq_ref, co_ref, eg_ref):
    x_s, p_s, rhs_s, qg_s, aqk_s, kdt_s, a_s, eg_s = buf
    dh, nh, blk = GDN_HEAD_DIM, GDN_HEADS, GDN_BLK
    n_chain = x_s.shape[0]
    eg_ref[0] = eg_s[...]
    ri = lax.broadcasted_iota(jnp.int32, (blk, blk), 0)
    ci = lax.broadcasted_iota(jnp.int32, (blk, blk), 1)
    same = lambda size: (ri // size) == (ci // size)
    for c in range(n_chain):
        y = x_s[c]
        x_s[c] = _dot(y, y).astype(BF16)
        yield
    for c in range(n_chain):
        y2 = x_s[c]
        p = p_s[c]
        p_s[c] = p + _dot(p.astype(BF16), y2)
        x_s[c] = _dot(y2, y2).astype(BF16)
        yield
    for c in range(n_chain):
        p = p_s[c]
        p_s[c] = p + _dot(p.astype(BF16), x_s[c])
        yield
    size = 2 * GDN_INV_BASE
    while size <= GDN_CHUNK:
        between = same(size) & jnp.logical_not(same(size // 2))
        for c in range(n_chain):
            a = a_s[c]
            x_s[c] = _dot(p_s[c].astype(BF16), jnp.where(between, a, jnp.zeros_like(a))).astype(BF16)
            yield
        for c in range(n_chain):
            t = p_s[c]
            p_s[c] = t + _dot(x_s[c], t.astype(BF16))
            yield
        size *= 2

    tok_half = lax.broadcasted_iota(jnp.int32, (blk, blk), 1) // GDN_CHUNK
    for c in range(n_chain):
        rhs_s[c] = _dot(p_s[c].astype(BF16), rhs_s[c]).astype(BF16)
        yield
    for c in range(n_chain):
        pb, h = divmod(c, nh)
        uw = rhs_s[c]
        a1 = _dot(aqk_s[c], uw)
        q_loc = qg_s[c] - a1[:, dh:]
        kdt = kdt_s[c]
        for half in range(blk // GDN_CHUNK):
            k1 = _dot(jnp.where(tok_half == half, kdt, jnp.zeros_like(kdt)), uw)
            n = pb * (blk // GDN_CHUNK) + half
            rows = slice(half * GDN_CHUNK, (half + 1) * GDN_CHUNK)
            nq_ref[0, h, n, :dh, :] = (-k1[:, dh:]).astype(BF16)
            nq_ref[0, h, n, dh:, :] = q_loc[rows].astype(BF16)
            co_ref[0, h, n, :dh, :] = k1[:, :dh].astype(BF16)
            co_ref[0, h, n, dh:, :] = a1[rows, :dh].astype(BF16)
        yield


def _interleave(major, minor, minor_per_major):
    for _ in major:
        for _ in range(minor_per_major):
            next(minor, None)
    for _ in minor:
        pass


def _gdn_intra_body(x_ref, prev_ref, w_ref, cw_ref, alog_ref, dtb_ref, lt_ref, nq_ref, co_ref, eg_ref, xp_s, *bufs,
                    tiles_per_seq, n_buf):
    j = pl.program_id(0)
    sets = (bufs[:n_buf], bufs[n_buf:])

    @pl.when(j == 0)
    def _():
        for ref in sets[1]:
            ref[...] = jnp.zeros(ref.shape, ref.dtype)

    first_tile = (j % tiles_per_seq) == 0
    for parity in range(2):
        @pl.when(j % 2 == parity)
        def _(parity=parity):
            chains = _gdn_chains(sets[1 - parity], nq_ref, co_ref, eg_ref)
            prep = _gdn_prep(x_ref, prev_ref, w_ref, cw_ref, alog_ref, dtb_ref, lt_ref, first_tile, xp_s,
                             sets[parity])
            _interleave(prep, chains, GDN_CHAIN_UNITS_PER_PREP_UNIT)


def _gdn_intra(x, w_gdn, conv_w, alog_l, dtb_l, lt):
    b, s, d = x.shape
    ts, nh, dh = GDN_TS, GDN_HEADS, GDN_HEAD_DIM
    tps = s // ts
    nt = b * tps
    n_chain = (ts // GDN_BLK) * nh
    sq = lambda dt: pltpu.VMEM((n_chain, GDN_BLK, GDN_BLK), dt)
    buf = lambda: [sq(BF16), sq(F32), pltpu.VMEM((n_chain, GDN_BLK, 2 * dh), BF16), sq(F32), sq(BF16), sq(BF16),
                   sq(BF16), pltpu.VMEM((ts, LANES), F32)]
    src = lambda j: jnp.minimum(j, nt - 1)
    dst = lambda j: jnp.maximum(j - 1, 0)
    cspec = lambda: pl.BlockSpec((1, nh, ts // GDN_CHUNK, GDN_SCAN_ROWS, dh),
                                 lambda j: (dst(j) // tps, 0, dst(j) % tps, 0, 0))
    cshape = jax.ShapeDtypeStruct((b, nh, s // GDN_CHUNK, GDN_SCAN_ROWS, dh), BF16)
    return pl.pallas_call(
        functools.partial(_gdn_intra_body, tiles_per_seq=tps, n_buf=len(buf())),
        grid=(nt + 1,),
        in_specs=[
            pl.BlockSpec((1, ts, d), lambda j: (src(j) // tps, src(j) % tps, 0)),
            pl.BlockSpec((1, GDN_XHALO, d),
                         lambda j: (src(j) // tps, jnp.maximum((src(j) % tps) * (ts // GDN_XHALO) - 1, 0), 0)),
            _const_spec(w_gdn.shape), _const_spec(conv_w.shape), _const_spec(alog_l.shape), _const_spec(dtb_l.shape),
            _const_spec(lt.shape),
        ],
        out_specs=[cspec(), cspec(), pl.BlockSpec((1, ts, LANES), lambda j: (dst(j) // tps, dst(j) % tps, 0))],
        out_shape=[cshape, cshape, jax.ShapeDtypeStruct((b, s, LANES), F32)],
        scratch_shapes=[pltpu.VMEM((ts + GDN_HALO, 3 * GDN_W), F32)] + buf() + buf(),
        compiler_params=_cparams("arbitrary"),
        name="gdn_intra",
    )(x, x, w_gdn, conv_w, alog_l, dtb_l, lt)


def _gdn_scan_body(nq_ref, co_ref, eg_ref, gate_ref, nw_ref, out_ref, st_s):
    nh, dh, ck = GDN_HEADS, GDN_HEAD_DIM, GDN_CHUNK
    st_s[...] = jnp.zeros(st_s.shape, F32)

    def chunk(n, carry):
        r0 = pl.multiple_of(n * ck, ck)
        d_row = eg_ref[0, pl.ds(r0 + ck - 1, 1), :]
        for h in range(nh):
            st = st_s[h]
            res = _dot(nq_ref[0, h, n], st.astype(BF16)) + co_ref[0, h, n].astype(F32)
            st_s[h] = st * d_row[:, SM_DECAY + h:SM_DECAY + h + 1] + res[:dh]
            o = res[dh:]
            ms = jnp.mean(o * o, axis=1, keepdims=True)
            gt = gate_ref[0, pl.ds(r0, ck), h * dh:(h + 1) * dh].astype(F32)
            out_ref[0, pl.ds(r0, ck), h * dh:(h + 1) * dh] = (
                o * lax.rsqrt(ms + RMS_EPS) * nw_ref[...] * jax.nn.silu(gt)).astype(out_ref.dtype)
        return carry

    lax.fori_loop(0, nq_ref.shape[2], chunk, 0, unroll=16)


def _gdn_scan(nq, co, eg, ggate, norm_w):
    b, nh, nc, rows, dh = nq.shape
    s = nc * GDN_CHUNK
    cspec = lambda: pl.BlockSpec((1, nh, nc, rows, dh), lambda bi: (bi, 0, 0, 0, 0))
    return pl.pallas_call(
        _gdn_scan_body,
        grid=(b,),
        in_specs=[cspec(), cspec(),
                  pl.BlockSpec((1, s, LANES), lambda bi: (bi, 0, 0)),
                  pl.BlockSpec((1, s, GDN_W), lambda bi: (bi, 0, 0)),
                  _const_spec(norm_w.shape)],
        out_specs=pl.BlockSpec((1, s, GDN_W), lambda bi: (bi, 0, 0)),
        out_shape=jax.ShapeDtypeStruct((b, s, GDN_W), BF16),
        scratch_shapes=[pltpu.VMEM((nh, dh, dh), F32)],
        compiler_params=_cparams("parallel"),
        name="gdn_scan",
    )(nq, co, eg, ggate, norm_w)


MERGE_TM = 1024


def _layer_norm(y, g, b):
    mu = jnp.mean(y, axis=1, keepdims=True)
    d = y - mu
    var = jnp.mean(d * d, axis=1, keepdims=True)
    return d * lax.rsqrt(var + LN_EPS) * g + b


def _merge_body(x_ref, oa_ref, ob_ref, ga_ref, gb_ref, wa_ref, wb_ref, wo_ref, g_ref, b_ref, y_ref, yb_ref):
    ya = _dot(oa_ref[...], wa_ref[...])
    yb = _dot(ob_ref[...], wb_ref[...])
    mixin = jax.nn.sigmoid(ga_ref[...].astype(F32)) * ya + jax.nn.sigmoid(gb_ref[...].astype(F32)) * yb
    mix = _dot(mixin.astype(BF16), wo_ref[...])
    y = _layer_norm(DEEPNORM_ALPHA * x_ref[...] + mix, g_ref[...], b_ref[...])
    y_ref[...] = y
    yb_ref[...] = y.astype(BF16)


def _merge(x2, oa, ob, mgate, wa, wb, wo, g, b):
    m = x2.shape[0]
    tm, d = MERGE_TM, D_MODEL
    row = lambda wd, col=0: pl.BlockSpec((tm, wd), lambda i, col=col: (i, col))
    return pl.pallas_call(
        _merge_body,
        grid=(m // tm,),
        in_specs=[row(d), row(NSA_Q_W), row(GDN_W), row(d, 0), row(d, 1),
                  _const_spec(wa.shape), _const_spec(wb.shape), _const_spec(wo.shape),
                  _const_spec(g.shape), _const_spec(b.shape)],
        out_specs=[row(d), row(d)],
        out_shape=[jax.ShapeDtypeStruct((m, d), F32), jax.ShapeDtypeStruct((m, d), BF16)],
        compiler_params=_cparams("parallel"),
        name="merge",
    )(x2, oa, ob, mgate, mgate, wa, wb, wo, g, b)


FFN_TM = 1024
FFN_HALO = 16
FFN_CK = 256


def _ffn_body(x_ref, xb_ref, prev_ref, wu_ref, cw_ref, wd_ref, g_ref, b_ref, out_ref, act_s,
              *, tiles_per_seq):
    i = pl.program_id(0)
    prev = prev_ref[...]
    prev = jnp.where(i % tiles_per_seq == 0, jnp.zeros_like(prev), prev)
    xc = jnp.concatenate([prev, xb_ref[...]], axis=0)

    def conv(h, c0):
        out = cw_ref[FFN_CONV - 1:FFN_CONV, c0:c0 + FFN_CK] * h[FFN_HALO:]
        for j in range(FFN_CONV - 1):
            shifted = pltpu.roll(h, FFN_CONV - 1 - j, axis=0)[FFN_HALO:]
            out = out + cw_ref[j:j + 1, c0:c0 + FFN_CK] * shifted
        return out

    for c in range(FFN_DIM // FFN_CK):
        c0 = c * FFN_CK
        hg = conv(_dot(xc, wu_ref[:, c0:c0 + FFN_CK]), c0)
        hv = conv(_dot(xc, wu_ref[:, FFN_DIM + c0:FFN_DIM + c0 + FFN_CK]), FFN_DIM + c0)
        act_s[:, c0:c0 + FFN_CK] = (jax.nn.silu(hg) * hv).astype(BF16)
    f = _dot(act_s[...], wd_ref[...])
    out_ref[...] = _layer_norm(DEEPNORM_ALPHA * x_ref[...] + f, g_ref[...], b_ref[...])


def _ffn(x1, x1b, wu, cw, wd, g, b, seq):
    m = x1.shape[0]
    tm, d = FFN_TM, D_MODEL
    return pl.pallas_call(
        functools.partial(_ffn_body, tiles_per_seq=seq // tm),
        grid=(m // tm,),
        in_specs=[pl.BlockSpec((tm, d), lambda i: (i, 0)),
                  pl.BlockSpec((tm, d), lambda i: (i, 0)),
                  pl.BlockSpec((FFN_HALO, d), lambda i: (jnp.maximum(i * (tm // FFN_HALO) - 1, 0), 0)),
                  _const_spec(wu.shape), _const_spec(cw.shape), _const_spec(wd.shape), _const_spec(g.shape), _const_spec(b.shape)],
        out_specs=pl.BlockSpec((tm, d), lambda i: (i, 0)),
        out_shape=jax.ShapeDtypeStruct((m, d), F32),
        scratch_shapes=[pltpu.VMEM((tm, FFN_DIM), BF16)],
        compiler_params=_cparams("parallel"),
        name="ffn",
    )(x1, x1b, x1b, wu, cw, wd, g, b)


def _lane_vec(vals, lane0):
    return jnp.zeros((1, LANES), F32).at[0, lane0:lane0 + vals.shape[0]].set(vals.astype(F32))


def _layer(x, w_in, cmp_pos, cmp_w1, cmp_w2, w_nsa_out, gdn_conv_w, gdn_a_log, gdn_dt_bias, gdn_norm_w,
           w_gdn_out, w_o, ln1_g, ln1_b, ffn_w_up, ffn_conv_w, ffn_w_down, ln2_g, ln2_b):
    b, s, d = x.shape
    m = b * s
    x2 = x.reshape(m, d)
    w_rows, w_gdn, w_t, w_g = _wprep(*w_in)
    keys, cmpkv, ggate, mgate, qvt, gt = _inproj(x2, w_rows, w_t, w_g)

    consts = _nsa_consts(s)
    post, w2sel = _compress_weights(cmp_pos, cmp_w2)
    cmp_kv = _compress(cmpkv.reshape(b, s // CMP_STRIDE, CMP_STRIDE * CMP_COLS), cmp_w1, post, w2sel,
                       consts["cmp_aug"])
    o_nsa = _nsa(qvt, keys.reshape(b, s, KEYS_COLS), cmp_kv, gt, consts, b, s)

    ck = GDN_CHUNK
    tri = np.tril(np.ones((ck, ck), np.float32))
    lt = jnp.asarray(np.kron(np.eye(GDN_BLK // ck, dtype=np.float32), tri))
    nq, co, eg = _gdn_intra(x, w_gdn, gdn_conv_w, _lane_vec(gdn_a_log, SM_DECAY), _lane_vec(gdn_dt_bias, SM_DECAY),
                            lt)
    o_gdn = _gdn_scan(nq, co, eg, ggate.reshape(b, s, GDN_W), gdn_norm_w.reshape(1, GDN_HEAD_DIM))

    x1, x1b = _merge(x2, o_nsa.reshape(m, NSA_Q_W), o_gdn.reshape(m, GDN_W), mgate,
                     w_nsa_out.astype(BF16), w_gdn_out.astype(BF16), w_o.astype(BF16),
                     ln1_g.reshape(1, d), ln1_b.reshape(1, d))
    out = _ffn(x1, x1b, ffn_w_up.astype(BF16), ffn_conv_w, ffn_w_down.astype(BF16),
               ln2_g.reshape(1, d), ln2_b.reshape(1, d), s)
    return out.reshape(b, s, d)


def kernel(x, w_in, nsa_cmp_pos, nsa_cmp_w1, nsa_cmp_w2, w_nsa_out, gdn_conv_w, gdn_a_log, gdn_dt_bias, gdn_norm_w, w_gdn_out, w_o, ln1_g, ln1_b, ffn_w_up, ffn_conv_w, ffn_w_down, ln2_g, ln2_b):
    w_in_t = jnp.swapaxes(w_in, 1, 2)
    for l in range(DEPTH):
        x = _layer(x, (w_in_t, l), nsa_cmp_pos[l], nsa_cmp_w1[l], nsa_cmp_w2[l], w_nsa_out[l], gdn_conv_w[l],
                   gdn_a_log[l], gdn_dt_bias[l], gdn_norm_w[l], w_gdn_out[l], w_o[l], ln1_g[l], ln1_b[l],
                   ffn_w_up[l], ffn_conv_w[l], ffn_w_down[l], ln2_g[l], ln2_b[l])
    return x
```

```python
import functools

import numpy as np
import jax
import jax.numpy as jnp
from jax import lax
from jax.experimental import pallas as pl
from jax.experimental.pallas import tpu as pltpu

F32 = jnp.float32
BF16 = jnp.bfloat16

D_MODEL = 1024
NSA_HEADS = 8
NSA_KV_GROUPS = 2
NSA_REP = NSA_HEADS // NSA_KV_GROUPS
NSA_HEAD_DIM = 64
CMP_LEN = 32
CMP_STRIDE = 16
SLC_LEN = 64
SLC_TOPK = 8
WIN_LEN = 512
FORCE_SCORE = 1.0e4
NEG = -1.0e30
GDN_HEADS = 4
GDN_HEAD_DIM = 128
GDN_CONV = 4
GDN_CHUNK = 64
FFN_DIM = 2816
FFN_CONV = 3
DEPTH = 1
DEEPNORM_ALPHA = (2.0 * DEPTH) ** 0.25
LN_EPS = 1e-5
RMS_EPS = 1e-6

NSA_Q_W = NSA_HEADS * NSA_HEAD_DIM
NSA_KV_W = NSA_KV_GROUPS * NSA_HEAD_DIM
GDN_W = GDN_HEADS * GDN_HEAD_DIM

LANES = 128
SUBLANES = 8
VMEM_LIMIT_BYTES = 56 * 1024 * 1024

AUG_SEL0 = 64
AUG_POS_HI = 96
AUG_POS_LO = 97
AUG_PAD = 98
BIG = 2.0 ** 100
POS_SPLIT = 256
Q_TILE = 128
N_SLC = 32
V_ROWS = 80
SEL_KC = 512
NSA_SUB = 4

NT_DIMS = (((1,), (1,)), ((), ()))


def _dot(a, b, **kw):
    return jnp.dot(a, b, preferred_element_type=F32, **kw)


def _dot_nt(a, b, **kw):
    return lax.dot_general(a, b, NT_DIMS, preferred_element_type=F32, **kw)


def _cparams(*sem):
    return pltpu.CompilerParams(dimension_semantics=sem, vmem_limit_bytes=VMEM_LIMIT_BYTES)


def _const_spec(shape):
    nd = len(shape)
    return pl.BlockSpec(shape, lambda *_: (0,) * nd, pipeline_mode=pl.Buffered(1))


_IN_WIDTHS = (NSA_Q_W,) + (NSA_KV_W,) * 6 + (3 * NSA_HEADS, 3 * GDN_W, GDN_HEADS, GDN_HEADS, GDN_W, 2 * D_MODEL)
(_C_Q, _C_CK, _C_CV, _C_SK, _C_SV, _C_WK, _C_WV, _C_GATE, _C_GQKV, _C_BETA, _C_DECAY, _C_GGATE, _C_MERGE,
 IN_WIDTH) = (int(v) for v in np.cumsum((0,) + _IN_WIDTHS))
_C_SMALL = _C_BETA // LANES * LANES
SM_BETA = _C_BETA - _C_SMALL
SM_DECAY = _C_DECAY - _C_SMALL
KEYS_COLS = 2 * NSA_KV_W
CMP_COLS = 2 * NSA_KV_W
_INPROJ_GROUPS = (("keys", KEYS_COLS, BF16), ("cmp", CMP_COLS, BF16), ("ggate", GDN_W, BF16),
                  ("merge", 2 * D_MODEL, BF16))
_INPROJ_WIDTH = sum(w for _, w, _ in _INPROJ_GROUPS)
_GDN_PROJ_WIDTH = 3 * GDN_W + LANES
_INPROJ_T_ROWS = NSA_Q_W + 4 * NSA_HEAD_DIM
_GATE_T_ROWS = 32
INPROJ_TM = 1024
INPROJ_TN = 512
WPREP_TK = 128


def _wprep_body(w_ref, rows_ref, gdn_ref, wt_ref, wg_ref):
    hd = NSA_HEAD_DIM
    feat = lambda c0, n: w_ref[0, c0:c0 + n, :]

    def put_t(ref, col, src):
        for r in range(0, src.shape[0], LANES):
            ref[:, col + r:col + r + LANES] = src[r:r + LANES].T.astype(BF16)

    for g in range(NSA_KV_GROUPS):
        put_t(rows_ref, 2 * g * hd, jnp.concatenate([feat(_C_SK + g * hd, hd), feat(_C_WK + g * hd, hd)], axis=0))
    c = 4 * hd
    for c0, n in ((_C_CK, 2 * NSA_KV_W), (_C_GGATE, GDN_W), (_C_MERGE, 2 * D_MODEL)):
        put_t(rows_ref, c, feat(c0, n))
        c += n
    put_t(gdn_ref, 0, feat(_C_GQKV, 3 * GDN_W))
    put_t(gdn_ref, 3 * GDN_W, feat(_C_SMALL, LANES))
    wt_ref[:NSA_Q_W, :] = feat(_C_Q, NSA_Q_W).astype(BF16)
    for j, c0 in enumerate((_C_SV, _C_WV, _C_SV + hd, _C_WV + hd)):
        wt_ref[NSA_Q_W + j * hd:NSA_Q_W + (j + 1) * hd, :] = feat(c0, hd).astype(BF16)
    wg_ref[...] = feat(_C_GATE, _GATE_T_ROWS).astype(BF16)


def _wprep(w_in_t, layer):
    k = w_in_t.shape[2]
    tk = WPREP_TK
    return pl.pallas_call(
        _wprep_body,
        grid=(k // tk,),
        in_specs=[pl.BlockSpec((1, IN_WIDTH, tk), lambda i: (layer, 0, i))],
        out_specs=[pl.BlockSpec((tk, _INPROJ_WIDTH), lambda i: (i, 0)),
                   pl.BlockSpec((tk, _GDN_PROJ_WIDTH), lambda i: (i, 0)),
                   pl.BlockSpec((_INPROJ_T_ROWS, tk), lambda i: (0, i)),
                   pl.BlockSpec((_GATE_T_ROWS, tk), lambda i: (0, i))],
        out_shape=[jax.ShapeDtypeStruct((k, _INPROJ_WIDTH), BF16), jax.ShapeDtypeStruct((k, _GDN_PROJ_WIDTH), BF16),
                   jax.ShapeDtypeStruct((_INPROJ_T_ROWS, k), BF16), jax.ShapeDtypeStruct((_GATE_T_ROWS, k), BF16)],
        compiler_params=_cparams("parallel"),
        name="wprep",
    )(w_in_t)


def _inproj_body(x_ref, w_ref, wt_ref, wg_ref, keys_ref, cmp_ref, ggate_ref, merge_ref, qvt_ref, gt_ref, cmp_s):
    x = x_ref[...].astype(BF16)
    outs = (keys_ref, None, ggate_ref, merge_ref)
    c0 = 0
    for ref, (name, width, _) in zip(outs, _INPROJ_GROUPS):
        for s in range(0, width, INPROJ_TN):
            e = min(s + INPROJ_TN, width)
            res = _dot(x, w_ref[:, c0 + s:c0 + e])
            if name == "cmp":
                for j in range(width // LANES):
                    cmp_s[j] = res[:, j * LANES:(j + 1) * LANES]
            else:
                ref[:, s:e] = res.astype(ref.dtype)
        c0 += width
    nblk = cmp_ref.shape[0]
    for l in range(CMP_STRIDE):
        for j in range(cmp_s.shape[0]):
            cmp_ref[:, l * CMP_COLS + j * LANES:l * CMP_COLS + (j + 1) * LANES] = (
                cmp_s[j, pl.ds(l, nblk, stride=CMP_STRIDE), :].astype(BF16))
    for s in range(0, _INPROJ_T_ROWS, 2 * LANES):
        qvt_ref[s:s + 2 * LANES, :] = _dot_nt(wt_ref[s:s + 2 * LANES, :], x).astype(qvt_ref.dtype)
    gt_ref[...] = _dot_nt(wg_ref[...], x)


def _inproj(x2, w_rows, w_t, w_g):
    m = x2.shape[0]
    tm = INPROJ_TM
    row_major = [(n, wd, dt) for n, wd, dt in _INPROJ_GROUPS if n != "cmp"]
    specs = {n: (pl.BlockSpec((tm, wd), lambda i: (i, 0)), jax.ShapeDtypeStruct((m, wd), dt))
             for n, wd, dt in row_major}
    specs["cmp"] = (pl.BlockSpec((tm // CMP_STRIDE, CMP_STRIDE * CMP_COLS), lambda i: (i, 0)),
                    jax.ShapeDtypeStruct((m // CMP_STRIDE, CMP_STRIDE * CMP_COLS), BF16))
    order = [n for n, _, _ in _INPROJ_GROUPS]
    return pl.pallas_call(
        _inproj_body,
        grid=(m // tm,),
        in_specs=[pl.BlockSpec((tm, D_MODEL), lambda i: (i, 0)), _const_spec(w_rows.shape),
                  _const_spec(w_t.shape), _const_spec(w_g.shape)],
        out_specs=[specs[n][0] for n in order]
        + [pl.BlockSpec((_INPROJ_T_ROWS, tm), lambda i: (0, i)), pl.BlockSpec((_GATE_T_ROWS, tm), lambda i: (0, i))],
        out_shape=[specs[n][1] for n in order]
        + [jax.ShapeDtypeStruct((_INPROJ_T_ROWS, m), BF16), jax.ShapeDtypeStruct((_GATE_T_ROWS, m), F32)],
        scratch_shapes=[pltpu.VMEM((CMP_COLS // LANES, tm, LANES), F32)],
        compiler_params=_cparams("parallel"),
        name="inproj",
    )(x2, w_rows, w_t, w_g)


def _compress_weights(cmp_pos, cmp_w2):
    hd, half = NSA_HEAD_DIM, CMP_LEN // 2
    posr = cmp_pos.reshape(2, 2, half, hd)
    post = jnp.broadcast_to(posr.transpose(1, 2, 0, 3)[:, :, :, None, :], (2, half, 2, 2, hd))
    post = jnp.concatenate([post.reshape(2, half * 4 * hd),
                            jnp.zeros((SUBLANES - 2, half * 4 * hd), cmp_pos.dtype)], axis=0)
    w2sel = jnp.zeros((2, 2, 2 * hd, LANES), cmp_w2.dtype)
    for g in range(2):
        w2sel = w2sel.at[:, g, g * hd:(g + 1) * hd, :hd].set(cmp_w2)
    return post.astype(BF16), w2sel.reshape(4, 2 * hd, LANES).astype(BF16)


def _compress_body(t_ref, w1_ref, pos_ref, w2_ref, aug_ref, out_ref, w1e_s):
    hd, half_len = NSA_HEAD_DIM, CMP_LEN // 2

    @pl.when(pl.program_id(0) == 0)
    def _():
        w1e_s[...] = jnp.zeros(w1e_s.shape, BF16)
        for which in range(2):
            for half in range(2):
                for l in range(half_len):
                    blk = w1_ref[which, (half * half_len + l) * hd:(half * half_len + l + 1) * hd, :].astype(BF16)
                    for g in range(NSA_KV_GROUPS):
                        r0 = l * CMP_COLS + which * LANES + g * hd
                        c0 = half * CMP_COLS + which * LANES + g * hd
                        w1e_s[r0:r0 + hd, c0:c0 + hd] = blk

    p = _dot(t_ref[0], w1e_s[...])
    pp = _dot(pos_ref[...], w1e_s[...])
    nxt = pltpu.roll(p[:, CMP_COLS:], p.shape[0] - 1, axis=0)
    pre = p[:, :CMP_COLS] + nxt + pp[0:1, :CMP_COLS] + pp[1:2, CMP_COLS:]
    h = jax.nn.gelu(pre).astype(BF16)
    n_idx = lax.broadcasted_iota(jnp.int32, (p.shape[0], LANES), 0)
    real = n_idx < p.shape[0] - 1
    for which in range(2):
        hw = h[:, which * LANES:(which + 1) * LANES]
        for g in range(2):
            o = jnp.where(real, _dot(hw, w2_ref[which * 2 + g]) + aug_ref[which], 0.0)
            out_ref[0, which * 2 + g] = (o if which == 0 else o.T).astype(out_ref.dtype)


def _compress(t2, w1, post, w2sel, aug):
    b, nblk, _ = t2.shape
    return pl.pallas_call(
        _compress_body,
        grid=(b,),
        in_specs=[pl.BlockSpec((1, nblk, CMP_STRIDE * CMP_COLS), lambda i: (i, 0, 0)),
                  _const_spec(w1.shape), _const_spec(post.shape), _const_spec(w2sel.shape), _const_spec(aug.shape)],
        out_specs=pl.BlockSpec((1, 4, nblk, LANES), lambda i: (i, 0, 0, 0)),
        out_shape=jax.ShapeDtypeStruct((b, 4, nblk, LANES), BF16),
        scratch_shapes=[pltpu.VMEM((CMP_STRIDE * CMP_COLS, 2 * CMP_COLS), BF16)],
        compiler_params=_cparams("arbitrary"),
        name="compress",
    )(t2, w1, post, w2sel, aug)


def _nsa_consts(s):
    hd, rep = NSA_HEAD_DIM, NSA_REP
    t = np.arange(s)
    kx_win = np.zeros((s + WIN_LEN, hd), np.float32)
    kx_win[WIN_LEN + t, AUG_POS_HI - hd] = t // POS_SPLIT
    kx_win[WIN_LEN + t, AUG_POS_LO - hd] = t % POS_SPLIT
    kx_win[:WIN_LEN, AUG_PAD - hd] = 1.0
    kx_sel = kx_win.copy()
    kx_sel[WIN_LEN + t, t // SLC_LEN] = 1.0
    vx_win = np.zeros((V_ROWS - hd, s + WIN_LEN), np.float32)
    vx_win[0, WIN_LEN:] = 1.0
    vx_sel = vx_win
    n_cmp = s // CMP_STRIDE
    cmp_aug = np.zeros((2, n_cmp, LANES), np.float32)
    end = np.arange(n_cmp) * CMP_STRIDE + CMP_LEN - 1
    cmp_aug[0, :, AUG_POS_HI] = end // POS_SPLIT
    cmp_aug[0, :, AUG_POS_LO] = end % POS_SPLIT
    qx = np.zeros((NSA_KV_GROUPS, LANES - AUG_POS_HI, rep * Q_TILE), np.float32)
    for h in range(NSA_HEADS):
        slope = 2.0 ** (-8.0 * (h + 1) / NSA_HEADS)
        lanes = slice((h % rep) * Q_TILE, (h % rep + 1) * Q_TILE)
        qx[h // rep, 0, lanes] = slope * POS_SPLIT
        qx[h // rep, 1, lanes] = slope
        qx[h // rep, AUG_PAD - AUG_POS_HI, lanes] = -BIG
    c0 = np.arange(n_cmp)[None, :] * CMP_STRIDE
    s0 = np.arange(s // SLC_LEN)[:, None] * SLC_LEN
    ov_t = ((c0 < s0 + SLC_LEN) & (c0 + CMP_LEN > s0)).astype(np.float32)
    ov_t[:, (s - CMP_LEN) // CMP_STRIDE + 1:] = 0.0
    kk = np.arange(Q_TILE)[:, None]
    qq = np.arange(Q_TILE)[None, :]
    causal = np.tile(np.where(kk <= qq, 0.0, NEG).astype(np.float32), (1, rep))
    after = np.tile(np.where(kk > qq, 0.0, NEG).astype(np.float32), (1, rep))
    j = jnp.asarray
    return dict(kx_sel=j(kx_sel, BF16), kx_win=j(kx_win, BF16), vx_sel=j(vx_sel, BF16), vx_win=j(vx_win, BF16),
                cmp_aug=j(cmp_aug), qx=j(qx), ov_t=j(ov_t), causal=j(causal), after=j(after))


def _nsa_body(qt_ref, k_ref, vt_ref, kc_ref, vct_ref, gt_ref, kxs_ref, kxw_ref, vxs_ref, vxw_ref, qx_ref, ovt_ref,
              causal_ref, after_ref, out_ref, ks_s, kw_s, vs_s, vw_s):
    hd, rep, tq = NSA_HEAD_DIM, NSA_REP, Q_TILE
    nq = rep * tq
    i = pl.program_id(2)

    @pl.when(i == 0)
    def _():
        keys = k_ref[0]
        ks_s[:WIN_LEN, :hd] = jnp.zeros((WIN_LEN, hd), BF16)
        ks_s[WIN_LEN:, :hd] = keys[:, :hd]
        ks_s[:, hd:] = kxs_ref[...]
        kw_s[:WIN_LEN, :hd] = jnp.zeros((WIN_LEN, hd), BF16)
        kw_s[WIN_LEN:, :hd] = keys[:, hd:]
        kw_s[:, hd:] = kxw_ref[...]
        vals = vt_ref[...]
        vs_s[:hd, :WIN_LEN] = jnp.zeros((hd, WIN_LEN), BF16)
        vs_s[:hd, WIN_LEN:] = vals[:hd]
        vs_s[hd:, :] = vxs_ref[...]
        vw_s[:hd, :WIN_LEN] = jnp.zeros((hd, WIN_LEN), BF16)
        vw_s[:hd, WIN_LEN:] = vals[hd:]
        vw_s[hd:, :] = vxw_ref[...]

    qx = qx_ref[0]
    sg_all = jax.nn.sigmoid(gt_ref[...])
    grp = pl.program_id(1)

    full_past = WIN_LEN // tq

    def front(sub, res, it, past):
        qt = qt_ref[:, sub * tq:(sub + 1) * tq]
        q64 = jnp.concatenate([qt[r * hd:(r + 1) * hd, :] for r in range(rep)], axis=1).astype(F32) * (hd ** -0.5)

        def q_aug(sel_rows):
            return jnp.concatenate([q64, sel_rows, qx], axis=0).astype(BF16)

        n_row = lax.broadcasted_iota(jnp.int32, (LANES, nq), 0)
        t_lane = it * tq + (lax.broadcasted_iota(jnp.int32, (LANES, nq), 1) & (tq - 1))
        valid = t_lane >= n_row * CMP_STRIDE + (CMP_LEN - 1)
        qa0 = q_aug(jnp.zeros((N_SLC, nq), F32))
        sc = jnp.where(valid, _dot(kc_ref[0, 0], qa0), NEG)
        mc = jnp.max(sc, axis=0, keepdims=True)
        ec = jnp.where(valid, jnp.exp(sc - mc), 0.0)
        lc = jnp.sum(ec, axis=0, keepdims=True)
        pc = ec * jnp.where(lc > 0.0, 1.0 / lc, 0.0)
        o_cmp = _dot(vct_ref[0, 0], pc.astype(BF16))[:hd]
        psum = pc[:, 0:tq] + pc[:, tq:2 * tq] + pc[:, 2 * tq:3 * tq] + pc[:, 3 * tq:4 * tq]
        score_t = _dot(ovt_ref[...], psum, precision=lax.Precision.HIGHEST)
        yield

        n_keys = (past + 1) * tq
        if past == full_past:
            w0 = pl.multiple_of(it * tq, tq)
        else:
            w0 = WIN_LEN
        diag = past * tq

        def one_shot_masks(s):
            parts = [s[:diag], s[diag:] + causal_ref[...]] if past else [s + causal_ref[...]]
            if past == full_past:
                parts = [s[:tq] + after_ref[...], s[tq:diag], parts[1]]
            return jnp.concatenate(parts, axis=0) if len(parts) > 1 else parts[0]

        s_w = one_shot_masks(_dot(kw_s[pl.ds(w0, n_keys), :], qa0))
        p_w = jnp.exp(s_w - jnp.max(s_w, axis=0, keepdims=True))
        acc_w = _dot(vw_s[:, pl.ds(w0, n_keys)], p_w.astype(BF16))
        o_win = acc_w[:hd] * (1.0 / acc_w[hd:hd + 1])
        yield

        jb = lax.broadcasted_iota(jnp.int32, (N_SLC, tq), 0)
        cur = (it * tq + lax.broadcasted_iota(jnp.int32, (N_SLC, tq), 1)) // SLC_LEN
        forced = (jb == 0) | (jb == cur) | (jb == cur - 1)
        score_t = jnp.where(forced, FORCE_SCORE, jnp.where(jb <= cur, score_t, -1.0))
        rank = jnp.zeros((N_SLC, tq), F32)
        for jp in range(N_SLC):
            other = score_t[jp:jp + 1, :]
            ge = jnp.where(other >= score_t, 1.0, 0.0)
            gt = jnp.where(other > score_t, 1.0, 0.0)
            rank = rank + jnp.where(jb > jp, ge, gt)
        sel = rank < float(SLC_TOPK)
        qa = q_aug(jnp.concatenate([jnp.where(sel, 0.0, -BIG)] * rep, axis=1))
        lo_blk = jnp.min(jnp.where(sel & (jb >= 2) & (jb <= cur), jb.astype(F32), float(N_SLC)))
        lo_key = (lo_blk.astype(jnp.int32) // 2) * tq
        yield

        s_main = _dot(ks_s[pl.ds(w0, n_keys), :], qa)
        if past < full_past:
            s_s = jnp.concatenate([s_main[:diag], s_main[diag:] + causal_ref[...]], axis=0) if past else (
                s_main + causal_ref[...])
            m_s = jnp.max(s_s, axis=0, keepdims=True)
            acc_s = _dot(vs_s[:, pl.ds(w0, n_keys)], jnp.exp(s_s - m_s).astype(BF16))
            res.update(early=False, acc_s=acc_s, o_cmp=o_cmp, o_win=o_win)
            return
        e_key = it * tq - WIN_LEN
        t0 = pl.multiple_of(jnp.where(e_key > 0, WIN_LEN, 0), tq)
        s_s = jnp.concatenate([_dot(ks_s[pl.ds(t0, tq), :], qa), s_main[:diag],
                               s_main[diag:] + causal_ref[...]], axis=0)
        m_s = jnp.max(s_s, axis=0, keepdims=True)
        p_s = jnp.exp(s_s - m_s).astype(BF16)
        acc_s = _dot(vs_s[:, pl.ds(t0, tq)], p_s[:tq]) + _dot(vs_s[:, pl.ds(w0, n_keys)], p_s[tq:])
        c_hi = (e_key - tq + SEL_KC - 1) // SEL_KC
        c_lo = jnp.where(lo_key < e_key, (lo_key - tq) // SEL_KC, c_hi)
        res.update(early=True, qa=qa, e_key=e_key, c_lo=c_lo, c_hi=c_hi, m_s=m_s, acc_s=acc_s, o_cmp=o_cmp,
                   o_win=o_win)

    def tail(sub, f):
        def early_step(c, carry):
            qa, e_key = f["qa"], f["e_key"]
            m, acc = carry
            k0 = tq + c * SEL_KC
            start = pl.multiple_of(WIN_LEN + k0, tq)
            k_abs = k0 + lax.broadcasted_iota(jnp.int32, (SEL_KC, nq), 0)
            s = jnp.where(k_abs < e_key, _dot(ks_s[pl.ds(start, SEL_KC), :], qa), NEG)
            m_new = jnp.maximum(m, jnp.max(s, axis=0, keepdims=True))
            p = jnp.exp(s - m_new).astype(BF16)
            return m_new, acc * jnp.exp(m - m_new) + _dot(vs_s[:, pl.ds(start, SEL_KC)], p)

        acc_s = f["acc_s"]
        if f["early"]:
            _, acc_s = lax.fori_loop(f["c_lo"], f["c_hi"], early_step, (f["m_s"], acc_s))
        o_slc = acc_s[:hd] * (1.0 / acc_s[hd:hd + 1])

        sg = sg_all[:, sub * tq:(sub + 1) * tq]
        gate = lambda br, r: jnp.where(grp == 0, sg[br * NSA_HEADS + r:br * NSA_HEADS + r + 1],
                                       sg[br * NSA_HEADS + rep + r:br * NSA_HEADS + rep + r + 1])
        for pair in range(rep // 2):
            halves = []
            for r in (2 * pair, 2 * pair + 1):
                lanes = slice(r * tq, (r + 1) * tq)
                halves.append(gate(0, r) * f["o_cmp"][:, lanes] + gate(1, r) * o_slc[:, lanes]
                              + gate(2, r) * f["o_win"][:, lanes])
            out_ref[0, sub * tq:(sub + 1) * tq, pair * LANES:(pair + 1) * LANES] = (
                jnp.concatenate(halves, axis=0).T.astype(out_ref.dtype))

    def run(tiles):
        fronts = [{} for _ in range(NSA_SUB)]
        for sub, (it, past) in enumerate(tiles):
            for _ in front(sub, fronts[sub], it, past):
                pass
        for sub in range(NSA_SUB):
            tail(sub, fronts[sub])

    n_short = -(-full_past // NSA_SUB)
    for step in range(n_short):
        @pl.when(i == step)
        def _(step=step):
            run([(step * NSA_SUB + sub, min(step * NSA_SUB + sub, full_past)) for sub in range(NSA_SUB)])

    @pl.when(i >= n_short)
    def _():
        run([(i * NSA_SUB + sub, full_past) for sub in range(NSA_SUB)])


def _nsa(qvt, keys, cmp_kv, gt, consts, b, s):
    tqs = NSA_SUB * Q_TILE
    nt = s // tqs
    c = consts
    in_specs = [
        pl.BlockSpec((2 * LANES, tqs), lambda bi, g, i: (g, bi * nt + i)),
        pl.BlockSpec((1, s, LANES), lambda bi, g, i: (bi, 0, g)),
        pl.BlockSpec((LANES, s), lambda bi, g, i: (NSA_Q_W // LANES + g, bi)),
        pl.BlockSpec((1, 1, s // CMP_STRIDE, LANES), lambda bi, g, i: (bi, g, 0, 0)),
        pl.BlockSpec((1, 1, s // CMP_STRIDE, LANES), lambda bi, g, i: (bi, 2 + g, 0, 0)),
        pl.BlockSpec((_GATE_T_ROWS, tqs), lambda bi, g, i: (0, bi * nt + i)),
        _const_spec(c["kx_sel"].shape), _const_spec(c["kx_win"].shape), _const_spec(c["vx_sel"].shape),
        _const_spec(c["vx_win"].shape),
        pl.BlockSpec((1,) + c["qx"].shape[1:], lambda bi, g, i: (g, 0, 0)),
        _const_spec(c["ov_t"].shape), _const_spec(c["causal"].shape), _const_spec(c["after"].shape),
    ]
    return pl.pallas_call(
        _nsa_body,
        grid=(b, NSA_KV_GROUPS, nt),
        in_specs=in_specs,
        out_specs=pl.BlockSpec((1, tqs, 2 * LANES), lambda bi, g, i: (bi, i, g)),
        out_shape=jax.ShapeDtypeStruct((b, s, NSA_Q_W), BF16),
        scratch_shapes=[pltpu.VMEM((s + WIN_LEN, LANES), BF16), pltpu.VMEM((s + WIN_LEN, LANES), BF16),
                        pltpu.VMEM((V_ROWS, s + WIN_LEN), BF16), pltpu.VMEM((V_ROWS, s + WIN_LEN), BF16)],
        compiler_params=_cparams("parallel", "parallel", "arbitrary"),
        name="nsa",
    )(qvt, keys, qvt, cmp_kv, cmp_kv, gt, c["kx_sel"], c["kx_win"], c["vx_sel"], c["vx_win"], c["qx"], c["ov_t"],
      c["causal"], c["after"])


GDN_TS = 512
GDN_BLK = 128
GDN_HALO = 8
GDN_XHALO = 16
GDN_INV_BASE = 8
GDN_CHAIN_UNITS_PER_PREP_UNIT = 4
GDN_SCAN_ROWS = GDN_HEAD_DIM + GDN_CHUNK


def _gdn_prep(x_ref, prev_ref, w_ref, cw_ref, alog_ref, dtb_ref, lt_ref, first_tile, xp_s, buf):
    x_s, p_s, rhs_s, qg_s, aqk_s, kdt_s, a_s, eg_s = buf
    ts, dh, nh, blk = GDN_TS, GDN_HEAD_DIM, GDN_HEADS, GDN_BLK
    proj = _dot(x_ref[0].astype(BF16), w_ref[...])
    hist = _dot(prev_ref[0].astype(BF16), w_ref[:, :3 * GDN_W])[GDN_XHALO - GDN_HALO:]
    xp_s[0:GDN_HALO, :] = jnp.where(first_tile, 0.0, hist)
    xp_s[GDN_HALO:, :] = proj[:, :3 * GDN_W]
    yield
    act = []
    for blk_i in range(3 * nh):
        lanes = slice(blk_i * dh, (blk_i + 1) * dh)
        xp = xp_s[:, lanes]
        conv = cw_ref[0:1, lanes] * xp
        for j in range(1, GDN_CONV):
            conv = pltpu.roll(conv, 1, axis=0) + cw_ref[j:j + 1, lanes] * xp
        act.append(jax.nn.silu(conv[GDN_HALO:]))
        yield

    sm = proj[:, 3 * GDN_W:]
    beta = jax.nn.sigmoid(sm)
    g = -jnp.exp(alog_ref[...]) * jax.nn.softplus(sm + dtb_ref[...])
    gcum = jnp.concatenate([_dot(lt_ref[...], g[r:r + blk], precision=lax.Precision.HIGHEST)
                            for r in range(0, ts, blk)], axis=0)
    eg = jnp.exp(gcum)
    eg_s[...] = eg

    ri = lax.broadcasted_iota(jnp.int32, (blk, blk), 0)
    ci = lax.broadcasted_iota(jnp.int32, (blk, blk), 1)
    same = (ri // GDN_CHUNK) == (ci // GDN_CHUNK)
    causal = same & (ri >= ci)
    strict = same & (ri > ci)
    eye = (ri == ci).astype(F32)
    same_base = (ri // GDN_INV_BASE) == (ci // GDN_INV_BASE)

    for pb in range(ts // blk):
        rows = slice(pb * blk, (pb + 1) * blk)
        gc = gcum[rows]
        gc_t = gc.T
        first = lax.broadcasted_iota(jnp.int32, (blk, LANES), 0) < GDN_CHUNK
        g_last = jnp.where(first, gc[GDN_CHUNK - 1:GDN_CHUNK, :], gc[blk - 1:blk, :])
        e_dec = jnp.exp(g_last - gc)
        for h in range(nh):
            q, k, v = act[h][rows], act[nh + h][rows], act[2 * nh + h][rows]
            q = q * lax.rsqrt(jnp.sum(q * q, axis=1, keepdims=True) + RMS_EPS) * (dh ** -0.5)
            k = k * lax.rsqrt(jnp.sum(k * k, axis=1, keepdims=True) + RMS_EPS)
            b_col = beta[rows, SM_BETA + h:SM_BETA + h + 1]
            eg_col = eg[rows, SM_DECAY + h:SM_DECAY + h + 1]
            gdiff = gc[:, SM_DECAY + h:SM_DECAY + h + 1] - gc_t[SM_DECAY + h:SM_DECAY + h + 1, :]
            decay = jnp.exp(jnp.where(causal, gdiff, NEG))
            kb = k * b_col
            kbf, kf, qf = kb.astype(BF16), k.astype(BF16), q.astype(BF16)
            a = jnp.where(strict, -_dot_nt(kbf, kf) * decay, 0.0)
            c = pb * nh + h
            a_base = jnp.where(same_base, a, 0.0)
            a_s[c] = a.astype(BF16)
            x_s[c] = a_base.astype(BF16)
            p_s[c] = eye + a_base
            rhs_s[c] = jnp.concatenate([v * b_col, kb * eg_col], axis=1).astype(BF16)
            qg_s[c] = q * eg_col
            aqk_s[c] = jnp.where(causal, _dot_nt(qf, kf) * decay, 0.0).astype(BF16)
            kdt_s[c] = (k * e_dec[:, SM_DECAY + h:SM_DECAY + h + 1]).T.astype(BF16)
            yield


def _gdn_chains(buf, nq_ref, co_ref, eg_ref):
    x_s, p_s, rhs_s, qg_s, aqk_s, kdt_s, a_s, eg_s = buf
    dh, nh, blk = GDN_HEAD_DIM, GDN_HEADS, GDN_BLK
    n_chain = x_s.shape[0]
    eg_ref[0] = eg_s[...]
    ri = lax.broadcasted_iota(jnp.int32, (blk, blk), 0)
    ci = lax.broadcasted_iota(jnp.int32, (blk, blk), 1)
    same = lambda size: (ri // size) == (ci // size)
    for c in range(n_chain):
        y = x_s[c]
        x_s[c] = _dot(y, y).astype(BF16)
        yield
    for c in range(n_chain):
        y2 = x_s[c]
        p = p_s[c]
        p_s[c] = p + _dot(p.astype(BF16), y2)
        x_s[c] = _dot(y2, y2).astype(BF16)
        yield
    for c in range(n_chain):
        p = p_s[c]
        p_s[c] = p + _dot(p.astype(BF16), x_s[c])
        yield
    size = 2 * GDN_INV_BASE
    while size <= GDN_CHUNK:
        between = same(size) & jnp.logical_not(same(size // 2))
        for c in range(n_chain):
            a = a_s[c]
            x_s[c] = _dot(p_s[c].astype(BF16), jnp.where(between, a, jnp.zeros_like(a))).astype(BF16)
            yield
        for c in range(n_chain):
            t = p_s[c]
            p_s[c] = t + _dot(x_s[c], t.astype(BF16))
            yield
        size *= 2

    tok_half = lax.broadcasted_iota(jnp.int32, (blk, blk), 1) // GDN_CHUNK
    for c in range(n_chain):
        rhs_s[c] = _dot(p_s[c].astype(BF16), rhs_s[c]).astype(BF16)
        yield
    for c in range(n_chain):
        pb, h = divmod(c, nh)
        uw = rhs_s[c]
        a1 = _dot(aqk_s[c], uw)
        q_loc = qg_s[c] - a1[:, dh:]
        kdt = kdt_s[c]
        for half in range(blk // GDN_CHUNK):
            k1 = _dot(jnp.where(tok_half == half, kdt, jnp.zeros_like(kdt)), uw)
            n = pb * (blk // GDN_CHUNK) + half
            rows = slice(half * GDN_CHUNK, (half + 1) * GDN_CHUNK)
            nq_ref[0, h, n, :dh, :] = (-k1[:, dh:]).astype(BF16)
            nq_ref[0, h, n, dh:, :] = q_loc[rows].astype(BF16)
            co_ref[0, h, n, :dh, :] = k1[:, :dh].astype(BF16)
            co_ref[0, h, n, dh:, :] = a1[rows, :dh].astype(BF16)
        yield


def _interleave(major, minor, minor_per_major):
    for _ in major:
        for _ in range(minor_per_major):
            next(minor, None)
    for _ in minor:
        pass


def _gdn_intra_body(x_ref, prev_ref, w_ref, cw_ref, alog_ref, dtb_ref, lt_ref, nq_ref, co_ref, eg_ref, xp_s, *bufs,
                    tiles_per_seq, n_buf):
    j = pl.program_id(0)
    sets = (bufs[:n_buf], bufs[n_buf:])

    @pl.when(j == 0)
    def _():
        for ref in sets[1]:
            ref[...] = jnp.zeros(ref.shape, ref.dtype)

    first_tile = (j % tiles_per_seq) == 0
    for parity in range(2):
        @pl.when(j % 2 == parity)
        def _(parity=parity):
            chains = _gdn_chains(sets[1 - parity], nq_ref, co_ref, eg_ref)
            prep = _gdn_prep(x_ref, prev_ref, w_ref, cw_ref, alog_ref, dtb_ref, lt_ref, first_tile, xp_s,
                             sets[parity])
            _interleave(prep, chains, GDN_CHAIN_UNITS_PER_PREP_UNIT)


def _gdn_intra(x, w_gdn, conv_w, alog_l, dtb_l, lt):
    b, s, d = x.shape
    ts, nh, dh = GDN_TS, GDN_HEADS, GDN_HEAD_DIM
    tps = s // ts
    nt = b * tps
    n_chain = (ts // GDN_BLK) * nh
    sq = lambda dt: pltpu.VMEM((n_chain, GDN_BLK, GDN_BLK), dt)
    buf = lambda: [sq(BF16), sq(F32), pltpu.VMEM((n_chain, GDN_BLK, 2 * dh), BF16), sq(F32), sq(BF16), sq(BF16),
                   sq(BF16), pltpu.VMEM((ts, LANES), F32)]
    src = lambda j: jnp.minimum(j, nt - 1)
    dst = lambda j: jnp.maximum(j - 1, 0)
    cspec = lambda: pl.BlockSpec((1, nh, ts // GDN_CHUNK, GDN_SCAN_ROWS, dh),
                                 lambda j: (dst(j) // tps, 0, dst(j) % tps, 0, 0))
    cshape = jax.ShapeDtypeStruct((b, nh, s // GDN_CHUNK, GDN_SCAN_ROWS, dh), BF16)
    return pl.pallas_call(
        functools.partial(_gdn_intra_body, tiles_per_seq=tps, n_buf=len(buf())),
        grid=(nt + 1,),
        in_specs=[
            pl.BlockSpec((1, ts, d), lambda j: (src(j) // tps, src(j) % tps, 0)),
            pl.BlockSpec((1, GDN_XHALO, d),
                         lambda j: (src(j) // tps, jnp.maximum((src(j) % tps) * (ts // GDN_XHALO) - 1, 0), 0)),
            _const_spec(w_gdn.shape), _const_spec(conv_w.shape), _const_spec(alog_l.shape), _const_spec(dtb_l.shape),
            _const_spec(lt.shape),
        ],
        out_specs=[cspec(), cspec(), pl.BlockSpec((1, ts, LANES), lambda j: (dst(j) // tps, dst(j) % tps, 0))],
        out_shape=[cshape, cshape, jax.ShapeDtypeStruct((b, s, LANES), F32)],
        scratch_shapes=[pltpu.VMEM((ts + GDN_HALO, 3 * GDN_W), F32)] + buf() + buf(),
        compiler_params=_cparams("arbitrary"),
        name="gdn_intra",
    )(x, x, w_gdn, conv_w, alog_l, dtb_l, lt)


def _gdn_scan_body(nq_ref, co_ref, eg_ref, gate_ref, nw_ref, out_ref, st_s):
    nh, dh, ck = GDN_HEADS, GDN_HEAD_DIM, GDN_CHUNK
    st_s[...] = jnp.zeros(st_s.shape, F32)

    def chunk(n, carry):
        r0 = pl.multiple_of(n * ck, ck)
        d_row = eg_ref[0, pl.ds(r0 + ck - 1, 1), :]
        for h in range(nh):
            st = st_s[h]
            res = _dot(nq_ref[0, h, n], st.astype(BF16)) + co_ref[0, h, n].astype(F32)
            st_s[h] = st * d_row[:, SM_DECAY + h:SM_DECAY + h + 1] + res[:dh]
            o = res[dh:]
            ms = jnp.mean(o * o, axis=1, keepdims=True)
            gt = gate_ref[0, pl.ds(r0, ck), h * dh:(h + 1) * dh].astype(F32)
            out_ref[0, pl.ds(r0, ck), h * dh:(h + 1) * dh] = (
                o * lax.rsqrt(ms + RMS_EPS) * nw_ref[...] * jax.nn.silu(gt)).astype(out_ref.dtype)
        return carry

    lax.fori_loop(0, nq_ref.shape[2], chunk, 0, unroll=16)


def _gdn_scan(nq, co, eg, ggate, norm_w):
    b, nh, nc, rows, dh = nq.shape
    s = nc * GDN_CHUNK
    cspec = lambda: pl.BlockSpec((1, nh, nc, rows, dh), lambda bi: (bi, 0, 0, 0, 0))
    return pl.pallas_call(
        _gdn_scan_body,
        grid=(b,),
        in_specs=[cspec(), cspec(),
                  pl.BlockSpec((1, s, LANES), lambda bi: (bi, 0, 0)),
                  pl.BlockSpec((1, s, GDN_W), lambda bi: (bi, 0, 0)),
                  _const_spec(norm_w.shape)],
        out_specs=pl.BlockSpec((1, s, GDN_W), lambda bi: (bi, 0, 0)),
        out_shape=jax.ShapeDtypeStruct((b, s, GDN_W), BF16),
        scratch_shapes=[pltpu.VMEM((nh, dh, dh), F32)],
        compiler_params=_cparams("parallel"),
        name="gdn_scan",
    )(nq, co, eg, ggate, norm_w)


MERGE_TM = 1024
MERGE_SUB = 256


def _layer_norm(y, g, b):
    mu = jnp.mean(y, axis=1, keepdims=True)
    d = y - mu
    var = jnp.mean(d * d, axis=1, keepdims=True)
    return d * lax.rsqrt(var + LN_EPS) * g + b


def _merge_body(x_ref, oa_ref, ob_ref, ga_ref, gb_ref, wa_ref, wb_ref, wo_ref, g_ref, b_ref, y_ref):
    def sub_tile(rows):
        ya = _dot(oa_ref[rows, :], wa_ref[...])
        yb = _dot(ob_ref[rows, :], wb_ref[...])
        yield
        mixin = (jax.nn.sigmoid(ga_ref[rows, :].astype(F32)) * ya
                 + jax.nn.sigmoid(gb_ref[rows, :].astype(F32)) * yb).astype(BF16)
        yield
        mix = _dot(mixin, wo_ref[...])
        yield
        y_ref[rows, :] = _layer_norm(DEEPNORM_ALPHA * x_ref[rows, :] + mix, g_ref[...], b_ref[...])
        yield

    n_stage = 4
    subs = [sub_tile(slice(k * MERGE_SUB, (k + 1) * MERGE_SUB)) for k in range(MERGE_TM // MERGE_SUB)]
    for step in range(n_stage + len(subs) - 1):
        for k, sub in enumerate(subs):
            if 0 <= step - k < n_stage:
                next(sub)


def _merge(x2, oa, ob, mgate, wa, wb, wo, g, b):
    m = x2.shape[0]
    tm, d = MERGE_TM, D_MODEL
    row = lambda wd, col=0: pl.BlockSpec((tm, wd), lambda i, col=col: (i, col))
    return pl.pallas_call(
        _merge_body,
        grid=(m // tm,),
        in_specs=[row(d), row(NSA_Q_W), row(GDN_W), row(d, 0), row(d, 1),
                  _const_spec(wa.shape), _const_spec(wb.shape), _const_spec(wo.shape),
                  _const_spec(g.shape), _const_spec(b.shape)],
        out_specs=row(d),
        out_shape=jax.ShapeDtypeStruct((m, d), F32),
        compiler_params=_cparams("parallel"),
        name="merge",
    )(x2, oa, ob, mgate, mgate, wa, wb, wo, g, b)


FFN_TM = 1024
FFN_HALO = 16
FFN_CK = 256


def _ffn_body(x_ref, prev_ref, wu_ref, cw_ref, wd_ref, g_ref, b_ref, out_ref, act_s, *, tiles_per_seq):
    i = pl.program_id(0)
    prev = prev_ref[...].astype(BF16)
    prev = jnp.where(i % tiles_per_seq == 0, jnp.zeros_like(prev), prev)
    xc = jnp.concatenate([prev, x_ref[...].astype(BF16)], axis=0)

    def conv(h, c0):
        out = cw_ref[FFN_CONV - 1:FFN_CONV, c0:c0 + FFN_CK] * h[FFN_HALO:]
        for j in range(FFN_CONV - 1):
            shifted = pltpu.roll(h, FFN_CONV - 1 - j, axis=0)[FFN_HALO:]
            out = out + cw_ref[j:j + 1, c0:c0 + FFN_CK] * shifted
        return out

    for c in range(FFN_DIM // FFN_CK):
        c0 = c * FFN_CK
        hg = conv(_dot(xc, wu_ref[:, c0:c0 + FFN_CK]), c0)
        hv = conv(_dot(xc, wu_ref[:, FFN_DIM + c0:FFN_DIM + c0 + FFN_CK]), FFN_DIM + c0)
        act_s[:, c0:c0 + FFN_CK] = (jax.nn.silu(hg) * hv).astype(BF16)
    f = _dot(act_s[...], wd_ref[...])
    out_ref[...] = _layer_norm(DEEPNORM_ALPHA * x_ref[...] + f, g_ref[...], b_ref[...])


def _ffn(x1, wu, cw, wd, g, b, seq):
    m = x1.shape[0]
    tm, d = FFN_TM, D_MODEL
    return pl.pallas_call(
        functools.partial(_ffn_body, tiles_per_seq=seq // tm),
        grid=(m // tm,),
        in_specs=[pl.BlockSpec((tm, d), lambda i: (i, 0)),
                  pl.BlockSpec((FFN_HALO, d), lambda i: (jnp.maximum(i * (tm // FFN_HALO) - 1, 0), 0)),
                  _const_spec(wu.shape), _const_spec(cw.shape), _const_spec(wd.shape), _const_spec(g.shape), _const_spec(b.shape)],
        out_specs=pl.BlockSpec((tm, d), lambda i: (i, 0)),
        out_shape=jax.ShapeDtypeStruct((m, d), F32),
        scratch_shapes=[pltpu.VMEM((tm, FFN_DIM), BF16)],
        compiler_params=_cparams("parallel"),
        name="ffn",
    )(x1, x1, wu, cw, wd, g, b)


def _lane_vec(vals, lane0):
    return jnp.zeros((1, LANES), F32).at[0, lane0:lane0 + vals.shape[0]].set(vals.astype(F32))


def _layer(x, w_in, cmp_pos, cmp_w1, cmp_w2, w_nsa_out, gdn_conv_w, gdn_a_log, gdn_dt_bias, gdn_norm_w,
           w_gdn_out, w_o, ln1_g, ln1_b, ffn_w_up, ffn_conv_w, ffn_w_down, ln2_g, ln2_b):
    b, s, d = x.shape
    m = b * s
    x2 = x.reshape(m, d)
    w_rows, w_gdn, w_t, w_g = _wprep(*w_in)
    keys, cmpkv, ggate, mgate, qvt, gt = _inproj(x2, w_rows, w_t, w_g)

    consts = _nsa_consts(s)
    post, w2sel = _compress_weights(cmp_pos, cmp_w2)
    cmp_kv = _compress(cmpkv.reshape(b, s // CMP_STRIDE, CMP_STRIDE * CMP_COLS), cmp_w1, post, w2sel,
                       consts["cmp_aug"])
    o_nsa = _nsa(qvt, keys.reshape(b, s, KEYS_COLS), cmp_kv, gt, consts, b, s)

    ck = GDN_CHUNK
    tri = np.tril(np.ones((ck, ck), np.float32))
    lt = jnp.asarray(np.kron(np.eye(GDN_BLK // ck, dtype=np.float32), tri))
    nq, co, eg = _gdn_intra(x, w_gdn, gdn_conv_w, _lane_vec(gdn_a_log, SM_DECAY), _lane_vec(gdn_dt_bias, SM_DECAY),
                            lt)
    o_gdn = _gdn_scan(nq, co, eg, ggate.reshape(b, s, GDN_W), gdn_norm_w.reshape(1, GDN_HEAD_DIM))

    x1 = _merge(x2, o_nsa.reshape(m, NSA_Q_W), o_gdn.reshape(m, GDN_W), mgate,
                     w_nsa_out.astype(BF16), w_gdn_out.astype(BF16), w_o.astype(BF16),
                     ln1_g.reshape(1, d), ln1_b.reshape(1, d))
    out = _ffn(x1, ffn_w_up.astype(BF16), ffn_conv_w, ffn_w_down.astype(BF16),
               ln2_g.reshape(1, d), ln2_b.reshape(1, d), s)
    return out.reshape(b, s, d)


def kernel(x, w_in, nsa_cmp_pos, nsa_cmp_w1, nsa_cmp_w2, w_nsa_out, gdn_conv_w, gdn_a_log, gdn_dt_bias, gdn_norm_w, w_gdn_out, w_o, ln1_g, ln1_b, ffn_w_up, ffn_conv_w, ffn_w_down, ln2_g, ln2_b):
    w_in_t = jnp.swapaxes(w_in, 1, 2)
    for l in range(DEPTH):
        x = _layer(x, (w_in_t, l), nsa_cmp_pos[l], nsa_cmp_w1[l], nsa_cmp_w2[l], w_nsa_out[l], gdn_conv_w[l],
                   gdn_a_log[l], gdn_dt_bias[l], gdn_norm_w[l], w_gdn_out[l], w_o[l], ln1_g[l], ln1_b[l],
                   ffn_w_up[l], ffn_conv_w[l], ffn_w_down[l], ln2_g[l], ln2_b[l])
    return x
```

```python
import functools

import numpy as np
import jax
import jax.numpy as jnp
from jax import lax
from jax.experimental import pallas as pl
from jax.experimental.pallas import tpu as pltpu

F32 = jnp.float32
BF16 = jnp.bfloat16

D_MODEL = 1024
NSA_HEADS = 8
NSA_KV_GROUPS = 2
NSA_REP = NSA_HEADS // NSA_KV_GROUPS
NSA_HEAD_DIM = 64
CMP_LEN = 32
CMP_STRIDE = 16
SLC_LEN = 64
SLC_TOPK = 8
WIN_LEN = 512
FORCE_SCORE = 1.0e4
NEG = -1.0e30
GDN_HEADS = 4
GDN_HEAD_DIM = 128
GDN_CONV = 4
GDN_CHUNK = 64
FFN_DIM = 2816
FFN_CONV = 3
DEPTH = 1
DEEPNORM_ALPHA = (2.0 * DEPTH) ** 0.25
LN_EPS = 1e-5
RMS_EPS = 1e-6

NSA_Q_W = NSA_HEADS * NSA_HEAD_DIM
NSA_KV_W = NSA_KV_GROUPS * NSA_HEAD_DIM
GDN_W = GDN_HEADS * GDN_HEAD_DIM

LANES = 128
SUBLANES = 8
VMEM_LIMIT_BYTES = 56 * 1024 * 1024

AUG_SEL0 = 64
AUG_POS_HI = 96
AUG_POS_LO = 97
AUG_PAD = 98
BIG = 2.0 ** 100
POS_SPLIT = 256
Q_TILE = 128
N_SLC = 32
V_ROWS = 80
SEL_KC = 512
NSA_SUB = 4

NT_DIMS = (((1,), (1,)), ((), ()))


def _dot(a, b, **kw):
    return jnp.dot(a, b, preferred_element_type=F32, **kw)


def _dot_nt(a, b, **kw):
    return lax.dot_general(a, b, NT_DIMS, preferred_element_type=F32, **kw)


def _cparams(*sem):
    return pltpu.CompilerParams(dimension_semantics=sem, vmem_limit_bytes=VMEM_LIMIT_BYTES)


def _const_spec(shape):
    nd = len(shape)
    return pl.BlockSpec(shape, lambda *_: (0,) * nd, pipeline_mode=pl.Buffered(1))


_IN_WIDTHS = (NSA_Q_W,) + (NSA_KV_W,) * 6 + (3 * NSA_HEADS, 3 * GDN_W, GDN_HEADS, GDN_HEADS, GDN_W, 2 * D_MODEL)
(_C_Q, _C_CK, _C_CV, _C_SK, _C_SV, _C_WK, _C_WV, _C_GATE, _C_GQKV, _C_BETA, _C_DECAY, _C_GGATE, _C_MERGE,
 IN_WIDTH) = (int(v) for v in np.cumsum((0,) + _IN_WIDTHS))
_C_SMALL = _C_BETA // LANES * LANES
SM_BETA = _C_BETA - _C_SMALL
SM_DECAY = _C_DECAY - _C_SMALL
KEYS_COLS = 2 * NSA_KV_W
CMP_COLS = 2 * NSA_KV_W
_INPROJ_GROUPS = (("keys", KEYS_COLS, BF16), ("cmp", CMP_COLS, BF16), ("ggate", GDN_W, BF16),
                  ("merge", 2 * D_MODEL, BF16))
_INPROJ_WIDTH = sum(w for _, w, _ in _INPROJ_GROUPS)
_GDN_PROJ_WIDTH = 3 * GDN_W + LANES
_INPROJ_T_ROWS = NSA_Q_W + 4 * NSA_HEAD_DIM
_GATE_T_ROWS = 32
INPROJ_TM = 1024
INPROJ_TN = 512
WPREP_TK = 128


def _wprep_body(w_ref, rows_ref, gdn_ref, wt_ref, wg_ref):
    hd = NSA_HEAD_DIM
    feat = lambda c0, n: w_ref[0, c0:c0 + n, :]

    def put_t(ref, col, src):
        for r in range(0, src.shape[0], LANES):
            ref[:, col + r:col + r + LANES] = src[r:r + LANES].T.astype(BF16)

    for g in range(NSA_KV_GROUPS):
        put_t(rows_ref, 2 * g * hd, jnp.concatenate([feat(_C_SK + g * hd, hd), feat(_C_WK + g * hd, hd)], axis=0))
    c = 4 * hd
    for c0, n in ((_C_CK, 2 * NSA_KV_W), (_C_GGATE, GDN_W), (_C_MERGE, 2 * D_MODEL)):
        put_t(rows_ref, c, feat(c0, n))
        c += n
    put_t(gdn_ref, 0, feat(_C_GQKV, 3 * GDN_W))
    put_t(gdn_ref, 3 * GDN_W, feat(_C_SMALL, LANES))
    wt_ref[:NSA_Q_W, :] = feat(_C_Q, NSA_Q_W).astype(BF16)
    for j, c0 in enumerate((_C_SV, _C_WV, _C_SV + hd, _C_WV + hd)):
        wt_ref[NSA_Q_W + j * hd:NSA_Q_W + (j + 1) * hd, :] = feat(c0, hd).astype(BF16)
    wg_ref[...] = feat(_C_GATE, _GATE_T_ROWS).astype(BF16)


def _wprep(w_in_t, layer):
    k = w_in_t.shape[2]
    tk = WPREP_TK
    return pl.pallas_call(
        _wprep_body,
        grid=(k // tk,),
        in_specs=[pl.BlockSpec((1, IN_WIDTH, tk), lambda i: (layer, 0, i))],
        out_specs=[pl.BlockSpec((tk, _INPROJ_WIDTH), lambda i: (i, 0)),
                   pl.BlockSpec((tk, _GDN_PROJ_WIDTH), lambda i: (i, 0)),
                   pl.BlockSpec((_INPROJ_T_ROWS, tk), lambda i: (0, i)),
                   pl.BlockSpec((_GATE_T_ROWS, tk), lambda i: (0, i))],
        out_shape=[jax.ShapeDtypeStruct((k, _INPROJ_WIDTH), BF16), jax.ShapeDtypeStruct((k, _GDN_PROJ_WIDTH), BF16),
                   jax.ShapeDtypeStruct((_INPROJ_T_ROWS, k), BF16), jax.ShapeDtypeStruct((_GATE_T_ROWS, k), BF16)],
        compiler_params=_cparams("parallel"),
        name="wprep",
    )(w_in_t)


def _inproj_body(x_ref, w_ref, wt_ref, wg_ref, keys_ref, cmp_ref, ggate_ref, merge_ref, qvt_ref, gt_ref, cmp_s):
    x = x_ref[...].astype(BF16)
    outs = (keys_ref, None, ggate_ref, merge_ref)
    c0 = 0
    for ref, (name, width, _) in zip(outs, _INPROJ_GROUPS):
        for s in range(0, width, INPROJ_TN):
            e = min(s + INPROJ_TN, width)
            res = _dot(x, w_ref[:, c0 + s:c0 + e])
            if name == "cmp":
                for j in range(width // LANES):
                    cmp_s[j] = res[:, j * LANES:(j + 1) * LANES]
            else:
                ref[:, s:e] = res.astype(ref.dtype)
        c0 += width
    nblk = cmp_ref.shape[0]
    for l in range(CMP_STRIDE):
        for j in range(cmp_s.shape[0]):
            cmp_ref[:, l * CMP_COLS + j * LANES:l * CMP_COLS + (j + 1) * LANES] = (
                cmp_s[j, pl.ds(l, nblk, stride=CMP_STRIDE), :].astype(BF16))
    for s in range(0, _INPROJ_T_ROWS, 2 * LANES):
        qvt_ref[s:s + 2 * LANES, :] = _dot_nt(wt_ref[s:s + 2 * LANES, :], x).astype(qvt_ref.dtype)
    gt_ref[...] = _dot_nt(wg_ref[...], x)


def _inproj(x2, w_rows, w_t, w_g):
    m = x2.shape[0]
    tm = INPROJ_TM
    row_major = [(n, wd, dt) for n, wd, dt in _INPROJ_GROUPS if n != "cmp"]
    specs = {n: (pl.BlockSpec((tm, wd), lambda i: (i, 0)), jax.ShapeDtypeStruct((m, wd), dt))
             for n, wd, dt in row_major}
    specs["cmp"] = (pl.BlockSpec((tm // CMP_STRIDE, CMP_STRIDE * CMP_COLS), lambda i: (i, 0)),
                    jax.ShapeDtypeStruct((m // CMP_STRIDE, CMP_STRIDE * CMP_COLS), BF16))
    order = [n for n, _, _ in _INPROJ_GROUPS]
    return pl.pallas_call(
        _inproj_body,
        grid=(m // tm,),
        in_specs=[pl.BlockSpec((tm, D_MODEL), lambda i: (i, 0)), _const_spec(w_rows.shape),
                  _const_spec(w_t.shape), _const_spec(w_g.shape)],
        out_specs=[specs[n][0] for n in order]
        + [pl.BlockSpec((_INPROJ_T_ROWS, tm), lambda i: (0, i)), pl.BlockSpec((_GATE_T_ROWS, tm), lambda i: (0, i))],
        out_shape=[specs[n][1] for n in order]
        + [jax.ShapeDtypeStruct((_INPROJ_T_ROWS, m), BF16), jax.ShapeDtypeStruct((_GATE_T_ROWS, m), F32)],
        scratch_shapes=[pltpu.VMEM((CMP_COLS // LANES, tm, LANES), F32)],
        compiler_params=_cparams("parallel"),
        name="inproj",
    )(x2, w_rows, w_t, w_g)


def _compress_weights(cmp_pos, cmp_w2):
    hd, half = NSA_HEAD_DIM, CMP_LEN // 2
    posr = cmp_pos.reshape(2, 2, half, hd)
    post = jnp.broadcast_to(posr.transpose(1, 2, 0, 3)[:, :, :, None, :], (2, half, 2, 2, hd))
    post = jnp.concatenate([post.reshape(2, half * 4 * hd),
                            jnp.zeros((SUBLANES - 2, half * 4 * hd), cmp_pos.dtype)], axis=0)
    w2sel = jnp.zeros((2, 2, 2 * hd, LANES), cmp_w2.dtype)
    for g in range(2):
        w2sel = w2sel.at[:, g, g * hd:(g + 1) * hd, :hd].set(cmp_w2)
    return post.astype(BF16), w2sel.reshape(4, 2 * hd, LANES).astype(BF16)


def _compress_body(t_ref, w1_ref, pos_ref, w2_ref, aug_ref, out_ref, w1e_s):
    hd, half_len = NSA_HEAD_DIM, CMP_LEN // 2

    @pl.when(pl.program_id(0) == 0)
    def _():
        w1e_s[...] = jnp.zeros(w1e_s.shape, BF16)
        for which in range(2):
            for half in range(2):
                for l in range(half_len):
                    blk = w1_ref[which, (half * half_len + l) * hd:(half * half_len + l + 1) * hd, :].astype(BF16)
                    for g in range(NSA_KV_GROUPS):
                        r0 = l * CMP_COLS + which * LANES + g * hd
                        c0 = half * CMP_COLS + which * LANES + g * hd
                        w1e_s[r0:r0 + hd, c0:c0 + hd] = blk

    p = _dot(t_ref[0], w1e_s[...])
    pp = _dot(pos_ref[...], w1e_s[...])
    nxt = pltpu.roll(p[:, CMP_COLS:], p.shape[0] - 1, axis=0)
    pre = p[:, :CMP_COLS] + nxt + pp[0:1, :CMP_COLS] + pp[1:2, CMP_COLS:]
    h = jax.nn.gelu(pre).astype(BF16)
    n_idx = lax.broadcasted_iota(jnp.int32, (p.shape[0], LANES), 0)
    real = n_idx < p.shape[0] - 1
    for which in range(2):
        hw = h[:, which * LANES:(which + 1) * LANES]
        for g in range(2):
            o = jnp.where(real, _dot(hw, w2_ref[which * 2 + g]) + aug_ref[which], 0.0)
            out_ref[0, which * 2 + g] = (o if which == 0 else o.T).astype(out_ref.dtype)


def _compress(t2, w1, post, w2sel, aug):
    b, nblk, _ = t2.shape
    return pl.pallas_call(
        _compress_body,
        grid=(b,),
        in_specs=[pl.BlockSpec((1, nblk, CMP_STRIDE * CMP_COLS), lambda i: (i, 0, 0)),
                  _const_spec(w1.shape), _const_spec(post.shape), _const_spec(w2sel.shape), _const_spec(aug.shape)],
        out_specs=pl.BlockSpec((1, 4, nblk, LANES), lambda i: (i, 0, 0, 0)),
        out_shape=jax.ShapeDtypeStruct((b, 4, nblk, LANES), BF16),
        scratch_shapes=[pltpu.VMEM((CMP_STRIDE * CMP_COLS, 2 * CMP_COLS), BF16)],
        compiler_params=_cparams("arbitrary"),
        name="compress",
    )(t2, w1, post, w2sel, aug)


def _nsa_consts(s):
    hd, rep = NSA_HEAD_DIM, NSA_REP
    t = np.arange(s)
    kx_win = np.zeros((s + WIN_LEN, hd), np.float32)
    kx_win[WIN_LEN + t, AUG_POS_HI - hd] = t // POS_SPLIT
    kx_win[WIN_LEN + t, AUG_POS_LO - hd] = t % POS_SPLIT
    kx_win[:WIN_LEN, AUG_PAD - hd] = 1.0
    kx_sel = kx_win.copy()
    kx_sel[WIN_LEN + t, t // SLC_LEN] = 1.0
    vx_win = np.zeros((V_ROWS - hd, s + WIN_LEN), np.float32)
    vx_win[0, WIN_LEN:] = 1.0
    vx_sel = vx_win
    n_cmp = s // CMP_STRIDE
    cmp_aug = np.zeros((2, n_cmp, LANES), np.float32)
    end = np.arange(n_cmp) * CMP_STRIDE + CMP_LEN - 1
    cmp_aug[0, :, AUG_POS_HI] = end // POS_SPLIT
    cmp_aug[0, :, AUG_POS_LO] = end % POS_SPLIT
    qx = np.zeros((NSA_KV_GROUPS, LANES - AUG_POS_HI, rep * Q_TILE), np.float32)
    for h in range(NSA_HEADS):
        slope = 2.0 ** (-8.0 * (h + 1) / NSA_HEADS)
        lanes = slice((h % rep) * Q_TILE, (h % rep + 1) * Q_TILE)
        qx[h // rep, 0, lanes] = slope * POS_SPLIT
        qx[h // rep, 1, lanes] = slope
        qx[h // rep, AUG_PAD - AUG_POS_HI, lanes] = -BIG
    c0 = np.arange(n_cmp)[None, :] * CMP_STRIDE
    s0 = np.arange(s // SLC_LEN)[:, None] * SLC_LEN
    ov_t = ((c0 < s0 + SLC_LEN) & (c0 + CMP_LEN > s0)).astype(np.float32)
    ov_t[:, (s - CMP_LEN) // CMP_STRIDE + 1:] = 0.0
    kk = np.arange(Q_TILE)[:, None]
    qq = np.arange(Q_TILE)[None, :]
    causal = np.tile(np.where(kk <= qq, 0.0, NEG).astype(np.float32), (1, rep))
    after = np.tile(np.where(kk > qq, 0.0, NEG).astype(np.float32), (1, rep))
    j = jnp.asarray
    return dict(kx_sel=j(kx_sel, BF16), kx_win=j(kx_win, BF16), vx_sel=j(vx_sel, BF16), vx_win=j(vx_win, BF16),
                cmp_aug=j(cmp_aug), qx=j(qx), ov_t=j(ov_t), causal=j(causal), after=j(after))


def _nsa_body(qt_ref, k_ref, vt_ref, kc_ref, vct_ref, gt_ref, kxs_ref, kxw_ref, vxs_ref, vxw_ref, qx_ref, ovt_ref,
              causal_ref, after_ref, out_ref, ks_s, kw_s, vs_s, vw_s):
    hd, rep, tq = NSA_HEAD_DIM, NSA_REP, Q_TILE
    nq = rep * tq
    i = pl.program_id(2)

    @pl.when(i == 0)
    def _():
        keys = k_ref[0]
        ks_s[:WIN_LEN, :hd] = jnp.zeros((WIN_LEN, hd), BF16)
        ks_s[WIN_LEN:, :hd] = keys[:, :hd]
        ks_s[:, hd:] = kxs_ref[...]
        kw_s[:WIN_LEN, :hd] = jnp.zeros((WIN_LEN, hd), BF16)
        kw_s[WIN_LEN:, :hd] = keys[:, hd:]
        kw_s[:, hd:] = kxw_ref[...]
        vals = vt_ref[...]
        vs_s[:hd, :WIN_LEN] = jnp.zeros((hd, WIN_LEN), BF16)
        vs_s[:hd, WIN_LEN:] = vals[:hd]
        vs_s[hd:, :] = vxs_ref[...]
        vw_s[:hd, :WIN_LEN] = jnp.zeros((hd, WIN_LEN), BF16)
        vw_s[:hd, WIN_LEN:] = vals[hd:]
        vw_s[hd:, :] = vxw_ref[...]

    qx = qx_ref[0]
    sg_all = jax.nn.sigmoid(gt_ref[...])
    grp = pl.program_id(1)

    full_past = WIN_LEN // tq

    def front(sub, res, it, past):
        qt = qt_ref[:, sub * tq:(sub + 1) * tq]
        q64 = jnp.concatenate([qt[r * hd:(r + 1) * hd, :] for r in range(rep)], axis=1).astype(F32) * (hd ** -0.5)

        def q_aug(sel_rows):
            return jnp.concatenate([q64, sel_rows, qx], axis=0).astype(BF16)

        n_row = lax.broadcasted_iota(jnp.int32, (LANES, nq), 0)
        t_lane = it * tq + (lax.broadcasted_iota(jnp.int32, (LANES, nq), 1) & (tq - 1))
        valid = t_lane >= n_row * CMP_STRIDE + (CMP_LEN - 1)
        qa0 = q_aug(jnp.zeros((N_SLC, nq), F32))
        sc = jnp.where(valid, _dot(kc_ref[0, 0], qa0), NEG)
        mc = jnp.max(sc, axis=0, keepdims=True)
        ec = jnp.where(valid, jnp.exp(sc - mc), 0.0)
        lc = jnp.sum(ec, axis=0, keepdims=True)
        pc = ec * jnp.where(lc > 0.0, 1.0 / lc, 0.0)
        o_cmp = _dot(vct_ref[0, 0], pc.astype(BF16))[:hd]
        psum = pc[:, 0:tq] + pc[:, tq:2 * tq] + pc[:, 2 * tq:3 * tq] + pc[:, 3 * tq:4 * tq]
        score_t = _dot(ovt_ref[...], psum, precision=lax.Precision.HIGHEST)
        yield

        n_keys = (past + 1) * tq
        if past == full_past:
            w0 = pl.multiple_of(it * tq, tq)
        else:
            w0 = WIN_LEN
        diag = past * tq

        def one_shot_masks(s):
            parts = [s[:diag], s[diag:] + causal_ref[...]] if past else [s + causal_ref[...]]
            if past == full_past:
                parts = [s[:tq] + after_ref[...], s[tq:diag], parts[1]]
            return jnp.concatenate(parts, axis=0) if len(parts) > 1 else parts[0]

        s_w = one_shot_masks(_dot(kw_s[pl.ds(w0, n_keys), :], qa0))
        p_w = jnp.exp(s_w - jnp.max(s_w, axis=0, keepdims=True))
        acc_w = _dot(vw_s[:, pl.ds(w0, n_keys)], p_w.astype(BF16))
        o_win = acc_w[:hd] * (1.0 / acc_w[hd:hd + 1])
        yield

        jb = lax.broadcasted_iota(jnp.int32, (N_SLC, tq), 0)
        cur = (it * tq + lax.broadcasted_iota(jnp.int32, (N_SLC, tq), 1)) // SLC_LEN
        forced = (jb == 0) | (jb == cur) | (jb == cur - 1)
        score_t = jnp.where(forced, FORCE_SCORE, jnp.where(jb <= cur, score_t, -1.0))
        rank = jnp.zeros((N_SLC, tq), F32)
        for jp in range(N_SLC):
            other = score_t[jp:jp + 1, :]
            ge = jnp.where(other >= score_t, 1.0, 0.0)
            gt = jnp.where(other > score_t, 1.0, 0.0)
            rank = rank + jnp.where(jb > jp, ge, gt)
        sel = rank < float(SLC_TOPK)
        qa = q_aug(jnp.concatenate([jnp.where(sel, 0.0, -BIG)] * rep, axis=1))
        lo_blk = jnp.min(jnp.where(sel & (jb >= 2) & (jb <= cur), jb.astype(F32), float(N_SLC)))
        lo_key = (lo_blk.astype(jnp.int32) // 2) * tq
        yield

        s_main = _dot(ks_s[pl.ds(w0, n_keys), :], qa)
        if past < full_past:
            s_s = jnp.concatenate([s_main[:diag], s_main[diag:] + causal_ref[...]], axis=0) if past else (
                s_main + causal_ref[...])
            m_s = jnp.max(s_s, axis=0, keepdims=True)
            acc_s = _dot(vs_s[:, pl.ds(w0, n_keys)], jnp.exp(s_s - m_s).astype(BF16))
            res.update(early=False, acc_s=acc_s, o_cmp=o_cmp, o_win=o_win)
            return
        e_key = it * tq - WIN_LEN
        t0 = pl.multiple_of(jnp.where(e_key > 0, WIN_LEN, 0), tq)
        s_s = jnp.concatenate([_dot(ks_s[pl.ds(t0, tq), :], qa), s_main[:diag],
                               s_main[diag:] + causal_ref[...]], axis=0)
        m_s = jnp.max(s_s, axis=0, keepdims=True)
        p_s = jnp.exp(s_s - m_s).astype(BF16)
        acc_s = _dot(vs_s[:, pl.ds(t0, tq)], p_s[:tq]) + _dot(vs_s[:, pl.ds(w0, n_keys)], p_s[tq:])
        c_hi = (e_key - tq + SEL_KC - 1) // SEL_KC
        c_lo = jnp.where(lo_key < e_key, (lo_key - tq) // SEL_KC, c_hi)
        res.update(early=True, qa=qa, e_key=e_key, c_lo=c_lo, c_hi=c_hi, m_s=m_s, acc_s=acc_s, o_cmp=o_cmp,
                   o_win=o_win)

    def tail(sub, f):
        def early_step(c, carry):
            qa, e_key = f["qa"], f["e_key"]
            m, acc = carry
            k0 = tq + c * SEL_KC
            start = pl.multiple_of(WIN_LEN + k0, tq)
            k_abs = k0 + lax.broadcasted_iota(jnp.int32, (SEL_KC, nq), 0)
            s = jnp.where(k_abs < e_key, _dot(ks_s[pl.ds(start, SEL_KC), :], qa), NEG)
            m_new = jnp.maximum(m, jnp.max(s, axis=0, keepdims=True))
            p = jnp.exp(s - m_new).astype(BF16)
            return m_new, acc * jnp.exp(m - m_new) + _dot(vs_s[:, pl.ds(start, SEL_KC)], p)

        acc_s = f["acc_s"]
        if f["early"]:
            _, acc_s = lax.fori_loop(f["c_lo"], f["c_hi"], early_step, (f["m_s"], acc_s))
        o_slc = acc_s[:hd] * (1.0 / acc_s[hd:hd + 1])

        sg = sg_all[:, sub * tq:(sub + 1) * tq]
        gate = lambda br, r: jnp.where(grp == 0, sg[br * NSA_HEADS + r:br * NSA_HEADS + r + 1],
                                       sg[br * NSA_HEADS + rep + r:br * NSA_HEADS + rep + r + 1])
        for pair in range(rep // 2):
            halves = []
            for r in (2 * pair, 2 * pair + 1):
                lanes = slice(r * tq, (r + 1) * tq)
                halves.append(gate(0, r) * f["o_cmp"][:, lanes] + gate(1, r) * o_slc[:, lanes]
                              + gate(2, r) * f["o_win"][:, lanes])
            out_ref[0, sub * tq:(sub + 1) * tq, pair * LANES:(pair + 1) * LANES] = (
                jnp.concatenate(halves, axis=0).T.astype(out_ref.dtype))

    def run(tiles):
        fronts = [{} for _ in range(NSA_SUB)]
        for sub, (it, past) in enumerate(tiles):
            for _ in front(sub, fronts[sub], it, past):
                pass
        for sub in range(NSA_SUB):
            tail(sub, fronts[sub])

    n_short = -(-full_past // NSA_SUB)
    for step in range(n_short):
        @pl.when(i == step)
        def _(step=step):
            run([(step * NSA_SUB + sub, min(step * NSA_SUB + sub, full_past)) for sub in range(NSA_SUB)])

    @pl.when(i >= n_short)
    def _():
        run([(i * NSA_SUB + sub, full_past) for sub in range(NSA_SUB)])


def _nsa(qvt, keys, cmp_kv, gt, consts, b, s):
    tqs = NSA_SUB * Q_TILE
    nt = s // tqs
    c = consts
    in_specs = [
        pl.BlockSpec((2 * LANES, tqs), lambda bi, g, i: (g, bi * nt + i)),
        pl.BlockSpec((1, s, LANES), lambda bi, g, i: (bi, 0, g)),
        pl.BlockSpec((LANES, s), lambda bi, g, i: (NSA_Q_W // LANES + g, bi)),
        pl.BlockSpec((1, 1, s // CMP_STRIDE, LANES), lambda bi, g, i: (bi, g, 0, 0)),
        pl.BlockSpec((1, 1, s // CMP_STRIDE, LANES), lambda bi, g, i: (bi, 2 + g, 0, 0)),
        pl.BlockSpec((_GATE_T_ROWS, tqs), lambda bi, g, i: (0, bi * nt + i)),
        _const_spec(c["kx_sel"].shape), _const_spec(c["kx_win"].shape), _const_spec(c["vx_sel"].shape),
        _const_spec(c["vx_win"].shape),
        pl.BlockSpec((1,) + c["qx"].shape[1:], lambda bi, g, i: (g, 0, 0)),
        _const_spec(c["ov_t"].shape), _const_spec(c["causal"].shape), _const_spec(c["after"].shape),
    ]
    return pl.pallas_call(
        _nsa_body,
        grid=(b, NSA_KV_GROUPS, nt),
        in_specs=in_specs,
        out_specs=pl.BlockSpec((1, tqs, 2 * LANES), lambda bi, g, i: (bi, i, g)),
        out_shape=jax.ShapeDtypeStruct((b, s, NSA_Q_W), BF16),
        scratch_shapes=[pltpu.VMEM((s + WIN_LEN, LANES), BF16), pltpu.VMEM((s + WIN_LEN, LANES), BF16),
                        pltpu.VMEM((V_ROWS, s + WIN_LEN), BF16), pltpu.VMEM((V_ROWS, s + WIN_LEN), BF16)],
        compiler_params=_cparams("parallel", "parallel", "arbitrary"),
        name="nsa",
    )(qvt, keys, qvt, cmp_kv, cmp_kv, gt, c["kx_sel"], c["kx_win"], c["vx_sel"], c["vx_win"], c["qx"], c["ov_t"],
      c["causal"], c["after"])


GDN_TS = 512
GDN_BLK = 128
GDN_HALO = 8
GDN_XHALO = 16
GDN_INV_BASE = 8
GDN_CHAIN_UNITS_PER_PREP_UNIT = 4
GDN_SCAN_ROWS = GDN_HEAD_DIM + GDN_CHUNK


def _gdn_prep(x_ref, prev_ref, w_ref, cw_ref, alog_ref, dtb_ref, lt_ref, first_tile, xp_s, buf):
    x_s, p_s, rhs_s, qg_s, aqk_s, kdt_s, a_s, eg_s = buf
    ts, dh, nh, blk = GDN_TS, GDN_HEAD_DIM, GDN_HEADS, GDN_BLK
    proj = _dot(x_ref[0].astype(BF16), w_ref[...])
    hist = _dot(prev_ref[0].astype(BF16), w_ref[:, :3 * GDN_W])[GDN_XHALO - GDN_HALO:]
    xp_s[0:GDN_HALO, :] = jnp.where(first_tile, 0.0, hist)
    xp_s[GDN_HALO:, :] = proj[:, :3 * GDN_W]
    yield
    act = []
    for blk_i in range(3 * nh):
        lanes = slice(blk_i * dh, (blk_i + 1) * dh)
        xp = xp_s[:, lanes]
        conv = cw_ref[0:1, lanes] * xp
        for j in range(1, GDN_CONV):
            conv = pltpu.roll(conv, 1, axis=0) + cw_ref[j:j + 1, lanes] * xp
        act.append(jax.nn.silu(conv[GDN_HALO:]))
        yield

    sm = proj[:, 3 * GDN_W:]
    beta = jax.nn.sigmoid(sm)
    g = -jnp.exp(alog_ref[...]) * jax.nn.softplus(sm + dtb_ref[...])
    gcum = jnp.concatenate([_dot(lt_ref[...], g[r:r + blk], precision=lax.Precision.HIGHEST)
                            for r in range(0, ts, blk)], axis=0)
    eg = jnp.exp(gcum)
    eg_s[...] = eg

    ri = lax.broadcasted_iota(jnp.int32, (blk, blk), 0)
    ci = lax.broadcasted_iota(jnp.int32, (blk, blk), 1)
    same = (ri // GDN_CHUNK) == (ci // GDN_CHUNK)
    causal = same & (ri >= ci)
    strict = same & (ri > ci)
    eye = (ri == ci).astype(F32)
    same_base = (ri // GDN_INV_BASE) == (ci // GDN_INV_BASE)

    for pb in range(ts // blk):
        rows = slice(pb * blk, (pb + 1) * blk)
        gc = gcum[rows]
        gc_t = gc.T
        first = lax.broadcasted_iota(jnp.int32, (blk, LANES), 0) < GDN_CHUNK
        g_last = jnp.where(first, gc[GDN_CHUNK - 1:GDN_CHUNK, :], gc[blk - 1:blk, :])
        e_dec = jnp.exp(g_last - gc)
        for h in range(nh):
            q, k, v = act[h][rows], act[nh + h][rows], act[2 * nh + h][rows]
            q = q * lax.rsqrt(jnp.sum(q * q, axis=1, keepdims=True) + RMS_EPS) * (dh ** -0.5)
            k = k * lax.rsqrt(jnp.sum(k * k, axis=1, keepdims=True) + RMS_EPS)
            b_col = beta[rows, SM_BETA + h:SM_BETA + h + 1]
            eg_col = eg[rows, SM_DECAY + h:SM_DECAY + h + 1]
            gdiff = gc[:, SM_DECAY + h:SM_DECAY + h + 1] - gc_t[SM_DECAY + h:SM_DECAY + h + 1, :]
            decay = jnp.exp(jnp.where(causal, gdiff, NEG))
            kb = k * b_col
            kbf, kf, qf = kb.astype(BF16), k.astype(BF16), q.astype(BF16)
            a = jnp.where(strict, -_dot_nt(kbf, kf) * decay, 0.0)
            c = pb * nh + h
            a_base = jnp.where(same_base, a, 0.0)
            a_s[c] = a.astype(BF16)
            x_s[c] = a_base.astype(BF16)
            p_s[c] = eye + a_base
            rhs_s[c] = jnp.concatenate([v * b_col, kb * eg_col], axis=1).astype(BF16)
            qg_s[c] = q * eg_col
            aqk_s[c] = jnp.where(causal, _dot_nt(qf, kf) * decay, 0.0).astype(BF16)
            kdt_s[c] = (k * e_dec[:, SM_DECAY + h:SM_DECAY + h + 1]).T.astype(BF16)
            yield


def _gdn_chains(buf, nq_ref, co_ref, eg_ref):
    x_s, p_s, rhs_s, qg_s, aqk_s, kdt_s, a_s, eg_s = buf
    dh, nh, blk = GDN_HEAD_DIM, GDN_HEADS, GDN_BLK
    n_chain = x_s.shape[0]
    eg_ref[0] = eg_s[...]
    ri = lax.broadcasted_iota(jnp.int32, (blk, blk), 0)
    ci = lax.broadcasted_iota(jnp.int32, (blk, blk), 1)
    same = lambda size: (ri // size) == (ci // size)
    for c in range(n_chain):
        y = x_s[c]
        x_s[c] = _dot(y, y).astype(BF16)
        yield
    for c in range(n_chain):
        y2 = x_s[c]
        p = p_s[c]
        p_s[c] = p + _dot(p.astype(BF16), y2)
        x_s[c] = _dot(y2, y2).astype(BF16)
        yield
    for c in range(n_chain):
        p = p_s[c]
        p_s[c] = p + _dot(p.astype(BF16), x_s[c])
        yield
    size = 2 * GDN_INV_BASE
    while size <= GDN_CHUNK:
        between = same(size) & jnp.logical_not(same(size // 2))
        for c in range(n_chain):
            a = a_s[c]
            x_s[c] = _dot(p_s[c].astype(BF16), jnp.where(between, a, jnp.zeros_like(a))).astype(BF16)
            yield
        for c in range(n_chain):
            t = p_s[c]
            p_s[c] = t + _dot(x_s[c], t.astype(BF16))
            yield
        size *= 2

    tok_half = lax.broadcasted_iota(jnp.int32, (blk, blk), 1) // GDN_CHUNK
    for c in range(n_chain):
        rhs_s[c] = _dot(p_s[c].astype(BF16), rhs_s[c]).astype(BF16)
        yield
    for c in range(n_chain):
        pb, h = divmod(c, nh)
        uw = rhs_s[c]
        a1 = _dot(aqk_s[c], uw)
        q_loc = qg_s[c] - a1[:, dh:]
        kdt = kdt_s[c]
        for half in range(blk // GDN_CHUNK):
            k1 = _dot(jnp.where(tok_half == half, kdt, jnp.zeros_like(kdt)), uw)
            n = pb * (blk // GDN_CHUNK) + half
            rows = slice(half * GDN_CHUNK, (half + 1) * GDN_CHUNK)
            nq_ref[0, h, n, :dh, :] = (-k1[:, dh:]).astype(BF16)
            nq_ref[0, h, n, dh:, :] = q_loc[rows].astype(BF16)
            co_ref[0, h, n, :dh, :] = k1[:, :dh].astype(BF16)
            co_ref[0, h, n, dh:, :] = a1[rows, :dh].astype(BF16)
        yield


def _interleave(major, minor, minor_per_major):
    for _ in major:
        for _ in range(minor_per_major):
            next(minor, None)
    for _ in minor:
        pass


def _gdn_intra_body(x_ref, prev_ref, w_ref, cw_ref, alog_ref, dtb_ref, lt_ref, nq_ref, co_ref, eg_ref, xp_s, *bufs,
                    tiles_per_seq, n_buf):
    j = pl.program_id(0)
    sets = (bufs[:n_buf], bufs[n_buf:])

    @pl.when(j == 0)
    def _():
        for ref in sets[1]:
            ref[...] = jnp.zeros(ref.shape, ref.dtype)

    first_tile = (j % tiles_per_seq) == 0
    for parity in range(2):
        @pl.when(j % 2 == parity)
        def _(parity=parity):
            chains = _gdn_chains(sets[1 - parity], nq_ref, co_ref, eg_ref)
            prep = _gdn_prep(x_ref, prev_ref, w_ref, cw_ref, alog_ref, dtb_ref, lt_ref, first_tile, xp_s,
                             sets[parity])
            _interleave(prep, chains, GDN_CHAIN_UNITS_PER_PREP_UNIT)


def _gdn_intra(x, w_gdn, conv_w, alog_l, dtb_l, lt):
    b, s, d = x.shape
    ts, nh, dh = GDN_TS, GDN_HEADS, GDN_HEAD_DIM
    tps = s // ts
    nt = b * tps
    n_chain = (ts // GDN_BLK) * nh
    sq = lambda dt: pltpu.VMEM((n_chain, GDN_BLK, GDN_BLK), dt)
    buf = lambda: [sq(BF16), sq(F32), pltpu.VMEM((n_chain, GDN_BLK, 2 * dh), BF16), sq(F32), sq(BF16), sq(BF16),
                   sq(BF16), pltpu.VMEM((ts, LANES), F32)]
    src = lambda j: jnp.minimum(j, nt - 1)
    dst = lambda j: jnp.maximum(j - 1, 0)
    cspec = lambda: pl.BlockSpec((1, nh, ts // GDN_CHUNK, GDN_SCAN_ROWS, dh),
                                 lambda j: (dst(j) // tps, 0, dst(j) % tps, 0, 0))
    cshape = jax.ShapeDtypeStruct((b, nh, s // GDN_CHUNK, GDN_SCAN_ROWS, dh), BF16)
    return pl.pallas_call(
        functools.partial(_gdn_intra_body, tiles_per_seq=tps, n_buf=len(buf())),
        grid=(nt + 1,),
        in_specs=[
            pl.BlockSpec((1, ts, d), lambda j: (src(j) // tps, src(j) % tps, 0)),
            pl.BlockSpec((1, GDN_XHALO, d),
                         lambda j: (src(j) // tps, jnp.maximum((src(j) % tps) * (ts // GDN_XHALO) - 1, 0), 0)),
            _const_spec(w_gdn.shape), _const_spec(conv_w.shape), _const_spec(alog_l.shape), _const_spec(dtb_l.shape),
            _const_spec(lt.shape),
        ],
        out_specs=[cspec(), cspec(), pl.BlockSpec((1, ts, LANES), lambda j: (dst(j) // tps, dst(j) % tps, 0))],
        out_shape=[cshape, cshape, jax.ShapeDtypeStruct((b, s, LANES), F32)],
        scratch_shapes=[pltpu.VMEM((ts + GDN_HALO, 3 * GDN_W), F32)] + buf() + buf(),
        compiler_params=_cparams("arbitrary"),
        name="gdn_intra",
    )(x, x, w_gdn, conv_w, alog_l, dtb_l, lt)


def _gdn_scan_body(nq_ref, co_ref, eg_ref, gate_ref, nw_ref, out_ref, st_s):
    nh, dh, ck = GDN_HEADS, GDN_HEAD_DIM, GDN_CHUNK
    st_s[...] = jnp.zeros(st_s.shape, F32)

    def chunk(n, carry):
        r0 = pl.multiple_of(n * ck, ck)
        d_row = eg_ref[0, pl.ds(r0 + ck - 1, 1), :]
        for h in range(nh):
            st = st_s[h]
            res = _dot(nq_ref[0, h, n], st.astype(BF16)) + co_ref[0, h, n].astype(F32)
            st_s[h] = st * d_row[:, SM_DECAY + h:SM_DECAY + h + 1] + res[:dh]
            o = res[dh:]
            ms = jnp.mean(o * o, axis=1, keepdims=True)
            gt = gate_ref[0, pl.ds(r0, ck), h * dh:(h + 1) * dh].astype(F32)
            out_ref[0, pl.ds(r0, ck), h * dh:(h + 1) * dh] = (
                o * lax.rsqrt(ms + RMS_EPS) * nw_ref[...] * jax.nn.silu(gt)).astype(out_ref.dtype)
        return carry

    lax.fori_loop(0, nq_ref.shape[2], chunk, 0, unroll=16)


def _gdn_scan(nq, co, eg, ggate, norm_w):
    b, nh, nc, rows, dh = nq.shape
    s = nc * GDN_CHUNK
    cspec = lambda: pl.BlockSpec((1, nh, nc, rows, dh), lambda bi: (bi, 0, 0, 0, 0))
    return pl.pallas_call(
        _gdn_scan_body,
        grid=(b,),
        in_specs=[cspec(), cspec(),
                  pl.BlockSpec((1, s, LANES), lambda bi: (bi, 0, 0)),
                  pl.BlockSpec((1, s, GDN_W), lambda bi: (bi, 0, 0)),
                  _const_spec(norm_w.shape)],
        out_specs=pl.BlockSpec((1, s, GDN_W), lambda bi: (bi, 0, 0)),
        out_shape=jax.ShapeDtypeStruct((b, s, GDN_W), BF16),
        scratch_shapes=[pltpu.VMEM((nh, dh, dh), F32)],
        compiler_params=_cparams("parallel"),
        name="gdn_scan",
    )(nq, co, eg, ggate, norm_w)


MERGE_TM = 1024
MERGE_SUB = 256


def _layer_norm(y, g, b):
    mu = jnp.mean(y, axis=1, keepdims=True)
    d = y - mu
    var = jnp.mean(d * d, axis=1, keepdims=True)
    return d * lax.rsqrt(var + LN_EPS) * g + b


def _merge_body(x_ref, oa_ref, ob_ref, ga_ref, gb_ref, wa_ref, wb_ref, wo_ref, g_ref, b_ref, y_ref):
    def sub_tile(rows):
        ya = _dot(oa_ref[rows, :], wa_ref[...])
        yb = _dot(ob_ref[rows, :], wb_ref[...])
        yield
        mixin = (jax.nn.sigmoid(ga_ref[rows, :].astype(F32)) * ya
                 + jax.nn.sigmoid(gb_ref[rows, :].astype(F32)) * yb).astype(BF16)
        yield
        mix = _dot(mixin, wo_ref[...])
        yield
        y_ref[rows, :] = _layer_norm(DEEPNORM_ALPHA * x_ref[rows, :] + mix, g_ref[...], b_ref[...])
        yield

    n_stage = 4
    subs = [sub_tile(slice(k * MERGE_SUB, (k + 1) * MERGE_SUB)) for k in range(MERGE_TM // MERGE_SUB)]
    for step in range(n_stage + len(subs) - 1):
        for k, sub in enumerate(subs):
            if 0 <= step - k < n_stage:
                next(sub)


def _merge(x2, oa, ob, mgate, wa, wb, wo, g, b):
    m = x2.shape[0]
    tm, d = MERGE_TM, D_MODEL
    row = lambda wd, col=0: pl.BlockSpec((tm, wd), lambda i, col=col: (i, col))
    return pl.pallas_call(
        _merge_body,
        grid=(m // tm,),
        in_specs=[row(d), row(NSA_Q_W), row(GDN_W), row(d, 0), row(d, 1),
                  _const_spec(wa.shape), _const_spec(wb.shape), _const_spec(wo.shape),
                  _const_spec(g.shape), _const_spec(b.shape)],
        out_specs=row(d),
        out_shape=jax.ShapeDtypeStruct((m, d), F32),
        compiler_params=_cparams("parallel"),
        name="merge",
    )(x2, oa, ob, mgate, mgate, wa, wb, wo, g, b)


FFN_TM = 1024
FFN_HALO = 16
FFN_CK = 256


def _ffn_body(x_ref, prev_ref, wu_ref, cw_ref, wd_ref, g_ref, b_ref, out_ref, act_s, *, tiles_per_seq):
    i = pl.program_id(0)
    prev = prev_ref[...].astype(BF16)
    prev = jnp.where(i % tiles_per_seq == 0, jnp.zeros_like(prev), prev)
    xc = jnp.concatenate([prev, x_ref[...].astype(BF16)], axis=0)

    def conv(h, c0):
        out = cw_ref[FFN_CONV - 1:FFN_CONV, c0:c0 + FFN_CK] * h[FFN_HALO:]
        for j in range(FFN_CONV - 1):
            shifted = pltpu.roll(h, FFN_CONV - 1 - j, axis=0)[FFN_HALO:]
            out = out + cw_ref[j:j + 1, c0:c0 + FFN_CK] * shifted
        return out

    for c in range(FFN_DIM // FFN_CK):
        c0 = c * FFN_CK
        hg = conv(_dot(xc, wu_ref[:, c0:c0 + FFN_CK]), c0)
        hv = conv(_dot(xc, wu_ref[:, FFN_DIM + c0:FFN_DIM + c0 + FFN_CK]), FFN_DIM + c0)
        act_s[:, c0:c0 + FFN_CK] = (jax.nn.silu(hg) * hv).astype(BF16)
    half = out_ref.shape[0] // 2
    f = [_dot(act_s[r:r + half, :], wd_ref[...]) for r in (0, half)]
    for k, r in enumerate((0, half)):
        out_ref[r:r + half, :] = _layer_norm(DEEPNORM_ALPHA * x_ref[r:r + half, :] + f[k], g_ref[...], b_ref[...])


def _ffn(x1, wu, cw, wd, g, b, seq):
    m = x1.shape[0]
    tm, d = FFN_TM, D_MODEL
    return pl.pallas_call(
        functools.partial(_ffn_body, tiles_per_seq=seq // tm),
        grid=(m // tm,),
        in_specs=[pl.BlockSpec((tm, d), lambda i: (i, 0)),
                  pl.BlockSpec((FFN_HALO, d), lambda i: (jnp.maximum(i * (tm // FFN_HALO) - 1, 0), 0)),
                  _const_spec(wu.shape), _const_spec(cw.shape), _const_spec(wd.shape), _const_spec(g.shape), _const_spec(b.shape)],
        out_specs=pl.BlockSpec((tm, d), lambda i: (i, 0)),
        out_shape=jax.ShapeDtypeStruct((m, d), F32),
        scratch_shapes=[pltpu.VMEM((tm, FFN_DIM), BF16)],
        compiler_params=_cparams("parallel"),
        name="ffn",
    )(x1, x1, wu, cw, wd, g, b)


def _lane_vec(vals, lane0):
    return jnp.zeros((1, LANES), F32).at[0, lane0:lane0 + vals.shape[0]].set(vals.astype(F32))


def _layer(x, w_in, cmp_pos, cmp_w1, cmp_w2, w_nsa_out, gdn_conv_w, gdn_a_log, gdn_dt_bias, gdn_norm_w,
           w_gdn_out, w_o, ln1_g, ln1_b, ffn_w_up, ffn_conv_w, ffn_w_down, ln2_g, ln2_b):
    b, s, d = x.shape
    m = b * s
    x2 = x.reshape(m, d)
    w_rows, w_gdn, w_t, w_g = _wprep(*w_in)
    keys, cmpkv, ggate, mgate, qvt, gt = _inproj(x2, w_rows, w_t, w_g)

    consts = _nsa_consts(s)
    post, w2sel = _compress_weights(cmp_pos, cmp_w2)
    cmp_kv = _compress(cmpkv.reshape(b, s // CMP_STRIDE, CMP_STRIDE * CMP_COLS), cmp_w1, post, w2sel,
                       consts["cmp_aug"])
    o_nsa = _nsa(qvt, keys.reshape(b, s, KEYS_COLS), cmp_kv, gt, consts, b, s)

    ck = GDN_CHUNK
    tri = np.tril(np.ones((ck, ck), np.float32))
    lt = jnp.asarray(np.kron(np.eye(GDN_BLK // ck, dtype=np.float32), tri))
    nq, co, eg = _gdn_intra(x, w_gdn, gdn_conv_w, _lane_vec(gdn_a_log, SM_DECAY), _lane_vec(gdn_dt_bias, SM_DECAY),
                            lt)
    o_gdn = _gdn_scan(nq, co, eg, ggate.reshape(b, s, GDN_W), gdn_norm_w.reshape(1, GDN_HEAD_DIM))

    x1 = _merge(x2, o_nsa.reshape(m, NSA_Q_W), o_gdn.reshape(m, GDN_W), mgate,
                     w_nsa_out.astype(BF16), w_gdn_out.astype(BF16), w_o.astype(BF16),
                     ln1_g.reshape(1, d), ln1_b.reshape(1, d))
    out = _ffn(x1, ffn_w_up.astype(BF16), ffn_conv_w, ffn_w_down.astype(BF16),
               ln2_g.reshape(1, d), ln2_b.reshape(1, d), s)
    return out.reshape(b, s, d)


def kernel(x, w_in, nsa_cmp_pos, nsa_cmp_w1, nsa_cmp_w2, w_nsa_out, gdn_conv_w, gdn_a_log, gdn_dt_bias, gdn_norm_w, w_gdn_out, w_o, ln1_g, ln1_b, ffn_w_up, ffn_conv_w, ffn_w_down, ln2_g, ln2_b):
    w_in_t = jnp.swapaxes(w_in, 1, 2)
    for l in range(DEPTH):
        x = _layer(x, (w_in_t, l), nsa_cmp_pos[l], nsa_cmp_w1[l], nsa_cmp_w2[l], w_nsa_out[l], gdn_conv_w[l],
                   gdn_a_log[l], gdn_dt_bias[l], gdn_norm_w[l], w_gdn_out[l], w_o[l], ln1_g[l], ln1_b[l],
                   ffn_w_up[l], ffn_conv_w[l], ffn_w_down[l], ln2_g[l], ln2_b[l])
    return x
```

```python
import functools

import numpy as np
import jax
import jax.numpy as jnp
from jax import lax
from jax.experimental import pallas as pl
from jax.experimental.pallas import tpu as pltpu

F32 = jnp.float32
BF16 = jnp.bfloat16

D_MODEL = 1024
NSA_HEADS = 8
NSA_KV_GROUPS = 2
NSA_REP = NSA_HEADS // NSA_KV_GROUPS
NSA_HEAD_DIM = 64
CMP_LEN = 32
CMP_STRIDE = 16
SLC_LEN = 64
SLC_TOPK = 8
WIN_LEN = 512
FORCE_SCORE = 1.0e4
NEG = -1.0e30
GDN_HEADS = 4
GDN_HEAD_DIM = 128
GDN_CONV = 4
GDN_CHUNK = 64
FFN_DIM = 2816
FFN_CONV = 3
DEPTH = 1
DEEPNORM_ALPHA = (2.0 * DEPTH) ** 0.25
LN_EPS = 1e-5
RMS_EPS = 1e-6

NSA_Q_W = NSA_HEADS * NSA_HEAD_DIM
NSA_KV_W = NSA_KV_GROUPS * NSA_HEAD_DIM
GDN_W = GDN_HEADS * GDN_HEAD_DIM

LANES = 128
SUBLANES = 8
VMEM_LIMIT_BYTES = 56 * 1024 * 1024

AUG_SEL0 = 64
AUG_POS_HI = 96
AUG_POS_LO = 97
AUG_PAD = 98
BIG = 2.0 ** 100
POS_SPLIT = 256
Q_TILE = 128
N_SLC = 32
V_ROWS = 80
SEL_KC = 512
NSA_SUB = 4

NT_DIMS = (((1,), (1,)), ((), ()))


def _dot(a, b, **kw):
    return jnp.dot(a, b, preferred_element_type=F32, **kw)


def _dot_nt(a, b, **kw):
    return lax.dot_general(a, b, NT_DIMS, preferred_element_type=F32, **kw)


def _cparams(*sem):
    return pltpu.CompilerParams(dimension_semantics=sem, vmem_limit_bytes=VMEM_LIMIT_BYTES)


def _const_spec(shape):
    nd = len(shape)
    return pl.BlockSpec(shape, lambda *_: (0,) * nd, pipeline_mode=pl.Buffered(1))


_IN_WIDTHS = (NSA_Q_W,) + (NSA_KV_W,) * 6 + (3 * NSA_HEADS, 3 * GDN_W, GDN_HEADS, GDN_HEADS, GDN_W, 2 * D_MODEL)
(_C_Q, _C_CK, _C_CV, _C_SK, _C_SV, _C_WK, _C_WV, _C_GATE, _C_GQKV, _C_BETA, _C_DECAY, _C_GGATE, _C_MERGE,
 IN_WIDTH) = (int(v) for v in np.cumsum((0,) + _IN_WIDTHS))
_C_SMALL = _C_BETA // LANES * LANES
SM_BETA = _C_BETA - _C_SMALL
SM_DECAY = _C_DECAY - _C_SMALL
KEYS_COLS = 2 * NSA_KV_W
CMP_COLS = 2 * NSA_KV_W
_INPROJ_GROUPS = (("keys", KEYS_COLS, BF16), ("cmp", CMP_COLS, BF16), ("ggate", GDN_W, BF16),
                  ("merge", 2 * D_MODEL, BF16))
_INPROJ_WIDTH = sum(w for _, w, _ in _INPROJ_GROUPS)
_GDN_PROJ_WIDTH = 3 * GDN_W + LANES
_INPROJ_T_ROWS = NSA_Q_W + 4 * NSA_HEAD_DIM
_GATE_T_ROWS = 32
INPROJ_TM = 1024
INPROJ_TN = 512
WPREP_TK = 128


def _wprep_body(w_ref, rows_ref, gdn_ref, wt_ref, wg_ref):
    hd = NSA_HEAD_DIM
    feat = lambda c0, n: w_ref[0, c0:c0 + n, :]

    def put_t(ref, col, src):
        for r in range(0, src.shape[0], LANES):
            ref[:, col + r:col + r + LANES] = src[r:r + LANES].T.astype(BF16)

    for g in range(NSA_KV_GROUPS):
        put_t(rows_ref, 2 * g * hd, jnp.concatenate([feat(_C_SK + g * hd, hd), feat(_C_WK + g * hd, hd)], axis=0))
    c = 4 * hd
    for c0, n in ((_C_CK, 2 * NSA_KV_W), (_C_GGATE, GDN_W), (_C_MERGE, 2 * D_MODEL)):
        put_t(rows_ref, c, feat(c0, n))
        c += n
    put_t(gdn_ref, 0, feat(_C_GQKV, 3 * GDN_W))
    put_t(gdn_ref, 3 * GDN_W, feat(_C_SMALL, LANES))
    wt_ref[:NSA_Q_W, :] = feat(_C_Q, NSA_Q_W).astype(BF16)
    for j, c0 in enumerate((_C_SV, _C_WV, _C_SV + hd, _C_WV + hd)):
        wt_ref[NSA_Q_W + j * hd:NSA_Q_W + (j + 1) * hd, :] = feat(c0, hd).astype(BF16)
    wg_ref[...] = feat(_C_GATE, _GATE_T_ROWS).astype(BF16)


def _wprep(w_in_t, layer):
    k = w_in_t.shape[2]
    tk = WPREP_TK
    return pl.pallas_call(
        _wprep_body,
        grid=(k // tk,),
        in_specs=[pl.BlockSpec((1, IN_WIDTH, tk), lambda i: (layer, 0, i))],
        out_specs=[pl.BlockSpec((tk, _INPROJ_WIDTH), lambda i: (i, 0)),
                   pl.BlockSpec((tk, _GDN_PROJ_WIDTH), lambda i: (i, 0)),
                   pl.BlockSpec((_INPROJ_T_ROWS, tk), lambda i: (0, i)),
                   pl.BlockSpec((_GATE_T_ROWS, tk), lambda i: (0, i))],
        out_shape=[jax.ShapeDtypeStruct((k, _INPROJ_WIDTH), BF16), jax.ShapeDtypeStruct((k, _GDN_PROJ_WIDTH), BF16),
                   jax.ShapeDtypeStruct((_INPROJ_T_ROWS, k), BF16), jax.ShapeDtypeStruct((_GATE_T_ROWS, k), BF16)],
        compiler_params=_cparams("parallel"),
        name="wprep",
    )(w_in_t)


def _inproj_body(x_ref, w_ref, wt_ref, wg_ref, keys_ref, cmp_ref, ggate_ref, merge_ref, qvt_ref, gt_ref, cmp_s):
    x = x_ref[...].astype(BF16)
    outs = (keys_ref, None, ggate_ref, merge_ref)
    c0 = 0
    for ref, (name, width, _) in zip(outs, _INPROJ_GROUPS):
        for s in range(0, width, INPROJ_TN):
            e = min(s + INPROJ_TN, width)
            res = _dot(x, w_ref[:, c0 + s:c0 + e])
            if name == "cmp":
                for j in range(width // LANES):
                    cmp_s[j] = res[:, j * LANES:(j + 1) * LANES]
            else:
                ref[:, s:e] = res.astype(ref.dtype)
        c0 += width
    nblk = cmp_ref.shape[0]
    for l in range(CMP_STRIDE):
        for j in range(cmp_s.shape[0]):
            cmp_ref[:, l * CMP_COLS + j * LANES:l * CMP_COLS + (j + 1) * LANES] = (
                cmp_s[j, pl.ds(l, nblk, stride=CMP_STRIDE), :].astype(BF16))
    for s in range(0, _INPROJ_T_ROWS, 2 * LANES):
        qvt_ref[s:s + 2 * LANES, :] = _dot_nt(wt_ref[s:s + 2 * LANES, :], x).astype(qvt_ref.dtype)
    gt_ref[...] = _dot_nt(wg_ref[...], x)


def _inproj(x2, w_rows, w_t, w_g):
    m = x2.shape[0]
    tm = INPROJ_TM
    row_major = [(n, wd, dt) for n, wd, dt in _INPROJ_GROUPS if n != "cmp"]
    specs = {n: (pl.BlockSpec((tm, wd), lambda i: (i, 0)), jax.ShapeDtypeStruct((m, wd), dt))
             for n, wd, dt in row_major}
    specs["cmp"] = (pl.BlockSpec((tm // CMP_STRIDE, CMP_STRIDE * CMP_COLS), lambda i: (i, 0)),
                    jax.ShapeDtypeStruct((m // CMP_STRIDE, CMP_STRIDE * CMP_COLS), BF16))
    order = [n for n, _, _ in _INPROJ_GROUPS]
    return pl.pallas_call(
        _inproj_body,
        grid=(m // tm,),
        in_specs=[pl.BlockSpec((tm, D_MODEL), lambda i: (i, 0)), _const_spec(w_rows.shape),
                  _const_spec(w_t.shape), _const_spec(w_g.shape)],
        out_specs=[specs[n][0] for n in order]
        + [pl.BlockSpec((_INPROJ_T_ROWS, tm), lambda i: (0, i)), pl.BlockSpec((_GATE_T_ROWS, tm), lambda i: (0, i))],
        out_shape=[specs[n][1] for n in order]
        + [jax.ShapeDtypeStruct((_INPROJ_T_ROWS, m), BF16), jax.ShapeDtypeStruct((_GATE_T_ROWS, m), F32)],
        scratch_shapes=[pltpu.VMEM((CMP_COLS // LANES, tm, LANES), F32)],
        compiler_params=_cparams("parallel"),
        name="inproj",
    )(x2, w_rows, w_t, w_g)


def _compress_weights(cmp_pos, cmp_w2):
    hd, half = NSA_HEAD_DIM, CMP_LEN // 2
    posr = cmp_pos.reshape(2, 2, half, hd)
    post = jnp.broadcast_to(posr.transpose(1, 2, 0, 3)[:, :, :, None, :], (2, half, 2, 2, hd))
    post = jnp.concatenate([post.reshape(2, half * 4 * hd),
                            jnp.zeros((SUBLANES - 2, half * 4 * hd), cmp_pos.dtype)], axis=0)
    w2sel = jnp.zeros((2, 2, 2 * hd, LANES), cmp_w2.dtype)
    for g in range(2):
        w2sel = w2sel.at[:, g, g * hd:(g + 1) * hd, :hd].set(cmp_w2)
    return post.astype(BF16), w2sel.reshape(4, 2 * hd, LANES).astype(BF16)


def _compress_body(t_ref, w1_ref, pos_ref, w2_ref, aug_ref, out_ref, w1e_s):
    hd, half_len = NSA_HEAD_DIM, CMP_LEN // 2

    @pl.when(pl.program_id(0) == 0)
    def _():
        w1e_s[...] = jnp.zeros(w1e_s.shape, BF16)
        for which in range(2):
            for half in range(2):
                for l in range(half_len):
                    blk = w1_ref[which, (half * half_len + l) * hd:(half * half_len + l + 1) * hd, :].astype(BF16)
                    for g in range(NSA_KV_GROUPS):
                        r0 = l * CMP_COLS + which * LANES + g * hd
                        c0 = half * CMP_COLS + which * LANES + g * hd
                        w1e_s[r0:r0 + hd, c0:c0 + hd] = blk

    p = _dot(t_ref[0], w1e_s[...])
    pp = _dot(pos_ref[...], w1e_s[...])
    nxt = pltpu.roll(p[:, CMP_COLS:], p.shape[0] - 1, axis=0)
    pre = p[:, :CMP_COLS] + nxt + pp[0:1, :CMP_COLS] + pp[1:2, CMP_COLS:]
    h = jax.nn.gelu(pre).astype(BF16)
    n_idx = lax.broadcasted_iota(jnp.int32, (p.shape[0], LANES), 0)
    real = n_idx < p.shape[0] - 1
    for which in range(2):
        hw = h[:, which * LANES:(which + 1) * LANES]
        for g in range(2):
            o = jnp.where(real, _dot(hw, w2_ref[which * 2 + g]) + aug_ref[which], 0.0)
            out_ref[0, which * 2 + g] = (o if which == 0 else o.T).astype(out_ref.dtype)


def _compress(t2, w1, post, w2sel, aug):
    b, nblk, _ = t2.shape
    return pl.pallas_call(
        _compress_body,
        grid=(b,),
        in_specs=[pl.BlockSpec((1, nblk, CMP_STRIDE * CMP_COLS), lambda i: (i, 0, 0)),
                  _const_spec(w1.shape), _const_spec(post.shape), _const_spec(w2sel.shape), _const_spec(aug.shape)],
        out_specs=pl.BlockSpec((1, 4, nblk, LANES), lambda i: (i, 0, 0, 0)),
        out_shape=jax.ShapeDtypeStruct((b, 4, nblk, LANES), BF16),
        scratch_shapes=[pltpu.VMEM((CMP_STRIDE * CMP_COLS, 2 * CMP_COLS), BF16)],
        compiler_params=_cparams("arbitrary"),
        name="compress",
    )(t2, w1, post, w2sel, aug)


def _nsa_consts(s):
    hd, rep = NSA_HEAD_DIM, NSA_REP
    t = np.arange(s)
    kx_win = np.zeros((s + WIN_LEN, hd), np.float32)
    kx_win[WIN_LEN + t, AUG_POS_HI - hd] = t // POS_SPLIT
    kx_win[WIN_LEN + t, AUG_POS_LO - hd] = t % POS_SPLIT
    kx_win[:WIN_LEN, AUG_PAD - hd] = 1.0
    kx_sel = kx_win.copy()
    kx_sel[WIN_LEN + t, t // SLC_LEN] = 1.0
    vx_win = np.zeros((V_ROWS - hd, s + WIN_LEN), np.float32)
    vx_win[0, WIN_LEN:] = 1.0
    vx_sel = vx_win
    n_cmp = s // CMP_STRIDE
    cmp_aug = np.zeros((2, n_cmp, LANES), np.float32)
    end = np.arange(n_cmp) * CMP_STRIDE + CMP_LEN - 1
    cmp_aug[0, :, AUG_POS_HI] = end // POS_SPLIT
    cmp_aug[0, :, AUG_POS_LO] = end % POS_SPLIT
    qx = np.zeros((NSA_KV_GROUPS, LANES - AUG_POS_HI, rep * Q_TILE), np.float32)
    for h in range(NSA_HEADS):
        slope = 2.0 ** (-8.0 * (h + 1) / NSA_HEADS)
        lanes = slice((h % rep) * Q_TILE, (h % rep + 1) * Q_TILE)
        qx[h // rep, 0, lanes] = slope * POS_SPLIT
        qx[h // rep, 1, lanes] = slope
        qx[h // rep, AUG_PAD - AUG_POS_HI, lanes] = -BIG
    c0 = np.arange(n_cmp)[None, :] * CMP_STRIDE
    s0 = np.arange(s // SLC_LEN)[:, None] * SLC_LEN
    ov_t = ((c0 < s0 + SLC_LEN) & (c0 + CMP_LEN > s0)).astype(np.float32)
    ov_t[:, (s - CMP_LEN) // CMP_STRIDE + 1:] = 0.0
    kk = np.arange(Q_TILE)[:, None]
    qq = np.arange(Q_TILE)[None, :]
    causal = np.tile(np.where(kk <= qq, 0.0, NEG).astype(np.float32), (1, rep))
    after = np.tile(np.where(kk > qq, 0.0, NEG).astype(np.float32), (1, rep))
    j = jnp.asarray
    return dict(kx_sel=j(kx_sel, BF16), kx_win=j(kx_win, BF16), vx_sel=j(vx_sel, BF16), vx_win=j(vx_win, BF16),
                cmp_aug=j(cmp_aug), qx=j(qx), ov_t=j(ov_t), causal=j(causal), after=j(after))


def _nsa_body(qt_ref, k_ref, vt_ref, kc_ref, vct_ref, gt_ref, kxs_ref, kxw_ref, vxs_ref, vxw_ref, qx_ref, ovt_ref,
              causal_ref, after_ref, out_ref, ks_s, kw_s, vs_s, vw_s):
    hd, rep, tq = NSA_HEAD_DIM, NSA_REP, Q_TILE
    nq = rep * tq
    i = pl.program_id(2)

    @pl.when(i == 0)
    def _():
        keys = k_ref[0]
        ks_s[:WIN_LEN, :hd] = jnp.zeros((WIN_LEN, hd), BF16)
        ks_s[WIN_LEN:, :hd] = keys[:, :hd]
        ks_s[:, hd:] = kxs_ref[...]
        kw_s[:WIN_LEN, :hd] = jnp.zeros((WIN_LEN, hd), BF16)
        kw_s[WIN_LEN:, :hd] = keys[:, hd:]
        kw_s[:, hd:] = kxw_ref[...]
        vals = vt_ref[...]
        vs_s[:hd, :WIN_LEN] = jnp.zeros((hd, WIN_LEN), BF16)
        vs_s[:hd, WIN_LEN:] = vals[:hd]
        vs_s[hd:, :] = vxs_ref[...]
        vw_s[:hd, :WIN_LEN] = jnp.zeros((hd, WIN_LEN), BF16)
        vw_s[:hd, WIN_LEN:] = vals[hd:]
        vw_s[hd:, :] = vxw_ref[...]

    qx = qx_ref[0]
    sg_all = jax.nn.sigmoid(gt_ref[...])
    grp = pl.program_id(1)

    full_past = WIN_LEN // tq

    def front(sub, res, it, past):
        qt = qt_ref[:, sub * tq:(sub + 1) * tq]
        q64 = jnp.concatenate([qt[r * hd:(r + 1) * hd, :] for r in range(rep)], axis=1).astype(F32) * (hd ** -0.5)

        def q_aug(sel_rows):
            return jnp.concatenate([q64, sel_rows, qx], axis=0).astype(BF16)

        n_row = lax.broadcasted_iota(jnp.int32, (LANES, nq), 0)
        t_lane = it * tq + (lax.broadcasted_iota(jnp.int32, (LANES, nq), 1) & (tq - 1))
        valid = t_lane >= n_row * CMP_STRIDE + (CMP_LEN - 1)
        qa0 = q_aug(jnp.zeros((N_SLC, nq), F32))
        sc = jnp.where(valid, _dot(kc_ref[0, 0], qa0), NEG)
        mc = jnp.max(sc, axis=0, keepdims=True)
        ec = jnp.where(valid, jnp.exp(sc - mc), 0.0)
        lc = jnp.sum(ec, axis=0, keepdims=True)
        pc = ec * jnp.where(lc > 0.0, 1.0 / lc, 0.0)
        o_cmp = _dot(vct_ref[0, 0], pc.astype(BF16))[:hd]
        psum = pc[:, 0:tq] + pc[:, tq:2 * tq] + pc[:, 2 * tq:3 * tq] + pc[:, 3 * tq:4 * tq]
        score_t = _dot(ovt_ref[...], psum, precision=lax.Precision.HIGHEST)
        yield

        n_keys = (past + 1) * tq
        if past == full_past:
            w0 = pl.multiple_of(it * tq, tq)
        else:
            w0 = WIN_LEN
        diag = past * tq

        def one_shot_masks(s):
            parts = [s[:diag], s[diag:] + causal_ref[...]] if past else [s + causal_ref[...]]
            if past == full_past:
                parts = [s[:tq] + after_ref[...], s[tq:diag], parts[1]]
            return jnp.concatenate(parts, axis=0) if len(parts) > 1 else parts[0]

        s_w = one_shot_masks(_dot(kw_s[pl.ds(w0, n_keys), :], qa0))
        p_w = jnp.exp(s_w - jnp.max(s_w, axis=0, keepdims=True))
        acc_w = _dot(vw_s[:, pl.ds(w0, n_keys)], p_w.astype(BF16))
        o_win = acc_w[:hd] * (1.0 / acc_w[hd:hd + 1])
        yield

        jb = lax.broadcasted_iota(jnp.int32, (N_SLC, tq), 0)
        cur = (it * tq + lax.broadcasted_iota(jnp.int32, (N_SLC, tq), 1)) // SLC_LEN
        forced = (jb == 0) | (jb == cur) | (jb == cur - 1)
        score_t = jnp.where(forced, FORCE_SCORE, jnp.where(jb <= cur, score_t, -1.0))
        rank = jnp.zeros((N_SLC, tq), F32)
        for jp in range(N_SLC):
            other = score_t[jp:jp + 1, :]
            ge = jnp.where(other >= score_t, 1.0, 0.0)
            gt = jnp.where(other > score_t, 1.0, 0.0)
            rank = rank + jnp.where(jb > jp, ge, gt)
        sel = rank < float(SLC_TOPK)
        qa = q_aug(jnp.concatenate([jnp.where(sel, 0.0, -BIG)] * rep, axis=1))
        lo_blk = jnp.min(jnp.where(sel & (jb >= 2) & (jb <= cur), jb.astype(F32), float(N_SLC)))
        lo_key = (lo_blk.astype(jnp.int32) // 2) * tq
        yield

        s_main = _dot(ks_s[pl.ds(w0, n_keys), :], qa)
        if past < full_past:
            s_s = jnp.concatenate([s_main[:diag], s_main[diag:] + causal_ref[...]], axis=0) if past else (
                s_main + causal_ref[...])
            m_s = jnp.max(s_s, axis=0, keepdims=True)
            acc_s = _dot(vs_s[:, pl.ds(w0, n_keys)], jnp.exp(s_s - m_s).astype(BF16))
            res.update(early=False, acc_s=acc_s, o_cmp=o_cmp, o_win=o_win)
            return
        e_key = it * tq - WIN_LEN
        t0 = pl.multiple_of(jnp.where(e_key > 0, WIN_LEN, 0), tq)
        s_s = jnp.concatenate([_dot(ks_s[pl.ds(t0, tq), :], qa), s_main[:diag],
                               s_main[diag:] + causal_ref[...]], axis=0)
        m_s = jnp.max(s_s, axis=0, keepdims=True)
        p_s = jnp.exp(s_s - m_s).astype(BF16)
        acc_s = _dot(vs_s[:, pl.ds(t0, tq)], p_s[:tq]) + _dot(vs_s[:, pl.ds(w0, n_keys)], p_s[tq:])
        c_hi = (e_key - tq + SEL_KC - 1) // SEL_KC
        c_lo = jnp.where(lo_key < e_key, (lo_key - tq) // SEL_KC, c_hi)
        res.update(early=True, qa=qa, e_key=e_key, c_lo=c_lo, c_hi=c_hi, m_s=m_s, acc_s=acc_s, o_cmp=o_cmp,
                   o_win=o_win)

    def tail(sub, f):
        def early_step(c, carry):
            qa, e_key = f["qa"], f["e_key"]
            m, acc = carry
            k0 = tq + c * SEL_KC
            start = pl.multiple_of(WIN_LEN + k0, tq)
            k_abs = k0 + lax.broadcasted_iota(jnp.int32, (SEL_KC, nq), 0)
            s = jnp.where(k_abs < e_key, _dot(ks_s[pl.ds(start, SEL_KC), :], qa), NEG)
            m_new = jnp.maximum(m, jnp.max(s, axis=0, keepdims=True))
            p = jnp.exp(s - m_new).astype(BF16)
            return m_new, acc * jnp.exp(m - m_new) + _dot(vs_s[:, pl.ds(start, SEL_KC)], p)

        acc_s = f["acc_s"]
        if f["early"]:
            _, acc_s = lax.fori_loop(f["c_lo"], f["c_hi"], early_step, (f["m_s"], acc_s))
        o_slc = acc_s[:hd] * (1.0 / acc_s[hd:hd + 1])

        sg = sg_all[:, sub * tq:(sub + 1) * tq]
        gate = lambda br, r: jnp.where(grp == 0, sg[br * NSA_HEADS + r:br * NSA_HEADS + r + 1],
                                       sg[br * NSA_HEADS + rep + r:br * NSA_HEADS + rep + r + 1])
        for pair in range(rep // 2):
            halves = []
            for r in (2 * pair, 2 * pair + 1):
                lanes = slice(r * tq, (r + 1) * tq)
                halves.append(gate(0, r) * f["o_cmp"][:, lanes] + gate(1, r) * o_slc[:, lanes]
                              + gate(2, r) * f["o_win"][:, lanes])
            out_ref[0, sub * tq:(sub + 1) * tq, pair * LANES:(pair + 1) * LANES] = (
                jnp.concatenate(halves, axis=0).T.astype(out_ref.dtype))

    def run(tiles):
        fronts = [{} for _ in range(NSA_SUB)]
        for sub, (it, past) in enumerate(tiles):
            for _ in front(sub, fronts[sub], it, past):
                pass
        for sub in range(NSA_SUB):
            tail(sub, fronts[sub])

    n_short = -(-full_past // NSA_SUB)
    for step in range(n_short):
        @pl.when(i == step)
        def _(step=step):
            run([(step * NSA_SUB + sub, min(step * NSA_SUB + sub, full_past)) for sub in range(NSA_SUB)])

    @pl.when(i >= n_short)
    def _():
        run([(i * NSA_SUB + sub, full_past) for sub in range(NSA_SUB)])


def _nsa(qvt, keys, cmp_kv, gt, consts, b, s):
    tqs = NSA_SUB * Q_TILE
    nt = s // tqs
    c = consts
    in_specs = [
        pl.BlockSpec((2 * LANES, tqs), lambda bi, g, i: (g, bi * nt + i)),
        pl.BlockSpec((1, s, LANES), lambda bi, g, i: (bi, 0, g)),
        pl.BlockSpec((LANES, s), lambda bi, g, i: (NSA_Q_W // LANES + g, bi)),
        pl.BlockSpec((1, 1, s // CMP_STRIDE, LANES), lambda bi, g, i: (bi, g, 0, 0)),
        pl.BlockSpec((1, 1, s // CMP_STRIDE, LANES), lambda bi, g, i: (bi, 2 + g, 0, 0)),
        pl.BlockSpec((_GATE_T_ROWS, tqs), lambda bi, g, i: (0, bi * nt + i)),
        _const_spec(c["kx_sel"].shape), _const_spec(c["kx_win"].shape), _const_spec(c["vx_sel"].shape),
        _const_spec(c["vx_win"].shape),
        pl.BlockSpec((1,) + c["qx"].shape[1:], lambda bi, g, i: (g, 0, 0)),
        _const_spec(c["ov_t"].shape), _const_spec(c["causal"].shape), _const_spec(c["after"].shape),
    ]
    return pl.pallas_call(
        _nsa_body,
        grid=(b, NSA_KV_GROUPS, nt),
        in_specs=in_specs,
        out_specs=pl.BlockSpec((1, tqs, 2 * LANES), lambda bi, g, i: (bi, i, g)),
        out_shape=jax.ShapeDtypeStruct((b, s, NSA_Q_W), BF16),
        scratch_shapes=[pltpu.VMEM((s + WIN_LEN, LANES), BF16), pltpu.VMEM((s + WIN_LEN, LANES), BF16),
                        pltpu.VMEM((V_ROWS, s + WIN_LEN), BF16), pltpu.VMEM((V_ROWS, s + WIN_LEN), BF16)],
        compiler_params=_cparams("parallel", "parallel", "arbitrary"),
        name="nsa",
    )(qvt, keys, qvt, cmp_kv, cmp_kv, gt, c["kx_sel"], c["kx_win"], c["vx_sel"], c["vx_win"], c["qx"], c["ov_t"],
      c["causal"], c["after"])


GDN_TS = 512
GDN_BLK = 128
GDN_HALO = 8
GDN_XHALO = 16
GDN_INV_BASE = 8
GDN_CHAIN_UNITS_PER_PREP_UNIT = 4
GDN_SCAN_ROWS = GDN_HEAD_DIM + GDN_CHUNK


def _gdn_prep(x_ref, prev_ref, w_ref, cw_ref, alog_ref, dtb_ref, lt_ref, first_tile, xp_s, buf):
    x_s, p_s, rhs_s, qg_s, aqk_s, kdt_s, a_s, eg_s = buf
    ts, dh, nh, blk = GDN_TS, GDN_HEAD_DIM, GDN_HEADS, GDN_BLK
    proj = _dot(x_ref[0].astype(BF16), w_ref[...])
    hist = _dot(prev_ref[0].astype(BF16), w_ref[:, :3 * GDN_W])[GDN_XHALO - GDN_HALO:]
    xp_s[0:GDN_HALO, :] = jnp.where(first_tile, 0.0, hist)
    xp_s[GDN_HALO:, :] = proj[:, :3 * GDN_W]
    yield
    act = []
    for blk_i in range(3 * nh):
        lanes = slice(blk_i * dh, (blk_i + 1) * dh)
        xp = xp_s[:, lanes]
        conv = cw_ref[0:1, lanes] * xp
        for j in range(1, GDN_CONV):
            conv = pltpu.roll(conv, 1, axis=0) + cw_ref[j:j + 1, lanes] * xp
        act.append(jax.nn.silu(conv[GDN_HALO:]))
        yield

    sm = proj[:, 3 * GDN_W:]
    beta = jax.nn.sigmoid(sm)
    g = -jnp.exp(alog_ref[...]) * jax.nn.softplus(sm + dtb_ref[...])
    gcum = jnp.concatenate([_dot(lt_ref[...], g[r:r + blk], precision=lax.Precision.HIGHEST)
                            for r in range(0, ts, blk)], axis=0)
    eg = jnp.exp(gcum)
    eg_s[...] = eg

    ri = lax.broadcasted_iota(jnp.int32, (blk, blk), 0)
    ci = lax.broadcasted_iota(jnp.int32, (blk, blk), 1)
    same = (ri // GDN_CHUNK) == (ci // GDN_CHUNK)
    causal = same & (ri >= ci)
    strict = same & (ri > ci)
    eye = (ri == ci).astype(F32)
    same_base = (ri // GDN_INV_BASE) == (ci // GDN_INV_BASE)

    for pb in range(ts // blk):
        rows = slice(pb * blk, (pb + 1) * blk)
        gc = gcum[rows]
        gc_t = gc.T
        first = lax.broadcasted_iota(jnp.int32, (blk, LANES), 0) < GDN_CHUNK
        g_last = jnp.where(first, gc[GDN_CHUNK - 1:GDN_CHUNK, :], gc[blk - 1:blk, :])
        e_dec = jnp.exp(g_last - gc)
        for h in range(nh):
            q, k, v = act[h][rows], act[nh + h][rows], act[2 * nh + h][rows]
            q = q * lax.rsqrt(jnp.sum(q * q, axis=1, keepdims=True) + RMS_EPS) * (dh ** -0.5)
            k = k * lax.rsqrt(jnp.sum(k * k, axis=1, keepdims=True) + RMS_EPS)
            b_col = beta[rows, SM_BETA + h:SM_BETA + h + 1]
            eg_col = eg[rows, SM_DECAY + h:SM_DECAY + h + 1]
            gdiff = gc[:, SM_DECAY + h:SM_DECAY + h + 1] - gc_t[SM_DECAY + h:SM_DECAY + h + 1, :]
            decay = jnp.exp(jnp.where(causal, gdiff, NEG))
            kb = k * b_col
            kbf, kf, qf = kb.astype(BF16), k.astype(BF16), q.astype(BF16)
            a = jnp.where(strict, -_dot_nt(kbf, kf) * decay, 0.0)
            c = pb * nh + h
            a_base = jnp.where(same_base, a, 0.0)
            a_s[c] = a.astype(BF16)
            x_s[c] = a_base.astype(BF16)
            p_s[c] = eye + a_base
            rhs_s[c] = jnp.concatenate([v * b_col, kb * eg_col], axis=1).astype(BF16)
            qg_s[c] = q * eg_col
            aqk_s[c] = jnp.where(causal, _dot_nt(qf, kf) * decay, 0.0).astype(BF16)
            kdt_s[c] = (k * e_dec[:, SM_DECAY + h:SM_DECAY + h + 1]).T.astype(BF16)
            yield


def _gdn_chains(buf, nq_ref, co_ref, eg_ref):
    x_s, p_s, rhs_s, qg_s, aqk_s, kdt_s, a_s, eg_s = buf
    dh, nh, blk = GDN_HEAD_DIM, GDN_HEADS, GDN_BLK
    n_chain = x_s.shape[0]
    eg_ref[0] = eg_s[...]
    ri = lax.broadcasted_iota(jnp.int32, (blk, blk), 0)
    ci = lax.broadcasted_iota(jnp.int32, (blk, blk), 1)
    same = lambda size: (ri // size) == (ci // size)
    for c in range(n_chain):
        y = x_s[c]
        x_s[c] = _dot(y, y).astype(BF16)
        yield
    for c in range(n_chain):
        y2 = x_s[c]
        p = p_s[c]
        p_s[c] = p + _dot(p.astype(BF16), y2)
        x_s[c] = _dot(y2, y2).astype(BF16)
        yield
    for c in range(n_chain):
        p = p_s[c]
        p_s[c] = p + _dot(p.astype(BF16), x_s[c])
        yield
    size = 2 * GDN_INV_BASE
    while size <= GDN_CHUNK:
        between = same(size) & jnp.logical_not(same(size // 2))
        for c in range(n_chain):
            a = a_s[c]
            x_s[c] = _dot(p_s[c].astype(BF16), jnp.where(between, a, jnp.zeros_like(a))).astype(BF16)
            yield
        for c in range(n_chain):
            t = p_s[c]
            p_s[c] = t + _dot(x_s[c], t.astype(BF16))
            yield
        size *= 2

    tok_half = lax.broadcasted_iota(jnp.int32, (blk, blk), 1) // GDN_CHUNK
    for c in range(n_chain):
        rhs_s[c] = _dot(p_s[c].astype(BF16), rhs_s[c]).astype(BF16)
        yield
    for c in range(n_chain):
        pb, h = divmod(c, nh)
        uw = rhs_s[c]
        a1 = _dot(aqk_s[c], uw)
        q_loc = qg_s[c] - a1[:, dh:]
        kdt = kdt_s[c]
        for half in range(blk // GDN_CHUNK):
            k1 = _dot(jnp.where(tok_half == half, kdt, jnp.zeros_like(kdt)), uw)
            n = pb * (blk // GDN_CHUNK) + half
            rows = slice(half * GDN_CHUNK, (half + 1) * GDN_CHUNK)
            nq_ref[0, h, n, :dh, :] = (-k1[:, dh:]).astype(BF16)
            nq_ref[0, h, n, dh:, :] = q_loc[rows].astype(BF16)
            co_ref[0, h, n, :dh, :] = k1[:, :dh].astype(BF16)
            co_ref[0, h, n, dh:, :] = a1[rows, :dh].astype(BF16)
        yield


def _interleave(major, minor, minor_per_major):
    for _ in major:
        for _ in range(minor_per_major):
            next(minor, None)
    for _ in minor:
        pass


def _gdn_intra_body(x_ref, prev_ref, w_ref, cw_ref, alog_ref, dtb_ref, lt_ref, nq_ref, co_ref, eg_ref, xp_s, *bufs,
                    tiles_per_seq, n_buf):
    j = pl.program_id(0)
    sets = (bufs[:n_buf], bufs[n_buf:])

    @pl.when(j == 0)
    def _():
        for ref in sets[1]:
            ref[...] = jnp.zeros(ref.shape, ref.dtype)

    first_tile = (j % tiles_per_seq) == 0
    for parity in range(2):
        @pl.when(j % 2 == parity)
        def _(parity=parity):
            chains = _gdn_chains(sets[1 - parity], nq_ref, co_ref, eg_ref)
            prep = _gdn_prep(x_ref, prev_ref, w_ref, cw_ref, alog_ref, dtb_ref, lt_ref, first_tile, xp_s,
                             sets[parity])
            _interleave(prep, chains, GDN_CHAIN_UNITS_PER_PREP_UNIT)


def _gdn_intra(x, w_gdn, conv_w, alog_l, dtb_l, lt):
    b, s, d = x.shape
    ts, nh, dh = GDN_TS, GDN_HEADS, GDN_HEAD_DIM
    tps = s // ts
    nt = b * tps
    n_chain = (ts // GDN_BLK) * nh
    sq = lambda dt: pltpu.VMEM((n_chain, GDN_BLK, GDN_BLK), dt)
    buf = lambda: [sq(BF16), sq(F32), pltpu.VMEM((n_chain, GDN_BLK, 2 * dh), BF16), sq(F32), sq(BF16), sq(BF16),
                   sq(BF16), pltpu.VMEM((ts, LANES), F32)]
    src = lambda j: jnp.minimum(j, nt - 1)
    dst = lambda j: jnp.maximum(j - 1, 0)
    cspec = lambda: pl.BlockSpec((1, nh, ts // GDN_CHUNK, GDN_SCAN_ROWS, dh),
                                 lambda j: (dst(j) // tps, 0, dst(j) % tps, 0, 0))
    cshape = jax.ShapeDtypeStruct((b, nh, s // GDN_CHUNK, GDN_SCAN_ROWS, dh), BF16)
    return pl.pallas_call(
        functools.partial(_gdn_intra_body, tiles_per_seq=tps, n_buf=len(buf())),
        grid=(nt + 1,),
        in_specs=[
            pl.BlockSpec((1, ts, d), lambda j: (src(j) // tps, src(j) % tps, 0)),
            pl.BlockSpec((1, GDN_XHALO, d),
                         lambda j: (src(j) // tps, jnp.maximum((src(j) % tps) * (ts // GDN_XHALO) - 1, 0), 0)),
            _const_spec(w_gdn.shape), _const_spec(conv_w.shape), _const_spec(alog_l.shape), _const_spec(dtb_l.shape),
            _const_spec(lt.shape),
        ],
        out_specs=[cspec(), cspec(), pl.BlockSpec((1, ts, LANES), lambda j: (dst(j) // tps, dst(j) % tps, 0))],
        out_shape=[cshape, cshape, jax.ShapeDtypeStruct((b, s, LANES), F32)],
        scratch_shapes=[pltpu.VMEM((ts + GDN_HALO, 3 * GDN_W), F32)] + buf() + buf(),
        compiler_params=_cparams("arbitrary"),
        name="gdn_intra",
    )(x, x, w_gdn, conv_w, alog_l, dtb_l, lt)


def _gdn_scan_body(nq_ref, co_ref, eg_ref, gate_ref, nw_ref, out_ref, st_s):
    nh, dh, ck = GDN_HEADS, GDN_HEAD_DIM, GDN_CHUNK
    st_s[...] = jnp.zeros(st_s.shape, F32)

    def chunk(n, carry):
        r0 = pl.multiple_of(n * ck, ck)
        d_row = eg_ref[0, pl.ds(r0 + ck - 1, 1), :]
        for h in range(nh):
            st = st_s[h]
            res = _dot(nq_ref[0, h, n], st.astype(BF16)) + co_ref[0, h, n].astype(F32)
            st_s[h] = st * d_row[:, SM_DECAY + h:SM_DECAY + h + 1] + res[:dh]
            o = res[dh:]
            ms = jnp.mean(o * o, axis=1, keepdims=True)
            gt = gate_ref[0, pl.ds(r0, ck), h * dh:(h + 1) * dh].astype(F32)
            out_ref[0, pl.ds(r0, ck), h * dh:(h + 1) * dh] = (
                o * lax.rsqrt(ms + RMS_EPS) * nw_ref[...] * jax.nn.silu(gt)).astype(out_ref.dtype)
        return carry

    lax.fori_loop(0, nq_ref.shape[2], chunk, 0, unroll=16)


def _gdn_scan(nq, co, eg, ggate, norm_w):
    b, nh, nc, rows, dh = nq.shape
    s = nc * GDN_CHUNK
    cspec = lambda: pl.BlockSpec((1, nh, nc, rows, dh), lambda bi: (bi, 0, 0, 0, 0))
    return pl.pallas_call(
        _gdn_scan_body,
        grid=(b,),
        in_specs=[cspec(), cspec(),
                  pl.BlockSpec((1, s, LANES), lambda bi: (bi, 0, 0)),
                  pl.BlockSpec((1, s, GDN_W), lambda bi: (bi, 0, 0)),
                  _const_spec(norm_w.shape)],
        out_specs=pl.BlockSpec((1, s, GDN_W), lambda bi: (bi, 0, 0)),
        out_shape=jax.ShapeDtypeStruct((b, s, GDN_W), BF16),
        scratch_shapes=[pltpu.VMEM((nh, dh, dh), F32)],
        compiler_params=_cparams("parallel"),
        name="gdn_scan",
    )(nq, co, eg, ggate, norm_w)


MERGE_TM = 1024
MERGE_SUB = 256


def _layer_norm(y, g, b):
    mu = jnp.mean(y, axis=1, keepdims=True)
    d = y - mu
    var = jnp.mean(d * d, axis=1, keepdims=True)
    return d * lax.rsqrt(var + LN_EPS) * g + b


def _merge_body(x_ref, oa_ref, ob_ref, ga_ref, gb_ref, wa_ref, wb_ref, wo_ref, g_ref, b_ref, y_ref):
    def sub_tile(rows):
        ya = _dot(oa_ref[rows, :], wa_ref[...])
        yb = _dot(ob_ref[rows, :], wb_ref[...])
        yield
        mixin = (jax.nn.sigmoid(ga_ref[rows, :].astype(F32)) * ya
                 + jax.nn.sigmoid(gb_ref[rows, :].astype(F32)) * yb).astype(BF16)
        yield
        mix = _dot(mixin, wo_ref[...])
        yield
        y_ref[rows, :] = _layer_norm(DEEPNORM_ALPHA * x_ref[rows, :] + mix, g_ref[...], b_ref[...])
        yield

    n_stage = 4
    subs = [sub_tile(slice(k * MERGE_SUB, (k + 1) * MERGE_SUB)) for k in range(MERGE_TM // MERGE_SUB)]
    for step in range(n_stage + len(subs) - 1):
        for k, sub in enumerate(subs):
            if 0 <= step - k < n_stage:
                next(sub)


def _merge(x2, oa, ob, mgate, wa, wb, wo, g, b):
    m = x2.shape[0]
    tm, d = MERGE_TM, D_MODEL
    row = lambda wd, col=0: pl.BlockSpec((tm, wd), lambda i, col=col: (i, col))
    return pl.pallas_call(
        _merge_body,
        grid=(m // tm,),
        in_specs=[row(d), row(NSA_Q_W), row(GDN_W), row(d, 0), row(d, 1),
                  _const_spec(wa.shape), _const_spec(wb.shape), _const_spec(wo.shape),
                  _const_spec(g.shape), _const_spec(b.shape)],
        out_specs=row(d),
        out_shape=jax.ShapeDtypeStruct((m, d), F32),
        compiler_params=_cparams("parallel"),
        name="merge",
    )(x2, oa, ob, mgate, mgate, wa, wb, wo, g, b)


FFN_TM = 1024
FFN_HALO = 16
FFN_CK = 256
FFN_OUT_PARTS = 4


def _ffn_body(x_ref, prev_ref, wu_ref, cw_ref, wd_ref, g_ref, b_ref, out_ref, act_s, *, tiles_per_seq):
    i = pl.program_id(0)
    prev = prev_ref[...].astype(BF16)
    prev = jnp.where(i % tiles_per_seq == 0, jnp.zeros_like(prev), prev)
    xc = jnp.concatenate([prev, x_ref[...].astype(BF16)], axis=0)

    def conv(h, c0):
        out = cw_ref[FFN_CONV - 1:FFN_CONV, c0:c0 + FFN_CK] * h[FFN_HALO:]
        for j in range(FFN_CONV - 1):
            shifted = pltpu.roll(h, FFN_CONV - 1 - j, axis=0)[FFN_HALO:]
            out = out + cw_ref[j:j + 1, c0:c0 + FFN_CK] * shifted
        return out

    for c in range(FFN_DIM // FFN_CK):
        c0 = c * FFN_CK
        hg = conv(_dot(xc, wu_ref[:, c0:c0 + FFN_CK]), c0)
        hv = conv(_dot(xc, wu_ref[:, FFN_DIM + c0:FFN_DIM + c0 + FFN_CK]), FFN_DIM + c0)
        act_s[:, c0:c0 + FFN_CK] = (jax.nn.silu(hg) * hv).astype(BF16)
    part = out_ref.shape[0] // FFN_OUT_PARTS
    starts = range(0, out_ref.shape[0], part)
    f = [_dot(act_s[r:r + part, :], wd_ref[...]) for r in starts]
    for k, r in enumerate(starts):
        out_ref[r:r + part, :] = _layer_norm(DEEPNORM_ALPHA * x_ref[r:r + part, :] + f[k], g_ref[...], b_ref[...])


def _ffn(x1, wu, cw, wd, g, b, seq):
    m = x1.shape[0]
    tm, d = FFN_TM, D_MODEL
    return pl.pallas_call(
        functools.partial(_ffn_body, tiles_per_seq=seq // tm),
        grid=(m // tm,),
        in_specs=[pl.BlockSpec((tm, d), lambda i: (i, 0)),
                  pl.BlockSpec((FFN_HALO, d), lambda i: (jnp.maximum(i * (tm // FFN_HALO) - 1, 0), 0)),
                  _const_spec(wu.shape), _const_spec(cw.shape), _const_spec(wd.shape), _const_spec(g.shape), _const_spec(b.shape)],
        out_specs=pl.BlockSpec((tm, d), lambda i: (i, 0)),
        out_shape=jax.ShapeDtypeStruct((m, d), F32),
        scratch_shapes=[pltpu.VMEM((tm, FFN_DIM), BF16)],
        compiler_params=_cparams("parallel"),
        name="ffn",
    )(x1, x1, wu, cw, wd, g, b)


def _lane_vec(vals, lane0):
    return jnp.zeros((1, LANES), F32).at[0, lane0:lane0 + vals.shape[0]].set(vals.astype(F32))


def _layer(x, w_in, cmp_pos, cmp_w1, cmp_w2, w_nsa_out, gdn_conv_w, gdn_a_log, gdn_dt_bias, gdn_norm_w,
           w_gdn_out, w_o, ln1_g, ln1_b, ffn_w_up, ffn_conv_w, ffn_w_down, ln2_g, ln2_b):
    b, s, d = x.shape
    m = b * s
    x2 = x.reshape(m, d)
    w_rows, w_gdn, w_t, w_g = _wprep(*w_in)
    keys, cmpkv, ggate, mgate, qvt, gt = _inproj(x2, w_rows, w_t, w_g)

    consts = _nsa_consts(s)
    post, w2sel = _compress_weights(cmp_pos, cmp_w2)
    cmp_kv = _compress(cmpkv.reshape(b, s // CMP_STRIDE, CMP_STRIDE * CMP_COLS), cmp_w1, post, w2sel,
                       consts["cmp_aug"])
    o_nsa = _nsa(qvt, keys.reshape(b, s, KEYS_COLS), cmp_kv, gt, consts, b, s)

    ck = GDN_CHUNK
    tri = np.tril(np.ones((ck, ck), np.float32))
    lt = jnp.asarray(np.kron(np.eye(GDN_BLK // ck, dtype=np.float32), tri))
    nq, co, eg = _gdn_intra(x, w_gdn, gdn_conv_w, _lane_vec(gdn_a_log, SM_DECAY), _lane_vec(gdn_dt_bias, SM_DECAY),
                            lt)
    o_gdn = _gdn_scan(nq, co, eg, ggate.reshape(b, s, GDN_W), gdn_norm_w.reshape(1, GDN_HEAD_DIM))

    x1 = _merge(x2, o_nsa.reshape(m, NSA_Q_W), o_gdn.reshape(m, GDN_W), mgate,
                     w_nsa_out.astype(BF16), w_gdn_out.astype(BF16), w_o.astype(BF16),
                     ln1_g.reshape(1, d), ln1_b.reshape(1, d))
    out = _ffn(x1, ffn_w_up.astype(BF16), ffn_conv_w, ffn_w_down.astype(BF16),
               ln2_g.reshape(1, d), ln2_b.reshape(1, d), s)
    return out.reshape(b, s, d)


def kernel(x, w_in, nsa_cmp_pos, nsa_cmp_w1, nsa_cmp_w2, w_nsa_out, gdn_conv_w, gdn_a_log, gdn_dt_bias, gdn_norm_w, w_gdn_out, w_o, ln1_g, ln1_b, ffn_w_up, ffn_conv_w, ffn_w_down, ln2_g, ln2_b):
    w_in_t = jnp.swapaxes(w_in, 1, 2)
    for l in range(DEPTH):
        x = _layer(x, (w_in_t, l), nsa_cmp_pos[l], nsa_cmp_w1[l], nsa_cmp_w2[l], w_nsa_out[l], gdn_conv_w[l],
                   gdn_a_log[l], gdn_dt_bias[l], gdn_norm_w[l], w_gdn_out[l], w_o[l], ln1_g[l], ln1_b[l],
                   ffn_w_up[l], ffn_conv_w[l], ffn_w_down[l], ln2_g[l], ln2_b[l])
    return x
```

```python
import functools

import numpy as np
import jax
import jax.numpy as jnp
from jax import lax
from jax.experimental import pallas as pl
from jax.experimental.pallas import tpu as pltpu

F32 = jnp.float32
BF16 = jnp.bfloat16

D_MODEL = 1024
NSA_HEADS = 8
NSA_KV_GROUPS = 2
NSA_REP = NSA_HEADS // NSA_KV_GROUPS
NSA_HEAD_DIM = 64
CMP_LEN = 32
CMP_STRIDE = 16
SLC_LEN = 64
SLC_TOPK = 8
WIN_LEN = 512
FORCE_SCORE = 1.0e4
NEG = -1.0e30
GDN_HEADS = 4
GDN_HEAD_DIM = 128
GDN_CONV = 4
GDN_CHUNK = 64
FFN_DIM = 2816
FFN_CONV = 3
DEPTH = 1
DEEPNORM_ALPHA = (2.0 * DEPTH) ** 0.25
LN_EPS = 1e-5
RMS_EPS = 1e-6

NSA_Q_W = NSA_HEADS * NSA_HEAD_DIM
NSA_KV_W = NSA_KV_GROUPS * NSA_HEAD_DIM
GDN_W = GDN_HEADS * GDN_HEAD_DIM

LANES = 128
SUBLANES = 8
VMEM_LIMIT_BYTES = 56 * 1024 * 1024

AUG_SEL0 = 64
AUG_POS_HI = 96
AUG_POS_LO = 97
AUG_PAD = 98
BIG = 2.0 ** 100
POS_SPLIT = 256
Q_TILE = 128
N_SLC = 32
V_ROWS = 80
SEL_KC = 512
NSA_SUB = 4

NT_DIMS = (((1,), (1,)), ((), ()))


def _dot(a, b, **kw):
    return jnp.dot(a, b, preferred_element_type=F32, **kw)


def _dot_nt(a, b, **kw):
    return lax.dot_general(a, b, NT_DIMS, preferred_element_type=F32, **kw)


def _cparams(*sem):
    return pltpu.CompilerParams(dimension_semantics=sem, vmem_limit_bytes=VMEM_LIMIT_BYTES)


def _const_spec(shape):
    nd = len(shape)
    return pl.BlockSpec(shape, lambda *_: (0,) * nd, pipeline_mode=pl.Buffered(1))


_IN_WIDTHS = (NSA_Q_W,) + (NSA_KV_W,) * 6 + (3 * NSA_HEADS, 3 * GDN_W, GDN_HEADS, GDN_HEADS, GDN_W, 2 * D_MODEL)
(_C_Q, _C_CK, _C_CV, _C_SK, _C_SV, _C_WK, _C_WV, _C_GATE, _C_GQKV, _C_BETA, _C_DECAY, _C_GGATE, _C_MERGE,
 IN_WIDTH) = (int(v) for v in np.cumsum((0,) + _IN_WIDTHS))
_C_SMALL = _C_BETA // LANES * LANES
SM_BETA = _C_BETA - _C_SMALL
SM_DECAY = _C_DECAY - _C_SMALL
KEYS_COLS = 2 * NSA_KV_W
CMP_COLS = 2 * NSA_KV_W
_INPROJ_GROUPS = (("keys", KEYS_COLS, BF16), ("cmp", CMP_COLS, BF16), ("ggate", GDN_W, BF16),
                  ("merge", 2 * D_MODEL, BF16))
_INPROJ_WIDTH = sum(w for _, w, _ in _INPROJ_GROUPS)
_GDN_PROJ_WIDTH = 3 * GDN_W + LANES
_INPROJ_T_ROWS = NSA_Q_W + 4 * NSA_HEAD_DIM
_GATE_T_ROWS = 32
INPROJ_TM = 1024
INPROJ_TN = 512
WPREP_TK = 128


def _wprep_body(w_ref, rows_ref, gdn_ref, wt_ref, wg_ref):
    hd = NSA_HEAD_DIM
    feat = lambda c0, n: w_ref[0, c0:c0 + n, :]

    def put_t(ref, col, src):
        for r in range(0, src.shape[0], LANES):
            ref[:, col + r:col + r + LANES] = src[r:r + LANES].T.astype(BF16)

    for g in range(NSA_KV_GROUPS):
        put_t(rows_ref, 2 * g * hd, jnp.concatenate([feat(_C_SK + g * hd, hd), feat(_C_WK + g * hd, hd)], axis=0))
    c = 4 * hd
    for c0, n in ((_C_CK, 2 * NSA_KV_W), (_C_GGATE, GDN_W), (_C_MERGE, 2 * D_MODEL)):
        put_t(rows_ref, c, feat(c0, n))
        c += n
    put_t(gdn_ref, 0, feat(_C_GQKV, 3 * GDN_W))
    put_t(gdn_ref, 3 * GDN_W, feat(_C_SMALL, LANES))
    wt_ref[:NSA_Q_W, :] = feat(_C_Q, NSA_Q_W).astype(BF16)
    for j, c0 in enumerate((_C_SV, _C_WV, _C_SV + hd, _C_WV + hd)):
        wt_ref[NSA_Q_W + j * hd:NSA_Q_W + (j + 1) * hd, :] = feat(c0, hd).astype(BF16)
    wg_ref[...] = feat(_C_GATE, _GATE_T_ROWS).astype(BF16)


def _wprep(w_in_t, layer):
    k = w_in_t.shape[2]
    tk = WPREP_TK
    return pl.pallas_call(
        _wprep_body,
        grid=(k // tk,),
        in_specs=[pl.BlockSpec((1, IN_WIDTH, tk), lambda i: (layer, 0, i))],
        out_specs=[pl.BlockSpec((tk, _INPROJ_WIDTH), lambda i: (i, 0)),
                   pl.BlockSpec((tk, _GDN_PROJ_WIDTH), lambda i: (i, 0)),
                   pl.BlockSpec((_INPROJ_T_ROWS, tk), lambda i: (0, i)),
                   pl.BlockSpec((_GATE_T_ROWS, tk), lambda i: (0, i))],
        out_shape=[jax.ShapeDtypeStruct((k, _INPROJ_WIDTH), BF16), jax.ShapeDtypeStruct((k, _GDN_PROJ_WIDTH), BF16),
                   jax.ShapeDtypeStruct((_INPROJ_T_ROWS, k), BF16), jax.ShapeDtypeStruct((_GATE_T_ROWS, k), BF16)],
        compiler_params=_cparams("parallel"),
        name="wprep",
    )(w_in_t)


def _inproj_body(x_ref, w_ref, wt_ref, wg_ref, keys_ref, cmp_ref, ggate_ref, merge_ref, qvt_ref, gt_ref, cmp_s):
    x = x_ref[...].astype(BF16)
    outs = (keys_ref, None, ggate_ref, merge_ref)
    c0 = 0
    for ref, (name, width, _) in zip(outs, _INPROJ_GROUPS):
        for s in range(0, width, INPROJ_TN):
            e = min(s + INPROJ_TN, width)
            res = _dot(x, w_ref[:, c0 + s:c0 + e])
            if name == "cmp":
                for j in range(width // LANES):
                    cmp_s[j] = res[:, j * LANES:(j + 1) * LANES]
            else:
                ref[:, s:e] = res.astype(ref.dtype)
        c0 += width
    nblk = cmp_ref.shape[0]
    for l in range(CMP_STRIDE):
        for j in range(cmp_s.shape[0]):
            cmp_ref[:, l * CMP_COLS + j * LANES:l * CMP_COLS + (j + 1) * LANES] = (
                cmp_s[j, pl.ds(l, nblk, stride=CMP_STRIDE), :].astype(BF16))
    for s in range(0, _INPROJ_T_ROWS, 2 * LANES):
        qvt_ref[s:s + 2 * LANES, :] = _dot_nt(wt_ref[s:s + 2 * LANES, :], x).astype(qvt_ref.dtype)
    gt_ref[...] = _dot_nt(wg_ref[...], x)


def _inproj(x2, w_rows, w_t, w_g):
    m = x2.shape[0]
    tm = INPROJ_TM
    row_major = [(n, wd, dt) for n, wd, dt in _INPROJ_GROUPS if n != "cmp"]
    specs = {n: (pl.BlockSpec((tm, wd), lambda i: (i, 0)), jax.ShapeDtypeStruct((m, wd), dt))
             for n, wd, dt in row_major}
    specs["cmp"] = (pl.BlockSpec((tm // CMP_STRIDE, CMP_STRIDE * CMP_COLS), lambda i: (i, 0)),
                    jax.ShapeDtypeStruct((m // CMP_STRIDE, CMP_STRIDE * CMP_COLS), BF16))
    order = [n for n, _, _ in _INPROJ_GROUPS]
    return pl.pallas_call(
        _inproj_body,
        grid=(m // tm,),
        in_specs=[pl.BlockSpec((tm, D_MODEL), lambda i: (i, 0)), _const_spec(w_rows.shape),
                  _const_spec(w_t.shape), _const_spec(w_g.shape)],
        out_specs=[specs[n][0] for n in order]
        + [pl.BlockSpec((_INPROJ_T_ROWS, tm), lambda i: (0, i)), pl.BlockSpec((_GATE_T_ROWS, tm), lambda i: (0, i))],
        out_shape=[specs[n][1] for n in order]
        + [jax.ShapeDtypeStruct((_INPROJ_T_ROWS, m), BF16), jax.ShapeDtypeStruct((_GATE_T_ROWS, m), F32)],
        scratch_shapes=[pltpu.VMEM((CMP_COLS // LANES, tm, LANES), F32)],
        compiler_params=_cparams("parallel"),
        name="inproj",
    )(x2, w_rows, w_t, w_g)


def _compress_weights(cmp_pos, cmp_w2):
    hd, half = NSA_HEAD_DIM, CMP_LEN // 2
    posr = cmp_pos.reshape(2, 2, half, hd)
    post = jnp.broadcast_to(posr.transpose(1, 2, 0, 3)[:, :, :, None, :], (2, half, 2, 2, hd))
    post = jnp.concatenate([post.reshape(2, half * 4 * hd),
                            jnp.zeros((SUBLANES - 2, half * 4 * hd), cmp_pos.dtype)], axis=0)
    w2sel = jnp.zeros((2, 2, 2 * hd, LANES), cmp_w2.dtype)
    for g in range(2):
        w2sel = w2sel.at[:, g, g * hd:(g + 1) * hd, :hd].set(cmp_w2)
    return post.astype(BF16), w2sel.reshape(4, 2 * hd, LANES).astype(BF16)


def _compress_body(t_ref, w1_ref, pos_ref, w2_ref, aug_ref, out_ref, w1e_s):
    hd, half_len = NSA_HEAD_DIM, CMP_LEN // 2

    @pl.when(pl.program_id(0) == 0)
    def _():
        w1e_s[...] = jnp.zeros(w1e_s.shape, BF16)
        for which in range(2):
            for half in range(2):
                for l in range(half_len):
                    blk = w1_ref[which, (half * half_len + l) * hd:(half * half_len + l + 1) * hd, :].astype(BF16)
                    for g in range(NSA_KV_GROUPS):
                        r0 = l * CMP_COLS + which * LANES + g * hd
                        c0 = half * CMP_COLS + which * LANES + g * hd
                        w1e_s[r0:r0 + hd, c0:c0 + hd] = blk

    p = _dot(t_ref[0], w1e_s[...])
    pp = _dot(pos_ref[...], w1e_s[...])
    nxt = pltpu.roll(p[:, CMP_COLS:], p.shape[0] - 1, axis=0)
    pre = p[:, :CMP_COLS] + nxt + pp[0:1, :CMP_COLS] + pp[1:2, CMP_COLS:]
    h = jax.nn.gelu(pre).astype(BF16)
    n_idx = lax.broadcasted_iota(jnp.int32, (p.shape[0], LANES), 0)
    real = n_idx < p.shape[0] - 1
    for which in range(2):
        hw = h[:, which * LANES:(which + 1) * LANES]
        for g in range(2):
            o = jnp.where(real, _dot(hw, w2_ref[which * 2 + g]) + aug_ref[which], 0.0)
            out_ref[0, which * 2 + g] = (o if which == 0 else o.T).astype(out_ref.dtype)


def _compress(t2, w1, post, w2sel, aug):
    b, nblk, _ = t2.shape
    return pl.pallas_call(
        _compress_body,
        grid=(b,),
        in_specs=[pl.BlockSpec((1, nblk, CMP_STRIDE * CMP_COLS), lambda i: (i, 0, 0)),
                  _const_spec(w1.shape), _const_spec(post.shape), _const_spec(w2sel.shape), _const_spec(aug.shape)],
        out_specs=pl.BlockSpec((1, 4, nblk, LANES), lambda i: (i, 0, 0, 0)),
        out_shape=jax.ShapeDtypeStruct((b, 4, nblk, LANES), BF16),
        scratch_shapes=[pltpu.VMEM((CMP_STRIDE * CMP_COLS, 2 * CMP_COLS), BF16)],
        compiler_params=_cparams("arbitrary"),
        name="compress",
    )(t2, w1, post, w2sel, aug)


def _nsa_consts(s):
    hd, rep = NSA_HEAD_DIM, NSA_REP
    t = np.arange(s)
    kx_win = np.zeros((s + WIN_LEN, hd), np.float32)
    kx_win[WIN_LEN + t, AUG_POS_HI - hd] = t // POS_SPLIT
    kx_win[WIN_LEN + t, AUG_POS_LO - hd] = t % POS_SPLIT
    kx_win[:WIN_LEN, AUG_PAD - hd] = 1.0
    kx_sel = kx_win.copy()
    kx_sel[WIN_LEN + t, t // SLC_LEN] = 1.0
    vx_win = np.zeros((V_ROWS - hd, s + WIN_LEN), np.float32)
    vx_win[0, WIN_LEN:] = 1.0
    vx_sel = vx_win
    n_cmp = s // CMP_STRIDE
    cmp_aug = np.zeros((2, n_cmp, LANES), np.float32)
    end = np.arange(n_cmp) * CMP_STRIDE + CMP_LEN - 1
    cmp_aug[0, :, AUG_POS_HI] = end // POS_SPLIT
    cmp_aug[0, :, AUG_POS_LO] = end % POS_SPLIT
    qx = np.zeros((NSA_KV_GROUPS, LANES - AUG_POS_HI, rep * Q_TILE), np.float32)
    for h in range(NSA_HEADS):
        slope = 2.0 ** (-8.0 * (h + 1) / NSA_HEADS)
        lanes = slice((h % rep) * Q_TILE, (h % rep + 1) * Q_TILE)
        qx[h // rep, 0, lanes] = slope * POS_SPLIT
        qx[h // rep, 1, lanes] = slope
        qx[h // rep, AUG_PAD - AUG_POS_HI, lanes] = -BIG
    c0 = np.arange(n_cmp)[None, :] * CMP_STRIDE
    s0 = np.arange(s // SLC_LEN)[:, None] * SLC_LEN
    ov_t = ((c0 < s0 + SLC_LEN) & (c0 + CMP_LEN > s0)).astype(np.float32)
    ov_t[:, (s - CMP_LEN) // CMP_STRIDE + 1:] = 0.0
    kk = np.arange(Q_TILE)[:, None]
    qq = np.arange(Q_TILE)[None, :]
    causal = np.tile(np.where(kk <= qq, 0.0, NEG).astype(np.float32), (1, rep))
    after = np.tile(np.where(kk > qq, 0.0, NEG).astype(np.float32), (1, rep))
    j = jnp.asarray
    return dict(kx_sel=j(kx_sel, BF16), kx_win=j(kx_win, BF16), vx_sel=j(vx_sel, BF16), vx_win=j(vx_win, BF16),
                cmp_aug=j(cmp_aug), qx=j(qx), ov_t=j(ov_t), causal=j(causal), after=j(after))


def _nsa_body(qt_ref, k_ref, vt_ref, kc_ref, vct_ref, gt_ref, kxs_ref, kxw_ref, vxs_ref, vxw_ref, qx_ref, ovt_ref,
              causal_ref, after_ref, out_ref, ks_s, kw_s, vs_s, vw_s):
    hd, rep, tq = NSA_HEAD_DIM, NSA_REP, Q_TILE
    nq = rep * tq
    i = pl.program_id(2)

    @pl.when(i == 0)
    def _():
        keys = k_ref[0]
        ks_s[:WIN_LEN, :hd] = jnp.zeros((WIN_LEN, hd), BF16)
        ks_s[WIN_LEN:, :hd] = keys[:, :hd]
        ks_s[:, hd:] = kxs_ref[...]
        kw_s[:WIN_LEN, :hd] = jnp.zeros((WIN_LEN, hd), BF16)
        kw_s[WIN_LEN:, :hd] = keys[:, hd:]
        kw_s[:, hd:] = kxw_ref[...]
        vals = vt_ref[...]
        vs_s[:hd, :WIN_LEN] = jnp.zeros((hd, WIN_LEN), BF16)
        vs_s[:hd, WIN_LEN:] = vals[:hd]
        vs_s[hd:, :] = vxs_ref[...]
        vw_s[:hd, :WIN_LEN] = jnp.zeros((hd, WIN_LEN), BF16)
        vw_s[:hd, WIN_LEN:] = vals[hd:]
        vw_s[hd:, :] = vxw_ref[...]

    qx = qx_ref[0]
    sg_all = jax.nn.sigmoid(gt_ref[...])
    grp = pl.program_id(1)

    full_past = WIN_LEN // tq

    def front(sub, res, it, past):
        qt = qt_ref[:, sub * tq:(sub + 1) * tq]
        q64 = jnp.concatenate([qt[r * hd:(r + 1) * hd, :] for r in range(rep)], axis=1).astype(F32) * (hd ** -0.5)

        def q_aug(sel_rows):
            return jnp.concatenate([q64, sel_rows, qx], axis=0).astype(BF16)

        n_row = lax.broadcasted_iota(jnp.int32, (LANES, nq), 0)
        t_lane = it * tq + (lax.broadcasted_iota(jnp.int32, (LANES, nq), 1) & (tq - 1))
        valid = t_lane >= n_row * CMP_STRIDE + (CMP_LEN - 1)
        qa0 = q_aug(jnp.zeros((N_SLC, nq), F32))
        sc = jnp.where(valid, _dot(kc_ref[0, 0], qa0), NEG)
        mc = jnp.max(sc, axis=0, keepdims=True)
        ec = jnp.where(valid, jnp.exp(sc - mc), 0.0)
        lc = jnp.sum(ec, axis=0, keepdims=True)
        pc = ec * jnp.where(lc > 0.0, 1.0 / lc, 0.0)
        o_cmp = _dot(vct_ref[0, 0], pc.astype(BF16))[:hd]
        psum = pc[:, 0:tq] + pc[:, tq:2 * tq] + pc[:, 2 * tq:3 * tq] + pc[:, 3 * tq:4 * tq]
        score_t = _dot(ovt_ref[...], psum, precision=lax.Precision.HIGHEST)
        yield

        n_keys = (past + 1) * tq
        if past == full_past:
            w0 = pl.multiple_of(it * tq, tq)
        else:
            w0 = WIN_LEN
        diag = past * tq

        def one_shot_masks(s):
            parts = [s[:diag], s[diag:] + causal_ref[...]] if past else [s + causal_ref[...]]
            if past == full_past:
                parts = [s[:tq] + after_ref[...], s[tq:diag], parts[1]]
            return jnp.concatenate(parts, axis=0) if len(parts) > 1 else parts[0]

        s_w = one_shot_masks(_dot(kw_s[pl.ds(w0, n_keys), :], qa0))
        p_w = jnp.exp(s_w - jnp.max(s_w, axis=0, keepdims=True))
        acc_w = _dot(vw_s[:, pl.ds(w0, n_keys)], p_w.astype(BF16))
        o_win = acc_w[:hd] * (1.0 / acc_w[hd:hd + 1])
        yield

        jb = lax.broadcasted_iota(jnp.int32, (N_SLC, tq), 0)
        cur = (it * tq + lax.broadcasted_iota(jnp.int32, (N_SLC, tq), 1)) // SLC_LEN
        forced = (jb == 0) | (jb == cur) | (jb == cur - 1)
        score_t = jnp.where(forced, FORCE_SCORE, jnp.where(jb <= cur, score_t, -1.0))
        rank = jnp.zeros((N_SLC, tq), F32)
        for jp in range(N_SLC):
            other = score_t[jp:jp + 1, :]
            ge = jnp.where(other >= score_t, 1.0, 0.0)
            gt = jnp.where(other > score_t, 1.0, 0.0)
            rank = rank + jnp.where(jb > jp, ge, gt)
        sel = rank < float(SLC_TOPK)
        qa = q_aug(jnp.concatenate([jnp.where(sel, 0.0, -BIG)] * rep, axis=1))
        lo_blk = jnp.min(jnp.where(sel & (jb >= 2) & (jb <= cur), jb.astype(F32), float(N_SLC)))
        lo_key = (lo_blk.astype(jnp.int32) // 2) * tq
        yield

        s_main = _dot(ks_s[pl.ds(w0, n_keys), :], qa)
        if past < full_past:
            s_s = jnp.concatenate([s_main[:diag], s_main[diag:] + causal_ref[...]], axis=0) if past else (
                s_main + causal_ref[...])
            m_s = jnp.max(s_s, axis=0, keepdims=True)
            acc_s = _dot(vs_s[:, pl.ds(w0, n_keys)], jnp.exp(s_s - m_s).astype(BF16))
            res.update(early=False, acc_s=acc_s, o_cmp=o_cmp, o_win=o_win)
            return
        e_key = it * tq - WIN_LEN
        t0 = pl.multiple_of(jnp.where(e_key > 0, WIN_LEN, 0), tq)
        s_s = jnp.concatenate([_dot(ks_s[pl.ds(t0, tq), :], qa), s_main[:diag],
                               s_main[diag:] + causal_ref[...]], axis=0)
        m_s = jnp.max(s_s, axis=0, keepdims=True)
        p_s = jnp.exp(s_s - m_s).astype(BF16)
        acc_s = _dot(vs_s[:, pl.ds(t0, tq)], p_s[:tq]) + _dot(vs_s[:, pl.ds(w0, n_keys)], p_s[tq:])
        c_hi = (e_key - tq + SEL_KC - 1) // SEL_KC
        c_lo = jnp.where(lo_key < e_key, (lo_key - tq) // SEL_KC, c_hi)
        res.update(early=True, qa=qa, e_key=e_key, c_lo=c_lo, c_hi=c_hi, m_s=m_s, acc_s=acc_s, o_cmp=o_cmp,
                   o_win=o_win)

    def tail(sub, f):
        def early_step(c, carry):
            qa, e_key = f["qa"], f["e_key"]
            m, acc = carry
            k0 = tq + c * SEL_KC
            start = pl.multiple_of(WIN_LEN + k0, tq)
            k_abs = k0 + lax.broadcasted_iota(jnp.int32, (SEL_KC, nq), 0)
            s = jnp.where(k_abs < e_key, _dot(ks_s[pl.ds(start, SEL_KC), :], qa), NEG)
            m_new = jnp.maximum(m, jnp.max(s, axis=0, keepdims=True))
            p = jnp.exp(s - m_new).astype(BF16)
            return m_new, acc * jnp.exp(m - m_new) + _dot(vs_s[:, pl.ds(start, SEL_KC)], p)

        acc_s = f["acc_s"]
        if f["early"]:
            _, acc_s = lax.fori_loop(f["c_lo"], f["c_hi"], early_step, (f["m_s"], acc_s))
        o_slc = acc_s[:hd] * (1.0 / acc_s[hd:hd + 1])

        sg = sg_all[:, sub * tq:(sub + 1) * tq]
        gate = lambda br, r: jnp.where(grp == 0, sg[br * NSA_HEADS + r:br * NSA_HEADS + r + 1],
                                       sg[br * NSA_HEADS + rep + r:br * NSA_HEADS + rep + r + 1])
        for pair in range(rep // 2):
            halves = []
            for r in (2 * pair, 2 * pair + 1):
                lanes = slice(r * tq, (r + 1) * tq)
                halves.append(gate(0, r) * f["o_cmp"][:, lanes] + gate(1, r) * o_slc[:, lanes]
                              + gate(2, r) * f["o_win"][:, lanes])
            out_ref[0, sub * tq:(sub + 1) * tq, pair * LANES:(pair + 1) * LANES] = (
                jnp.concatenate(halves, axis=0).T.astype(out_ref.dtype))

    def run(tiles):
        fronts = [{} for _ in range(NSA_SUB)]
        for sub, (it, past) in enumerate(tiles):
            for _ in front(sub, fronts[sub], it, past):
                pass
        for sub in range(NSA_SUB):
            tail(sub, fronts[sub])

    n_short = -(-full_past // NSA_SUB)
    for step in range(n_short):
        @pl.when(i == step)
        def _(step=step):
            run([(step * NSA_SUB + sub, min(step * NSA_SUB + sub, full_past)) for sub in range(NSA_SUB)])

    @pl.when(i >= n_short)
    def _():
        run([(i * NSA_SUB + sub, full_past) for sub in range(NSA_SUB)])


def _nsa(qvt, keys, cmp_kv, gt, consts, b, s):
    tqs = NSA_SUB * Q_TILE
    nt = s // tqs
    c = consts
    in_specs = [
        pl.BlockSpec((2 * LANES, tqs), lambda bi, g, i: (g, bi * nt + i)),
        pl.BlockSpec((1, s, LANES), lambda bi, g, i: (bi, 0, g)),
        pl.BlockSpec((LANES, s), lambda bi, g, i: (NSA_Q_W // LANES + g, bi)),
        pl.BlockSpec((1, 1, s // CMP_STRIDE, LANES), lambda bi, g, i: (bi, g, 0, 0)),
        pl.BlockSpec((1, 1, s // CMP_STRIDE, LANES), lambda bi, g, i: (bi, 2 + g, 0, 0)),
        pl.BlockSpec((_GATE_T_ROWS, tqs), lambda bi, g, i: (0, bi * nt + i)),
        _const_spec(c["kx_sel"].shape), _const_spec(c["kx_win"].shape), _const_spec(c["vx_sel"].shape),
        _const_spec(c["vx_win"].shape),
        pl.BlockSpec((1,) + c["qx"].shape[1:], lambda bi, g, i: (g, 0, 0)),
        _const_spec(c["ov_t"].shape), _const_spec(c["causal"].shape), _const_spec(c["after"].shape),
    ]
    return pl.pallas_call(
        _nsa_body,
        grid=(b, NSA_KV_GROUPS, nt),
        in_specs=in_specs,
        out_specs=pl.BlockSpec((1, tqs, 2 * LANES), lambda bi, g, i: (bi, i, g)),
        out_shape=jax.ShapeDtypeStruct((b, s, NSA_Q_W), BF16),
        scratch_shapes=[pltpu.VMEM((s + WIN_LEN, LANES), BF16), pltpu.VMEM((s + WIN_LEN, LANES), BF16),
                        pltpu.VMEM((V_ROWS, s + WIN_LEN), BF16), pltpu.VMEM((V_ROWS, s + WIN_LEN), BF16)],
        compiler_params=_cparams("parallel", "parallel", "arbitrary"),
        name="nsa",
    )(qvt, keys, qvt, cmp_kv, cmp_kv, gt, c["kx_sel"], c["kx_win"], c["vx_sel"], c["vx_win"], c["qx"], c["ov_t"],
      c["causal"], c["after"])


GDN_TS = 512
GDN_BLK = 128
GDN_HALO = 8
GDN_XHALO = 16
GDN_INV_BASE = 8
GDN_CHAIN_UNITS_PER_PREP_UNIT = 4
GDN_SCAN_ROWS = GDN_HEAD_DIM + GDN_CHUNK


def _gdn_prep(x_ref, prev_ref, w_ref, cw_ref, alog_ref, dtb_ref, lt_ref, first_tile, xp_s, buf):
    x_s, p_s, rhs_s, qg_s, aqk_s, kdt_s, a_s, eg_s = buf
    ts, dh, nh, blk = GDN_TS, GDN_HEAD_DIM, GDN_HEADS, GDN_BLK
    proj = _dot(x_ref[0].astype(BF16), w_ref[...])
    hist = _dot(prev_ref[0].astype(BF16), w_ref[:, :3 * GDN_W])[GDN_XHALO - GDN_HALO:]
    xp_s[0:GDN_HALO, :] = jnp.where(first_tile, 0.0, hist)
    xp_s[GDN_HALO:, :] = proj[:, :3 * GDN_W]
    yield
    act = []
    for blk_i in range(3 * nh):
        lanes = slice(blk_i * dh, (blk_i + 1) * dh)
        xp = xp_s[:, lanes]
        conv = cw_ref[0:1, lanes] * xp
        for j in range(1, GDN_CONV):
            conv = pltpu.roll(conv, 1, axis=0) + cw_ref[j:j + 1, lanes] * xp
        act.append(jax.nn.silu(conv[GDN_HALO:]))
        yield

    sm = proj[:, 3 * GDN_W:]
    beta = jax.nn.sigmoid(sm)
    g = -jnp.exp(alog_ref[...]) * jax.nn.softplus(sm + dtb_ref[...])
    gcum = jnp.concatenate([_dot(lt_ref[...], g[r:r + blk], precision=lax.Precision.HIGHEST)
                            for r in range(0, ts, blk)], axis=0)
    eg = jnp.exp(gcum)
    eg_s[...] = eg

    ri = lax.broadcasted_iota(jnp.int32, (blk, blk), 0)
    ci = lax.broadcasted_iota(jnp.int32, (blk, blk), 1)
    same = (ri // GDN_CHUNK) == (ci // GDN_CHUNK)
    causal = same & (ri >= ci)
    strict = same & (ri > ci)
    eye = (ri == ci).astype(F32)
    same_base = (ri // GDN_INV_BASE) == (ci // GDN_INV_BASE)

    for pb in range(ts // blk):
        rows = slice(pb * blk, (pb + 1) * blk)
        gc = gcum[rows]
        gc_t = gc.T
        first = lax.broadcasted_iota(jnp.int32, (blk, LANES), 0) < GDN_CHUNK
        g_last = jnp.where(first, gc[GDN_CHUNK - 1:GDN_CHUNK, :], gc[blk - 1:blk, :])
        e_dec = jnp.exp(g_last - gc)
        for h in range(nh):
            q, k, v = act[h][rows], act[nh + h][rows], act[2 * nh + h][rows]
            q = q * lax.rsqrt(jnp.sum(q * q, axis=1, keepdims=True) + RMS_EPS) * (dh ** -0.5)
            k = k * lax.rsqrt(jnp.sum(k * k, axis=1, keepdims=True) + RMS_EPS)
            b_col = beta[rows, SM_BETA + h:SM_BETA + h + 1]
            eg_col = eg[rows, SM_DECAY + h:SM_DECAY + h + 1]
            gdiff = gc[:, SM_DECAY + h:SM_DECAY + h + 1] - gc_t[SM_DECAY + h:SM_DECAY + h + 1, :]
            decay = jnp.exp(jnp.where(causal, gdiff, NEG))
            kb = k * b_col
            kbf, kf, qf = kb.astype(BF16), k.astype(BF16), q.astype(BF16)
            a = jnp.where(strict, -_dot_nt(kbf, kf) * decay, 0.0)
            c = pb * nh + h
            a_base = jnp.where(same_base, a, 0.0)
            a_s[c] = a.astype(BF16)
            x_s[c] = a_base.astype(BF16)
            p_s[c] = eye + a_base
            rhs_s[c] = jnp.concatenate([v * b_col, kb * eg_col], axis=1).astype(BF16)
            qg_s[c] = q * eg_col
            aqk_s[c] = jnp.where(causal, _dot_nt(qf, kf) * decay, 0.0).astype(BF16)
            kdt_s[c] = (k * e_dec[:, SM_DECAY + h:SM_DECAY + h + 1]).T.astype(BF16)
            yield


def _gdn_chains(buf, nq_ref, co_ref, eg_ref):
    x_s, p_s, rhs_s, qg_s, aqk_s, kdt_s, a_s, eg_s = buf
    dh, nh, blk = GDN_HEAD_DIM, GDN_HEADS, GDN_BLK
    n_chain = x_s.shape[0]
    eg_ref[0] = eg_s[...]
    ri = lax.broadcasted_iota(jnp.int32, (blk, blk), 0)
    ci = lax.broadcasted_iota(jnp.int32, (blk, blk), 1)
    same = lambda size: (ri // size) == (ci // size)
    for c in range(n_chain):
        y = x_s[c]
        x_s[c] = _dot(y, y).astype(BF16)
        yield
    for c in range(n_chain):
        y2 = x_s[c]
        p = p_s[c]
        p_s[c] = p + _dot(p.astype(BF16), y2)
        x_s[c] = _dot(y2, y2).astype(BF16)
        yield
    for c in range(n_chain):
        p = p_s[c]
        p_s[c] = p + _dot(p.astype(BF16), x_s[c])
        yield
    size = 2 * GDN_INV_BASE
    while size <= GDN_CHUNK:
        between = same(size) & jnp.logical_not(same(size // 2))
        for c in range(n_chain):
            a = a_s[c]
            x_s[c] = _dot(p_s[c].astype(BF16), jnp.where(between, a, jnp.zeros_like(a))).astype(BF16)
            yield
        for c in range(n_chain):
            t = p_s[c]
            p_s[c] = t + _dot(x_s[c], t.astype(BF16))
            yield
        size *= 2

    tok_half = lax.broadcasted_iota(jnp.int32, (blk, blk), 1) // GDN_CHUNK
    for c in range(n_chain):
        rhs_s[c] = _dot(p_s[c].astype(BF16), rhs_s[c]).astype(BF16)
        yield
    for c in range(n_chain):
        pb, h = divmod(c, nh)
        uw = rhs_s[c]
        a1 = _dot(aqk_s[c], uw)
        q_loc = qg_s[c] - a1[:, dh:]
        kdt = kdt_s[c]
        for half in range(blk // GDN_CHUNK):
            k1 = _dot(jnp.where(tok_half == half, kdt, jnp.zeros_like(kdt)), uw)
            n = pb * (blk // GDN_CHUNK) + half
            rows = slice(half * GDN_CHUNK, (half + 1) * GDN_CHUNK)
            nq_ref[0, h, n, :dh, :] = (-k1[:, dh:]).astype(BF16)
            nq_ref[0, h, n, dh:, :] = q_loc[rows].astype(BF16)
            co_ref[0, h, n, :dh, :] = k1[:, :dh].astype(BF16)
            co_ref[0, h, n, dh:, :] = a1[rows, :dh].astype(BF16)
        yield


def _interleave(major, minor, minor_per_major):
    for _ in major:
        for _ in range(minor_per_major):
            next(minor, None)
    for _ in minor:
        pass


def _gdn_intra_body(x_ref, prev_ref, w_ref, cw_ref, alog_ref, dtb_ref, lt_ref, nq_ref, co_ref, eg_ref, xp_s, *bufs,
                    tiles_per_seq, n_buf):
    j = pl.program_id(0)
    sets = (bufs[:n_buf], bufs[n_buf:])

    @pl.when(j == 0)
    def _():
        for ref in sets[1]:
            ref[...] = jnp.zeros(ref.shape, ref.dtype)

    first_tile = (j % tiles_per_seq) == 0
    for parity in range(2):
        @pl.when(j % 2 == parity)
        def _(parity=parity):
            chains = _gdn_chains(sets[1 - parity], nq_ref, co_ref, eg_ref)
            prep = _gdn_prep(x_ref, prev_ref, w_ref, cw_ref, alog_ref, dtb_ref, lt_ref, first_tile, xp_s,
                             sets[parity])
            _interleave(prep, chains, GDN_CHAIN_UNITS_PER_PREP_UNIT)


def _gdn_intra(x, w_gdn, conv_w, alog_l, dtb_l, lt):
    b, s, d = x.shape
    ts, nh, dh = GDN_TS, GDN_HEADS, GDN_HEAD_DIM
    tps = s // ts
    nt = b * tps
    n_chain = (ts // GDN_BLK) * nh
    sq = lambda dt: pltpu.VMEM((n_chain, GDN_BLK, GDN_BLK), dt)
    buf = lambda: [sq(BF16), sq(F32), pltpu.VMEM((n_chain, GDN_BLK, 2 * dh), BF16), sq(F32), sq(BF16), sq(BF16),
                   sq(BF16), pltpu.VMEM((ts, LANES), F32)]
    src = lambda j: jnp.minimum(j, nt - 1)
    dst = lambda j: jnp.maximum(j - 1, 0)
    cspec = lambda: pl.BlockSpec((1, nh, ts // GDN_CHUNK, GDN_SCAN_ROWS, dh),
                                 lambda j: (dst(j) // tps, 0, dst(j) % tps, 0, 0))
    cshape = jax.ShapeDtypeStruct((b, nh, s // GDN_CHUNK, GDN_SCAN_ROWS, dh), BF16)
    return pl.pallas_call(
        functools.partial(_gdn_intra_body, tiles_per_seq=tps, n_buf=len(buf())),
        grid=(nt + 1,),
        in_specs=[
            pl.BlockSpec((1, ts, d), lambda j: (src(j) // tps, src(j) % tps, 0)),
            pl.BlockSpec((1, GDN_XHALO, d),
                         lambda j: (src(j) // tps, jnp.maximum((src(j) % tps) * (ts // GDN_XHALO) - 1, 0), 0)),
            _const_spec(w_gdn.shape), _const_spec(conv_w.shape), _const_spec(alog_l.shape), _const_spec(dtb_l.shape),
            _const_spec(lt.shape),
        ],
        out_specs=[cspec(), cspec(), pl.BlockSpec((1, ts, LANES), lambda j: (dst(j) // tps, dst(j) % tps, 0))],
        out_shape=[cshape, cshape, jax.ShapeDtypeStruct((b, s, LANES), F32)],
        scratch_shapes=[pltpu.VMEM((ts + GDN_HALO, 3 * GDN_W), F32)] + buf() + buf(),
        compiler_params=_cparams("arbitrary"),
        name="gdn_intra",
    )(x, x, w_gdn, conv_w, alog_l, dtb_l, lt)


def _gdn_scan_body(nq_ref, co_ref, eg_ref, gate_ref, nw_ref, out_ref, st_s):
    nh, dh, ck = GDN_HEADS, GDN_HEAD_DIM, GDN_CHUNK
    st_s[...] = jnp.zeros(st_s.shape, F32)

    def chunk(n, carry):
        r0 = pl.multiple_of(n * ck, ck)
        d_row = eg_ref[0, pl.ds(r0 + ck - 1, 1), :]
        for h in range(nh):
            st = st_s[h]
            res = _dot(nq_ref[0, h, n], st.astype(BF16)) + co_ref[0, h, n].astype(F32)
            st_s[h] = st * d_row[:, SM_DECAY + h:SM_DECAY + h + 1] + res[:dh]
            o = res[dh:]
            ms = jnp.mean(o * o, axis=1, keepdims=True)
            gt = gate_ref[0, pl.ds(r0, ck), h * dh:(h + 1) * dh].astype(F32)
            out_ref[0, pl.ds(r0, ck), h * dh:(h + 1) * dh] = (
                o * lax.rsqrt(ms + RMS_EPS) * nw_ref[...] * jax.nn.silu(gt)).astype(out_ref.dtype)
        return carry

    lax.fori_loop(0, nq_ref.shape[2], chunk, 0, unroll=16)


def _gdn_scan(nq, co, eg, ggate, norm_w):
    b, nh, nc, rows, dh = nq.shape
    s = nc * GDN_CHUNK
    cspec = lambda: pl.BlockSpec((1, nh, nc, rows, dh), lambda bi: (bi, 0, 0, 0, 0))
    return pl.pallas_call(
        _gdn_scan_body,
        grid=(b,),
        in_specs=[cspec(), cspec(),
                  pl.BlockSpec((1, s, LANES), lambda bi: (bi, 0, 0)),
                  pl.BlockSpec((1, s, GDN_W), lambda bi: (bi, 0, 0)),
                  _const_spec(norm_w.shape)],
        out_specs=pl.BlockSpec((1, s, GDN_W), lambda bi: (bi, 0, 0)),
        out_shape=jax.ShapeDtypeStruct((b, s, GDN_W), BF16),
        scratch_shapes=[pltpu.VMEM((nh, dh, dh), F32)],
        compiler_params=_cparams("parallel"),
        name="gdn_scan",
    )(nq, co, eg, ggate, norm_w)


MERGE_TM = 1024
MERGE_SUB = 256


def _layer_norm(y, g, b):
    mu = jnp.mean(y, axis=1, keepdims=True)
    d = y - mu
    var = jnp.mean(d * d, axis=1, keepdims=True)
    return d * lax.rsqrt(var + LN_EPS) * g + b


def _merge_body(x_ref, oa_ref, ob_ref, ga_ref, gb_ref, wa_ref, wb_ref, wo_ref, g_ref, b_ref, y_ref):
    def sub_tile(rows):
        ya = _dot(oa_ref[rows, :], wa_ref[...])
        yb = _dot(ob_ref[rows, :], wb_ref[...])
        yield
        mixin = (jax.nn.sigmoid(ga_ref[rows, :].astype(F32)) * ya
                 + jax.nn.sigmoid(gb_ref[rows, :].astype(F32)) * yb).astype(BF16)
        yield
        mix = _dot(mixin, wo_ref[...])
        yield
        y_ref[rows, :] = _layer_norm(DEEPNORM_ALPHA * x_ref[rows, :] + mix, g_ref[...], b_ref[...])
        yield

    n_stage = 4
    subs = [sub_tile(slice(k * MERGE_SUB, (k + 1) * MERGE_SUB)) for k in range(MERGE_TM // MERGE_SUB)]
    for step in range(n_stage + len(subs) - 1):
        for k, sub in enumerate(subs):
            if 0 <= step - k < n_stage:
                next(sub)


def _merge(x2, oa, ob, mgate, wa, wb, wo, g, b):
    m = x2.shape[0]
    tm, d = MERGE_TM, D_MODEL
    row = lambda wd, col=0: pl.BlockSpec((tm, wd), lambda i, col=col: (i, col))
    return pl.pallas_call(
        _merge_body,
        grid=(m // tm,),
        in_specs=[row(d), row(NSA_Q_W), row(GDN_W), row(d, 0), row(d, 1),
                  _const_spec(wa.shape), _const_spec(wb.shape), _const_spec(wo.shape),
                  _const_spec(g.shape), _const_spec(b.shape)],
        out_specs=row(d),
        out_shape=jax.ShapeDtypeStruct((m, d), F32),
        compiler_params=_cparams("parallel"),
        name="merge",
    )(x2, oa, ob, mgate, mgate, wa, wb, wo, g, b)


FFN_TM = 1024
FFN_HALO = 16
FFN_CK = 256
FFN_OUT_PARTS = 8


def _ffn_body(x_ref, prev_ref, wu_ref, cw_ref, wd_ref, g_ref, b_ref, out_ref, act_s, *, tiles_per_seq):
    i = pl.program_id(0)
    prev = prev_ref[...].astype(BF16)
    prev = jnp.where(i % tiles_per_seq == 0, jnp.zeros_like(prev), prev)
    xc = jnp.concatenate([prev, x_ref[...].astype(BF16)], axis=0)

    def conv(h, c0):
        out = cw_ref[FFN_CONV - 1:FFN_CONV, c0:c0 + FFN_CK] * h[FFN_HALO:]
        for j in range(FFN_CONV - 1):
            shifted = pltpu.roll(h, FFN_CONV - 1 - j, axis=0)[FFN_HALO:]
            out = out + cw_ref[j:j + 1, c0:c0 + FFN_CK] * shifted
        return out

    for c in range(FFN_DIM // FFN_CK):
        c0 = c * FFN_CK
        hg = conv(_dot(xc, wu_ref[:, c0:c0 + FFN_CK]), c0)
        hv = conv(_dot(xc, wu_ref[:, FFN_DIM + c0:FFN_DIM + c0 + FFN_CK]), FFN_DIM + c0)
        act_s[:, c0:c0 + FFN_CK] = (jax.nn.silu(hg) * hv).astype(BF16)
    part = out_ref.shape[0] // FFN_OUT_PARTS
    starts = range(0, out_ref.shape[0], part)
    f = [_dot(act_s[r:r + part, :], wd_ref[...]) for r in starts]
    for k, r in enumerate(starts):
        out_ref[r:r + part, :] = _layer_norm(DEEPNORM_ALPHA * x_ref[r:r + part, :] + f[k], g_ref[...], b_ref[...])


def _ffn(x1, wu, cw, wd, g, b, seq):
    m = x1.shape[0]
    tm, d = FFN_TM, D_MODEL
    return pl.pallas_call(
        functools.partial(_ffn_body, tiles_per_seq=seq // tm),
        grid=(m // tm,),
        in_specs=[pl.BlockSpec((tm, d), lambda i: (i, 0)),
                  pl.BlockSpec((FFN_HALO, d), lambda i: (jnp.maximum(i * (tm // FFN_HALO) - 1, 0), 0)),
                  _const_spec(wu.shape), _const_spec(cw.shape), _const_spec(wd.shape), _const_spec(g.shape), _const_spec(b.shape)],
        out_specs=pl.BlockSpec((tm, d), lambda i: (i, 0)),
        out_shape=jax.ShapeDtypeStruct((m, d), F32),
        scratch_shapes=[pltpu.VMEM((tm, FFN_DIM), BF16)],
        compiler_params=_cparams("parallel"),
        name="ffn",
    )(x1, x1, wu, cw, wd, g, b)


def _lane_vec(vals, lane0):
    return jnp.zeros((1, LANES), F32).at[0, lane0:lane0 + vals.shape[0]].set(vals.astype(F32))


def _layer(x, w_in, cmp_pos, cmp_w1, cmp_w2, w_nsa_out, gdn_conv_w, gdn_a_log, gdn_dt_bias, gdn_norm_w,
           w_gdn_out, w_o, ln1_g, ln1_b, ffn_w_up, ffn_conv_w, ffn_w_down, ln2_g, ln2_b):
    b, s, d = x.shape
    m = b * s
    x2 = x.reshape(m, d)
    w_rows, w_gdn, w_t, w_g = _wprep(*w_in)
    keys, cmpkv, ggate, mgate, qvt, gt = _inproj(x2, w_rows, w_t, w_g)

    consts = _nsa_consts(s)
    post, w2sel = _compress_weights(cmp_pos, cmp_w2)
    cmp_kv = _compress(cmpkv.reshape(b, s // CMP_STRIDE, CMP_STRIDE * CMP_COLS), cmp_w1, post, w2sel,
                       consts["cmp_aug"])
    o_nsa = _nsa(qvt, keys.reshape(b, s, KEYS_COLS), cmp_kv, gt, consts, b, s)

    ck = GDN_CHUNK
    tri = np.tril(np.ones((ck, ck), np.float32))
    lt = jnp.asarray(np.kron(np.eye(GDN_BLK // ck, dtype=np.float32), tri))
    nq, co, eg = _gdn_intra(x, w_gdn, gdn_conv_w, _lane_vec(gdn_a_log, SM_DECAY), _lane_vec(gdn_dt_bias, SM_DECAY),
                            lt)
    o_gdn = _gdn_scan(nq, co, eg, ggate.reshape(b, s, GDN_W), gdn_norm_w.reshape(1, GDN_HEAD_DIM))

    x1 = _merge(x2, o_nsa.reshape(m, NSA_Q_W), o_gdn.reshape(m, GDN_W), mgate,
                     w_nsa_out.astype(BF16), w_gdn_out.astype(BF16), w_o.astype(BF16),
                     ln1_g.reshape(1, d), ln1_b.reshape(1, d))
    out = _ffn(x1, ffn_w_up.astype(BF16), ffn_conv_w, ffn_w_down.astype(BF16),
               ln2_g.reshape(1, d), ln2_b.reshape(1, d), s)
    return out.reshape(b, s, d)


def kernel(x, w_in, nsa_cmp_pos, nsa_cmp_w1, nsa_cmp_w2, w_nsa_out, gdn_conv_w, gdn_a_log, gdn_dt_bias, gdn_norm_w, w_gdn_out, w_o, ln1_g, ln1_b, ffn_w_up, ffn_conv_w, ffn_w_down, ln2_g, ln2_b):
    w_in_t = jnp.swapaxes(w_in, 1, 2)
    for l in range(DEPTH):
        x = _layer(x, (w_in_t, l), nsa_cmp_pos[l], nsa_cmp_w1[l], nsa_cmp_w2[l], w_nsa_out[l], gdn_conv_w[l],
                   gdn_a_log[l], gdn_dt_bias[l], gdn_norm_w[l], w_gdn_out[l], w_o[l], ln1_g[l], ln1_b[l],
                   ffn_w_up[l], ffn_conv_w[l], ffn_w_down[l], ln2_g[l], ln2_b[l])
    return x
```
